```python
import math
import jax, jax.numpy as jnp
from jax import lax
import numpy as np


D_MODEL = 2048
BATCH = 4
SEQ = 2048
DEPTH = 2
DEC_BATCH = 32
DEC_SEQ = 8
PAST_LEN = 16384
PAGE_SIZE = 128

N_A_LAYERS = DEPTH // 2
N_B_LAYERS = DEPTH - N_A_LAYERS
GLA_HEADS = 4
GLA_KEY_DIM = D_MODEL // 2
GLA_VALUE_DIM = D_MODEL
GLA_DK = GLA_KEY_DIM // GLA_HEADS
GLA_DV = GLA_VALUE_DIM // GLA_HEADS
GLA_GATE_RANK = 16
GLA_GATE_TEMP = 16.0
GLA_CHUNK = 64
SWA_HEAD_DIM = 64
SWA_HEADS = D_MODEL // SWA_HEAD_DIM
SWA_KV_HEADS = 4
SWA_WIDTH = SWA_HEADS * SWA_HEAD_DIM
WINDOW = 128
ATTN_BLOCK = WINDOW
RMS_EPS = 1e-6

kernel_name = 'gla_swa_sink_yoco_hybrid_step'


def rmsnorm(x, g):
    x32 = x.astype(jnp.float32)
    y = x32 * lax.rsqrt(jnp.mean(x32 * x32, axis=-1, keepdims=True) + RMS_EPS)
    return (y * g.astype(jnp.float32)).astype(x.dtype)


def alibi_slopes(n_heads):
    return jnp.exp2(-8.0 * jnp.arange(1, n_heads + 1, dtype=jnp.float32) / n_heads)


def gla_chunked(q, k, v, logg, s0):
    B, L, H, DK = q.shape
    DV = v.shape[-1]
    C = math.gcd(L, GLA_CHUNK)
    n = L // C

    def blocks(t):
        return t.astype(jnp.float32).reshape(B, n, C, H, t.shape[-1]).transpose(1, 0, 3, 2, 4)

    causal = jnp.tril(jnp.ones((C, C), dtype=bool))[:, :, None]

    def step(S, inp):
        qc, kc, vc, gc = inp
        b = jnp.cumsum(gc, axis=2)
        o_inter = jnp.einsum('bhtd,bhde->bhte', qc * jnp.exp(b), S)
        diff = b[:, :, :, None, :] - b[:, :, None, :, :]
        decay = jnp.exp(jnp.where(causal, diff, -jnp.inf))
        scores = jnp.einsum('bhtd,bhsd,bhtsd->bhts', qc, kc, decay)
        o_intra = jnp.einsum('bhts,bhse->bhte', scores, vc)
        b_last = b[:, :, -1:, :]
        S_new = (jnp.exp(b_last[:, :, 0, :])[..., None] * S
                 + jnp.einsum('bhsd,bhse->bhde', kc * jnp.exp(b_last - b), vc))
        return S_new, o_inter + o_intra

    S_fin, o = lax.scan(step, s0.astype(jnp.float32),
                        (blocks(q), blocks(k), blocks(v), blocks(logg)))
    o = o.transpose(1, 0, 3, 2, 4).reshape(B, L, H, DV)
    return o, S_fin


def gla_layer(h, s0, g_norm, w_in, w_gate_up, b_gate, g_onorm, w_out):
    B, L, _ = h.shape
    xn = rmsnorm(h, g_norm)
    proj = xn @ w_in
    q, k, v, r, glow = jnp.split(
        proj, [GLA_KEY_DIM, 2 * GLA_KEY_DIM, 2 * GLA_KEY_DIM + GLA_VALUE_DIM,
               2 * GLA_KEY_DIM + 2 * GLA_VALUE_DIM], axis=-1)
    logg = jax.nn.log_sigmoid((glow @ w_gate_up + b_gate).astype(jnp.float32)) / GLA_GATE_TEMP
    q = q.reshape(B, L, GLA_HEADS, GLA_DK).astype(jnp.float32) * (GLA_DK ** -0.5)
    k = k.reshape(B, L, GLA_HEADS, GLA_DK)
    v = v.reshape(B, L, GLA_HEADS, GLA_DV)
    logg = logg.reshape(B, L, GLA_HEADS, GLA_DK)
    o, S = gla_chunked(q, k, v, logg, s0)
    o = rmsnorm(o, g_onorm.reshape(GLA_HEADS, GLA_DV))
    o = o.reshape(B, L, GLA_VALUE_DIM).astype(h.dtype) * jax.nn.silu(r)
    return h + o @ w_out, S.astype(s0.dtype)


def sink_attention(q, k, v, qpos, kpos, sinks):
    B, N, Tq, H, HD = q.shape
    KVH = k.shape[3]
    G = H // KVH
    qg = q.reshape(B, N, Tq, KVH, G, HD).astype(jnp.float32)
    s = jnp.einsum('bnqkgd,bnskd->bkgnqs', qg, k.astype(jnp.float32)) * (HD ** -0.5)
    dist = qpos[:, :, None] - kpos[:, None, :]
    allowed = (dist >= 0) & (dist <= WINDOW) & (kpos >= 0)[:, None, :]
    slopes = alibi_slopes(H).reshape(KVH, G)[:, :, None, None, None]
    s = s - slopes * dist.astype(jnp.float32)
    s = jnp.where(allowed, s, -jnp.inf)
    sink = sinks.astype(jnp.float32).reshape(KVH, G)[:, :, None, None, None]
    m = jnp.maximum(jnp.max(s, axis=-1, keepdims=True), sink)
    p = jnp.exp(s - m)
    denom = jnp.sum(p, axis=-1, keepdims=True) + jnp.exp(sink - m)
    o = jnp.einsum('bkgnqs,bnskd->bnqkgd', p / denom, v.astype(jnp.float32))
    return o.reshape(B, N * Tq, H * HD)


def swa_layer(h, k, v, k_buf, v_buf, g_norm, w_in, sinks, w_out):
    B, L, _ = h.shape
    xn = rmsnorm(h, g_norm)
    q, z = jnp.split(xn @ w_in, 2, axis=-1)
    q = q.reshape(B, L, SWA_HEADS, SWA_HEAD_DIM)
    if k_buf is None:
        nb = L // ATTN_BLOCK
        qb = q.reshape(B, nb, ATTN_BLOCK, SWA_HEADS, SWA_HEAD_DIM)

        def band(t):
            prev = jnp.pad(t, ((0, 0), (ATTN_BLOCK, 0), (0, 0), (0, 0)))[:, :L]
            shp = (B, nb, ATTN_BLOCK, SWA_KV_HEADS, SWA_HEAD_DIM)
            return jnp.concatenate([prev.reshape(shp), t.reshape(shp)], axis=2)

        kb, vb = band(k), band(v)
        qpos = jnp.arange(L).reshape(nb, ATTN_BLOCK)
        kpos = (jnp.arange(nb)[:, None] - 1) * ATTN_BLOCK + jnp.arange(2 * ATTN_BLOCK)[None, :]
    else:
        Wb = k_buf.shape[1]
        qb = q[:, None]
        kb = jnp.concatenate([k_buf.astype(k.dtype), k], axis=1)[:, None]
        vb = jnp.concatenate([v_buf.astype(v.dtype), v], axis=1)[:, None]
        qpos = (Wb + jnp.arange(L))[None]
        kpos = jnp.arange(Wb + L)[None]
    o = sink_attention(qb, kb, vb, qpos, kpos, sinks)
    o = o.astype(h.dtype) * jax.nn.silu(z)
    return h + o @ w_out


def trunk(x, gla_init, k_buf, v_buf, g_norm_a, w_in_a, w_gate_up, b_gate, g_onorm_a, w_out_a,
          g_norm_kv, w_kv, g_norm_b, w_in_b, sinks, w_out_b, g_final):
    B, L, _ = x.shape
    h = x
    new_states = []
    k = v = None
    for layer in range(DEPTH):
        if layer < N_A_LAYERS:
            h, S = gla_layer(h, gla_init[layer], g_norm_a[layer], w_in_a[layer], w_gate_up[layer],
                             b_gate[layer], g_onorm_a[layer], w_out_a[layer])
            new_states.append(S)
        else:
            if layer == N_A_LAYERS:
                kv = rmsnorm(h, g_norm_kv) @ w_kv
                k, v = jnp.split(kv, 2, axis=-1)
                k = k.reshape(B, L, SWA_KV_HEADS, SWA_HEAD_DIM)
                v = v.reshape(B, L, SWA_KV_HEADS, SWA_HEAD_DIM)
            j = layer - N_A_LAYERS
            h = swa_layer(h, k, v, k_buf, v_buf, g_norm_b[j], w_in_b[j], sinks[j], w_out_b[j])
    y = rmsnorm(h, g_final)
    return y, jnp.stack(new_states, axis=0), k, v


def setup_inputs(seed: int = 0) -> dict:
    key = jax.random.key(seed)
    ks = jax.random.split(key, 20)
    f32 = jnp.float32
    D = D_MODEL
    n_in_a = 2 * GLA_KEY_DIM + 2 * GLA_VALUE_DIM + GLA_GATE_RANK
    win_rows = min(WINDOW, PAST_LEN)
    nrm = lambda k_, shp, sc: jax.random.normal(k_, shp, f32) * sc
    return {
        'x_prompt': nrm(ks[0], (BATCH, SEQ, D), 1.0),
        'x_sample': nrm(ks[1], (DEC_BATCH, DEC_SEQ, D), 1.0),
        'state_gla': nrm(ks[2], (N_A_LAYERS, DEC_BATCH, GLA_HEADS, GLA_DK, GLA_DV), 0.3),
        'cache_k_win': nrm(ks[3], (DEC_BATCH, win_rows, SWA_KV_HEADS, SWA_HEAD_DIM), 1.0),
        'cache_v_win': nrm(ks[4], (DEC_BATCH, win_rows, SWA_KV_HEADS, SWA_HEAD_DIM), 1.0),
        'g_norm_a': 1.0 + nrm(ks[5], (N_A_LAYERS, D), 0.02),
        'w_in_a': nrm(ks[6], (N_A_LAYERS, D, n_in_a), D ** -0.5),
        'w_gate_up': nrm(ks[7], (N_A_LAYERS, GLA_GATE_RANK, GLA_KEY_DIM), GLA_GATE_RANK ** -0.5),
        'b_gate': nrm(ks[8], (N_A_LAYERS, GLA_KEY_DIM), 0.1),
        'g_onorm_a': 1.0 + nrm(ks[9], (N_A_LAYERS, GLA_VALUE_DIM), 0.02),
        'w_out_a': nrm(ks[10], (N_A_LAYERS, GLA_VALUE_DIM, D), GLA_VALUE_DIM ** -0.5),
        'g_norm_kv': 1.0 + nrm(ks[11], (D,), 0.02),
        'w_kv': nrm(ks[12], (D, 2 * SWA_KV_HEADS * SWA_HEAD_DIM), D ** -0.5),
        'g_norm_b': 1.0 + nrm(ks[13], (N_B_LAYERS, D), 0.02),
        'w_in_b': nrm(ks[14], (N_B_LAYERS, D, 2 * SWA_WIDTH), D ** -0.5),
        'sinks': nrm(ks[15], (N_B_LAYERS, SWA_HEADS), 1.0),
        'w_out_b': nrm(ks[16], (N_B_LAYERS, SWA_WIDTH, D), SWA_WIDTH ** -0.5),
        'g_final': 1.0 + nrm(ks[17], (D,), 0.02),
    }


def reference(x_prompt, x_sample, state_gla, cache_k_win, cache_v_win, g_norm_a, w_in_a,
              w_gate_up, b_gate, g_onorm_a, w_out_a, g_norm_kv, w_kv, g_norm_b, w_in_b, sinks,
              w_out_b, g_final):
    gla0 = jnp.zeros((N_A_LAYERS, x_prompt.shape[0], GLA_HEADS, GLA_DK, GLA_DV), x_prompt.dtype)
    y_prompt, gla_prompt, k_p, v_p = trunk(
        x_prompt, gla0, None, None, g_norm_a, w_in_a, w_gate_up, b_gate, g_onorm_a, w_out_a,
        g_norm_kv, w_kv, g_norm_b, w_in_b, sinks, w_out_b, g_final)
    y_sample, gla_sample, k_s, v_s = trunk(
        x_sample, state_gla, cache_k_win, cache_v_win, g_norm_a, w_in_a, w_gate_up, b_gate,
        g_onorm_a, w_out_a, g_norm_kv, w_kv, g_norm_b, w_in_b, sinks, w_out_b, g_final)
    win_p = min(WINDOW, k_p.shape[1])
    k_win_prompt = k_p[:, -win_p:]
    v_win_prompt = v_p[:, -win_p:]
    wb = cache_k_win.shape[1]
    k_win_sample = jnp.concatenate([cache_k_win, k_s.astype(cache_k_win.dtype)], axis=1)[:, -wb:]
    v_win_sample = jnp.concatenate([cache_v_win, v_s.astype(cache_v_win.dtype)], axis=1)[:, -wb:]
    return (y_prompt, y_sample, gla_prompt, gla_sample, k_win_prompt, v_win_prompt,
            k_win_sample, v_win_sample)
```

```python
import functools

import jax
import jax.numpy as jnp
from jax import lax
from jax.experimental import pallas as pl
from jax.experimental.pallas import tpu as pltpu

F32 = jnp.float32
BF16 = jnp.bfloat16

D_MODEL = 2048
GLA_HEADS = 4
GLA_KEY_DIM = D_MODEL // 2
GLA_VALUE_DIM = D_MODEL
GLA_DK = GLA_KEY_DIM // GLA_HEADS
GLA_DV = GLA_VALUE_DIM // GLA_HEADS
GLA_GATE_RANK = 16
GLA_GATE_TEMP = 16.0
GLA_MAIN_COLS = 2 * GLA_KEY_DIM + 2 * GLA_VALUE_DIM
SWA_HEAD_DIM = 64
SWA_HEADS = D_MODEL // SWA_HEAD_DIM
SWA_KV_HEADS = 4
SWA_GROUP = SWA_HEADS // SWA_KV_HEADS
SWA_WIDTH = SWA_HEADS * SWA_HEAD_DIM
SWA_KV_WIDTH = SWA_KV_HEADS * SWA_HEAD_DIM
WINDOW = 128
RMS_EPS = 1e-6

V7X_VMEM_LIMIT_BYTES = 56 * 1024 * 1024
GLA_PROMPT_CHUNK = 128
PROMPT_ROW_TILE = 1024
RESIDUAL_ROW_TILE = 512

_NT_DIMS = (((1,), (1,)), ((), ()))
_TN_DIMS = (((0,), (0,)), ((), ()))


def _params(semantics):
    return pltpu.CompilerParams(dimension_semantics=semantics,
                                vmem_limit_bytes=V7X_VMEM_LIMIT_BYTES)


def _silu(x):
    return x / (1.0 + jnp.exp(-x))


def _rms_scale(x):
    return lax.rsqrt(jnp.mean(x * x, axis=-1, keepdims=True) + RMS_EPS)


def _norm_matmul_kernel(x_ref, g_ref, w_ref, *rest, has_extra):
    if has_extra:
        w2_ref, o_ref, o2_ref, xn_ref = rest
    else:
        o_ref, xn_ref = rest

    @pl.when(pl.program_id(1) == 0)
    def _():
        x = x_ref[...]
        xn_ref[...] = (x * _rms_scale(x) * g_ref[...]).astype(BF16)
        if has_extra:
            o2_ref[...] = jnp.dot(xn_ref[...], w2_ref[...],
                                  preferred_element_type=F32).astype(o2_ref.dtype)

    o_ref[...] = jnp.dot(xn_ref[...], w_ref[...],
                         preferred_element_type=F32).astype(o_ref.dtype)


def _norm_matmul(x, g, w, *, tm, tn, out_dtype, w_extra=None):
    m, k = x.shape
    n = w.shape[1]
    grid = (m // tm, n // tn)
    in_specs = [
        pl.BlockSpec((tm, k), lambda i, j: (i, 0)),
        pl.BlockSpec((1, k), lambda i, j: (0, 0)),
        pl.BlockSpec((k, tn), lambda i, j: (0, j)),
    ]
    out_shape = [jax.ShapeDtypeStruct((m, n), out_dtype)]
    out_specs = [pl.BlockSpec((tm, tn), lambda i, j: (i, j))]
    args = [x, g.reshape(1, k), w]
    if w_extra is not None:
        n2 = w_extra.shape[1]
        in_specs.append(pl.BlockSpec((k, n2), lambda i, j: (0, 0)))
        out_shape.append(jax.ShapeDtypeStruct((m, n2), F32))
        out_specs.append(pl.BlockSpec((tm, n2), lambda i, j: (i, 0)))
        args.append(w_extra)
    res = pl.pallas_call(
        functools.partial(_norm_matmul_kernel, has_extra=w_extra is not None),
        grid=grid,
        in_specs=in_specs,
        out_specs=out_specs,
        out_shape=out_shape,
        scratch_shapes=[pltpu.VMEM((tm, k), BF16)],
        compiler_params=_params(("parallel", "arbitrary")),
        name=f"norm_matmul_{m}x{n}",
    )(*args)
    return res if w_extra is not None else res[0]


def _matmul_residual_kernel(a_ref, w_ref, r_ref, *rest, final_norm):
    if final_norm:
        g_ref, o_ref = rest
    else:
        (o_ref,) = rest
    h = r_ref[...] + jnp.dot(a_ref[...].astype(BF16), w_ref[...],
                             preferred_element_type=F32)
    if final_norm:
        h = h * _rms_scale(h) * g_ref[...]
    o_ref[...] = h


def _matmul_residual(a, w, res, *, tm, g_final=None):
    m, k = a.shape
    n = w.shape[1]
    in_specs = [
        pl.BlockSpec((tm, k), lambda i: (i, 0)),
        pl.BlockSpec((k, n), lambda i: (0, 0)),
        pl.BlockSpec((tm, n), lambda i: (i, 0)),
    ]
    args = [a, w, res]
    if g_final is not None:
        in_specs.append(pl.BlockSpec((1, n), lambda i: (0, 0)))
        args.append(g_final.reshape(1, n))
    return pl.pallas_call(
        functools.partial(_matmul_residual_kernel, final_norm=g_final is not None),
        grid=(m // tm,),
        in_specs=in_specs,
        out_specs=pl.BlockSpec((tm, n), lambda i: (i, 0)),
        out_shape=jax.ShapeDtypeStruct((m, n), F32),
        compiler_params=_params(("parallel",)),
        name=f"matmul_residual_{m}" + ("_final" if g_final is not None else ""),
    )(*args)


def _gla_kernel(q_ref, k_ref, v_ref, r_ref, glow_ref, wg_ref, bg_ref, gon_ref, *rest,
                has_init):
    if has_init:
        s0_ref, o_ref, sfin_ref, s_ref = rest
    else:
        o_ref, sfin_ref, s_ref = rest
    c = pl.program_id(2)
    chunk = q_ref.shape[0]

    @pl.when(c == 0)
    def _():
        if has_init:
            s_ref[...] = s0_ref[...]
        else:
            s_ref[...] = jnp.zeros_like(s_ref)

    x = jnp.dot(glow_ref[...], wg_ref[...], precision=lax.Precision.HIGHEST,
                preferred_element_type=F32) + bg_ref[...]
    logg = -(jnp.maximum(-x, 0.0) + jnp.log(1.0 + jnp.exp(-jnp.abs(x)))) * (1.0 / GLA_GATE_TEMP)

    row = lax.broadcasted_iota(jnp.int32, (chunk, chunk), 0)
    col = lax.broadcasted_iota(jnp.int32, (chunk, chunk), 1)
    causal = col <= row
    if chunk >= 16:
        tril = jnp.where(causal, 1.0, 0.0).astype(BF16)
        hi = logg.astype(BF16)
        lo = (logg - hi.astype(F32)).astype(BF16)
        bcum = (jnp.dot(tril, hi, preferred_element_type=F32)
                + jnp.dot(tril, lo, preferred_element_type=F32))
    else:
        bcum = jnp.dot(jnp.where(causal, 1.0, 0.0), logg, precision=lax.Precision.HIGHEST,
                       preferred_element_type=F32)
    b_last = bcum[chunk - 1:chunk, :]
    b_mid = bcum[chunk // 2 - 1:chunk // 2, :]

    q = q_ref[...].astype(F32) * (GLA_DK ** -0.5)
    k = k_ref[...].astype(F32)
    v = v_ref[...].astype(BF16)
    q_inter = (q * jnp.exp(bcum)).astype(BF16)
    q_intra = (q * jnp.exp(bcum - b_mid)).astype(BF16)
    k_intra = (k * jnp.exp(b_mid - bcum)).astype(BF16)
    k_state = (k * jnp.exp(b_last - bcum)).astype(BF16)

    s_old = s_ref[...]
    o = jnp.dot(q_inter, s_old.astype(BF16), preferred_element_type=F32)
    scores = lax.dot_general(q_intra, k_intra, _NT_DIMS, preferred_element_type=F32)
    scores = jnp.where(causal, scores, 0.0).astype(BF16)
    o = o + jnp.dot(scores, v, preferred_element_type=F32)

    r = r_ref[...].astype(F32)
    o_ref[...] = (o * _rms_scale(o) * gon_ref[...] * _silu(r)).astype(o_ref.dtype)

    decay = jnp.exp(jnp.broadcast_to(b_last, (128, GLA_DK))).T
    upd = lax.dot_general(k_state, v, _TN_DIMS, preferred_element_type=F32)
    for j in range(GLA_DV // 128):
        sl = slice(j * 128, (j + 1) * 128)
        s_ref[:, sl] = s_old[:, sl] * decay + upd[:, sl]

    @pl.when(c == pl.num_programs(2) - 1)
    def _():
        sfin_ref[...] = s_ref[...]


def _gla(proj, glow, wg, bg, gon, *, batch, seq, chunk, out_dtype, s0=None):
    n = seq // chunk
    kb = GLA_KEY_DIM // GLA_DK
    vb = 2 * GLA_KEY_DIM // GLA_DV
    rb = vb + GLA_VALUE_DIM // GLA_DV
    in_specs = [
        pl.BlockSpec((chunk, GLA_DK), lambda b, h, c: (b * n + c, h)),
        pl.BlockSpec((chunk, GLA_DK), lambda b, h, c: (b * n + c, kb + h)),
        pl.BlockSpec((chunk, GLA_DV), lambda b, h, c: (b * n + c, vb + h)),
        pl.BlockSpec((chunk, GLA_DV), lambda b, h, c: (b * n + c, rb + h)),
        pl.BlockSpec((chunk, GLA_GATE_RANK), lambda b, h, c: (b * n + c, 0)),
        pl.BlockSpec((GLA_GATE_RANK, GLA_DK), lambda b, h, c: (0, h)),
        pl.BlockSpec((1, GLA_DK), lambda b, h, c: (0, h)),
        pl.BlockSpec((1, GLA_DV), lambda b, h, c: (0, h)),
    ]
    args = [proj, proj, proj, proj, glow, wg, bg.reshape(1, -1), gon.reshape(1, -1)]
    state_spec = pl.BlockSpec((None, None, GLA_DK, GLA_DV), lambda b, h, c: (b, h, 0, 0))
    if s0 is not None:
        in_specs.append(state_spec)
        args.append(s0)
    return pl.pallas_call(
        functools.partial(_gla_kernel, has_init=s0 is not None),
        grid=(batch, GLA_HEADS, n),
        in_specs=in_specs,
        out_specs=[
            pl.BlockSpec((chunk, GLA_DV), lambda b, h, c: (b * n + c, h)),
            state_spec,
        ],
        out_shape=[
            jax.ShapeDtypeStruct((batch * seq, GLA_VALUE_DIM), out_dtype),
            jax.ShapeDtypeStruct((batch, GLA_HEADS, GLA_DK, GLA_DV), F32),
        ],
        scratch_shapes=[pltpu.VMEM((GLA_DK, GLA_DV), F32)],
        compiler_params=_params(("parallel", "parallel", "arbitrary")),
        name=f"gla_chunk{chunk}",
    )(*args)


def _alibi_slope(head):
    return 2.0 ** (-8.0 * (head + 1) / SWA_HEADS)


def _attend(q, z, kcat, vcat, sinks_ref, allowed, dist, out_ref, tq):
    gw = SWA_GROUP * SWA_HEAD_DIM
    for g in range(SWA_KV_HEADS):
        q_g = q[:, g * gw:(g + 1) * gw]
        qs = jnp.concatenate(
            [q_g[:, h * SWA_HEAD_DIM:(h + 1) * SWA_HEAD_DIM] for h in range(SWA_GROUP)],
            axis=0).astype(BF16)
        k_g = kcat[:, g * SWA_HEAD_DIM:(g + 1) * SWA_HEAD_DIM]
        v_g = vcat[:, g * SWA_HEAD_DIM:(g + 1) * SWA_HEAD_DIM]
        s = lax.dot_general(qs, k_g, _NT_DIMS, preferred_element_type=F32)
        s = s * (SWA_HEAD_DIM ** -0.5)
        probs, inv = [], []
        for hl in range(SWA_GROUP):
            head = g * SWA_GROUP + hl
            sh = s[hl * tq:(hl + 1) * tq]
            sh = jnp.where(allowed, sh - _alibi_slope(head) * dist, -jnp.inf)
            sink = sinks_ref[head]
            m = jnp.maximum(jnp.max(sh, axis=-1, keepdims=True), sink)
            p = jnp.exp(sh - m)
            denom = jnp.sum(p, axis=-1, keepdims=True) + jnp.exp(sink - m)
            probs.append(p)
            inv.append(1.0 / denom)
        p_all = jnp.concatenate(probs, axis=0).astype(BF16)
        o = jnp.dot(p_all, v_g, preferred_element_type=F32)
        o = o * jnp.concatenate(inv, axis=0)
        o_g = jnp.concatenate([o[hl * tq:(hl + 1) * tq] for hl in range(SWA_GROUP)], axis=1)
        z_g = z[:, g * gw:(g + 1) * gw].astype(F32)
        out_ref[:, g * gw:(g + 1) * gw] = (o_g * _silu(z_g)).astype(out_ref.dtype)


def _attn_prompt_kernel(sinks_ref, q_ref, z_ref, kp_ref, ko_ref, vp_ref, vo_ref, out_ref):
    blk = pl.program_id(1)
    kcat = jnp.concatenate([kp_ref[...], ko_ref[...]], axis=0).astype(BF16)
    vcat = jnp.concatenate([vp_ref[...], vo_ref[...]], axis=0).astype(BF16)
    shape = (WINDOW, 2 * WINDOW)
    qi = lax.broadcasted_iota(jnp.int32, shape, 0)
    kj = lax.broadcasted_iota(jnp.int32, shape, 1)
    dist = WINDOW + qi - kj
    allowed = (dist >= 0) & (dist <= WINDOW) & ((kj >= WINDOW) | (blk > 0))
    _attend(q_ref[...], z_ref[...], kcat, vcat, sinks_ref, allowed, dist.astype(F32),
            out_ref, WINDOW)


def _attn_prompt(qz, kv, sinks, *, batch, seq):
    nb = seq // WINDOW
    row = lambda b, i: b * nb + i
    prev = lambda b, i: b * nb + jnp.maximum(i - 1, 0)
    return pl.pallas_call(
        _attn_prompt_kernel,
        grid=(batch, nb),
        in_specs=[
            pl.BlockSpec(memory_space=pltpu.SMEM),
            pl.BlockSpec((WINDOW, SWA_WIDTH), lambda b, i: (row(b, i), 0)),
            pl.BlockSpec((WINDOW, SWA_WIDTH), lambda b, i: (row(b, i), 1)),
            pl.BlockSpec((WINDOW, SWA_KV_WIDTH), lambda b, i: (prev(b, i), 0)),
            pl.BlockSpec((WINDOW, SWA_KV_WIDTH), lambda b, i: (row(b, i), 0)),
            pl.BlockSpec((WINDOW, SWA_KV_WIDTH), lambda b, i: (prev(b, i), 1)),
            pl.BlockSpec((WINDOW, SWA_KV_WIDTH), lambda b, i: (row(b, i), 1)),
        ],
        out_specs=pl.BlockSpec((WINDOW, SWA_WIDTH), lambda b, i: (row(b, i), 0)),
        out_shape=jax.ShapeDtypeStruct((batch * seq, SWA_WIDTH), BF16),
        compiler_params=_params(("parallel", "arbitrary")),
        name="attn_prompt",
    )(sinks, qz, qz, kv, kv, kv, kv)


def _attn_sample_kernel(sinks_ref, q_ref, z_ref, kn_ref, vn_ref, kc_ref, vc_ref,
                        out_ref, kwin_ref, vwin_ref, *, tq):
    kc, vc, kn, vn = kc_ref[...], vc_ref[...], kn_ref[...], vn_ref[...]
    pad = jnp.zeros((WINDOW - tq, SWA_KV_WIDTH), F32)
    kcat = jnp.concatenate([kc, kn, pad], axis=0).astype(BF16)
    vcat = jnp.concatenate([vc, vn, pad], axis=0).astype(BF16)
    shape = (tq, 2 * WINDOW)
    qi = lax.broadcasted_iota(jnp.int32, shape, 0)
    kj = lax.broadcasted_iota(jnp.int32, shape, 1)
    dist = WINDOW + qi - kj
    allowed = (dist >= 0) & (dist <= WINDOW)
    _attend(q_ref[...], z_ref[...], kcat, vcat, sinks_ref, allowed, dist.astype(F32),
            out_ref, tq)
    kwin_ref[0:WINDOW - tq, :] = kc[tq:, :]
    kwin_ref[WINDOW - tq:, :] = kn
    vwin_ref[0:WINDOW - tq, :] = vc[tq:, :]
    vwin_ref[WINDOW - tq:, :] = vn


def _attn_sample(qz, kv, cache_k, cache_v, sinks, *, batch, seq):
    win_spec = pl.BlockSpec((None, WINDOW, SWA_KV_WIDTH), lambda b: (b, 0, 0))
    win_shape = jax.ShapeDtypeStruct((batch, WINDOW, SWA_KV_WIDTH), F32)
    return pl.pallas_call(
        functools.partial(_attn_sample_kernel, tq=seq),
        grid=(batch,),
        in_specs=[
            pl.BlockSpec(memory_space=pltpu.SMEM),
            pl.BlockSpec((seq, SWA_WIDTH), lambda b: (b, 0)),
            pl.BlockSpec((seq, SWA_WIDTH), lambda b: (b, 1)),
            pl.BlockSpec((seq, SWA_KV_WIDTH), lambda b: (b, 0)),
            pl.BlockSpec((seq, SWA_KV_WIDTH), lambda b: (b, 1)),
            win_spec,
            win_spec,
        ],
        out_specs=[pl.BlockSpec((seq, SWA_WIDTH), lambda b: (b, 0)), win_spec, win_spec],
        out_shape=[jax.ShapeDtypeStruct((batch * seq, SWA_WIDTH), F32), win_shape, win_shape],
        compiler_params=_params(("parallel",)),
        name="attn_sample",
    )(sinks, qz, qz, kv, kv, cache_k, cache_v)


def _trunk(x, weights, *, batch, seq, row_tile, res_tile, chunk, act_dtype,
           gla_init=None, cache=None):
    (g_norm_a, w_in_a, w_glow, w_gate_up, b_gate, g_onorm_a, w_out_a,
     g_norm_kv, w_kv, g_norm_b, w_in_b, sinks, w_out_b, g_final) = weights
    h = x.reshape(batch * seq, D_MODEL)

    proj, glow = _norm_matmul(h, g_norm_a, w_in_a, tm=row_tile, tn=1024,
                              out_dtype=act_dtype, w_extra=w_glow)
    o, s_fin = _gla(proj, glow, w_gate_up, b_gate, g_onorm_a, batch=batch, seq=seq,
                    chunk=chunk, out_dtype=act_dtype, s0=gla_init)
    h = _matmul_residual(o, w_out_a, h, tm=res_tile)

    kv = _norm_matmul(h, g_norm_kv, w_kv, tm=row_tile, tn=2 * SWA_KV_WIDTH, out_dtype=F32)
    qz = _norm_matmul(h, g_norm_b, w_in_b, tm=row_tile, tn=1024, out_dtype=act_dtype)
    if cache is None:
        att = _attn_prompt(qz, kv, sinks, batch=batch, seq=seq)
        kv4 = kv.reshape(batch, seq, 2, SWA_KV_HEADS, SWA_HEAD_DIM)
        k_win = kv4[:, seq - WINDOW:, 0]
        v_win = kv4[:, seq - WINDOW:, 1]
    else:
        cache_k, cache_v = cache
        att, k_win, v_win = _attn_sample(
            qz, kv, cache_k.reshape(batch, WINDOW, SWA_KV_WIDTH),
            cache_v.reshape(batch, WINDOW, SWA_KV_WIDTH), sinks, batch=batch, seq=seq)
        k_win = k_win.reshape(batch, WINDOW, SWA_KV_HEADS, SWA_HEAD_DIM)
        v_win = v_win.reshape(batch, WINDOW, SWA_KV_HEADS, SWA_HEAD_DIM)
    y = _matmul_residual(att, w_out_b, h, tm=res_tile, g_final=g_final)
    return y.reshape(batch, seq, D_MODEL), s_fin[None], k_win, v_win


def kernel(x_prompt, x_sample, state_gla, cache_k_win, cache_v_win, g_norm_a, w_in_a,
           w_gate_up, b_gate, g_onorm_a, w_out_a, g_norm_kv, w_kv, g_norm_b, w_in_b, sinks,
           w_out_b, g_final):
    assert w_in_a.shape[0] == 1 and w_in_b.shape[0] == 1, "one GLA layer, one SWA layer"
    assert cache_k_win.shape[1] == WINDOW
    weights = (
        g_norm_a[0],
        w_in_a[0, :, :GLA_MAIN_COLS].astype(BF16),
        w_in_a[0, :, GLA_MAIN_COLS:].astype(BF16),
        w_gate_up[0], b_gate[0], g_onorm_a[0],
        w_out_a[0].astype(BF16),
        g_norm_kv, w_kv.astype(BF16),
        g_norm_b[0], w_in_b[0].astype(BF16), sinks[0], w_out_b[0].astype(BF16),
        g_final,
    )
    pb, ps, _ = x_prompt.shape
    sb, ss, _ = x_sample.shape
    y_p, gla_p, k_p, v_p = _trunk(
        x_prompt, weights, batch=pb, seq=ps, row_tile=PROMPT_ROW_TILE,
        res_tile=RESIDUAL_ROW_TILE, chunk=GLA_PROMPT_CHUNK, act_dtype=BF16)
    y_s, gla_s, k_s, v_s = _trunk(
        x_sample, weights, batch=sb, seq=ss, row_tile=sb * ss, res_tile=sb * ss,
        chunk=ss, act_dtype=F32, gla_init=state_gla.reshape(state_gla.shape[1:]),
        cache=(cache_k_win, cache_v_win))
    return (y_p, y_s, gla_p, gla_s, k_p, v_p, k_s, v_s)
```

```python
import functools

import jax
import jax.numpy as jnp
from jax import lax
from jax.experimental import pallas as pl
from jax.experimental.pallas import tpu as pltpu

F32 = jnp.float32
BF16 = jnp.bfloat16

D_MODEL = 2048
GLA_HEADS = 4
GLA_KEY_DIM = D_MODEL // 2
GLA_VALUE_DIM = D_MODEL
GLA_DK = GLA_KEY_DIM // GLA_HEADS
GLA_DV = GLA_VALUE_DIM // GLA_HEADS
GLA_GATE_RANK = 16
GLA_GATE_TEMP = 16.0
GLA_MAIN_COLS = 2 * GLA_KEY_DIM + 2 * GLA_VALUE_DIM
SWA_HEAD_DIM = 64
SWA_HEADS = D_MODEL // SWA_HEAD_DIM
SWA_KV_HEADS = 4
SWA_GROUP = SWA_HEADS // SWA_KV_HEADS
SWA_WIDTH = SWA_HEADS * SWA_HEAD_DIM
SWA_KV_WIDTH = SWA_KV_HEADS * SWA_HEAD_DIM
WINDOW = 128
RMS_EPS = 1e-6
LOG2E = 1.4426950408889634

V7X_VMEM_LIMIT_BYTES = 56 * 1024 * 1024
GLA_PROMPT_CHUNK = 128
PROMPT_ROW_TILE = 1024
RESIDUAL_ROW_TILE = 512

_NT_DIMS = (((1,), (1,)), ((), ()))
_TN_DIMS = (((0,), (0,)), ((), ()))


def _params(semantics):
    return pltpu.CompilerParams(dimension_semantics=semantics,
                                vmem_limit_bytes=V7X_VMEM_LIMIT_BYTES)


def _silu(x):
    return x / (1.0 + jnp.exp(-x))


def _rms_scale(x):
    return lax.rsqrt(jnp.mean(x * x, axis=-1, keepdims=True) + RMS_EPS)


def _norm_matmul_kernel(x_ref, g_ref, w_ref, *rest, has_extra):
    if has_extra:
        w2_ref, o_ref, o2_ref, xn_ref = rest
    else:
        o_ref, xn_ref = rest

    @pl.when(pl.program_id(1) == 0)
    def _():
        x = x_ref[...]
        xn_ref[...] = (x * _rms_scale(x) * g_ref[...]).astype(BF16)
        if has_extra:
            o2_ref[...] = jnp.dot(xn_ref[...], w2_ref[...],
                                  preferred_element_type=F32).astype(o2_ref.dtype)

    o_ref[...] = jnp.dot(xn_ref[...], w_ref[...],
                         preferred_element_type=F32).astype(o_ref.dtype)


def _norm_matmul(x, g, w, *, tm, tn, out_dtype, n=None, w_extra=None):
    m, k = x.shape
    n = w.shape[1] if n is None else n
    assert n % tn == 0 and m % tm == 0
    grid = (m // tm, n // tn)
    in_specs = [
        pl.BlockSpec((tm, k), lambda i, j: (i, 0)),
        pl.BlockSpec((1, k), lambda i, j: (0, 0)),
        pl.BlockSpec((k, tn), lambda i, j: (0, j)),
    ]
    out_shape = [jax.ShapeDtypeStruct((m, n), out_dtype)]
    out_specs = [pl.BlockSpec((tm, tn), lambda i, j: (i, j))]
    args = [x, g.reshape(1, k), w]
    if w_extra is not None:
        n2 = w_extra.shape[1]
        in_specs.append(pl.BlockSpec((k, n2), lambda i, j: (0, 0)))
        out_shape.append(jax.ShapeDtypeStruct((m, n2), F32))
        out_specs.append(pl.BlockSpec((tm, n2), lambda i, j: (i, 0)))
        args.append(w_extra)
    res = pl.pallas_call(
        functools.partial(_norm_matmul_kernel, has_extra=w_extra is not None),
        grid=grid,
        in_specs=in_specs,
        out_specs=out_specs,
        out_shape=out_shape,
        scratch_shapes=[pltpu.VMEM((tm, k), BF16)],
        compiler_params=_params(("parallel", "arbitrary")),
        name=f"norm_matmul_{m}x{n}",
    )(*args)
    return res if w_extra is not None else res[0]


def _matmul_residual_kernel(a_ref, w_ref, r_ref, *rest, final_norm):
    if final_norm:
        g_ref, o_ref = rest
    else:
        (o_ref,) = rest
    h = r_ref[...] + jnp.dot(a_ref[...].astype(BF16), w_ref[...],
                             preferred_element_type=F32)
    if final_norm:
        h = h * _rms_scale(h) * g_ref[...]
    o_ref[...] = h


def _matmul_residual(a, w, res, *, tm, g_final=None):
    m, k = a.shape
    n = w.shape[1]
    in_specs = [
        pl.BlockSpec((tm, k), lambda i: (i, 0)),
        pl.BlockSpec((k, n), lambda i: (0, 0)),
        pl.BlockSpec((tm, n), lambda i: (i, 0)),
    ]
    args = [a, w, res]
    if g_final is not None:
        in_specs.append(pl.BlockSpec((1, n), lambda i: (0, 0)))
        args.append(g_final.reshape(1, n))
    return pl.pallas_call(
        functools.partial(_matmul_residual_kernel, final_norm=g_final is not None),
        grid=(m // tm,),
        in_specs=in_specs,
        out_specs=pl.BlockSpec((tm, n), lambda i: (i, 0)),
        out_shape=jax.ShapeDtypeStruct((m, n), F32),
        compiler_params=_params(("parallel",)),
        name=f"matmul_residual_{m}" + ("_final" if g_final is not None else ""),
    )(*args)


def _gla_kernel(q_ref, k_ref, v_ref, r_ref, glow_ref, wg_ref, bg_ref, gon_ref, *rest,
                has_init, n_chunks):
    rest = list(rest)
    s0_ref = rest.pop(0) if has_init else None
    o_ref, sfin_ref = rest[:2]
    s_ref = rest[2] if n_chunks > 1 else None
    c = pl.program_id(1)
    chunk = q_ref.shape[0]

    if n_chunks > 1:
        @pl.when(c == 0)
        def _():
            if has_init:
                s_ref[...] = s0_ref[...]
            else:
                s_ref[...] = jnp.zeros_like(s_ref)

    x = jnp.dot(glow_ref[...], wg_ref[...], precision=lax.Precision.HIGHEST,
                preferred_element_type=F32) + bg_ref[...]
    logg = -(jnp.maximum(-x, 0.0) + jnp.log(1.0 + jnp.exp(-jnp.abs(x)))) * (1.0 / GLA_GATE_TEMP)

    row = lax.broadcasted_iota(jnp.int32, (chunk, chunk), 0)
    col = lax.broadcasted_iota(jnp.int32, (chunk, chunk), 1)
    causal = col <= row
    if chunk >= 16:
        tril = jnp.where(causal, 1.0, 0.0).astype(BF16)
        hi = logg.astype(BF16)
        lo = (logg - hi.astype(F32)).astype(BF16)
        bcum = (jnp.dot(tril, hi, preferred_element_type=F32)
                + jnp.dot(tril, lo, preferred_element_type=F32))
    else:
        bcum = jnp.dot(jnp.where(causal, 1.0, 0.0), logg, precision=lax.Precision.HIGHEST,
                       preferred_element_type=F32)
    for h in range(GLA_HEADS):
        ks = slice(h * GLA_DK, (h + 1) * GLA_DK)
        vs = slice(h * GLA_DV, (h + 1) * GLA_DV)
        b = bcum[:, ks]
        b_last = b[chunk - 1:chunk, :]
        b_mid = b[chunk // 2 - 1:chunk // 2, :]

        q = q_ref[:, ks].astype(F32) * (GLA_DK ** -0.5)
        k = k_ref[:, ks].astype(F32)
        v = v_ref[:, vs].astype(BF16)
        q_inter = (q * jnp.exp(b)).astype(BF16)
        q_intra = (q * jnp.exp(b - b_mid)).astype(BF16)
        k_intra = (k * jnp.exp(b_mid - b)).astype(BF16)
        k_state = (k * jnp.exp(b_last - b)).astype(BF16)

        if n_chunks > 1:
            s_old = s_ref[h]
        else:
            s_old = s0_ref[h] if has_init else jnp.zeros((GLA_DK, GLA_DV), F32)
        o = jnp.dot(q_inter, s_old.astype(BF16), preferred_element_type=F32)
        scores = lax.dot_general(q_intra, k_intra, _NT_DIMS, preferred_element_type=F32)
        scores = jnp.where(causal, scores, 0.0).astype(BF16)
        o = o + jnp.dot(scores, v, preferred_element_type=F32)

        r = r_ref[:, vs].astype(F32)
        o_ref[:, vs] = (o * _rms_scale(o) * gon_ref[:, vs] * _silu(r)).astype(o_ref.dtype)

        decay = jnp.exp(jnp.broadcast_to(b_last, (128, GLA_DK))).T
        upd = lax.dot_general(k_state, v, _TN_DIMS, preferred_element_type=F32)
        s_new = s_old * jnp.concatenate([decay] * (GLA_DV // 128), axis=1) + upd
        if n_chunks > 1:
            s_ref[h] = s_new

            @pl.when(c == n_chunks - 1)
            def _():
                sfin_ref[h] = s_new
        else:
            sfin_ref[h] = s_new


def _gla(proj, glow, wg, bg, gon, *, batch, seq, chunk, out_dtype, s0=None):
    n = seq // chunk
    row = lambda b, c: b * n + c
    kb, vb, rb = 1, 2 * GLA_KEY_DIM // GLA_VALUE_DIM, 2 * GLA_KEY_DIM // GLA_VALUE_DIM + 1
    in_specs = [
        pl.BlockSpec((chunk, GLA_KEY_DIM), lambda b, c: (row(b, c), 0)),
        pl.BlockSpec((chunk, GLA_KEY_DIM), lambda b, c: (row(b, c), kb)),
        pl.BlockSpec((chunk, GLA_VALUE_DIM), lambda b, c: (row(b, c), vb)),
        pl.BlockSpec((chunk, GLA_VALUE_DIM), lambda b, c: (row(b, c), rb)),
        pl.BlockSpec((chunk, GLA_GATE_RANK), lambda b, c: (row(b, c), 0)),
        pl.BlockSpec((GLA_GATE_RANK, GLA_KEY_DIM), lambda b, c: (0, 0)),
        pl.BlockSpec((1, GLA_KEY_DIM), lambda b, c: (0, 0)),
        pl.BlockSpec((1, GLA_VALUE_DIM), lambda b, c: (0, 0)),
    ]
    args = [proj, proj, proj, proj, glow, wg, bg.reshape(1, -1), gon.reshape(1, -1)]
    state_spec = pl.BlockSpec((None, GLA_HEADS, GLA_DK, GLA_DV), lambda b, c: (b, 0, 0, 0))
    if s0 is not None:
        in_specs.append(state_spec)
        args.append(s0)
    scratch = [pltpu.VMEM((GLA_HEADS, GLA_DK, GLA_DV), F32)] if n > 1 else []
    return pl.pallas_call(
        functools.partial(_gla_kernel, has_init=s0 is not None, n_chunks=n),
        grid=(batch, n),
        in_specs=in_specs,
        out_specs=[
            pl.BlockSpec((chunk, GLA_VALUE_DIM), lambda b, c: (row(b, c), 0)),
            state_spec,
        ],
        out_shape=[
            jax.ShapeDtypeStruct((batch * seq, GLA_VALUE_DIM), out_dtype),
            jax.ShapeDtypeStruct((batch, GLA_HEADS, GLA_DK, GLA_DV), F32),
        ],
        scratch_shapes=scratch,
        compiler_params=_params(("parallel", "arbitrary")),
        name=f"gla_chunk{chunk}",
    )(*args)


def _alibi_slope(head):
    return 2.0 ** (-8.0 * (head + 1) / SWA_HEADS)


def _attend(q, z, kcat, vcat, sinks_ref, allowed, dist, out_ref, tq):
    gw = SWA_GROUP * SWA_HEAD_DIM
    for g in range(SWA_KV_HEADS):
        q_g = q[:, g * gw:(g + 1) * gw]
        qs = jnp.concatenate(
            [q_g[:, h * SWA_HEAD_DIM:(h + 1) * SWA_HEAD_DIM] for h in range(SWA_GROUP)],
            axis=0).astype(BF16)
        k_g = kcat[:, g * SWA_HEAD_DIM:(g + 1) * SWA_HEAD_DIM]
        v_g = vcat[:, g * SWA_HEAD_DIM:(g + 1) * SWA_HEAD_DIM]
        s = lax.dot_general(qs, k_g, _NT_DIMS, preferred_element_type=F32)
        s = s * (SWA_HEAD_DIM ** -0.5)
        probs, inv = [], []
        for hl in range(SWA_GROUP):
            head = g * SWA_GROUP + hl
            sh = s[hl * tq:(hl + 1) * tq]
            sh = jnp.where(allowed, sh - _alibi_slope(head) * dist, -jnp.inf)
            sink = sinks_ref[head]
            m = jnp.maximum(jnp.max(sh, axis=-1, keepdims=True), sink)
            p = jnp.exp(sh - m)
            denom = jnp.sum(p, axis=-1, keepdims=True) + jnp.exp(sink - m)
            probs.append(p)
            inv.append(1.0 / denom)
        p_all = jnp.concatenate(probs, axis=0).astype(BF16)
        o = jnp.dot(p_all, v_g, preferred_element_type=F32)
        o = o * jnp.concatenate(inv, axis=0)
        o_g = jnp.concatenate([o[hl * tq:(hl + 1) * tq] for hl in range(SWA_GROUP)], axis=1)
        z_g = z[:, g * gw:(g + 1) * gw].astype(F32)
        out_ref[:, g * gw:(g + 1) * gw] = (o_g * _silu(z_g)).astype(out_ref.dtype)


def _attn_prompt_kernel(sinks_ref, q_ref, z_ref, kp_ref, ko_ref, vp_ref, vo_ref, out_ref,
                        bias_ref):
    hd, nkeys = SWA_HEAD_DIM, 2 * WINDOW
    pair_w = 2 * hd
    pairs_per_group = SWA_GROUP // 2
    blk = pl.program_id(1)

    @pl.when((pl.program_id(0) == 0) & (blk == 0))
    def _():
        kj = lax.broadcasted_iota(jnp.int32, (nkeys, WINDOW), 0)
        qi = lax.broadcasted_iota(jnp.int32, (nkeys, WINDOW), 1)
        dist = WINDOW + qi - kj
        ok = (dist >= 0) & (dist <= WINDOW)
        ok_first = ok & (kj >= WINDOW)
        distf = dist.astype(F32)
        for h in range(SWA_HEADS):
            pen = (-_alibi_slope(h) * LOG2E) * distf
            sl = slice((h % 2) * WINDOW, (h % 2 + 1) * WINDOW)
            bias_ref[0, h // 2, :, sl] = jnp.where(ok_first, pen, -jnp.inf)
            bias_ref[1, h // 2, :, sl] = jnp.where(ok, pen, -jnp.inf)

    tbl = jnp.minimum(blk, 1)
    k = jnp.concatenate([kp_ref[...], ko_ref[...]], axis=0)
    v = jnp.concatenate([vp_ref[...], vo_ref[...]], axis=0)
    vt = v.T
    ones = jnp.ones((16, nkeys), F32)
    lane = lax.broadcasted_iota(jnp.int32, (nkeys, pair_w), 1)
    qlane = lax.broadcasted_iota(jnp.int32, (WINDOW, pair_w), 1)
    right_half = lax.broadcasted_iota(jnp.int32, (1, 2 * WINDOW), 1) >= WINDOW
    qk_scale = (hd ** -0.5) * LOG2E

    k2, vt1 = [], []
    for g in range(SWA_KV_HEADS):
        kblk = k[:, (g // 2) * pair_w:(g // 2 + 1) * pair_w]
        k_here = jnp.where((lane < hd) if g % 2 == 0 else (lane >= hd), kblk, 0.0)
        k2.append((k_here + pltpu.roll(k_here, hd, axis=1)).astype(BF16))
        vt1.append(jnp.concatenate([vt[g * hd:(g + 1) * hd], ones], axis=0).astype(BF16))

    def scores(pair):
        col = pair * pair_w
        q_pair = q_ref[:, col:col + pair_w]
        zero = jnp.zeros_like(q_pair)
        rq = jnp.concatenate([jnp.where(qlane < hd, q_pair, zero),
                              jnp.where(qlane >= hd, q_pair, zero)], axis=0)
        return lax.dot_general(k2[pair // pairs_per_group], rq, _NT_DIMS,
                               preferred_element_type=F32)

    n_pairs = SWA_HEADS // 2
    st_next = scores(0)
    for pair in range(n_pairs):
        st = st_next
        if pair + 1 < n_pairs:
            st_next = scores(pair + 1)
        s2 = st * qk_scale + bias_ref[tbl, pair]
        sink2 = jnp.where(right_half, sinks_ref[2 * pair + 1], sinks_ref[2 * pair]) * LOG2E
        m = jnp.maximum(jnp.max(s2, axis=0, keepdims=True), sink2)
        p = jnp.exp2(s2 - m).astype(BF16)
        oa = jnp.dot(vt1[pair // pairs_per_group], p, preferred_element_type=F32)
        denom = oa[hd:hd + 1] + jnp.exp2(sink2 - m)
        on = oa[0:hd] * (1.0 / denom)
        o_pair = jnp.concatenate([on[:, :WINDOW], on[:, WINDOW:]], axis=0).T
        col = pair * pair_w
        z_pair = z_ref[:, col:col + pair_w].astype(F32)
        out_ref[:, col:col + pair_w] = (o_pair * _silu(z_pair)).astype(out_ref.dtype)


def _attn_prompt(qz, kv, sinks, *, batch, seq):
    nb = seq // WINDOW
    row = lambda b, i: b * nb + i
    prev = lambda b, i: b * nb + jnp.maximum(i - 1, 0)
    return pl.pallas_call(
        _attn_prompt_kernel,
        grid=(batch, nb),
        in_specs=[
            pl.BlockSpec(memory_space=pltpu.SMEM),
            pl.BlockSpec((WINDOW, SWA_WIDTH), lambda b, i: (row(b, i), 0)),
            pl.BlockSpec((WINDOW, SWA_WIDTH), lambda b, i: (row(b, i), 1)),
            pl.BlockSpec((WINDOW, SWA_KV_WIDTH), lambda b, i: (prev(b, i), 0)),
            pl.BlockSpec((WINDOW, SWA_KV_WIDTH), lambda b, i: (row(b, i), 0)),
            pl.BlockSpec((WINDOW, SWA_KV_WIDTH), lambda b, i: (prev(b, i), 1)),
            pl.BlockSpec((WINDOW, SWA_KV_WIDTH), lambda b, i: (row(b, i), 1)),
        ],
        out_specs=pl.BlockSpec((WINDOW, SWA_WIDTH), lambda b, i: (row(b, i), 0)),
        out_shape=jax.ShapeDtypeStruct((batch * seq, SWA_WIDTH), BF16),
        scratch_shapes=[pltpu.VMEM((2, SWA_HEADS // 2, 2 * WINDOW, 2 * WINDOW), F32)],
        compiler_params=_params(("arbitrary", "arbitrary")),
        name="attn_prompt",
    )(sinks, qz, qz, kv, kv, kv, kv)


def _attn_sample_kernel(sinks_ref, q_ref, z_ref, kn_ref, vn_ref, kc_ref, vc_ref,
                        out_ref, kwin_ref, vwin_ref, *, tq):
    kc, vc, kn, vn = kc_ref[...], vc_ref[...], kn_ref[...], vn_ref[...]
    pad = jnp.zeros((WINDOW - tq, SWA_KV_WIDTH), F32)
    kcat = jnp.concatenate([kc, kn, pad], axis=0).astype(BF16)
    vcat = jnp.concatenate([vc, vn, pad], axis=0).astype(BF16)
    shape = (tq, 2 * WINDOW)
    qi = lax.broadcasted_iota(jnp.int32, shape, 0)
    kj = lax.broadcasted_iota(jnp.int32, shape, 1)
    dist = WINDOW + qi - kj
    allowed = (dist >= 0) & (dist <= WINDOW)
    _attend(q_ref[...], z_ref[...], kcat, vcat, sinks_ref, allowed, dist.astype(F32),
            out_ref, tq)
    kwin_ref[0:WINDOW - tq, :] = kc[tq:, :]
    kwin_ref[WINDOW - tq:, :] = kn
    vwin_ref[0:WINDOW - tq, :] = vc[tq:, :]
    vwin_ref[WINDOW - tq:, :] = vn


def _attn_sample(qz, kv, cache_k, cache_v, sinks, *, batch, seq):
    win_spec = pl.BlockSpec((None, WINDOW, SWA_KV_WIDTH), lambda b: (b, 0, 0))
    win_shape = jax.ShapeDtypeStruct((batch, WINDOW, SWA_KV_WIDTH), F32)
    return pl.pallas_call(
        functools.partial(_attn_sample_kernel, tq=seq),
        grid=(batch,),
        in_specs=[
            pl.BlockSpec(memory_space=pltpu.SMEM),
            pl.BlockSpec((seq, SWA_WIDTH), lambda b: (b, 0)),
            pl.BlockSpec((seq, SWA_WIDTH), lambda b: (b, 1)),
            pl.BlockSpec((seq, SWA_KV_WIDTH), lambda b: (b, 0)),
            pl.BlockSpec((seq, SWA_KV_WIDTH), lambda b: (b, 1)),
            win_spec,
            win_spec,
        ],
        out_specs=[pl.BlockSpec((seq, SWA_WIDTH), lambda b: (b, 0)), win_spec, win_spec],
        out_shape=[jax.ShapeDtypeStruct((batch * seq, SWA_WIDTH), F32), win_shape, win_shape],
        compiler_params=_params(("parallel",)),
        name="attn_sample",
    )(sinks, qz, qz, kv, kv, cache_k, cache_v)


def _trunk(x, weights, *, batch, seq, row_tile, res_tile, chunk, act_dtype,
           gla_init=None, cache=None):
    (g_norm_a, w_in_a, w_glow, w_gate_up, b_gate, g_onorm_a, w_out_a,
     g_norm_kv, w_kv, g_norm_b, w_in_b, sinks, w_out_b, g_final) = weights
    h = x.reshape(batch * seq, D_MODEL)

    proj, glow = _norm_matmul(h, g_norm_a, w_in_a, tm=row_tile, tn=1024, n=GLA_MAIN_COLS,
                              out_dtype=act_dtype, w_extra=w_glow)
    o, s_fin = _gla(proj, glow, w_gate_up, b_gate, g_onorm_a, batch=batch, seq=seq,
                    chunk=chunk, out_dtype=act_dtype, s0=gla_init)
    h = _matmul_residual(o, w_out_a, h, tm=res_tile)

    kv = _norm_matmul(h, g_norm_kv, w_kv, tm=row_tile, tn=2 * SWA_KV_WIDTH, out_dtype=F32)
    qz = _norm_matmul(h, g_norm_b, w_in_b, tm=row_tile, tn=1024, out_dtype=act_dtype)
    if cache is None:
        att = _attn_prompt(qz, kv, sinks, batch=batch, seq=seq)
        kv_win = kv.reshape(batch, seq, 2 * SWA_KV_WIDTH)[:, seq - WINDOW:]
        kv_win = kv_win.reshape(batch, WINDOW, 2, SWA_KV_HEADS, SWA_HEAD_DIM)
        k_win, v_win = kv_win[:, :, 0], kv_win[:, :, 1]
    else:
        cache_k, cache_v = cache
        att, k_win, v_win = _attn_sample(
            qz, kv, cache_k.reshape(batch, WINDOW, SWA_KV_WIDTH),
            cache_v.reshape(batch, WINDOW, SWA_KV_WIDTH), sinks, batch=batch, seq=seq)
        k_win = k_win.reshape(batch, WINDOW, SWA_KV_HEADS, SWA_HEAD_DIM)
        v_win = v_win.reshape(batch, WINDOW, SWA_KV_HEADS, SWA_HEAD_DIM)
    y = _matmul_residual(att, w_out_b, h, tm=res_tile, g_final=g_final)
    return y.reshape(batch, seq, D_MODEL), s_fin[None], k_win, v_win


def kernel(x_prompt, x_sample, state_gla, cache_k_win, cache_v_win, g_norm_a, w_in_a,
           w_gate_up, b_gate, g_onorm_a, w_out_a, g_norm_kv, w_kv, g_norm_b, w_in_b, sinks,
           w_out_b, g_final):
    assert w_in_a.shape[0] == 1 and w_in_b.shape[0] == 1, "one GLA layer, one SWA layer"
    assert cache_k_win.shape[1] == WINDOW
    weights = (
        g_norm_a[0],
        w_in_a.reshape(w_in_a.shape[1:]).astype(BF16),
        w_in_a[0, :, GLA_MAIN_COLS:].astype(BF16),
        w_gate_up[0], b_gate[0], g_onorm_a[0],
        w_out_a[0].astype(BF16),
        g_norm_kv, w_kv.astype(BF16),
        g_norm_b[0], w_in_b[0].astype(BF16), sinks[0], w_out_b[0].astype(BF16),
        g_final,
    )
    pb, ps, _ = x_prompt.shape
    sb, ss, _ = x_sample.shape
    y_p, gla_p, k_p, v_p = _trunk(
        x_prompt, weights, batch=pb, seq=ps, row_tile=PROMPT_ROW_TILE,
        res_tile=RESIDUAL_ROW_TILE, chunk=GLA_PROMPT_CHUNK, act_dtype=BF16)
    y_s, gla_s, k_s, v_s = _trunk(
        x_sample, weights, batch=sb, seq=ss, row_tile=sb * ss, res_tile=sb * ss,
        chunk=ss, act_dtype=F32, gla_init=state_gla.reshape(state_gla.shape[1:]),
        cache=(cache_k_win, cache_v_win))
    return (y_p, y_s, gla_p, gla_s, k_p, v_p, k_s, v_s)
```

```python
import functools

import jax
import jax.numpy as jnp
from jax import lax
from jax.experimental import pallas as pl
from jax.experimental.pallas import tpu as pltpu

F32 = jnp.float32
BF16 = jnp.bfloat16

D_MODEL = 2048
GLA_HEADS = 4
GLA_KEY_DIM = D_MODEL // 2
GLA_VALUE_DIM = D_MODEL
GLA_DK = GLA_KEY_DIM // GLA_HEADS
GLA_DV = GLA_VALUE_DIM // GLA_HEADS
GLA_GATE_RANK = 16
GLA_GATE_TEMP = 16.0
GLA_MAIN_COLS = 2 * GLA_KEY_DIM + 2 * GLA_VALUE_DIM
SWA_HEAD_DIM = 64
SWA_HEADS = D_MODEL // SWA_HEAD_DIM
SWA_KV_HEADS = 4
SWA_GROUP = SWA_HEADS // SWA_KV_HEADS
SWA_WIDTH = SWA_HEADS * SWA_HEAD_DIM
SWA_KV_WIDTH = SWA_KV_HEADS * SWA_HEAD_DIM
WINDOW = 128
RMS_EPS = 1e-6
LOG2E = 1.4426950408889634
LN2 = 0.6931471805599453

V7X_VMEM_LIMIT_BYTES = 56 * 1024 * 1024
GLA_PROMPT_CHUNK = 128
NORM_ROW_CHUNK = 256
GATE_CUMSUM_ROWS = 128
GATE_ROW_TILE = 512
PROMPT_ROW_TILE = 1024
RESIDUAL_ROW_TILE = 512

_NT_DIMS = (((1,), (1,)), ((), ()))
_TN_DIMS = (((0,), (0,)), ((), ()))


def _params(semantics):
    return pltpu.CompilerParams(dimension_semantics=semantics,
                                vmem_limit_bytes=V7X_VMEM_LIMIT_BYTES)


def _silu(x):
    return x / (1.0 + jnp.exp(-x))


def _rms_scale(x):
    return lax.rsqrt(jnp.mean(x * x, axis=-1, keepdims=True) + RMS_EPS)


def _norm_matmul_kernel(x_ref, g_ref, w_ref, *rest, has_extra):
    if has_extra:
        w2_ref, o_ref, o2_ref, xn_ref = rest
    else:
        o_ref, xn_ref = rest

    first = pl.program_id(1) == 0

    @pl.when(first)
    def _():
        tm = x_ref.shape[0]
        rc = min(tm, NORM_ROW_CHUNK)
        for c in range(tm // rc):
            rows = slice(c * rc, (c + 1) * rc)
            x = x_ref[rows, :]
            xn = (x * _rms_scale(x) * g_ref[...]).astype(BF16)
            xn_ref[rows, :] = xn
            o_ref[rows, :] = jnp.dot(xn, w_ref[...],
                                     preferred_element_type=F32).astype(o_ref.dtype)
            if has_extra:
                o2_ref[rows, :] = jnp.dot(xn, w2_ref[...],
                                          preferred_element_type=F32).astype(o2_ref.dtype)

    @pl.when(jnp.logical_not(first))
    def _():
        o_ref[...] = jnp.dot(xn_ref[...], w_ref[...],
                             preferred_element_type=F32).astype(o_ref.dtype)


def _norm_matmul(x, g, w, *, tm, tn, out_dtype, n=None, w_extra=None):
    m, k = x.shape
    n = w.shape[1] if n is None else n
    assert n % tn == 0 and m % tm == 0
    grid = (m // tm, n // tn)
    in_specs = [
        pl.BlockSpec((tm, k), lambda i, j: (i, 0)),
        pl.BlockSpec((1, k), lambda i, j: (0, 0)),
        pl.BlockSpec((k, tn), lambda i, j: (0, j)),
    ]
    out_shape = [jax.ShapeDtypeStruct((m, n), out_dtype)]
    out_specs = [pl.BlockSpec((tm, tn), lambda i, j: (i, j))]
    args = [x, g.reshape(1, k), w]
    if w_extra is not None:
        n2 = w_extra.shape[1]
        in_specs.append(pl.BlockSpec((k, n2), lambda i, j: (0, 0)))
        out_shape.append(jax.ShapeDtypeStruct((m, n2), F32))
        out_specs.append(pl.BlockSpec((tm, n2), lambda i, j: (i, 0)))
        args.append(w_extra)
    res = pl.pallas_call(
        functools.partial(_norm_matmul_kernel, has_extra=w_extra is not None),
        grid=grid,
        in_specs=in_specs,
        out_specs=out_specs,
        out_shape=out_shape,
        scratch_shapes=[pltpu.VMEM((tm, k), BF16)],
        compiler_params=_params(("parallel", "arbitrary")),
        name=f"norm_matmul_{m}x{n}",
    )(*args)
    return res if w_extra is not None else res[0]


def _matmul_residual_kernel(a_ref, w_ref, r_ref, *rest, final_norm):
    if final_norm:
        g_ref, o_ref = rest
    else:
        (o_ref,) = rest
    h = r_ref[...] + jnp.dot(a_ref[...].astype(BF16), w_ref[...],
                             preferred_element_type=F32)
    if final_norm:
        h = h * _rms_scale(h) * g_ref[...]
    o_ref[...] = h


def _matmul_residual(a, w, res, *, tm, g_final=None):
    m, k = a.shape
    n = w.shape[1]
    in_specs = [
        pl.BlockSpec((tm, k), lambda i: (i, 0)),
        pl.BlockSpec((k, n), lambda i: (0, 0)),
        pl.BlockSpec((tm, n), lambda i: (i, 0)),
    ]
    args = [a, w, res]
    if g_final is not None:
        in_specs.append(pl.BlockSpec((1, n), lambda i: (0, 0)))
        args.append(g_final.reshape(1, n))
    return pl.pallas_call(
        functools.partial(_matmul_residual_kernel, final_norm=g_final is not None),
        grid=(m // tm,),
        in_specs=in_specs,
        out_specs=pl.BlockSpec((tm, n), lambda i: (i, 0)),
        out_shape=jax.ShapeDtypeStruct((m, n), F32),
        compiler_params=_params(("parallel",)),
        name=f"matmul_residual_{m}" + ("_final" if g_final is not None else ""),
    )(*args)


def _split_bf16(x):
    hi = x.astype(BF16)
    return hi, (x - hi.astype(F32)).astype(BF16)


def _gla_gates_kernel(glow_ref, wg_ref, bg_ref, bcum_ref, *, chunk):
    rows = glow_ref.shape[0]
    g_hi, g_lo = _split_bf16(glow_ref[...])
    w_hi, w_lo = _split_bf16(wg_ref[...])
    x = jnp.dot(jnp.concatenate([g_hi, g_lo, g_hi], axis=1),
                jnp.concatenate([w_hi, w_hi, w_lo], axis=0),
                preferred_element_type=F32) + bg_ref[...]
    softplus2 = jnp.log2(1.0 + jnp.exp2(jnp.abs(x) * (-LOG2E)))
    logg = jnp.minimum(x, 0.0) * (1.0 / GLA_GATE_TEMP) - softplus2 * (LN2 / GLA_GATE_TEMP)

    span = GATE_CUMSUM_ROWS
    row = lax.broadcasted_iota(jnp.int32, (span, span), 0)
    col = lax.broadcasted_iota(jnp.int32, (span, span), 1)
    same_chunk = (row // chunk) == (col // chunk) if chunk < span else True
    tril = jnp.where((col <= row) & same_chunk, 1.0, 0.0).astype(BF16)
    tril2 = jnp.concatenate([tril, tril], axis=1)
    for i in range(rows // span):
        hi, lo = _split_bf16(logg[i * span:(i + 1) * span])
        bcum_ref[i * span:(i + 1) * span, :] = jnp.dot(
            tril2, jnp.concatenate([hi, lo], axis=0), preferred_element_type=F32)


def _gla_gates(glow, wg, bg, *, chunk, tm):
    m = glow.shape[0]
    assert m % tm == 0 and tm % GATE_CUMSUM_ROWS == 0
    assert GATE_CUMSUM_ROWS % chunk == 0 or chunk % GATE_CUMSUM_ROWS == 0
    assert chunk <= GATE_CUMSUM_ROWS, "cumulative sums do not cross row spans"
    return pl.pallas_call(
        functools.partial(_gla_gates_kernel, chunk=chunk),
        grid=(m // tm,),
        in_specs=[
            pl.BlockSpec((tm, GLA_GATE_RANK), lambda i: (i, 0)),
            pl.BlockSpec((GLA_GATE_RANK, GLA_KEY_DIM), lambda i: (0, 0)),
            pl.BlockSpec((1, GLA_KEY_DIM), lambda i: (0, 0)),
        ],
        out_specs=pl.BlockSpec((tm, GLA_KEY_DIM), lambda i: (i, 0)),
        out_shape=jax.ShapeDtypeStruct((m, GLA_KEY_DIM), F32),
        compiler_params=_params(("parallel",)),
        name=f"gla_gates_{m}",
    )(glow, wg, bg.reshape(1, -1))


def _gla_kernel(q_ref, k_ref, v_ref, r_ref, bcum_ref, gon_ref, *rest, has_init, n_chunks):
    rest = list(rest)
    s0_ref = rest.pop(0) if has_init else None
    o_ref, sfin_ref = rest[:2]
    s_ref = rest[2] if n_chunks > 1 else None
    c = pl.program_id(1)
    chunk = q_ref.shape[0]

    if n_chunks > 1:
        @pl.when(c == 0)
        def _():
            if has_init:
                s_ref[...] = s0_ref[...]
            else:
                s_ref[...] = jnp.zeros_like(s_ref)

    row = lax.broadcasted_iota(jnp.int32, (chunk, chunk), 0)
    col = lax.broadcasted_iota(jnp.int32, (chunk, chunk), 1)
    causal = col <= row
    heads = range(GLA_HEADS)
    ks = [slice(h * GLA_DK, (h + 1) * GLA_DK) for h in heads]
    vs = [slice(h * GLA_DV, (h + 1) * GLA_DV) for h in heads]

    def state(h):
        if n_chunks > 1:
            return s_ref[h]
        return s0_ref[h] if has_init else jnp.zeros((GLA_DK, GLA_DV), F32)

    q_inter, k_state, scores, decay = [], [], [], []
    for h in heads:
        b = bcum_ref[:, ks[h]]
        b_last = b[chunk - 1:chunk, :]
        b_mid = b[chunk // 2 - 1:chunk // 2, :]
        q = q_ref[:, ks[h]].astype(F32) * (GLA_DK ** -0.5)
        k = k_ref[:, ks[h]].astype(F32)
        q_inter.append((q * jnp.exp(b)).astype(BF16))
        q_intra = (q * jnp.exp(b - b_mid)).astype(BF16)
        k_intra = (k * jnp.exp(b_mid - b)).astype(BF16)
        k_state.append((k * jnp.exp(b_last - b)).astype(BF16))
        scores.append(lax.dot_general(q_intra, k_intra, _NT_DIMS,
                                      preferred_element_type=F32))
        decay.append(jnp.exp(jnp.broadcast_to(b_last, (128, GLA_DK))).T)

    o = []
    for h in heads:
        sc = jnp.where(causal, scores[h], 0.0).astype(BF16)
        v = v_ref[:, vs[h]].astype(BF16)
        o.append(jnp.dot(q_inter[h], state(h).astype(BF16), preferred_element_type=F32)
                 + jnp.dot(sc, v, preferred_element_type=F32))
        upd = lax.dot_general(k_state[h], v, _TN_DIMS, preferred_element_type=F32)
        s_new = state(h) * jnp.concatenate([decay[h]] * (GLA_DV // 128), axis=1) + upd
        if n_chunks > 1:
            s_ref[h] = s_new
        else:
            sfin_ref[h] = s_new

    scale = [_rms_scale(o[h]) for h in heads]
    for h in heads:
        r = r_ref[:, vs[h]].astype(F32)
        o_ref[:, vs[h]] = (o[h] * scale[h] * gon_ref[:, vs[h]] * _silu(r)).astype(o_ref.dtype)

    if n_chunks > 1:
        @pl.when(c == n_chunks - 1)
        def _():
            sfin_ref[...] = s_ref[...]


def _gla(proj, bcum, gon, *, batch, seq, chunk, out_dtype, s0=None):
    n = seq // chunk
    row = lambda b, c: b * n + c
    kb, vb, rb = 1, 2 * GLA_KEY_DIM // GLA_VALUE_DIM, 2 * GLA_KEY_DIM // GLA_VALUE_DIM + 1
    in_specs = [
        pl.BlockSpec((chunk, GLA_KEY_DIM), lambda b, c: (row(b, c), 0)),
        pl.BlockSpec((chunk, GLA_KEY_DIM), lambda b, c: (row(b, c), kb)),
        pl.BlockSpec((chunk, GLA_VALUE_DIM), lambda b, c: (row(b, c), vb)),
        pl.BlockSpec((chunk, GLA_VALUE_DIM), lambda b, c: (row(b, c), rb)),
        pl.BlockSpec((chunk, GLA_KEY_DIM), lambda b, c: (row(b, c), 0)),
        pl.BlockSpec((1, GLA_VALUE_DIM), lambda b, c: (0, 0)),
    ]
    args = [proj, proj, proj, proj, bcum, gon.reshape(1, -1)]
    state_spec = pl.BlockSpec((None, GLA_HEADS, GLA_DK, GLA_DV), lambda b, c: (b, 0, 0, 0))
    if s0 is not None:
        in_specs.append(state_spec)
        args.append(s0)
    scratch = [pltpu.VMEM((GLA_HEADS, GLA_DK, GLA_DV), F32)] if n > 1 else []
    return pl.pallas_call(
        functools.partial(_gla_kernel, has_init=s0 is not None, n_chunks=n),
        grid=(batch, n),
        in_specs=in_specs,
        out_specs=[
            pl.BlockSpec((chunk, GLA_VALUE_DIM), lambda b, c: (row(b, c), 0)),
            state_spec,
        ],
        out_shape=[
            jax.ShapeDtypeStruct((batch * seq, GLA_VALUE_DIM), out_dtype),
            jax.ShapeDtypeStruct((batch, GLA_HEADS, GLA_DK, GLA_DV), F32),
        ],
        scratch_shapes=scratch,
        compiler_params=_params(("parallel", "arbitrary")),
        name=f"gla_chunk{chunk}",
    )(*args)


def _alibi_slope(head):
    return 2.0 ** (-8.0 * (head + 1) / SWA_HEADS)


def _attend(q, z, kcat, vcat, sinks_ref, allowed, dist, out_ref, tq):
    gw = SWA_GROUP * SWA_HEAD_DIM
    for g in range(SWA_KV_HEADS):
        q_g = q[:, g * gw:(g + 1) * gw]
        qs = jnp.concatenate(
            [q_g[:, h * SWA_HEAD_DIM:(h + 1) * SWA_HEAD_DIM] for h in range(SWA_GROUP)],
            axis=0).astype(BF16)
        k_g = kcat[:, g * SWA_HEAD_DIM:(g + 1) * SWA_HEAD_DIM]
        v_g = vcat[:, g * SWA_HEAD_DIM:(g + 1) * SWA_HEAD_DIM]
        s = lax.dot_general(qs, k_g, _NT_DIMS, preferred_element_type=F32)
        s = s * (SWA_HEAD_DIM ** -0.5)
        probs, inv = [], []
        for hl in range(SWA_GROUP):
            head = g * SWA_GROUP + hl
            sh = s[hl * tq:(hl + 1) * tq]
            sh = jnp.where(allowed, sh - _alibi_slope(head) * dist, -jnp.inf)
            sink = sinks_ref[head]
            m = jnp.maximum(jnp.max(sh, axis=-1, keepdims=True), sink)
            p = jnp.exp(sh - m)
            denom = jnp.sum(p, axis=-1, keepdims=True) + jnp.exp(sink - m)
            probs.append(p)
            inv.append(1.0 / denom)
        p_all = jnp.concatenate(probs, axis=0).astype(BF16)
        o = jnp.dot(p_all, v_g, preferred_element_type=F32)
        o = o * jnp.concatenate(inv, axis=0)
        o_g = jnp.concatenate([o[hl * tq:(hl + 1) * tq] for hl in range(SWA_GROUP)], axis=1)
        z_g = z[:, g * gw:(g + 1) * gw].astype(F32)
        out_ref[:, g * gw:(g + 1) * gw] = (o_g * _silu(z_g)).astype(out_ref.dtype)


def _attn_prompt_kernel(sinks_ref, q_ref, z_ref, kp_ref, ko_ref, vp_ref, vo_ref, out_ref,
                        bias_ref):
    hd, nkeys = SWA_HEAD_DIM, 2 * WINDOW
    pair_w = 2 * hd
    pairs_per_group = SWA_GROUP // 2
    blk = pl.program_id(1)

    @pl.when((pl.program_id(0) == 0) & (blk == 0))
    def _():
        kj = lax.broadcasted_iota(jnp.int32, (nkeys, WINDOW), 0)
        qi = lax.broadcasted_iota(jnp.int32, (nkeys, WINDOW), 1)
        dist = WINDOW + qi - kj
        ok = (dist >= 0) & (dist <= WINDOW)
        ok_first = ok & (kj >= WINDOW)
        distf = dist.astype(F32)
        for h in range(SWA_HEADS):
            pen = (-_alibi_slope(h) * LOG2E) * distf
            sl = slice((h % 2) * WINDOW, (h % 2 + 1) * WINDOW)
            bias_ref[0, h // 2, :, sl] = jnp.where(ok_first, pen, -jnp.inf)
            bias_ref[1, h // 2, :, sl] = jnp.where(ok, pen, -jnp.inf)

    tbl = jnp.minimum(blk, 1)
    k = jnp.concatenate([kp_ref[...], ko_ref[...]], axis=0)
    v = jnp.concatenate([vp_ref[...], vo_ref[...]], axis=0)
    vt = v.T
    ones = jnp.ones((16, nkeys), F32)
    lane = lax.broadcasted_iota(jnp.int32, (nkeys, pair_w), 1)
    qlane = lax.broadcasted_iota(jnp.int32, (WINDOW, pair_w), 1)
    right_half = lax.broadcasted_iota(jnp.int32, (1, 2 * WINDOW), 1) >= WINDOW
    qk_scale = (hd ** -0.5) * LOG2E

    k2, vt1 = [], []
    for g in range(SWA_KV_HEADS):
        kblk = k[:, (g // 2) * pair_w:(g // 2 + 1) * pair_w]
        k_here = jnp.where((lane < hd) if g % 2 == 0 else (lane >= hd), kblk, 0.0)
        k2.append((k_here + pltpu.roll(k_here, hd, axis=1)).astype(BF16))
        vt1.append(jnp.concatenate([vt[g * hd:(g + 1) * hd], ones], axis=0).astype(BF16))

    def scores(pair):
        col = pair * pair_w
        q_pair = q_ref[:, col:col + pair_w]
        zero = jnp.zeros_like(q_pair)
        rq = jnp.concatenate([jnp.where(qlane < hd, q_pair, zero),
                              jnp.where(qlane >= hd, q_pair, zero)], axis=0)
        return lax.dot_general(k2[pair // pairs_per_group], rq, _NT_DIMS,
                               preferred_element_type=F32)

    n_pairs = SWA_HEADS // 2
    st_next = scores(0)
    for pair in range(n_pairs):
        st = st_next
        if pair + 1 < n_pairs:
            st_next = scores(pair + 1)
        s2 = st * qk_scale + bias_ref[tbl, pair]
        sink2 = jnp.where(right_half, sinks_ref[2 * pair + 1], sinks_ref[2 * pair]) * LOG2E
        m = jnp.maximum(jnp.max(s2, axis=0, keepdims=True), sink2)
        p = jnp.exp2(s2 - m).astype(BF16)
        oa = jnp.dot(vt1[pair // pairs_per_group], p, preferred_element_type=F32)
        denom = oa[hd:hd + 1] + jnp.exp2(sink2 - m)
        on = oa[0:hd] * (1.0 / denom)
        o_pair = jnp.concatenate([on[:, :WINDOW], on[:, WINDOW:]], axis=0).T
        col = pair * pair_w
        z_pair = z_ref[:, col:col + pair_w].astype(F32)
        out_ref[:, col:col + pair_w] = (o_pair * _silu(z_pair)).astype(out_ref.dtype)


def _attn_prompt(qz, kv, sinks, *, batch, seq):
    nb = seq // WINDOW
    row = lambda b, i: b * nb + i
    prev = lambda b, i: b * nb + jnp.maximum(i - 1, 0)
    return pl.pallas_call(
        _attn_prompt_kernel,
        grid=(batch, nb),
        in_specs=[
            pl.BlockSpec(memory_space=pltpu.SMEM),
            pl.BlockSpec((WINDOW, SWA_WIDTH), lambda b, i: (row(b, i), 0)),
            pl.BlockSpec((WINDOW, SWA_WIDTH), lambda b, i: (row(b, i), 1)),
            pl.BlockSpec((WINDOW, SWA_KV_WIDTH), lambda b, i: (prev(b, i), 0)),
            pl.BlockSpec((WINDOW, SWA_KV_WIDTH), lambda b, i: (row(b, i), 0)),
            pl.BlockSpec((WINDOW, SWA_KV_WIDTH), lambda b, i: (prev(b, i), 1)),
            pl.BlockSpec((WINDOW, SWA_KV_WIDTH), lambda b, i: (row(b, i), 1)),
        ],
        out_specs=pl.BlockSpec((WINDOW, SWA_WIDTH), lambda b, i: (row(b, i), 0)),
        out_shape=jax.ShapeDtypeStruct((batch * seq, SWA_WIDTH), BF16),
        scratch_shapes=[pltpu.VMEM((2, SWA_HEADS // 2, 2 * WINDOW, 2 * WINDOW), F32)],
        compiler_params=_params(("arbitrary", "arbitrary")),
        name="attn_prompt",
    )(sinks, qz, qz, kv, kv, kv, kv)


def _attn_sample_kernel(sinks_ref, q_ref, z_ref, kn_ref, vn_ref, kc_ref, vc_ref,
                        out_ref, kwin_ref, vwin_ref, *, tq):
    kc, vc, kn, vn = kc_ref[...], vc_ref[...], kn_ref[...], vn_ref[...]
    pad = jnp.zeros((WINDOW - tq, SWA_KV_WIDTH), F32)
    kcat = jnp.concatenate([kc, kn, pad], axis=0).astype(BF16)
    vcat = jnp.concatenate([vc, vn, pad], axis=0).astype(BF16)
    shape = (tq, 2 * WINDOW)
    qi = lax.broadcasted_iota(jnp.int32, shape, 0)
    kj = lax.broadcasted_iota(jnp.int32, shape, 1)
    dist = WINDOW + qi - kj
    allowed = (dist >= 0) & (dist <= WINDOW)
    _attend(q_ref[...], z_ref[...], kcat, vcat, sinks_ref, allowed, dist.astype(F32),
            out_ref, tq)
    kwin_ref[0:WINDOW - tq, :] = kc[tq:, :]
    kwin_ref[WINDOW - tq:, :] = kn
    vwin_ref[0:WINDOW - tq, :] = vc[tq:, :]
    vwin_ref[WINDOW - tq:, :] = vn


def _attn_sample(qz, kv, cache_k, cache_v, sinks, *, batch, seq):
    win_spec = pl.BlockSpec((None, WINDOW, SWA_KV_WIDTH), lambda b: (b, 0, 0))
    win_shape = jax.ShapeDtypeStruct((batch, WINDOW, SWA_KV_WIDTH), F32)
    return pl.pallas_call(
        functools.partial(_attn_sample_kernel, tq=seq),
        grid=(batch,),
        in_specs=[
            pl.BlockSpec(memory_space=pltpu.SMEM),
            pl.BlockSpec((seq, SWA_WIDTH), lambda b: (b, 0)),
            pl.BlockSpec((seq, SWA_WIDTH), lambda b: (b, 1)),
            pl.BlockSpec((seq, SWA_KV_WIDTH), lambda b: (b, 0)),
            pl.BlockSpec((seq, SWA_KV_WIDTH), lambda b: (b, 1)),
            win_spec,
            win_spec,
        ],
        out_specs=[pl.BlockSpec((seq, SWA_WIDTH), lambda b: (b, 0)), win_spec, win_spec],
        out_shape=[jax.ShapeDtypeStruct((batch * seq, SWA_WIDTH), F32), win_shape, win_shape],
        compiler_params=_params(("parallel",)),
        name="attn_sample",
    )(sinks, qz, qz, kv, kv, cache_k, cache_v)


def _trunk(x, weights, *, batch, seq, row_tile, res_tile, chunk, act_dtype,
           gla_init=None, cache=None):
    (g_norm_a, w_in_a, w_glow, w_gate_up, b_gate, g_onorm_a, w_out_a,
     g_norm_kv, w_kv, g_norm_b, w_in_b, sinks, w_out_b, g_final) = weights
    h = x.reshape(batch * seq, D_MODEL)

    proj, glow = _norm_matmul(h, g_norm_a, w_in_a, tm=row_tile, tn=1024, n=GLA_MAIN_COLS,
                              out_dtype=act_dtype, w_extra=w_glow)
    bcum = _gla_gates(glow, w_gate_up, b_gate, chunk=chunk, tm=min(row_tile, GATE_ROW_TILE))
    o, s_fin = _gla(proj, bcum, g_onorm_a, batch=batch, seq=seq,
                    chunk=chunk, out_dtype=act_dtype, s0=gla_init)
    h = _matmul_residual(o, w_out_a, h, tm=res_tile)

    kv = _norm_matmul(h, g_norm_kv, w_kv, tm=row_tile, tn=2 * SWA_KV_WIDTH, out_dtype=F32)
    qz = _norm_matmul(h, g_norm_b, w_in_b, tm=row_tile, tn=1024, out_dtype=act_dtype)
    if cache is None:
        att = _attn_prompt(qz, kv, sinks, batch=batch, seq=seq)
        kv_win = kv.reshape(batch, seq, 2 * SWA_KV_WIDTH)[:, seq - WINDOW:]
        kv_win = kv_win.reshape(batch, WINDOW, 2, SWA_KV_HEADS, SWA_HEAD_DIM)
        k_win, v_win = kv_win[:, :, 0], kv_win[:, :, 1]
    else:
        cache_k, cache_v = cache
        att, k_win, v_win = _attn_sample(
            qz, kv, cache_k.reshape(batch, WINDOW, SWA_KV_WIDTH),
            cache_v.reshape(batch, WINDOW, SWA_KV_WIDTH), sinks, batch=batch, seq=seq)
        k_win = k_win.reshape(batch, WINDOW, SWA_KV_HEADS, SWA_HEAD_DIM)
        v_win = v_win.reshape(batch, WINDOW, SWA_KV_HEADS, SWA_HEAD_DIM)
    y = _matmul_residual(att, w_out_b, h, tm=res_tile, g_final=g_final)
    return y.reshape(batch, seq, D_MODEL), s_fin[None], k_win, v_win


def kernel(x_prompt, x_sample, state_gla, cache_k_win, cache_v_win, g_norm_a, w_in_a,
           w_gate_up, b_gate, g_onorm_a, w_out_a, g_norm_kv, w_kv, g_norm_b, w_in_b, sinks,
           w_out_b, g_final):
    assert w_in_a.shape[0] == 1 and w_in_b.shape[0] == 1, "one GLA layer, one SWA layer"
    assert cache_k_win.shape[1] == WINDOW
    weights = (
        g_norm_a[0],
        w_in_a.reshape(w_in_a.shape[1:]).astype(BF16),
        w_in_a[0, :, GLA_MAIN_COLS:].astype(BF16),
        w_gate_up[0], b_gate[0], g_onorm_a[0],
        w_out_a[0].astype(BF16),
        g_norm_kv, w_kv.astype(BF16),
        g_norm_b[0], w_in_b[0].astype(BF16), sinks[0], w_out_b[0].astype(BF16),
        g_final,
    )
    pb, ps, _ = x_prompt.shape
    sb, ss, _ = x_sample.shape
    y_p, gla_p, k_p, v_p = _trunk(
        x_prompt, weights, batch=pb, seq=ps, row_tile=PROMPT_ROW_TILE,
        res_tile=RESIDUAL_ROW_TILE, chunk=GLA_PROMPT_CHUNK, act_dtype=BF16)
    y_s, gla_s, k_s, v_s = _trunk(
        x_sample, weights, batch=sb, seq=ss, row_tile=sb * ss, res_tile=sb * ss,
        chunk=ss, act_dtype=F32, gla_init=state_gla.reshape(state_gla.shape[1:]),
        cache=(cache_k_win, cache_v_win))
    return (y_p, y_s, gla_p, gla_s, k_p, v_p, k_s, v_s)
```

```python
import functools

import jax
import jax.numpy as jnp
from jax import lax
from jax.experimental import pallas as pl
from jax.experimental.pallas import tpu as pltpu

F32 = jnp.float32
BF16 = jnp.bfloat16

D_MODEL = 2048
GLA_HEADS = 4
GLA_KEY_DIM = D_MODEL // 2
GLA_VALUE_DIM = D_MODEL
GLA_DK = GLA_KEY_DIM // GLA_HEADS
GLA_DV = GLA_VALUE_DIM // GLA_HEADS
GLA_GATE_RANK = 16
GLA_GATE_TEMP = 16.0
GLA_MAIN_COLS = 2 * GLA_KEY_DIM + 2 * GLA_VALUE_DIM
SWA_HEAD_DIM = 64
SWA_HEADS = D_MODEL // SWA_HEAD_DIM
SWA_KV_HEADS = 4
SWA_GROUP = SWA_HEADS // SWA_KV_HEADS
SWA_WIDTH = SWA_HEADS * SWA_HEAD_DIM
SWA_KV_WIDTH = SWA_KV_HEADS * SWA_HEAD_DIM
WINDOW = 128
RMS_EPS = 1e-6
LOG2E = 1.4426950408889634
LN2 = 0.6931471805599453

V7X_VMEM_LIMIT_BYTES = 56 * 1024 * 1024
GLA_PROMPT_CHUNK = 128
NORM_ROW_CHUNK = 256
GATE_CUMSUM_ROWS = 128
GATE_ROW_TILE = 512
PROMPT_ROW_TILE = 1024
RESIDUAL_ROW_TILE = 512

_NT_DIMS = (((1,), (1,)), ((), ()))
_TN_DIMS = (((0,), (0,)), ((), ()))


def _params(semantics):
    return pltpu.CompilerParams(dimension_semantics=semantics,
                                vmem_limit_bytes=V7X_VMEM_LIMIT_BYTES)


def _silu(x):
    return x / (1.0 + jnp.exp(-x))


def _rms_scale(x):
    return lax.rsqrt(jnp.mean(x * x, axis=-1, keepdims=True) + RMS_EPS)


def _norm_matmul_kernel(x_ref, g_ref, w_ref, *rest, has_extra, extra_gain, emit_weights):
    rest = list(rest)
    g2_ref = rest.pop(0) if extra_gain else None
    w2_ref = rest.pop(0) if has_extra else None
    o_ref = rest.pop(0)
    o2_ref = rest.pop(0) if has_extra else None
    wb_ref = rest.pop(0) if emit_weights else None
    (xn_ref,) = rest

    first = pl.program_id(1) == 0

    def weights():
        w = w_ref[...].astype(BF16)
        if emit_weights:
            wb_ref[...] = w
        return w

    @pl.when(first)
    def _():
        tm = x_ref.shape[0]
        rc = min(tm, NORM_ROW_CHUNK)
        w = weights()
        for c in range(tm // rc):
            rows = slice(c * rc, (c + 1) * rc)
            x = x_ref[rows, :]
            xs = x * _rms_scale(x)
            xn = (xs * g_ref[...]).astype(BF16)
            xn_ref[rows, :] = xn
            o_ref[rows, :] = jnp.dot(xn, w, preferred_element_type=F32).astype(o_ref.dtype)
            if has_extra:
                xn2 = (xs * g2_ref[...]).astype(BF16) if extra_gain else xn
                o2_ref[rows, :] = jnp.dot(xn2, w2_ref[...].astype(BF16),
                                          preferred_element_type=F32).astype(o2_ref.dtype)

    @pl.when(jnp.logical_not(first))
    def _():
        o_ref[...] = jnp.dot(xn_ref[...], weights(),
                             preferred_element_type=F32).astype(o_ref.dtype)


def _norm_matmul(x, g, w, *, tm, tn, out_dtype, n=None, w_extra=None, g_extra=None,
                 emit_weights=False):
    m, k = x.shape
    n = w.shape[-1] if n is None else n
    assert n % tn == 0 and m % tm == 0
    assert not emit_weights or m == tm, "each weight block must be visited exactly once"
    grid = (m // tm, n // tn)
    if w.ndim == 3:
        w_spec = pl.BlockSpec((None, k, tn), lambda i, j: (0, 0, j))
    else:
        w_spec = pl.BlockSpec((k, tn), lambda i, j: (0, j))
    in_specs = [
        pl.BlockSpec((tm, k), lambda i, j: (i, 0)),
        pl.BlockSpec((1, k), lambda i, j: (0, 0)),
        w_spec,
    ]
    out_shape = [jax.ShapeDtypeStruct((m, n), out_dtype)]
    out_specs = [pl.BlockSpec((tm, tn), lambda i, j: (i, j))]
    args = [x, g.reshape(1, k), w]
    if g_extra is not None:
        in_specs.append(pl.BlockSpec((1, k), lambda i, j: (0, 0)))
        args.append(g_extra.reshape(1, k))
    if w_extra is not None:
        n2 = w_extra.shape[1]
        in_specs.append(pl.BlockSpec((k, n2), lambda i, j: (0, 0)))
        out_shape.append(jax.ShapeDtypeStruct((m, n2), F32))
        out_specs.append(pl.BlockSpec((tm, n2), lambda i, j: (i, 0)))
        args.append(w_extra)
    if emit_weights:
        out_shape.append(jax.ShapeDtypeStruct((k, n), BF16))
        out_specs.append(pl.BlockSpec((k, tn), lambda i, j: (0, j)))
    res = pl.pallas_call(
        functools.partial(_norm_matmul_kernel, has_extra=w_extra is not None,
                          extra_gain=g_extra is not None, emit_weights=emit_weights),
        grid=grid,
        in_specs=in_specs,
        out_specs=out_specs,
        out_shape=out_shape,
        scratch_shapes=[pltpu.VMEM((tm, k), BF16)],
        compiler_params=_params(("parallel", "arbitrary")),
        name=f"norm_matmul_{m}x{n}",
    )(*args)
    return res if len(res) > 1 else res[0]


def _matmul_residual_kernel(a_ref, w_ref, r_ref, *rest, final_norm):
    if final_norm:
        g_ref, o_ref = rest
    else:
        (o_ref,) = rest
    h = r_ref[...] + jnp.dot(a_ref[...].astype(BF16), w_ref[...],
                             preferred_element_type=F32)
    if final_norm:
        h = h * _rms_scale(h) * g_ref[...]
    o_ref[...] = h


def _matmul_residual(a, w, res, *, tm, g_final=None):
    m, k = a.shape
    n = w.shape[1]
    in_specs = [
        pl.BlockSpec((tm, k), lambda i: (i, 0)),
        pl.BlockSpec((k, n), lambda i: (0, 0)),
        pl.BlockSpec((tm, n), lambda i: (i, 0)),
    ]
    args = [a, w, res]
    if g_final is not None:
        in_specs.append(pl.BlockSpec((1, n), lambda i: (0, 0)))
        args.append(g_final.reshape(1, n))
    return pl.pallas_call(
        functools.partial(_matmul_residual_kernel, final_norm=g_final is not None),
        grid=(m // tm,),
        in_specs=in_specs,
        out_specs=pl.BlockSpec((tm, n), lambda i: (i, 0)),
        out_shape=jax.ShapeDtypeStruct((m, n), F32),
        compiler_params=_params(("parallel",)),
        name=f"matmul_residual_{m}" + ("_final" if g_final is not None else ""),
    )(*args)


def _split_bf16(x):
    hi = x.astype(BF16)
    return hi, (x - hi.astype(F32)).astype(BF16)


def _gla_gates_kernel(glow_ref, wg_ref, bg_ref, bcum_ref, *, chunk):
    rows = glow_ref.shape[0]
    g_hi, g_lo = _split_bf16(glow_ref[...])
    w_hi, w_lo = _split_bf16(wg_ref[...])
    x = jnp.dot(jnp.concatenate([g_hi, g_lo, g_hi], axis=1),
                jnp.concatenate([w_hi, w_hi, w_lo], axis=0),
                preferred_element_type=F32) + bg_ref[...]
    softplus2 = jnp.log2(1.0 + jnp.exp2(jnp.abs(x) * (-LOG2E)))
    logg = jnp.minimum(x, 0.0) * (1.0 / GLA_GATE_TEMP) - softplus2 * (LN2 / GLA_GATE_TEMP)

    span = GATE_CUMSUM_ROWS
    row = lax.broadcasted_iota(jnp.int32, (span, span), 0)
    col = lax.broadcasted_iota(jnp.int32, (span, span), 1)
    same_chunk = (row // chunk) == (col // chunk) if chunk < span else True
    tril = jnp.where((col <= row) & same_chunk, 1.0, 0.0).astype(BF16)
    tril2 = jnp.concatenate([tril, tril], axis=1)
    for i in range(rows // span):
        hi, lo = _split_bf16(logg[i * span:(i + 1) * span])
        bcum_ref[i * span:(i + 1) * span, :] = jnp.dot(
            tril2, jnp.concatenate([hi, lo], axis=0), preferred_element_type=F32)


def _gla_gates(glow, wg, bg, *, chunk, tm):
    m = glow.shape[0]
    assert m % tm == 0 and tm % GATE_CUMSUM_ROWS == 0
    assert GATE_CUMSUM_ROWS % chunk == 0 or chunk % GATE_CUMSUM_ROWS == 0
    assert chunk <= GATE_CUMSUM_ROWS, "cumulative sums do not cross row spans"
    return pl.pallas_call(
        functools.partial(_gla_gates_kernel, chunk=chunk),
        grid=(m // tm,),
        in_specs=[
            pl.BlockSpec((tm, GLA_GATE_RANK), lambda i: (i, 0)),
            pl.BlockSpec((GLA_GATE_RANK, GLA_KEY_DIM), lambda i: (0, 0)),
            pl.BlockSpec((1, GLA_KEY_DIM), lambda i: (0, 0)),
        ],
        out_specs=pl.BlockSpec((tm, GLA_KEY_DIM), lambda i: (i, 0)),
        out_shape=jax.ShapeDtypeStruct((m, GLA_KEY_DIM), F32),
        compiler_params=_params(("parallel",)),
        name=f"gla_gates_{m}",
    )(glow, wg, bg.reshape(1, -1))


def _gla_kernel(q_ref, k_ref, v_ref, r_ref, bcum_ref, gon_ref, *rest, has_init, n_chunks):
    rest = list(rest)
    s0_ref = rest.pop(0) if has_init else None
    o_ref, sfin_ref = rest[:2]
    s_ref = rest[2] if n_chunks > 1 else None
    c = pl.program_id(1)
    chunk = q_ref.shape[0]

    if n_chunks > 1:
        @pl.when(c == 0)
        def _():
            if has_init:
                s_ref[...] = s0_ref[...]
            else:
                s_ref[...] = jnp.zeros_like(s_ref)

    row = lax.broadcasted_iota(jnp.int32, (chunk, chunk), 0)
    col = lax.broadcasted_iota(jnp.int32, (chunk, chunk), 1)
    causal = col <= row
    heads = range(GLA_HEADS)
    ks = [slice(h * GLA_DK, (h + 1) * GLA_DK) for h in heads]
    vs = [slice(h * GLA_DV, (h + 1) * GLA_DV) for h in heads]

    def state(h):
        if n_chunks > 1:
            return s_ref[h]
        return s0_ref[h] if has_init else jnp.zeros((GLA_DK, GLA_DV), F32)

    q_inter, k_state, scores, decay = [], [], [], []
    for h in heads:
        b = bcum_ref[:, ks[h]]
        b_last = b[chunk - 1:chunk, :]
        b_mid = b[chunk // 2 - 1:chunk // 2, :]
        q = q_ref[:, ks[h]].astype(F32) * (GLA_DK ** -0.5)
        k = k_ref[:, ks[h]].astype(F32)
        q_inter.append((q * jnp.exp(b)).astype(BF16))
        q_intra = (q * jnp.exp(b - b_mid)).astype(BF16)
        k_intra = (k * jnp.exp(b_mid - b)).astype(BF16)
        k_state.append((k * jnp.exp(b_last - b)).astype(BF16))
        scores.append(lax.dot_general(q_intra, k_intra, _NT_DIMS,
                                      preferred_element_type=F32))
        decay.append(jnp.exp(jnp.broadcast_to(b_last, (128, GLA_DK))).T)

    o = []
    for h in heads:
        sc = jnp.where(causal, scores[h], 0.0).astype(BF16)
        v = v_ref[:, vs[h]].astype(BF16)
        o.append(jnp.dot(q_inter[h], state(h).astype(BF16), preferred_element_type=F32)
                 + jnp.dot(sc, v, preferred_element_type=F32))
        upd = lax.dot_general(k_state[h], v, _TN_DIMS, preferred_element_type=F32)
        s_new = state(h) * jnp.concatenate([decay[h]] * (GLA_DV // 128), axis=1) + upd
        if n_chunks > 1:
            s_ref[h] = s_new
        else:
            sfin_ref[h] = s_new

    scale = [_rms_scale(o[h]) for h in heads]
    for h in heads:
        r = r_ref[:, vs[h]].astype(F32)
        o_ref[:, vs[h]] = (o[h] * scale[h] * gon_ref[:, vs[h]] * _silu(r)).astype(o_ref.dtype)

    if n_chunks > 1:
        @pl.when(c == n_chunks - 1)
        def _():
            sfin_ref[...] = s_ref[...]


def _gla(proj, bcum, gon, *, batch, seq, chunk, out_dtype, s0=None):
    n = seq // chunk
    row = lambda b, c: b * n + c
    kb, vb, rb = 1, 2 * GLA_KEY_DIM // GLA_VALUE_DIM, 2 * GLA_KEY_DIM // GLA_VALUE_DIM + 1
    in_specs = [
        pl.BlockSpec((chunk, GLA_KEY_DIM), lambda b, c: (row(b, c), 0)),
        pl.BlockSpec((chunk, GLA_KEY_DIM), lambda b, c: (row(b, c), kb)),
        pl.BlockSpec((chunk, GLA_VALUE_DIM), lambda b, c: (row(b, c), vb)),
        pl.BlockSpec((chunk, GLA_VALUE_DIM), lambda b, c: (row(b, c), rb)),
        pl.BlockSpec((chunk, GLA_KEY_DIM), lambda b, c: (row(b, c), 0)),
        pl.BlockSpec((1, GLA_VALUE_DIM), lambda b, c: (0, 0)),
    ]
    args = [proj, proj, proj, proj, bcum, gon.reshape(1, -1)]
    state_spec = pl.BlockSpec((None, GLA_HEADS, GLA_DK, GLA_DV), lambda b, c: (b, 0, 0, 0))
    if s0 is not None:
        in_specs.append(state_spec)
        args.append(s0)
    scratch = [pltpu.VMEM((GLA_HEADS, GLA_DK, GLA_DV), F32)] if n > 1 else []
    return pl.pallas_call(
        functools.partial(_gla_kernel, has_init=s0 is not None, n_chunks=n),
        grid=(batch, n),
        in_specs=in_specs,
        out_specs=[
            pl.BlockSpec((chunk, GLA_VALUE_DIM), lambda b, c: (row(b, c), 0)),
            state_spec,
        ],
        out_shape=[
            jax.ShapeDtypeStruct((batch * seq, GLA_VALUE_DIM), out_dtype),
            jax.ShapeDtypeStruct((batch, GLA_HEADS, GLA_DK, GLA_DV), F32),
        ],
        scratch_shapes=scratch,
        compiler_params=_params(("parallel", "arbitrary")),
        name=f"gla_chunk{chunk}",
    )(*args)


def _alibi_slope(head):
    return 2.0 ** (-8.0 * (head + 1) / SWA_HEADS)


def _attend(q, z, kcat, vcat, sinks_ref, allowed, dist, out_ref, tq):
    gw = SWA_GROUP * SWA_HEAD_DIM
    for g in range(SWA_KV_HEADS):
        q_g = q[:, g * gw:(g + 1) * gw]
        qs = jnp.concatenate(
            [q_g[:, h * SWA_HEAD_DIM:(h + 1) * SWA_HEAD_DIM] for h in range(SWA_GROUP)],
            axis=0).astype(BF16)
        k_g = kcat[:, g * SWA_HEAD_DIM:(g + 1) * SWA_HEAD_DIM]
        v_g = vcat[:, g * SWA_HEAD_DIM:(g + 1) * SWA_HEAD_DIM]
        s = lax.dot_general(qs, k_g, _NT_DIMS, preferred_element_type=F32)
        s = s * (SWA_HEAD_DIM ** -0.5)
        probs, inv = [], []
        for hl in range(SWA_GROUP):
            head = g * SWA_GROUP + hl
            sh = s[hl * tq:(hl + 1) * tq]
            sh = jnp.where(allowed, sh - _alibi_slope(head) * dist, -jnp.inf)
            sink = sinks_ref[head]
            m = jnp.maximum(jnp.max(sh, axis=-1, keepdims=True), sink)
            p = jnp.exp(sh - m)
            denom = jnp.sum(p, axis=-1, keepdims=True) + jnp.exp(sink - m)
            probs.append(p)
            inv.append(1.0 / denom)
        p_all = jnp.concatenate(probs, axis=0).astype(BF16)
        o = jnp.dot(p_all, v_g, preferred_element_type=F32)
        o = o * jnp.concatenate(inv, axis=0)
        o_g = jnp.concatenate([o[hl * tq:(hl + 1) * tq] for hl in range(SWA_GROUP)], axis=1)
        z_g = z[:, g * gw:(g + 1) * gw].astype(F32)
        out_ref[:, g * gw:(g + 1) * gw] = (o_g * _silu(z_g)).astype(out_ref.dtype)


def _attn_prompt_kernel(sinks_ref, q_ref, z_ref, kp_ref, ko_ref, vp_ref, vo_ref, out_ref,
                        bias_ref):
    hd, nkeys = SWA_HEAD_DIM, 2 * WINDOW
    pair_w = 2 * hd
    pairs_per_group = SWA_GROUP // 2
    blk = pl.program_id(1)

    @pl.when((pl.program_id(0) == 0) & (blk == 0))
    def _():
        kj = lax.broadcasted_iota(jnp.int32, (nkeys, WINDOW), 0)
        qi = lax.broadcasted_iota(jnp.int32, (nkeys, WINDOW), 1)
        dist = WINDOW + qi - kj
        ok = (dist >= 0) & (dist <= WINDOW)
        ok_first = ok & (kj >= WINDOW)
        distf = dist.astype(F32)
        for h in range(SWA_HEADS):
            pen = (-_alibi_slope(h) * LOG2E) * distf
            sl = slice((h % 2) * WINDOW, (h % 2 + 1) * WINDOW)
            bias_ref[0, h // 2, :, sl] = jnp.where(ok_first, pen, -jnp.inf)
            bias_ref[1, h // 2, :, sl] = jnp.where(ok, pen, -jnp.inf)

    tbl = jnp.minimum(blk, 1)
    k = jnp.concatenate([kp_ref[...], ko_ref[...]], axis=0)
    v = jnp.concatenate([vp_ref[...], vo_ref[...]], axis=0)
    vt = v.T
    ones = jnp.ones((16, nkeys), F32)
    lane = lax.broadcasted_iota(jnp.int32, (nkeys, pair_w), 1)
    qlane = lax.broadcasted_iota(jnp.int32, (WINDOW, pair_w), 1)
    qk_scale = (hd ** -0.5) * LOG2E

    k2, vt1 = [], []
    for g in range(SWA_KV_HEADS):
        kblk = k[:, (g // 2) * pair_w:(g // 2 + 1) * pair_w]
        k_here = jnp.where((lane < hd) if g % 2 == 0 else (lane >= hd), kblk, 0.0)
        k2.append((k_here + pltpu.roll(k_here, hd, axis=1)).astype(BF16))
        vt1.append(jnp.concatenate([vt[g * hd:(g + 1) * hd], ones], axis=0).astype(BF16))

    quad_pairs = 2
    n_quads = SWA_HEADS // (2 * quad_pairs)
    quads_per_group = pairs_per_group // quad_pairs

    def scores(quad):
        rows = []
        for j in range(quad_pairs):
            col = (quad * quad_pairs + j) * pair_w
            q_pair = q_ref[:, col:col + pair_w]
            zero = jnp.zeros_like(q_pair)
            rows += [jnp.where(qlane < hd, q_pair, zero), jnp.where(qlane >= hd, q_pair, zero)]
        return lax.dot_general(k2[quad // quads_per_group], jnp.concatenate(rows, axis=0),
                               _NT_DIMS, preferred_element_type=F32)

    quarter = lax.broadcasted_iota(jnp.int32, (1, 2 * quad_pairs * WINDOW), 1) // WINDOW
    st_next = scores(0)
    for quad in range(n_quads):
        st = st_next
        if quad + 1 < n_quads:
            st_next = scores(quad + 1)
        pair0 = quad * quad_pairs
        bias = jnp.concatenate([bias_ref[tbl, pair0 + j] for j in range(quad_pairs)], axis=1)
        s2 = st * qk_scale + bias
        sink2 = sinks_ref[2 * pair0]
        for t in range(1, 2 * quad_pairs):
            sink2 = jnp.where(quarter == t, sinks_ref[2 * pair0 + t], sink2)
        sink2 = sink2 * LOG2E
        m = jnp.maximum(jnp.max(s2, axis=0, keepdims=True), sink2)
        p = jnp.exp2(s2 - m).astype(BF16)
        oa = jnp.dot(vt1[quad // quads_per_group], p, preferred_element_type=F32)
        denom = oa[hd:hd + 1] + jnp.exp2(sink2 - m)
        on = oa[0:hd] * (1.0 / denom)
        for j in range(quad_pairs):
            lo = 2 * j * WINDOW
            o_pair = jnp.concatenate([on[:, lo:lo + WINDOW],
                                      on[:, lo + WINDOW:lo + 2 * WINDOW]], axis=0).T
            col = (pair0 + j) * pair_w
            z_pair = z_ref[:, col:col + pair_w].astype(F32)
            out_ref[:, col:col + pair_w] = (o_pair * _silu(z_pair)).astype(out_ref.dtype)


def _attn_prompt(qz, kv, sinks, *, batch, seq):
    nb = seq // WINDOW
    row = lambda b, i: b * nb + i
    prev = lambda b, i: b * nb + jnp.maximum(i - 1, 0)
    return pl.pallas_call(
        _attn_prompt_kernel,
        grid=(batch, nb),
        in_specs=[
            pl.BlockSpec(memory_space=pltpu.SMEM),
            pl.BlockSpec((WINDOW, SWA_WIDTH), lambda b, i: (row(b, i), 0)),
            pl.BlockSpec((WINDOW, SWA_WIDTH), lambda b, i: (row(b, i), 1)),
            pl.BlockSpec((WINDOW, SWA_KV_WIDTH), lambda b, i: (prev(b, i), 0)),
            pl.BlockSpec((WINDOW, SWA_KV_WIDTH), lambda b, i: (row(b, i), 0)),
            pl.BlockSpec((WINDOW, SWA_KV_WIDTH), lambda b, i: (prev(b, i), 1)),
            pl.BlockSpec((WINDOW, SWA_KV_WIDTH), lambda b, i: (row(b, i), 1)),
        ],
        out_specs=pl.BlockSpec((WINDOW, SWA_WIDTH), lambda b, i: (row(b, i), 0)),
        out_shape=jax.ShapeDtypeStruct((batch * seq, SWA_WIDTH), BF16),
        scratch_shapes=[pltpu.VMEM((2, SWA_HEADS // 2, 2 * WINDOW, 2 * WINDOW), F32)],
        compiler_params=_params(("arbitrary", "arbitrary")),
        name="attn_prompt",
    )(sinks, qz, qz, kv, kv, kv, kv)


def _attn_sample_kernel(sinks_ref, q_ref, z_ref, kn_ref, vn_ref, kc_ref, vc_ref,
                        out_ref, kwin_ref, vwin_ref, *, tq):
    kc, vc, kn, vn = kc_ref[...], vc_ref[...], kn_ref[...], vn_ref[...]
    pad = jnp.zeros((WINDOW - tq, SWA_KV_WIDTH), F32)
    kcat = jnp.concatenate([kc, kn, pad], axis=0).astype(BF16)
    vcat = jnp.concatenate([vc, vn, pad], axis=0).astype(BF16)
    shape = (tq, 2 * WINDOW)
    qi = lax.broadcasted_iota(jnp.int32, shape, 0)
    kj = lax.broadcasted_iota(jnp.int32, shape, 1)
    dist = WINDOW + qi - kj
    allowed = (dist >= 0) & (dist <= WINDOW)
    _attend(q_ref[...], z_ref[...], kcat, vcat, sinks_ref, allowed, dist.astype(F32),
            out_ref, tq)
    kwin_ref[0:WINDOW - tq, :] = kc[tq:, :]
    kwin_ref[WINDOW - tq:, :] = kn
    vwin_ref[0:WINDOW - tq, :] = vc[tq:, :]
    vwin_ref[WINDOW - tq:, :] = vn


def _attn_sample(qz, kv, cache_k, cache_v, sinks, *, batch, seq):
    win_spec = pl.BlockSpec((None, WINDOW, SWA_KV_WIDTH), lambda b: (b, 0, 0))
    win_shape = jax.ShapeDtypeStruct((batch, WINDOW, SWA_KV_WIDTH), F32)
    return pl.pallas_call(
        functools.partial(_attn_sample_kernel, tq=seq),
        grid=(batch,),
        in_specs=[
            pl.BlockSpec(memory_space=pltpu.SMEM),
            pl.BlockSpec((seq, SWA_WIDTH), lambda b: (b, 0)),
            pl.BlockSpec((seq, SWA_WIDTH), lambda b: (b, 1)),
            pl.BlockSpec((seq, SWA_KV_WIDTH), lambda b: (b, 0)),
            pl.BlockSpec((seq, SWA_KV_WIDTH), lambda b: (b, 1)),
            win_spec,
            win_spec,
        ],
        out_specs=[pl.BlockSpec((seq, SWA_WIDTH), lambda b: (b, 0)), win_spec, win_spec],
        out_shape=[jax.ShapeDtypeStruct((batch * seq, SWA_WIDTH), F32), win_shape, win_shape],
        compiler_params=_params(("parallel",)),
        name="attn_sample",
    )(sinks, qz, qz, kv, kv, cache_k, cache_v)


def _trunk(x, weights, *, batch, seq, row_tile, res_tile, chunk, act_dtype,
           gla_init=None, cache=None, emit_weights=False):
    (g_norm_a, w_in_a, w_glow, w_gate_up, b_gate, g_onorm_a, w_out_a,
     g_norm_kv, w_kv, g_norm_b, w_in_b, sinks, w_out_b, g_final) = weights
    h = x.reshape(batch * seq, D_MODEL)

    proj, glow, *w_in_a_bf16 = _norm_matmul(
        h, g_norm_a, w_in_a, tm=row_tile, tn=1024, n=GLA_MAIN_COLS, out_dtype=act_dtype,
        w_extra=w_glow, emit_weights=emit_weights)
    bcum = _gla_gates(glow, w_gate_up, b_gate, chunk=chunk, tm=min(row_tile, GATE_ROW_TILE))
    o, s_fin = _gla(proj, bcum, g_onorm_a, batch=batch, seq=seq,
                    chunk=chunk, out_dtype=act_dtype, s0=gla_init)
    h = _matmul_residual(o, w_out_a, h, tm=res_tile)

    qz, kv, *w_in_b_bf16 = _norm_matmul(
        h, g_norm_b, w_in_b, tm=row_tile, tn=1024, out_dtype=act_dtype,
        w_extra=w_kv, g_extra=g_norm_kv, emit_weights=emit_weights)
    if cache is None:
        att = _attn_prompt(qz, kv, sinks, batch=batch, seq=seq)
        kv_win = kv.reshape(batch, seq, 2 * SWA_KV_WIDTH)[:, seq - WINDOW:]
        kv_win = kv_win.reshape(batch, WINDOW, 2, SWA_KV_HEADS, SWA_HEAD_DIM)
        k_win, v_win = kv_win[:, :, 0], kv_win[:, :, 1]
    else:
        cache_k, cache_v = cache
        att, k_win, v_win = _attn_sample(
            qz, kv, cache_k.reshape(batch, WINDOW, SWA_KV_WIDTH),
            cache_v.reshape(batch, WINDOW, SWA_KV_WIDTH), sinks, batch=batch, seq=seq)
        k_win = k_win.reshape(batch, WINDOW, SWA_KV_HEADS, SWA_HEAD_DIM)
        v_win = v_win.reshape(batch, WINDOW, SWA_KV_HEADS, SWA_HEAD_DIM)
    y = _matmul_residual(att, w_out_b, h, tm=res_tile, g_final=g_final)
    outs = (y.reshape(batch, seq, D_MODEL), s_fin[None], k_win, v_win)
    return outs, (w_in_a_bf16 + w_in_b_bf16)


def kernel(x_prompt, x_sample, state_gla, cache_k_win, cache_v_win, g_norm_a, w_in_a,
           w_gate_up, b_gate, g_onorm_a, w_out_a, g_norm_kv, w_kv, g_norm_b, w_in_b, sinks,
           w_out_b, g_final):
    assert w_in_a.shape[0] == 1 and w_in_b.shape[0] == 1, "one GLA layer, one SWA layer"
    assert cache_k_win.shape[1] == WINDOW
    pb, ps, _ = x_prompt.shape
    sb, ss, _ = x_sample.shape

    def weights(w_in_a_any, w_in_b_any):
        return (
            g_norm_a[0], w_in_a_any, w_in_a[0, :, GLA_MAIN_COLS:],
            w_gate_up[0], b_gate[0], g_onorm_a[0], w_out_a[0].astype(BF16),
            g_norm_kv, w_kv.astype(BF16),
            g_norm_b[0], w_in_b_any, sinks[0], w_out_b[0].astype(BF16), g_final,
        )

    (y_s, gla_s, k_s, v_s), (w_in_a_bf16, w_in_b_bf16) = _trunk(
        x_sample, weights(w_in_a, w_in_b), batch=sb, seq=ss, row_tile=sb * ss,
        res_tile=sb * ss, chunk=ss, act_dtype=F32,
        gla_init=state_gla.reshape(state_gla.shape[1:]),
        cache=(cache_k_win, cache_v_win), emit_weights=True)
    (y_p, gla_p, k_p, v_p), _ = _trunk(
        x_prompt, weights(w_in_a_bf16, w_in_b_bf16), batch=pb, seq=ps,
        row_tile=PROMPT_ROW_TILE, res_tile=RESIDUAL_ROW_TILE, chunk=GLA_PROMPT_CHUNK,
        act_dtype=BF16)
    return (y_p, y_s, gla_p, gla_s, k_p, v_p, k_s, v_s)
```

```python
import functools

import jax
import jax.numpy as jnp
from jax import lax
from jax.experimental import pallas as pl
from jax.experimental.pallas import tpu as pltpu

F32 = jnp.float32
BF16 = jnp.bfloat16

D_MODEL = 2048
GLA_HEADS = 4
GLA_KEY_DIM = D_MODEL // 2
GLA_VALUE_DIM = D_MODEL
GLA_DK = GLA_KEY_DIM // GLA_HEADS
GLA_DV = GLA_VALUE_DIM // GLA_HEADS
GLA_GATE_RANK = 16
GLA_GATE_TEMP = 16.0
GLA_MAIN_COLS = 2 * GLA_KEY_DIM + 2 * GLA_VALUE_DIM
SWA_HEAD_DIM = 64
SWA_HEADS = D_MODEL // SWA_HEAD_DIM
SWA_KV_HEADS = 4
SWA_GROUP = SWA_HEADS // SWA_KV_HEADS
SWA_WIDTH = SWA_HEADS * SWA_HEAD_DIM
SWA_KV_WIDTH = SWA_KV_HEADS * SWA_HEAD_DIM
WINDOW = 128
RMS_EPS = 1e-6
LOG2E = 1.4426950408889634
LN2 = 0.6931471805599453

V7X_VMEM_LIMIT_BYTES = 56 * 1024 * 1024
GLA_PROMPT_CHUNK = 128
NORM_ROW_CHUNK = 256
GATE_CUMSUM_ROWS = 128
GATE_ROW_TILE = 512
PROMPT_ROW_TILE = 1024
RESIDUAL_ROW_TILE = 512

_NT_DIMS = (((1,), (1,)), ((), ()))
_TN_DIMS = (((0,), (0,)), ((), ()))


def _params(semantics):
    return pltpu.CompilerParams(dimension_semantics=semantics,
                                vmem_limit_bytes=V7X_VMEM_LIMIT_BYTES)


def _silu(x):
    return x / (1.0 + jnp.exp(-x))


def _rms_scale(x):
    return lax.rsqrt(jnp.mean(x * x, axis=-1, keepdims=True) + RMS_EPS)


def _norm_matmul_kernel(x_ref, g_ref, w_ref, *rest, has_extra, extra_gain, emit_weights,
                        w_transposed):
    dims = _NT_DIMS if w_transposed else (((1,), (0,)), ((), ()))

    def mm(a, b):
        return lax.dot_general(a, b, dims, preferred_element_type=F32)

    rest = list(rest)
    g2_ref = rest.pop(0) if extra_gain else None
    w2_ref = rest.pop(0) if has_extra else None
    o_ref = rest.pop(0)
    o2_ref = rest.pop(0) if has_extra else None
    wb_ref = rest.pop(0) if emit_weights else None
    (xn_ref,) = rest

    first = pl.program_id(1) == 0

    def weights():
        w = w_ref[...].astype(BF16)
        if emit_weights:
            wb_ref[...] = w
        return w

    @pl.when(first)
    def _():
        tm = x_ref.shape[0]
        rc = min(tm, NORM_ROW_CHUNK)
        w = weights()
        for c in range(tm // rc):
            rows = slice(c * rc, (c + 1) * rc)
            x = x_ref[rows, :]
            xs = x * _rms_scale(x)
            xn = (xs * g_ref[...]).astype(BF16)
            xn_ref[rows, :] = xn
            o_ref[rows, :] = mm(xn, w).astype(o_ref.dtype)
            if has_extra:
                xn2 = (xs * g2_ref[...]).astype(BF16) if extra_gain else xn
                o2_ref[rows, :] = mm(xn2, w2_ref[...].astype(BF16)).astype(o2_ref.dtype)

    @pl.when(jnp.logical_not(first))
    def _():
        o_ref[...] = mm(xn_ref[...], weights()).astype(o_ref.dtype)


def _norm_matmul(x, g, w, *, tm, tn, out_dtype, n=None, w_extra=None, g_extra=None,
                 emit_weights=False, w_transposed=False):
    m, k = x.shape
    n_axis = -2 if w_transposed else -1
    n = w.shape[n_axis] if n is None else n
    assert n % tn == 0 and m % tm == 0
    assert not emit_weights or m == tm, "each weight block must be visited exactly once"
    grid = (m // tm, n // tn)
    w_block = (tn, k) if w_transposed else (k, tn)
    w_index = (lambda i, j: (j, 0)) if w_transposed else (lambda i, j: (0, j))
    if w.ndim == 3:
        w_spec = pl.BlockSpec((None,) + w_block, lambda i, j: (0,) + w_index(i, j))
    else:
        w_spec = pl.BlockSpec(w_block, w_index)
    in_specs = [
        pl.BlockSpec((tm, k), lambda i, j: (i, 0)),
        pl.BlockSpec((1, k), lambda i, j: (0, 0)),
        w_spec,
    ]
    out_shape = [jax.ShapeDtypeStruct((m, n), out_dtype)]
    out_specs = [pl.BlockSpec((tm, tn), lambda i, j: (i, j))]
    args = [x, g.reshape(1, k), w]
    if g_extra is not None:
        in_specs.append(pl.BlockSpec((1, k), lambda i, j: (0, 0)))
        args.append(g_extra.reshape(1, k))
    if w_extra is not None:
        n2 = w_extra.shape[n_axis]
        in_specs.append(pl.BlockSpec(w_extra.shape, lambda i, j: (0, 0)))
        out_shape.append(jax.ShapeDtypeStruct((m, n2), F32))
        out_specs.append(pl.BlockSpec((tm, n2), lambda i, j: (i, 0)))
        args.append(w_extra)
    if emit_weights:
        out_shape.append(jax.ShapeDtypeStruct((n, k) if w_transposed else (k, n), BF16))
        out_specs.append(pl.BlockSpec(w_block, w_index))
    res = pl.pallas_call(
        functools.partial(_norm_matmul_kernel, has_extra=w_extra is not None,
                          extra_gain=g_extra is not None, emit_weights=emit_weights,
                          w_transposed=w_transposed),
        grid=grid,
        in_specs=in_specs,
        out_specs=out_specs,
        out_shape=out_shape,
        scratch_shapes=[pltpu.VMEM((tm, k), BF16)],
        compiler_params=_params(("parallel", "arbitrary")),
        name=f"norm_matmul_{m}x{n}",
    )(*args)
    return res if len(res) > 1 else res[0]


def _matmul_residual_kernel(a_ref, w_ref, r_ref, *rest, final_norm):
    if final_norm:
        g_ref, o_ref = rest
    else:
        (o_ref,) = rest
    h = r_ref[...] + jnp.dot(a_ref[...].astype(BF16), w_ref[...],
                             preferred_element_type=F32)
    if final_norm:
        h = h * _rms_scale(h) * g_ref[...]
    o_ref[...] = h


def _matmul_residual(a, w, res, *, tm, g_final=None):
    m, k = a.shape
    n = w.shape[1]
    in_specs = [
        pl.BlockSpec((tm, k), lambda i: (i, 0)),
        pl.BlockSpec((k, n), lambda i: (0, 0)),
        pl.BlockSpec((tm, n), lambda i: (i, 0)),
    ]
    args = [a, w, res]
    if g_final is not None:
        in_specs.append(pl.BlockSpec((1, n), lambda i: (0, 0)))
        args.append(g_final.reshape(1, n))
    return pl.pallas_call(
        functools.partial(_matmul_residual_kernel, final_norm=g_final is not None),
        grid=(m // tm,),
        in_specs=in_specs,
        out_specs=pl.BlockSpec((tm, n), lambda i: (i, 0)),
        out_shape=jax.ShapeDtypeStruct((m, n), F32),
        compiler_params=_params(("parallel",)),
        name=f"matmul_residual_{m}" + ("_final" if g_final is not None else ""),
    )(*args)


def _split_bf16(x):
    hi = x.astype(BF16)
    return hi, (x - hi.astype(F32)).astype(BF16)


def _gla_gates_kernel(glow_ref, wg_ref, bg_ref, bcum_ref, *, chunk):
    rows = glow_ref.shape[0]
    g_hi, g_lo = _split_bf16(glow_ref[...])
    w_hi, w_lo = _split_bf16(wg_ref[...])
    x = jnp.dot(jnp.concatenate([g_hi, g_lo, g_hi], axis=1),
                jnp.concatenate([w_hi, w_hi, w_lo], axis=0),
                preferred_element_type=F32) + bg_ref[...]
    softplus2 = jnp.log2(1.0 + jnp.exp2(jnp.abs(x) * (-LOG2E)))
    logg = jnp.minimum(x, 0.0) * (1.0 / GLA_GATE_TEMP) - softplus2 * (LN2 / GLA_GATE_TEMP)

    span = GATE_CUMSUM_ROWS
    row = lax.broadcasted_iota(jnp.int32, (span, span), 0)
    col = lax.broadcasted_iota(jnp.int32, (span, span), 1)
    same_chunk = (row // chunk) == (col // chunk) if chunk < span else True
    tril = jnp.where((col <= row) & same_chunk, 1.0, 0.0).astype(BF16)
    tril2 = jnp.concatenate([tril, tril], axis=1)
    for i in range(rows // span):
        hi, lo = _split_bf16(logg[i * span:(i + 1) * span])
        bcum_ref[i * span:(i + 1) * span, :] = jnp.dot(
            tril2, jnp.concatenate([hi, lo], axis=0), preferred_element_type=F32)


def _gla_gates(glow, wg, bg, *, chunk, tm):
    m = glow.shape[0]
    assert m % tm == 0 and tm % GATE_CUMSUM_ROWS == 0
    assert GATE_CUMSUM_ROWS % chunk == 0 or chunk % GATE_CUMSUM_ROWS == 0
    assert chunk <= GATE_CUMSUM_ROWS, "cumulative sums do not cross row spans"
    return pl.pallas_call(
        functools.partial(_gla_gates_kernel, chunk=chunk),
        grid=(m // tm,),
        in_specs=[
            pl.BlockSpec((tm, GLA_GATE_RANK), lambda i: (i, 0)),
            pl.BlockSpec((GLA_GATE_RANK, GLA_KEY_DIM), lambda i: (0, 0)),
            pl.BlockSpec((1, GLA_KEY_DIM), lambda i: (0, 0)),
        ],
        out_specs=pl.BlockSpec((tm, GLA_KEY_DIM), lambda i: (i, 0)),
        out_shape=jax.ShapeDtypeStruct((m, GLA_KEY_DIM), F32),
        compiler_params=_params(("parallel",)),
        name=f"gla_gates_{m}",
    )(glow, wg, bg.reshape(1, -1))


def _gla_kernel(q_ref, k_ref, v_ref, r_ref, bcum_ref, gon_ref, *rest, has_init, n_chunks):
    rest = list(rest)
    s0_ref = rest.pop(0) if has_init else None
    o_ref, sfin_ref = rest[:2]
    s_ref = rest[2] if n_chunks > 1 else None
    c = pl.program_id(1)
    chunk = q_ref.shape[0]

    if n_chunks > 1:
        @pl.when(c == 0)
        def _():
            if has_init:
                s_ref[...] = s0_ref[...]
            else:
                s_ref[...] = jnp.zeros_like(s_ref)

    row = lax.broadcasted_iota(jnp.int32, (chunk, chunk), 0)
    col = lax.broadcasted_iota(jnp.int32, (chunk, chunk), 1)
    causal = col <= row
    heads = range(GLA_HEADS)
    ks = [slice(h * GLA_DK, (h + 1) * GLA_DK) for h in heads]
    vs = [slice(h * GLA_DV, (h + 1) * GLA_DV) for h in heads]

    def state(h):
        if n_chunks > 1:
            return s_ref[h]
        return s0_ref[h] if has_init else jnp.zeros((GLA_DK, GLA_DV), F32)

    q_inter, k_state, scores, decay = [], [], [], []
    for h in heads:
        b = bcum_ref[:, ks[h]]
        b_last = b[chunk - 1:chunk, :]
        b_mid = b[chunk // 2 - 1:chunk // 2, :]
        q = q_ref[:, ks[h]].astype(F32) * (GLA_DK ** -0.5)
        k = k_ref[:, ks[h]].astype(F32)
        q_inter.append((q * jnp.exp(b)).astype(BF16))
        q_intra = (q * jnp.exp(b - b_mid)).astype(BF16)
        k_intra = (k * jnp.exp(b_mid - b)).astype(BF16)
        k_state.append((k * jnp.exp(b_last - b)).astype(BF16))
        scores.append(lax.dot_general(q_intra, k_intra, _NT_DIMS,
                                      preferred_element_type=F32))
        decay.append(jnp.exp(jnp.broadcast_to(b_last, (128, GLA_DK))).T)

    o = []
    for h in heads:
        sc = jnp.where(causal, scores[h], 0.0).astype(BF16)
        v = v_ref[:, vs[h]].astype(BF16)
        o.append(jnp.dot(q_inter[h], state(h).astype(BF16), preferred_element_type=F32)
                 + jnp.dot(sc, v, preferred_element_type=F32))
        upd = lax.dot_general(k_state[h], v, _TN_DIMS, preferred_element_type=F32)
        s_new = state(h) * jnp.concatenate([decay[h]] * (GLA_DV // 128), axis=1) + upd
        if n_chunks > 1:
            s_ref[h] = s_new
        else:
            sfin_ref[h] = s_new

    scale = [_rms_scale(o[h]) for h in heads]
    for h in heads:
        r = r_ref[:, vs[h]].astype(F32)
        o_ref[:, vs[h]] = (o[h] * scale[h] * gon_ref[:, vs[h]] * _silu(r)).astype(o_ref.dtype)

    if n_chunks > 1:
        @pl.when(c == n_chunks - 1)
        def _():
            sfin_ref[...] = s_ref[...]


def _gla(proj, bcum, gon, *, batch, seq, chunk, out_dtype, s0=None):
    n = seq // chunk
    row = lambda b, c: b * n + c
    kb, vb, rb = 1, 2 * GLA_KEY_DIM // GLA_VALUE_DIM, 2 * GLA_KEY_DIM // GLA_VALUE_DIM + 1
    in_specs = [
        pl.BlockSpec((chunk, GLA_KEY_DIM), lambda b, c: (row(b, c), 0)),
        pl.BlockSpec((chunk, GLA_KEY_DIM), lambda b, c: (row(b, c), kb)),
        pl.BlockSpec((chunk, GLA_VALUE_DIM), lambda b, c: (row(b, c), vb)),
        pl.BlockSpec((chunk, GLA_VALUE_DIM), lambda b, c: (row(b, c), rb)),
        pl.BlockSpec((chunk, GLA_KEY_DIM), lambda b, c: (row(b, c), 0)),
        pl.BlockSpec((1, GLA_VALUE_DIM), lambda b, c: (0, 0)),
    ]
    args = [proj, proj, proj, proj, bcum, gon.reshape(1, -1)]
    state_spec = pl.BlockSpec((None, GLA_HEADS, GLA_DK, GLA_DV), lambda b, c: (b, 0, 0, 0))
    if s0 is not None:
        in_specs.append(state_spec)
        args.append(s0)
    scratch = [pltpu.VMEM((GLA_HEADS, GLA_DK, GLA_DV), F32)] if n > 1 else []
    return pl.pallas_call(
        functools.partial(_gla_kernel, has_init=s0 is not None, n_chunks=n),
        grid=(batch, n),
        in_specs=in_specs,
        out_specs=[
            pl.BlockSpec((chunk, GLA_VALUE_DIM), lambda b, c: (row(b, c), 0)),
            state_spec,
        ],
        out_shape=[
            jax.ShapeDtypeStruct((batch * seq, GLA_VALUE_DIM), out_dtype),
            jax.ShapeDtypeStruct((batch, GLA_HEADS, GLA_DK, GLA_DV), F32),
        ],
        scratch_shapes=scratch,
        compiler_params=_params(("parallel", "arbitrary")),
        name=f"gla_chunk{chunk}",
    )(*args)


def _alibi_slope(head):
    return 2.0 ** (-8.0 * (head + 1) / SWA_HEADS)


def _attend(q, z, kcat, vcat, sinks_ref, allowed, dist, out_ref, tq):
    gw = SWA_GROUP * SWA_HEAD_DIM
    for g in range(SWA_KV_HEADS):
        q_g = q[:, g * gw:(g + 1) * gw]
        qs = jnp.concatenate(
            [q_g[:, h * SWA_HEAD_DIM:(h + 1) * SWA_HEAD_DIM] for h in range(SWA_GROUP)],
            axis=0).astype(BF16)
        k_g = kcat[:, g * SWA_HEAD_DIM:(g + 1) * SWA_HEAD_DIM]
        v_g = vcat[:, g * SWA_HEAD_DIM:(g + 1) * SWA_HEAD_DIM]
        s = lax.dot_general(qs, k_g, _NT_DIMS, preferred_element_type=F32)
        s = s * (SWA_HEAD_DIM ** -0.5)
        probs, inv = [], []
        for hl in range(SWA_GROUP):
            head = g * SWA_GROUP + hl
            sh = s[hl * tq:(hl + 1) * tq]
            sh = jnp.where(allowed, sh - _alibi_slope(head) * dist, -jnp.inf)
            sink = sinks_ref[head]
            m = jnp.maximum(jnp.max(sh, axis=-1, keepdims=True), sink)
            p = jnp.exp(sh - m)
            denom = jnp.sum(p, axis=-1, keepdims=True) + jnp.exp(sink - m)
            probs.append(p)
            inv.append(1.0 / denom)
        p_all = jnp.concatenate(probs, axis=0).astype(BF16)
        o = jnp.dot(p_all, v_g, preferred_element_type=F32)
        o = o * jnp.concatenate(inv, axis=0)
        o_g = jnp.concatenate([o[hl * tq:(hl + 1) * tq] for hl in range(SWA_GROUP)], axis=1)
        z_g = z[:, g * gw:(g + 1) * gw].astype(F32)
        out_ref[:, g * gw:(g + 1) * gw] = (o_g * _silu(z_g)).astype(out_ref.dtype)


def _attn_prompt_kernel(sinks_ref, q_ref, z_ref, kp_ref, ko_ref, vp_ref, vo_ref, out_ref,
                        bias_ref):
    hd, nkeys = SWA_HEAD_DIM, 2 * WINDOW
    pair_w = 2 * hd
    pairs_per_group = SWA_GROUP // 2
    blk = pl.program_id(1)

    @pl.when((pl.program_id(0) == 0) & (blk == 0))
    def _():
        kj = lax.broadcasted_iota(jnp.int32, (nkeys, WINDOW), 0)
        qi = lax.broadcasted_iota(jnp.int32, (nkeys, WINDOW), 1)
        dist = WINDOW + qi - kj
        ok = (dist >= 0) & (dist <= WINDOW)
        ok_first = ok & (kj >= WINDOW)
        distf = dist.astype(F32)
        for h in range(SWA_HEADS):
            pen = (-_alibi_slope(h) * LOG2E) * distf
            sl = slice((h % 2) * WINDOW, (h % 2 + 1) * WINDOW)
            bias_ref[0, h // 2, :, sl] = jnp.where(ok_first, pen, -jnp.inf)
            bias_ref[1, h // 2, :, sl] = jnp.where(ok, pen, -jnp.inf)

    tbl = jnp.minimum(blk, 1)
    k = jnp.concatenate([kp_ref[...], ko_ref[...]], axis=0)
    v = jnp.concatenate([vp_ref[...], vo_ref[...]], axis=0)
    vt = v.T
    ones = jnp.ones((16, nkeys), F32)
    lane = lax.broadcasted_iota(jnp.int32, (nkeys, pair_w), 1)
    qlane = lax.broadcasted_iota(jnp.int32, (WINDOW, pair_w), 1)
    qk_scale = (hd ** -0.5) * LOG2E

    k2, vt1 = [], []
    for g in range(SWA_KV_HEADS):
        kblk = k[:, (g // 2) * pair_w:(g // 2 + 1) * pair_w]
        k_here = jnp.where((lane < hd) if g % 2 == 0 else (lane >= hd), kblk, 0.0)
        k2.append((k_here + pltpu.roll(k_here, hd, axis=1)).astype(BF16))
        vt1.append(jnp.concatenate([vt[g * hd:(g + 1) * hd], ones], axis=0).astype(BF16))

    quad_pairs = 2
    n_quads = SWA_HEADS // (2 * quad_pairs)
    quads_per_group = pairs_per_group // quad_pairs

    def scores(quad):
        rows = []
        for j in range(quad_pairs):
            col = (quad * quad_pairs + j) * pair_w
            q_pair = q_ref[:, col:col + pair_w]
            zero = jnp.zeros_like(q_pair)
            rows += [jnp.where(qlane < hd, q_pair, zero), jnp.where(qlane >= hd, q_pair, zero)]
        return lax.dot_general(k2[quad // quads_per_group], jnp.concatenate(rows, axis=0),
                               _NT_DIMS, preferred_element_type=F32)

    quarter = lax.broadcasted_iota(jnp.int32, (1, 2 * quad_pairs * WINDOW), 1) // WINDOW
    st_next = scores(0)
    for quad in range(n_quads):
        st = st_next
        if quad + 1 < n_quads:
            st_next = scores(quad + 1)
        pair0 = quad * quad_pairs
        bias = jnp.concatenate([bias_ref[tbl, pair0 + j] for j in range(quad_pairs)], axis=1)
        s2 = st * qk_scale + bias
        sink2 = sinks_ref[2 * pair0]
        for t in range(1, 2 * quad_pairs):
            sink2 = jnp.where(quarter == t, sinks_ref[2 * pair0 + t], sink2)
        sink2 = sink2 * LOG2E
        m = jnp.maximum(jnp.max(s2, axis=0, keepdims=True), sink2)
        p = jnp.exp2(s2 - m).astype(BF16)
        oa = jnp.dot(vt1[quad // quads_per_group], p, preferred_element_type=F32)
        denom = oa[hd:hd + 1] + jnp.exp2(sink2 - m)
        on = oa[0:hd] * (1.0 / denom)
        for j in range(quad_pairs):
            lo = 2 * j * WINDOW
            o_pair = jnp.concatenate([on[:, lo:lo + WINDOW],
                                      on[:, lo + WINDOW:lo + 2 * WINDOW]], axis=0).T
            col = (pair0 + j) * pair_w
            z_pair = z_ref[:, col:col + pair_w].astype(F32)
            out_ref[:, col:col + pair_w] = (o_pair * _silu(z_pair)).astype(out_ref.dtype)


def _attn_prompt(qz, kv, sinks, *, batch, seq):
    nb = seq // WINDOW
    row = lambda b, i: b * nb + i
    prev = lambda b, i: b * nb + jnp.maximum(i - 1, 0)
    return pl.pallas_call(
        _attn_prompt_kernel,
        grid=(batch, nb),
        in_specs=[
            pl.BlockSpec(memory_space=pltpu.SMEM),
            pl.BlockSpec((WINDOW, SWA_WIDTH), lambda b, i: (row(b, i), 0)),
            pl.BlockSpec((WINDOW, SWA_WIDTH), lambda b, i: (row(b, i), 1)),
            pl.BlockSpec((WINDOW, SWA_KV_WIDTH), lambda b, i: (prev(b, i), 0)),
            pl.BlockSpec((WINDOW, SWA_KV_WIDTH), lambda b, i: (row(b, i), 0)),
            pl.BlockSpec((WINDOW, SWA_KV_WIDTH), lambda b, i: (prev(b, i), 1)),
            pl.BlockSpec((WINDOW, SWA_KV_WIDTH), lambda b, i: (row(b, i), 1)),
        ],
        out_specs=pl.BlockSpec((WINDOW, SWA_WIDTH), lambda b, i: (row(b, i), 0)),
        out_shape=jax.ShapeDtypeStruct((batch * seq, SWA_WIDTH), BF16),
        scratch_shapes=[pltpu.VMEM((2, SWA_HEADS // 2, 2 * WINDOW, 2 * WINDOW), F32)],
        compiler_params=_params(("arbitrary", "arbitrary")),
        name="attn_prompt",
    )(sinks, qz, qz, kv, kv, kv, kv)


def _attn_sample_kernel(sinks_ref, q_ref, z_ref, kn_ref, vn_ref, kc_ref, vc_ref,
                        out_ref, kwin_ref, vwin_ref, *, tq):
    kc, vc, kn, vn = kc_ref[...], vc_ref[...], kn_ref[...], vn_ref[...]
    pad = jnp.zeros((WINDOW - tq, SWA_KV_WIDTH), F32)
    kcat = jnp.concatenate([kc, kn, pad], axis=0).astype(BF16)
    vcat = jnp.concatenate([vc, vn, pad], axis=0).astype(BF16)
    shape = (tq, 2 * WINDOW)
    qi = lax.broadcasted_iota(jnp.int32, shape, 0)
    kj = lax.broadcasted_iota(jnp.int32, shape, 1)
    dist = WINDOW + qi - kj
    allowed = (dist >= 0) & (dist <= WINDOW)
    _attend(q_ref[...], z_ref[...], kcat, vcat, sinks_ref, allowed, dist.astype(F32),
            out_ref, tq)
    kwin_ref[0:WINDOW - tq, :] = kc[tq:, :]
    kwin_ref[WINDOW - tq:, :] = kn
    vwin_ref[0:WINDOW - tq, :] = vc[tq:, :]
    vwin_ref[WINDOW - tq:, :] = vn


def _attn_sample(qz, kv, cache_k, cache_v, sinks, *, batch, seq):
    win_spec = pl.BlockSpec((None, WINDOW, SWA_KV_WIDTH), lambda b: (b, 0, 0))
    win_shape = jax.ShapeDtypeStruct((batch, WINDOW, SWA_KV_WIDTH), F32)
    return pl.pallas_call(
        functools.partial(_attn_sample_kernel, tq=seq),
        grid=(batch,),
        in_specs=[
            pl.BlockSpec(memory_space=pltpu.SMEM),
            pl.BlockSpec((seq, SWA_WIDTH), lambda b: (b, 0)),
            pl.BlockSpec((seq, SWA_WIDTH), lambda b: (b, 1)),
            pl.BlockSpec((seq, SWA_KV_WIDTH), lambda b: (b, 0)),
            pl.BlockSpec((seq, SWA_KV_WIDTH), lambda b: (b, 1)),
            win_spec,
            win_spec,
        ],
        out_specs=[pl.BlockSpec((seq, SWA_WIDTH), lambda b: (b, 0)), win_spec, win_spec],
        out_shape=[jax.ShapeDtypeStruct((batch * seq, SWA_WIDTH), F32), win_shape, win_shape],
        compiler_params=_params(("parallel",)),
        name="attn_sample",
    )(sinks, qz, qz, kv, kv, cache_k, cache_v)


def _trunk(x, weights, *, batch, seq, row_tile, res_tile, chunk, act_dtype,
           gla_init=None, cache=None, emit_weights=False):
    (g_norm_a, w_in_a, w_glow, w_gate_up, b_gate, g_onorm_a, w_out_a,
     g_norm_kv, w_kv, g_norm_b, w_in_b, sinks, w_out_b, g_final) = weights
    h = x.reshape(batch * seq, D_MODEL)

    proj, glow, *w_in_a_bf16 = _norm_matmul(
        h, g_norm_a, w_in_a, tm=row_tile, tn=1024, n=GLA_MAIN_COLS, out_dtype=act_dtype,
        w_extra=w_glow, emit_weights=emit_weights, w_transposed=True)
    bcum = _gla_gates(glow, w_gate_up, b_gate, chunk=chunk, tm=min(row_tile, GATE_ROW_TILE))
    o, s_fin = _gla(proj, bcum, g_onorm_a, batch=batch, seq=seq,
                    chunk=chunk, out_dtype=act_dtype, s0=gla_init)
    h = _matmul_residual(o, w_out_a, h, tm=res_tile)

    qz, kv, *w_in_b_bf16 = _norm_matmul(
        h, g_norm_b, w_in_b, tm=row_tile, tn=1024, out_dtype=act_dtype,
        w_extra=w_kv, g_extra=g_norm_kv, emit_weights=emit_weights)
    if cache is None:
        att = _attn_prompt(qz, kv, sinks, batch=batch, seq=seq)
        kv_win = kv.reshape(batch, seq, 2 * SWA_KV_WIDTH)[:, seq - WINDOW:]
        kv_win = kv_win.reshape(batch, WINDOW, 2, SWA_KV_HEADS, SWA_HEAD_DIM)
        k_win, v_win = kv_win[:, :, 0], kv_win[:, :, 1]
    else:
        cache_k, cache_v = cache
        att, k_win, v_win = _attn_sample(
            qz, kv, cache_k.reshape(batch, WINDOW, SWA_KV_WIDTH),
            cache_v.reshape(batch, WINDOW, SWA_KV_WIDTH), sinks, batch=batch, seq=seq)
        k_win = k_win.reshape(batch, WINDOW, SWA_KV_HEADS, SWA_HEAD_DIM)
        v_win = v_win.reshape(batch, WINDOW, SWA_KV_HEADS, SWA_HEAD_DIM)
    y = _matmul_residual(att, w_out_b, h, tm=res_tile, g_final=g_final)
    outs = (y.reshape(batch, seq, D_MODEL), s_fin[None], k_win, v_win)
    return outs, (w_in_a_bf16 + w_in_b_bf16)


def kernel(x_prompt, x_sample, state_gla, cache_k_win, cache_v_win, g_norm_a, w_in_a,
           w_gate_up, b_gate, g_onorm_a, w_out_a, g_norm_kv, w_kv, g_norm_b, w_in_b, sinks,
           w_out_b, g_final):
    assert w_in_a.shape[0] == 1 and w_in_b.shape[0] == 1, "one GLA layer, one SWA layer"
    assert cache_k_win.shape[1] == WINDOW
    pb, ps, _ = x_prompt.shape
    sb, ss, _ = x_sample.shape

    def weights(w_in_a_any, w_in_b_any):
        return (
            g_norm_a[0], w_in_a_any, w_in_a_t[0, GLA_MAIN_COLS:, :],
            w_gate_up[0], b_gate[0], g_onorm_a[0], w_out_a[0].astype(BF16),
            g_norm_kv, w_kv.astype(BF16),
            g_norm_b[0], w_in_b_any, sinks[0], w_out_b[0].astype(BF16), g_final,
        )

    w_in_a_t = jnp.swapaxes(w_in_a, 1, 2)

    (y_s, gla_s, k_s, v_s), (w_in_a_bf16, w_in_b_bf16) = _trunk(
        x_sample, weights(w_in_a_t, w_in_b), batch=sb, seq=ss, row_tile=sb * ss,
        res_tile=sb * ss, chunk=ss, act_dtype=F32,
        gla_init=state_gla.reshape(state_gla.shape[1:]),
        cache=(cache_k_win, cache_v_win), emit_weights=True)
    (y_p, gla_p, k_p, v_p), _ = _trunk(
        x_prompt, weights(w_in_a_bf16, w_in_b_bf16), batch=pb, seq=ps,
        row_tile=PROMPT_ROW_TILE, res_tile=RESIDUAL_ROW_TILE, chunk=GLA_PROMPT_CHUNK,
        act_dtype=BF16)
    return (y_p, y_s, gla_p, gla_s, k_p, v_p, k_s, v_s)
```

```python
import functools

import jax
import jax.numpy as jnp
from jax import lax
from jax.experimental import pallas as pl
from jax.experimental.pallas import tpu as pltpu

F32 = jnp.float32
BF16 = jnp.bfloat16

D_MODEL = 2048
GLA_HEADS = 4
GLA_KEY_DIM = D_MODEL // 2
GLA_VALUE_DIM = D_MODEL
GLA_DK = GLA_KEY_DIM // GLA_HEADS
GLA_DV = GLA_VALUE_DIM // GLA_HEADS
GLA_GATE_RANK = 16
GLA_GATE_TEMP = 16.0
GLA_MAIN_COLS = 2 * GLA_KEY_DIM + 2 * GLA_VALUE_DIM
SWA_HEAD_DIM = 64
SWA_HEADS = D_MODEL // SWA_HEAD_DIM
SWA_KV_HEADS = 4
SWA_GROUP = SWA_HEADS // SWA_KV_HEADS
SWA_WIDTH = SWA_HEADS * SWA_HEAD_DIM
SWA_KV_WIDTH = SWA_KV_HEADS * SWA_HEAD_DIM
WINDOW = 128
RMS_EPS = 1e-6
LOG2E = 1.4426950408889634
LN2 = 0.6931471805599453

V7X_VMEM_LIMIT_BYTES = 56 * 1024 * 1024
GLA_PROMPT_CHUNK = 128
SAMPLE_SEQS_PER_STEP = 4
NORM_ROW_CHUNK = 256
GATE_CUMSUM_ROWS = 128
GATE_ROW_TILE = 512
PROMPT_ROW_TILE = 1024
RESIDUAL_ROW_TILE = 512

_NT_DIMS = (((1,), (1,)), ((), ()))
_TN_DIMS = (((0,), (0,)), ((), ()))


def _params(semantics):
    return pltpu.CompilerParams(dimension_semantics=semantics,
                                vmem_limit_bytes=V7X_VMEM_LIMIT_BYTES)


def _silu(x):
    return x / (1.0 + jnp.exp(-x))


def _rms_scale(x):
    return lax.rsqrt(jnp.mean(x * x, axis=-1, keepdims=True) + RMS_EPS)


def _norm_matmul_kernel(x_ref, g_ref, w_ref, *rest, has_extra, extra_gain, emit_weights,
                        w_transposed):
    dims = _NT_DIMS if w_transposed else (((1,), (0,)), ((), ()))

    def mm(a, b):
        return lax.dot_general(a, b, dims, preferred_element_type=F32)

    rest = list(rest)
    g2_ref = rest.pop(0) if extra_gain else None
    w2_ref = rest.pop(0) if has_extra else None
    o_ref = rest.pop(0)
    o2_ref = rest.pop(0) if has_extra else None
    wb_ref = rest.pop(0) if emit_weights else None
    (xn_ref,) = rest

    first = pl.program_id(1) == 0

    def weights():
        w = w_ref[...].astype(BF16)
        if emit_weights:
            wb_ref[...] = w
        return w

    @pl.when(first)
    def _():
        tm = x_ref.shape[0]
        rc = min(tm, NORM_ROW_CHUNK)
        w = weights()
        for c in range(tm // rc):
            rows = slice(c * rc, (c + 1) * rc)
            x = x_ref[rows, :]
            xs = x * _rms_scale(x)
            xn = (xs * g_ref[...]).astype(BF16)
            xn_ref[rows, :] = xn
            o_ref[rows, :] = mm(xn, w).astype(o_ref.dtype)
            if has_extra:
                xn2 = (xs * g2_ref[...]).astype(BF16) if extra_gain else xn
                o2_ref[rows, :] = mm(xn2, w2_ref[...].astype(BF16)).astype(o2_ref.dtype)

    @pl.when(jnp.logical_not(first))
    def _():
        o_ref[...] = mm(xn_ref[...], weights()).astype(o_ref.dtype)


def _norm_matmul(x, g, w, *, tm, tn, out_dtype, n=None, w_extra=None, g_extra=None,
                 emit_weights=False, w_transposed=False):
    m, k = x.shape
    n_axis = -2 if w_transposed else -1
    n = w.shape[n_axis] if n is None else n
    assert n % tn == 0 and m % tm == 0
    assert not emit_weights or m == tm, "each weight block must be visited exactly once"
    grid = (m // tm, n // tn)
    w_block = (tn, k) if w_transposed else (k, tn)
    w_index = (lambda i, j: (j, 0)) if w_transposed else (lambda i, j: (0, j))
    if w.ndim == 3:
        w_spec = pl.BlockSpec((None,) + w_block, lambda i, j: (0,) + w_index(i, j))
    else:
        w_spec = pl.BlockSpec(w_block, w_index)
    in_specs = [
        pl.BlockSpec((tm, k), lambda i, j: (i, 0)),
        pl.BlockSpec((1, k), lambda i, j: (0, 0)),
        w_spec,
    ]
    out_shape = [jax.ShapeDtypeStruct((m, n), out_dtype)]
    out_specs = [pl.BlockSpec((tm, tn), lambda i, j: (i, j))]
    args = [x, g.reshape(1, k), w]
    if g_extra is not None:
        in_specs.append(pl.BlockSpec((1, k), lambda i, j: (0, 0)))
        args.append(g_extra.reshape(1, k))
    if w_extra is not None:
        n2 = w_extra.shape[n_axis]
        in_specs.append(pl.BlockSpec(w_extra.shape, lambda i, j: (0, 0)))
        out_shape.append(jax.ShapeDtypeStruct((m, n2), F32))
        out_specs.append(pl.BlockSpec((tm, n2), lambda i, j: (i, 0)))
        args.append(w_extra)
    if emit_weights:
        out_shape.append(jax.ShapeDtypeStruct((n, k) if w_transposed else (k, n), BF16))
        out_specs.append(pl.BlockSpec(w_block, w_index))
    res = pl.pallas_call(
        functools.partial(_norm_matmul_kernel, has_extra=w_extra is not None,
                          extra_gain=g_extra is not None, emit_weights=emit_weights,
                          w_transposed=w_transposed),
        grid=grid,
        in_specs=in_specs,
        out_specs=out_specs,
        out_shape=out_shape,
        scratch_shapes=[pltpu.VMEM((tm, k), BF16)],
        compiler_params=_params(("parallel", "arbitrary")),
        name=f"norm_matmul_{m}x{n}",
    )(*args)
    return res if len(res) > 1 else res[0]


def _matmul_residual_kernel(a_ref, w_ref, r_ref, *rest, final_norm):
    if final_norm:
        g_ref, o_ref = rest
    else:
        (o_ref,) = rest
    h = r_ref[...] + jnp.dot(a_ref[...].astype(BF16), w_ref[...],
                             preferred_element_type=F32)
    if final_norm:
        h = h * _rms_scale(h) * g_ref[...]
    o_ref[...] = h


def _matmul_residual(a, w, res, *, tm, g_final=None):
    m, k = a.shape
    n = w.shape[1]
    in_specs = [
        pl.BlockSpec((tm, k), lambda i: (i, 0)),
        pl.BlockSpec((k, n), lambda i: (0, 0)),
        pl.BlockSpec((tm, n), lambda i: (i, 0)),
    ]
    args = [a, w, res]
    if g_final is not None:
        in_specs.append(pl.BlockSpec((1, n), lambda i: (0, 0)))
        args.append(g_final.reshape(1, n))
    return pl.pallas_call(
        functools.partial(_matmul_residual_kernel, final_norm=g_final is not None),
        grid=(m // tm,),
        in_specs=in_specs,
        out_specs=pl.BlockSpec((tm, n), lambda i: (i, 0)),
        out_shape=jax.ShapeDtypeStruct((m, n), F32),
        compiler_params=_params(("parallel",)),
        name=f"matmul_residual_{m}" + ("_final" if g_final is not None else ""),
    )(*args)


def _split_bf16(x):
    hi = x.astype(BF16)
    return hi, (x - hi.astype(F32)).astype(BF16)


def _gla_gates_kernel(glow_ref, wg_ref, bg_ref, bcum_ref, *, chunk):
    rows = glow_ref.shape[0]
    g_hi, g_lo = _split_bf16(glow_ref[...])
    w_hi, w_lo = _split_bf16(wg_ref[...])
    x = jnp.dot(jnp.concatenate([g_hi, g_lo, g_hi], axis=1),
                jnp.concatenate([w_hi, w_hi, w_lo], axis=0),
                preferred_element_type=F32) + bg_ref[...]
    softplus2 = jnp.log2(1.0 + jnp.exp2(jnp.abs(x) * (-LOG2E)))
    logg = jnp.minimum(x, 0.0) * (1.0 / GLA_GATE_TEMP) - softplus2 * (LN2 / GLA_GATE_TEMP)

    span = GATE_CUMSUM_ROWS
    row = lax.broadcasted_iota(jnp.int32, (span, span), 0)
    col = lax.broadcasted_iota(jnp.int32, (span, span), 1)
    same_chunk = (row // chunk) == (col // chunk) if chunk < span else True
    tril = jnp.where((col <= row) & same_chunk, 1.0, 0.0).astype(BF16)
    tril2 = jnp.concatenate([tril, tril], axis=1)
    for i in range(rows // span):
        hi, lo = _split_bf16(logg[i * span:(i + 1) * span])
        bcum_ref[i * span:(i + 1) * span, :] = jnp.dot(
            tril2, jnp.concatenate([hi, lo], axis=0), preferred_element_type=F32)


def _gla_gates(glow, wg, bg, *, chunk, tm):
    m = glow.shape[0]
    assert m % tm == 0 and tm % GATE_CUMSUM_ROWS == 0
    assert GATE_CUMSUM_ROWS % chunk == 0 or chunk % GATE_CUMSUM_ROWS == 0
    assert chunk <= GATE_CUMSUM_ROWS, "cumulative sums do not cross row spans"
    return pl.pallas_call(
        functools.partial(_gla_gates_kernel, chunk=chunk),
        grid=(m // tm,),
        in_specs=[
            pl.BlockSpec((tm, GLA_GATE_RANK), lambda i: (i, 0)),
            pl.BlockSpec((GLA_GATE_RANK, GLA_KEY_DIM), lambda i: (0, 0)),
            pl.BlockSpec((1, GLA_KEY_DIM), lambda i: (0, 0)),
        ],
        out_specs=pl.BlockSpec((tm, GLA_KEY_DIM), lambda i: (i, 0)),
        out_shape=jax.ShapeDtypeStruct((m, GLA_KEY_DIM), F32),
        compiler_params=_params(("parallel",)),
        name=f"gla_gates_{m}",
    )(glow, wg, bg.reshape(1, -1))


def _gla_kernel(q_ref, k_ref, v_ref, r_ref, bcum_ref, gon_ref, *rest, has_init, n_chunks):
    rest = list(rest)
    s0_ref = rest.pop(0) if has_init else None
    o_ref, sfin_ref = rest[:2]
    s_ref = rest[2] if n_chunks > 1 else None
    c = pl.program_id(1)
    chunk = q_ref.shape[0]

    if n_chunks > 1:
        @pl.when(c == 0)
        def _():
            if has_init:
                s_ref[...] = s0_ref[...]
            else:
                s_ref[...] = jnp.zeros_like(s_ref)

    row = lax.broadcasted_iota(jnp.int32, (chunk, chunk), 0)
    col = lax.broadcasted_iota(jnp.int32, (chunk, chunk), 1)
    causal = col <= row
    heads = range(GLA_HEADS)
    ks = [slice(h * GLA_DK, (h + 1) * GLA_DK) for h in heads]
    vs = [slice(h * GLA_DV, (h + 1) * GLA_DV) for h in heads]

    def state(h):
        if n_chunks > 1:
            return s_ref[h]
        return s0_ref[h] if has_init else jnp.zeros((GLA_DK, GLA_DV), F32)

    q_inter, k_state, scores, decay = [], [], [], []
    for h in heads:
        b = bcum_ref[:, ks[h]]
        b_last = b[chunk - 1:chunk, :]
        b_mid = b[chunk // 2 - 1:chunk // 2, :]
        q = q_ref[:, ks[h]].astype(F32) * (GLA_DK ** -0.5)
        k = k_ref[:, ks[h]].astype(F32)
        q_inter.append((q * jnp.exp(b)).astype(BF16))
        q_intra = (q * jnp.exp(b - b_mid)).astype(BF16)
        k_intra = (k * jnp.exp(b_mid - b)).astype(BF16)
        k_state.append((k * jnp.exp(b_last - b)).astype(BF16))
        scores.append(lax.dot_general(q_intra, k_intra, _NT_DIMS,
                                      preferred_element_type=F32))
        decay.append(jnp.exp(jnp.broadcast_to(b_last, (128, GLA_DK))).T)

    o = []
    for h in heads:
        sc = jnp.where(causal, scores[h], 0.0).astype(BF16)
        v = v_ref[:, vs[h]].astype(BF16)
        o.append(jnp.dot(q_inter[h], state(h).astype(BF16), preferred_element_type=F32)
                 + jnp.dot(sc, v, preferred_element_type=F32))
        upd = lax.dot_general(k_state[h], v, _TN_DIMS, preferred_element_type=F32)
        s_new = state(h) * jnp.concatenate([decay[h]] * (GLA_DV // 128), axis=1) + upd
        if n_chunks > 1:
            s_ref[h] = s_new
        else:
            sfin_ref[h] = s_new

    scale = [_rms_scale(o[h]) for h in heads]
    for h in heads:
        r = r_ref[:, vs[h]].astype(F32)
        o_ref[:, vs[h]] = (o[h] * scale[h] * gon_ref[:, vs[h]] * _silu(r)).astype(o_ref.dtype)

    if n_chunks > 1:
        @pl.when(c == n_chunks - 1)
        def _():
            sfin_ref[...] = s_ref[...]


def _gla(proj, bcum, gon, *, batch, seq, chunk, out_dtype, s0=None):
    n = seq // chunk
    row = lambda b, c: b * n + c
    kb, vb, rb = 1, 2 * GLA_KEY_DIM // GLA_VALUE_DIM, 2 * GLA_KEY_DIM // GLA_VALUE_DIM + 1
    in_specs = [
        pl.BlockSpec((chunk, GLA_KEY_DIM), lambda b, c: (row(b, c), 0)),
        pl.BlockSpec((chunk, GLA_KEY_DIM), lambda b, c: (row(b, c), kb)),
        pl.BlockSpec((chunk, GLA_VALUE_DIM), lambda b, c: (row(b, c), vb)),
        pl.BlockSpec((chunk, GLA_VALUE_DIM), lambda b, c: (row(b, c), rb)),
        pl.BlockSpec((chunk, GLA_KEY_DIM), lambda b, c: (row(b, c), 0)),
        pl.BlockSpec((1, GLA_VALUE_DIM), lambda b, c: (0, 0)),
    ]
    args = [proj, proj, proj, proj, bcum, gon.reshape(1, -1)]
    state_spec = pl.BlockSpec((None, GLA_HEADS, GLA_DK, GLA_DV), lambda b, c: (b, 0, 0, 0))
    if s0 is not None:
        in_specs.append(state_spec)
        args.append(s0)
    scratch = [pltpu.VMEM((GLA_HEADS, GLA_DK, GLA_DV), F32)] if n > 1 else []
    return pl.pallas_call(
        functools.partial(_gla_kernel, has_init=s0 is not None, n_chunks=n),
        grid=(batch, n),
        in_specs=in_specs,
        out_specs=[
            pl.BlockSpec((chunk, GLA_VALUE_DIM), lambda b, c: (row(b, c), 0)),
            state_spec,
        ],
        out_shape=[
            jax.ShapeDtypeStruct((batch * seq, GLA_VALUE_DIM), out_dtype),
            jax.ShapeDtypeStruct((batch, GLA_HEADS, GLA_DK, GLA_DV), F32),
        ],
        scratch_shapes=scratch,
        compiler_params=_params(("parallel", "arbitrary")),
        name=f"gla_chunk{chunk}",
    )(*args)


def _alibi_slope(head):
    return 2.0 ** (-8.0 * (head + 1) / SWA_HEADS)


def _attn_prompt_kernel(sinks_ref, q_ref, z_ref, kp_ref, ko_ref, vp_ref, vo_ref, out_ref,
                        bias_ref):
    hd, nkeys = SWA_HEAD_DIM, 2 * WINDOW
    pair_w = 2 * hd
    pairs_per_group = SWA_GROUP // 2
    blk = pl.program_id(1)

    @pl.when((pl.program_id(0) == 0) & (blk == 0))
    def _():
        kj = lax.broadcasted_iota(jnp.int32, (nkeys, WINDOW), 0)
        qi = lax.broadcasted_iota(jnp.int32, (nkeys, WINDOW), 1)
        dist = WINDOW + qi - kj
        ok = (dist >= 0) & (dist <= WINDOW)
        ok_first = ok & (kj >= WINDOW)
        distf = dist.astype(F32)
        for h in range(SWA_HEADS):
            pen = (-_alibi_slope(h) * LOG2E) * distf
            sl = slice((h % 2) * WINDOW, (h % 2 + 1) * WINDOW)
            bias_ref[0, h // 2, :, sl] = jnp.where(ok_first, pen, -jnp.inf)
            bias_ref[1, h // 2, :, sl] = jnp.where(ok, pen, -jnp.inf)

    tbl = jnp.minimum(blk, 1)
    k = jnp.concatenate([kp_ref[...], ko_ref[...]], axis=0)
    v = jnp.concatenate([vp_ref[...], vo_ref[...]], axis=0)
    vt = v.T
    ones = jnp.ones((16, nkeys), F32)
    lane = lax.broadcasted_iota(jnp.int32, (nkeys, pair_w), 1)
    qlane = lax.broadcasted_iota(jnp.int32, (WINDOW, pair_w), 1)
    qk_scale = (hd ** -0.5) * LOG2E

    k2, vt1 = [], []
    for g in range(SWA_KV_HEADS):
        kblk = k[:, (g // 2) * pair_w:(g // 2 + 1) * pair_w]
        k_here = jnp.where((lane < hd) if g % 2 == 0 else (lane >= hd), kblk, 0.0)
        k2.append((k_here + pltpu.roll(k_here, hd, axis=1)).astype(BF16))
        vt1.append(jnp.concatenate([vt[g * hd:(g + 1) * hd], ones], axis=0).astype(BF16))

    quad_pairs = 2
    n_quads = SWA_HEADS // (2 * quad_pairs)
    quads_per_group = pairs_per_group // quad_pairs

    def scores(quad):
        rows = []
        for j in range(quad_pairs):
            col = (quad * quad_pairs + j) * pair_w
            q_pair = q_ref[:, col:col + pair_w]
            zero = jnp.zeros_like(q_pair)
            rows += [jnp.where(qlane < hd, q_pair, zero), jnp.where(qlane >= hd, q_pair, zero)]
        return lax.dot_general(k2[quad // quads_per_group], jnp.concatenate(rows, axis=0),
                               _NT_DIMS, preferred_element_type=F32)

    quarter = lax.broadcasted_iota(jnp.int32, (1, 2 * quad_pairs * WINDOW), 1) // WINDOW
    st_next = scores(0)
    for quad in range(n_quads):
        st = st_next
        if quad + 1 < n_quads:
            st_next = scores(quad + 1)
        pair0 = quad * quad_pairs
        bias = jnp.concatenate([bias_ref[tbl, pair0 + j] for j in range(quad_pairs)], axis=1)
        s2 = st * qk_scale + bias
        sink2 = sinks_ref[2 * pair0]
        for t in range(1, 2 * quad_pairs):
            sink2 = jnp.where(quarter == t, sinks_ref[2 * pair0 + t], sink2)
        sink2 = sink2 * LOG2E
        m = jnp.maximum(jnp.max(s2, axis=0, keepdims=True), sink2)
        p = jnp.exp2(s2 - m).astype(BF16)
        oa = jnp.dot(vt1[quad // quads_per_group], p, preferred_element_type=F32)
        denom = oa[hd:hd + 1] + jnp.exp2(sink2 - m)
        on = oa[0:hd] * (1.0 / denom)
        for j in range(quad_pairs):
            lo = 2 * j * WINDOW
            o_pair = jnp.concatenate([on[:, lo:lo + WINDOW],
                                      on[:, lo + WINDOW:lo + 2 * WINDOW]], axis=0).T
            col = (pair0 + j) * pair_w
            z_pair = z_ref[:, col:col + pair_w].astype(F32)
            out_ref[:, col:col + pair_w] = (o_pair * _silu(z_pair)).astype(out_ref.dtype)


def _attn_prompt(qz, kv, sinks, *, batch, seq):
    nb = seq // WINDOW
    row = lambda b, i: b * nb + i
    prev = lambda b, i: b * nb + jnp.maximum(i - 1, 0)
    return pl.pallas_call(
        _attn_prompt_kernel,
        grid=(batch, nb),
        in_specs=[
            pl.BlockSpec(memory_space=pltpu.SMEM),
            pl.BlockSpec((WINDOW, SWA_WIDTH), lambda b, i: (row(b, i), 0)),
            pl.BlockSpec((WINDOW, SWA_WIDTH), lambda b, i: (row(b, i), 1)),
            pl.BlockSpec((WINDOW, SWA_KV_WIDTH), lambda b, i: (prev(b, i), 0)),
            pl.BlockSpec((WINDOW, SWA_KV_WIDTH), lambda b, i: (row(b, i), 0)),
            pl.BlockSpec((WINDOW, SWA_KV_WIDTH), lambda b, i: (prev(b, i), 1)),
            pl.BlockSpec((WINDOW, SWA_KV_WIDTH), lambda b, i: (row(b, i), 1)),
        ],
        out_specs=pl.BlockSpec((WINDOW, SWA_WIDTH), lambda b, i: (row(b, i), 0)),
        out_shape=jax.ShapeDtypeStruct((batch * seq, SWA_WIDTH), BF16),
        scratch_shapes=[pltpu.VMEM((2, SWA_HEADS // 2, 2 * WINDOW, 2 * WINDOW), F32)],
        compiler_params=_params(("arbitrary", "arbitrary")),
        name="attn_prompt",
    )(sinks, qz, qz, kv, kv, kv, kv)


def _attn_sample_kernel(sink_ref, slope_ref, q_ref, z_ref, kn_ref, vn_ref, kc_ref, vc_ref,
                        out_ref, kwin_ref, vwin_ref, *, tq, nb):
    hd, nk = SWA_HEAD_DIM, 2 * WINDOW
    rows = SWA_HEADS * tq
    grows = SWA_GROUP * tq
    seqs = range(nb)
    groups = range(SWA_KV_HEADS)

    lane = lax.broadcasted_iota(jnp.int32, (rows, nk), 1)
    tok = lax.broadcasted_iota(jnp.int32, (rows, nk), 0) % tq
    in_buffer = lane < WINDOW
    dist = jnp.where(in_buffer, WINDOW + tok - lane, (nk - tq) + tok - lane)
    allowed = (dist >= 0) & (dist <= WINDOW) & (in_buffer | (lane >= nk - tq))
    penalty = slope_ref[...] * dist.astype(F32)
    new_lanes = lax.broadcasted_iota(jnp.int32, (hd, WINDOW), 1) >= WINDOW - tq

    def new_rows_t(ref, b):
        x = jnp.concatenate([jnp.zeros((WINDOW - tq, SWA_KV_WIDTH), F32),
                             ref[b * tq:(b + 1) * tq, :]], axis=0)
        xt = [x[:, c * WINDOW:(c + 1) * WINDOW].T for c in range(SWA_KV_WIDTH // WINDOW)]
        per_block = WINDOW // hd
        return [xt[g // per_block][(g % per_block) * hd:(g % per_block + 1) * hd]
                for g in groups]

    k_all, v_all = {}, {}
    for b in seqs:
        kn_t, vn_t = new_rows_t(kn_ref, b), new_rows_t(vn_ref, b)
        for g in groups:
            kc, vc = kc_ref[b, g], vc_ref[b, g]
            k_all[b, g] = jnp.concatenate([kc, kn_t[g]], axis=1).astype(BF16)
            v_all[b, g] = jnp.concatenate([vc, vn_t[g]], axis=1).astype(BF16)
            kwin_ref[b, g] = jnp.where(new_lanes, kn_t[g], pltpu.roll(kc, WINDOW - tq, axis=1))
            vwin_ref[b, g] = jnp.where(new_lanes, vn_t[g], pltpu.roll(vc, WINDOW - tq, axis=1))

    s = []
    for b in seqs:
        q = q_ref[b * tq:(b + 1) * tq, :]
        parts = []
        for g in groups:
            qs = jnp.concatenate([q[:, h * hd:(h + 1) * hd]
                                  for h in range(g * SWA_GROUP, (g + 1) * SWA_GROUP)], axis=0)
            parts.append(jnp.dot(qs.astype(BF16), k_all[b, g], preferred_element_type=F32))
        s.append(jnp.concatenate(parts, axis=0))

    sink = sink_ref[...]
    s = [jnp.where(allowed, sb * (hd ** -0.5) - penalty, -jnp.inf) for sb in s]
    m = [jnp.maximum(jnp.max(sb, axis=-1, keepdims=True), sink) for sb in s]
    p = [jnp.exp(s[b] - m[b]) for b in seqs]
    inv = [1.0 / (jnp.sum(p[b], axis=-1, keepdims=True) + jnp.exp(sink - m[b])) for b in seqs]

    for b in seqs:
        pb = p[b].astype(BF16)
        o = jnp.concatenate(
            [lax.dot_general(pb[g * grows:(g + 1) * grows], v_all[b, g], _NT_DIMS,
                             preferred_element_type=F32) for g in groups], axis=0) * inv[b]
        o = jnp.concatenate([o[h * tq:(h + 1) * tq] for h in range(SWA_HEADS)], axis=1)
        z = z_ref[b * tq:(b + 1) * tq, :]
        out_ref[b * tq:(b + 1) * tq, :] = (o * _silu(z)).astype(out_ref.dtype)


def _attn_sample(qz, kv, cache_k, cache_v, sinks, *, batch, seq):
    nb = SAMPLE_SEQS_PER_STEP
    assert batch % nb == 0
    rows = nb * seq
    sink_col = jnp.repeat(sinks, seq).reshape(SWA_HEADS * seq, 1)
    slope_col = jnp.repeat(jnp.asarray([_alibi_slope(h) for h in range(SWA_HEADS)], F32),
                           seq).reshape(SWA_HEADS * seq, 1)
    col_spec = pl.BlockSpec((SWA_HEADS * seq, 1), lambda i: (0, 0))
    win_spec = pl.BlockSpec((nb, SWA_KV_HEADS, SWA_HEAD_DIM, WINDOW), lambda i: (i, 0, 0, 0))
    win_shape = jax.ShapeDtypeStruct((batch, SWA_KV_HEADS, SWA_HEAD_DIM, WINDOW), F32)
    to_lanes = lambda c: jnp.transpose(c, (0, 2, 3, 1))
    att, k_win, v_win = pl.pallas_call(
        functools.partial(_attn_sample_kernel, tq=seq, nb=nb),
        grid=(batch // nb,),
        in_specs=[
            col_spec,
            col_spec,
            pl.BlockSpec((rows, SWA_WIDTH), lambda i: (i, 0)),
            pl.BlockSpec((rows, SWA_WIDTH), lambda i: (i, 1)),
            pl.BlockSpec((rows, SWA_KV_WIDTH), lambda i: (i, 0)),
            pl.BlockSpec((rows, SWA_KV_WIDTH), lambda i: (i, 1)),
            win_spec,
            win_spec,
        ],
        out_specs=[pl.BlockSpec((rows, SWA_WIDTH), lambda i: (i, 0)), win_spec, win_spec],
        out_shape=[jax.ShapeDtypeStruct((batch * seq, SWA_WIDTH), F32), win_shape, win_shape],
        compiler_params=_params(("parallel",)),
        name="attn_sample",
    )(sink_col, slope_col, qz, qz, kv, kv, to_lanes(cache_k), to_lanes(cache_v))
    return att, jnp.transpose(k_win, (0, 3, 1, 2)), jnp.transpose(v_win, (0, 3, 1, 2))


def _trunk(x, weights, *, batch, seq, row_tile, res_tile, chunk, act_dtype,
           gla_init=None, cache=None, emit_weights=False):
    (g_norm_a, w_in_a, w_glow, w_gate_up, b_gate, g_onorm_a, w_out_a,
     g_norm_kv, w_kv, g_norm_b, w_in_b, sinks, w_out_b, g_final) = weights
    h = x.reshape(batch * seq, D_MODEL)

    proj, glow, *w_in_a_bf16 = _norm_matmul(
        h, g_norm_a, w_in_a, tm=row_tile, tn=1024, n=GLA_MAIN_COLS, out_dtype=act_dtype,
        w_extra=w_glow, emit_weights=emit_weights, w_transposed=True)
    bcum = _gla_gates(glow, w_gate_up, b_gate, chunk=chunk, tm=min(row_tile, GATE_ROW_TILE))
    o, s_fin = _gla(proj, bcum, g_onorm_a, batch=batch, seq=seq,
                    chunk=chunk, out_dtype=act_dtype, s0=gla_init)
    h = _matmul_residual(o, w_out_a, h, tm=res_tile)

    qz, kv, *w_in_b_bf16 = _norm_matmul(
        h, g_norm_b, w_in_b, tm=row_tile, tn=1024, out_dtype=act_dtype,
        w_extra=w_kv, g_extra=g_norm_kv, emit_weights=emit_weights)
    if cache is None:
        att = _attn_prompt(qz, kv, sinks, batch=batch, seq=seq)
        kv_win = kv.reshape(batch, seq, 2 * SWA_KV_WIDTH)[:, seq - WINDOW:]
        kv_win = kv_win.reshape(batch, WINDOW, 2, SWA_KV_HEADS, SWA_HEAD_DIM)
        k_win, v_win = kv_win[:, :, 0], kv_win[:, :, 1]
    else:
        cache_k, cache_v = cache
        att, k_win, v_win = _attn_sample(qz, kv, cache_k, cache_v, sinks,
                                         batch=batch, seq=seq)
    y = _matmul_residual(att, w_out_b, h, tm=res_tile, g_final=g_final)
    outs = (y.reshape(batch, seq, D_MODEL), s_fin[None], k_win, v_win)
    return outs, (w_in_a_bf16 + w_in_b_bf16)


def kernel(x_prompt, x_sample, state_gla, cache_k_win, cache_v_win, g_norm_a, w_in_a,
           w_gate_up, b_gate, g_onorm_a, w_out_a, g_norm_kv, w_kv, g_norm_b, w_in_b, sinks,
           w_out_b, g_final):
    assert w_in_a.shape[0] == 1 and w_in_b.shape[0] == 1, "one GLA layer, one SWA layer"
    assert cache_k_win.shape[1] == WINDOW
    pb, ps, _ = x_prompt.shape
    sb, ss, _ = x_sample.shape

    def weights(w_in_a_any, w_in_b_any):
        return (
            g_norm_a[0], w_in_a_any, w_in_a_t[0, GLA_MAIN_COLS:, :],
            w_gate_up[0], b_gate[0], g_onorm_a[0], w_out_a[0].astype(BF16),
            g_norm_kv, w_kv.astype(BF16),
            g_norm_b[0], w_in_b_any, sinks[0], w_out_b[0].astype(BF16), g_final,
        )

    w_in_a_t = jnp.swapaxes(w_in_a, 1, 2)

    (y_s, gla_s, k_s, v_s), (w_in_a_bf16, w_in_b_bf16) = _trunk(
        x_sample, weights(w_in_a_t, w_in_b), batch=sb, seq=ss, row_tile=sb * ss,
        res_tile=sb * ss, chunk=ss, act_dtype=F32,
        gla_init=state_gla.reshape(state_gla.shape[1:]),
        cache=(cache_k_win, cache_v_win), emit_weights=True)
    (y_p, gla_p, k_p, v_p), _ = _trunk(
        x_prompt, weights(w_in_a_bf16, w_in_b_bf16), batch=pb, seq=ps,
        row_tile=PROMPT_ROW_TILE, res_tile=RESIDUAL_ROW_TILE, chunk=GLA_PROMPT_CHUNK,
        act_dtype=BF16)
    return (y_p, y_s, gla_p, gla_s, k_p, v_p, k_s, v_s)
```

```python
import functools

import jax
import jax.numpy as jnp
from jax import lax
from jax.experimental import pallas as pl
from jax.experimental.pallas import tpu as pltpu

F32 = jnp.float32
BF16 = jnp.bfloat16

D_MODEL = 2048
GLA_HEADS = 4
GLA_KEY_DIM = D_MODEL // 2
GLA_VALUE_DIM = D_MODEL
GLA_DK = GLA_KEY_DIM // GLA_HEADS
GLA_DV = GLA_VALUE_DIM // GLA_HEADS
GLA_GATE_RANK = 16
GLA_GATE_TEMP = 16.0
GLA_MAIN_COLS = 2 * GLA_KEY_DIM + 2 * GLA_VALUE_DIM
SWA_HEAD_DIM = 64
SWA_HEADS = D_MODEL // SWA_HEAD_DIM
SWA_KV_HEADS = 4
SWA_GROUP = SWA_HEADS // SWA_KV_HEADS
SWA_WIDTH = SWA_HEADS * SWA_HEAD_DIM
SWA_KV_WIDTH = SWA_KV_HEADS * SWA_HEAD_DIM
WINDOW = 128
RMS_EPS = 1e-6
LOG2E = 1.4426950408889634
LN2 = 0.6931471805599453

V7X_VMEM_LIMIT_BYTES = 56 * 1024 * 1024
GLA_PROMPT_CHUNK = 128
GLA_SEQS_PER_STEP = 2
ATTN_BLOCKS_PER_STEP = 2
SAMPLE_SEQS_PER_STEP = 4
NORM_ROW_CHUNK = 256
GATE_CUMSUM_ROWS = 128
GATE_ROW_TILE = 1024
PROMPT_ROW_TILE = 1024
PROMPT_COL_TILE = 2048
SAMPLE_COL_TILE = 1024
QZ_COL_TILE = 1024
RESIDUAL_ROW_TILE = 512

_NT_DIMS = (((1,), (1,)), ((), ()))
_TN_DIMS = (((0,), (0,)), ((), ()))


def _params(semantics):
    return pltpu.CompilerParams(dimension_semantics=semantics,
                                vmem_limit_bytes=V7X_VMEM_LIMIT_BYTES)


def _silu(x):
    return x / (1.0 + jnp.exp(-x))


def _rms_scale(x):
    return lax.rsqrt(jnp.mean(x * x, axis=-1, keepdims=True) + RMS_EPS)


def _norm_matmul_kernel(x_ref, g_ref, w_ref, *rest, has_extra, extra_gain, emit_weights,
                        w_transposed):
    dims = _NT_DIMS if w_transposed else (((1,), (0,)), ((), ()))

    def mm(a, b):
        return lax.dot_general(a, b, dims, preferred_element_type=F32)

    rest = list(rest)
    g2_ref = rest.pop(0) if extra_gain else None
    w2_ref = rest.pop(0) if has_extra else None
    o_ref = rest.pop(0)
    o2_ref = rest.pop(0) if has_extra else None
    wb_ref = rest.pop(0) if emit_weights else None
    (xn_ref,) = rest

    first = pl.program_id(1) == 0

    def weights():
        w = w_ref[...].astype(BF16)
        if emit_weights:
            wb_ref[...] = w
        return w

    @pl.when(first)
    def _():
        tm = x_ref.shape[0]
        rc = min(tm, NORM_ROW_CHUNK)
        w = weights()
        for c in range(tm // rc):
            rows = slice(c * rc, (c + 1) * rc)
            x = x_ref[rows, :]
            xs = x * _rms_scale(x)
            xn = (xs * g_ref[...]).astype(BF16)
            xn_ref[rows, :] = xn
            o_ref[rows, :] = mm(xn, w).astype(o_ref.dtype)
            if has_extra:
                xn2 = (xs * g2_ref[...]).astype(BF16) if extra_gain else xn
                o2_ref[rows, :] = mm(xn2, w2_ref[...].astype(BF16)).astype(o2_ref.dtype)

    @pl.when(jnp.logical_not(first))
    def _():
        o_ref[...] = mm(xn_ref[...], weights()).astype(o_ref.dtype)


def _norm_matmul(x, g, w, *, tm, tn, out_dtype, n=None, w_extra=None, g_extra=None,
                 emit_weights=False, w_transposed=False):
    m, k = x.shape
    n_axis = -2 if w_transposed else -1
    n = w.shape[n_axis] if n is None else n
    assert n % tn == 0 and m % tm == 0
    assert not emit_weights or m == tm, "each weight block must be visited exactly once"
    grid = (m // tm, n // tn)
    w_block = (tn, k) if w_transposed else (k, tn)
    w_index = (lambda i, j: (j, 0)) if w_transposed else (lambda i, j: (0, j))
    if w.ndim == 3:
        w_spec = pl.BlockSpec((None,) + w_block, lambda i, j: (0,) + w_index(i, j))
    else:
        w_spec = pl.BlockSpec(w_block, w_index)
    in_specs = [
        pl.BlockSpec((tm, k), lambda i, j: (i, 0)),
        pl.BlockSpec((1, k), lambda i, j: (0, 0)),
        w_spec,
    ]
    out_shape = [jax.ShapeDtypeStruct((m, n), out_dtype)]
    out_specs = [pl.BlockSpec((tm, tn), lambda i, j: (i, j))]
    args = [x, g.reshape(1, k), w]
    if g_extra is not None:
        in_specs.append(pl.BlockSpec((1, k), lambda i, j: (0, 0)))
        args.append(g_extra.reshape(1, k))
    if w_extra is not None:
        n2 = w_extra.shape[n_axis]
        in_specs.append(pl.BlockSpec(w_extra.shape, lambda i, j: (0, 0)))
        out_shape.append(jax.ShapeDtypeStruct((m, n2), F32))
        out_specs.append(pl.BlockSpec((tm, n2), lambda i, j: (i, 0)))
        args.append(w_extra)
    if emit_weights:
        out_shape.append(jax.ShapeDtypeStruct((n, k) if w_transposed else (k, n), BF16))
        out_specs.append(pl.BlockSpec(w_block, w_index))
    res = pl.pallas_call(
        functools.partial(_norm_matmul_kernel, has_extra=w_extra is not None,
                          extra_gain=g_extra is not None, emit_weights=emit_weights,
                          w_transposed=w_transposed),
        grid=grid,
        in_specs=in_specs,
        out_specs=out_specs,
        out_shape=out_shape,
        scratch_shapes=[pltpu.VMEM((tm, k), BF16)],
        compiler_params=_params(("parallel", "arbitrary")),
        name=f"norm_matmul_{m}x{n}",
    )(*args)
    return res if len(res) > 1 else res[0]


def _matmul_residual_kernel(a_ref, w_ref, r_ref, *rest, final_norm):
    if final_norm:
        g_ref, o_ref = rest
    else:
        (o_ref,) = rest
    h = r_ref[...] + jnp.dot(a_ref[...].astype(BF16), w_ref[...],
                             preferred_element_type=F32)
    if final_norm:
        h = h * _rms_scale(h) * g_ref[...]
    o_ref[...] = h


def _matmul_residual(a, w, res, *, tm, g_final=None):
    m, k = a.shape
    n = w.shape[1]
    in_specs = [
        pl.BlockSpec((tm, k), lambda i: (i, 0)),
        pl.BlockSpec((k, n), lambda i: (0, 0)),
        pl.BlockSpec((tm, n), lambda i: (i, 0)),
    ]
    args = [a, w, res]
    if g_final is not None:
        in_specs.append(pl.BlockSpec((1, n), lambda i: (0, 0)))
        args.append(g_final.reshape(1, n))
    return pl.pallas_call(
        functools.partial(_matmul_residual_kernel, final_norm=g_final is not None),
        grid=(m // tm,),
        in_specs=in_specs,
        out_specs=pl.BlockSpec((tm, n), lambda i: (i, 0)),
        out_shape=jax.ShapeDtypeStruct((m, n), F32),
        compiler_params=_params(("parallel",)),
        name=f"matmul_residual_{m}" + ("_final" if g_final is not None else ""),
    )(*args)


def _split_bf16(x):
    hi = x.astype(BF16)
    return hi, (x - hi.astype(F32)).astype(BF16)


def _gla_gates_kernel(glow_ref, wg_ref, bg_ref, bcum_ref, *, chunk):
    rows = glow_ref.shape[0]
    g_hi, g_lo = _split_bf16(glow_ref[...])
    w_hi, w_lo = _split_bf16(wg_ref[...])
    x = jnp.dot(jnp.concatenate([g_hi, g_lo, g_hi], axis=1),
                jnp.concatenate([w_hi, w_hi, w_lo], axis=0),
                preferred_element_type=F32) + bg_ref[...]
    softplus2 = jnp.log2(1.0 + jnp.exp2(jnp.abs(x) * (-LOG2E)))
    logg = jnp.minimum(x, 0.0) * (1.0 / GLA_GATE_TEMP) - softplus2 * (LN2 / GLA_GATE_TEMP)

    span = GATE_CUMSUM_ROWS
    row = lax.broadcasted_iota(jnp.int32, (span, span), 0)
    col = lax.broadcasted_iota(jnp.int32, (span, span), 1)
    same_chunk = (row // chunk) == (col // chunk) if chunk < span else True
    tril = jnp.where((col <= row) & same_chunk, 1.0, 0.0).astype(BF16)
    tril2 = jnp.concatenate([tril, tril], axis=1)
    for i in range(rows // span):
        hi, lo = _split_bf16(logg[i * span:(i + 1) * span])
        bcum_ref[i * span:(i + 1) * span, :] = jnp.dot(
            tril2, jnp.concatenate([hi, lo], axis=0), preferred_element_type=F32)


def _gla_gates(glow, wg, bg, *, chunk, tm):
    m = glow.shape[0]
    assert m % tm == 0 and tm % GATE_CUMSUM_ROWS == 0
    assert GATE_CUMSUM_ROWS % chunk == 0 or chunk % GATE_CUMSUM_ROWS == 0
    assert chunk <= GATE_CUMSUM_ROWS, "cumulative sums do not cross row spans"
    return pl.pallas_call(
        functools.partial(_gla_gates_kernel, chunk=chunk),
        grid=(m // tm,),
        in_specs=[
            pl.BlockSpec((tm, GLA_GATE_RANK), lambda i: (i, 0)),
            pl.BlockSpec((GLA_GATE_RANK, GLA_KEY_DIM), lambda i: (0, 0)),
            pl.BlockSpec((1, GLA_KEY_DIM), lambda i: (0, 0)),
        ],
        out_specs=pl.BlockSpec((tm, GLA_KEY_DIM), lambda i: (i, 0)),
        out_shape=jax.ShapeDtypeStruct((m, GLA_KEY_DIM), F32),
        compiler_params=_params(("parallel",)),
        name=f"gla_gates_{m}",
    )(glow, wg, bg.reshape(1, -1))


def _gla_kernel(q_ref, k_ref, v_ref, r_ref, bcum_ref, gon_ref, *rest, has_init, n_chunks):
    rest = list(rest)
    s0_ref = rest.pop(0) if has_init else None
    o_ref, sfin_ref = rest[:2]
    s_ref = rest[2] if n_chunks > 1 else None
    c = pl.program_id(1)
    nseq, chunk = q_ref.shape[:2]
    chains = [(s, h) for s in range(nseq) for h in range(GLA_HEADS)]

    if n_chunks > 1:
        @pl.when(c == 0)
        def _():
            if has_init:
                s_ref[...] = s0_ref[...]
            else:
                s_ref[...] = jnp.zeros_like(s_ref)

    row = lax.broadcasted_iota(jnp.int32, (chunk, chunk), 0)
    col = lax.broadcasted_iota(jnp.int32, (chunk, chunk), 1)
    causal = col <= row
    heads = range(GLA_HEADS)
    ks = [slice(h * GLA_DK, (h + 1) * GLA_DK) for h in heads]
    vs = [slice(h * GLA_DV, (h + 1) * GLA_DV) for h in heads]

    def state(s, h):
        if n_chunks > 1:
            return s_ref[s, h]
        return s0_ref[s, h] if has_init else jnp.zeros((GLA_DK, GLA_DV), F32)

    q_inter, k_state, scores, decay = {}, {}, {}, {}
    for s, h in chains:
        b = bcum_ref[s, :, ks[h]]
        b_last = b[chunk - 1:chunk, :]
        b_mid = b[chunk // 2 - 1:chunk // 2, :]
        q = q_ref[s, :, ks[h]].astype(F32) * (GLA_DK ** -0.5)
        k = k_ref[s, :, ks[h]].astype(F32)
        q_inter[s, h] = (q * jnp.exp(b)).astype(BF16)
        q_intra = (q * jnp.exp(b - b_mid)).astype(BF16)
        k_intra = (k * jnp.exp(b_mid - b)).astype(BF16)
        k_state[s, h] = (k * jnp.exp(b_last - b)).astype(BF16)
        scores[s, h] = lax.dot_general(q_intra, k_intra, _NT_DIMS,
                                       preferred_element_type=F32)
        decay[s, h] = jnp.exp(jnp.broadcast_to(b_last, (128, GLA_DK))).T

    o = {}
    for s, h in chains:
        sc = jnp.where(causal, scores[s, h], 0.0).astype(BF16)
        v = v_ref[s, :, vs[h]].astype(BF16)
        o[s, h] = (jnp.dot(q_inter[s, h], state(s, h).astype(BF16),
                           preferred_element_type=F32)
                   + jnp.dot(sc, v, preferred_element_type=F32))
        upd = lax.dot_general(k_state[s, h], v, _TN_DIMS, preferred_element_type=F32)
        s_new = (state(s, h) * jnp.concatenate([decay[s, h]] * (GLA_DV // 128), axis=1)
                 + upd)
        if n_chunks > 1:
            s_ref[s, h] = s_new
        else:
            sfin_ref[s, h] = s_new

    scale = {sh: _rms_scale(o[sh]) for sh in chains}
    for s, h in chains:
        r = r_ref[s, :, vs[h]].astype(F32)
        o_ref[s, :, vs[h]] = (o[s, h] * scale[s, h] * gon_ref[:, vs[h]]
                              * _silu(r)).astype(o_ref.dtype)

    if n_chunks > 1:
        @pl.when(c == n_chunks - 1)
        def _():
            sfin_ref[...] = s_ref[...]


def _gla(proj, bcum, gon, *, batch, seq, chunk, out_dtype, s0=None):
    n = seq // chunk
    ns = GLA_SEQS_PER_STEP
    assert batch % ns == 0
    proj = proj.reshape(batch, seq, proj.shape[-1])
    bcum = bcum.reshape(batch, seq, bcum.shape[-1])
    kb, vb, rb = 1, 2 * GLA_KEY_DIM // GLA_VALUE_DIM, 2 * GLA_KEY_DIM // GLA_VALUE_DIM + 1
    in_specs = [
        pl.BlockSpec((ns, chunk, GLA_KEY_DIM), lambda b, c: (b, c, 0)),
        pl.BlockSpec((ns, chunk, GLA_KEY_DIM), lambda b, c: (b, c, kb)),
        pl.BlockSpec((ns, chunk, GLA_VALUE_DIM), lambda b, c: (b, c, vb)),
        pl.BlockSpec((ns, chunk, GLA_VALUE_DIM), lambda b, c: (b, c, rb)),
        pl.BlockSpec((ns, chunk, GLA_KEY_DIM), lambda b, c: (b, c, 0)),
        pl.BlockSpec((1, GLA_VALUE_DIM), lambda b, c: (0, 0)),
    ]
    args = [proj, proj, proj, proj, bcum, gon.reshape(1, -1)]
    state_spec = pl.BlockSpec((ns, GLA_HEADS, GLA_DK, GLA_DV), lambda b, c: (b, 0, 0, 0))
    if s0 is not None:
        in_specs.append(state_spec)
        args.append(s0)
    scratch = [pltpu.VMEM((ns, GLA_HEADS, GLA_DK, GLA_DV), F32)] if n > 1 else []
    o, s_fin = pl.pallas_call(
        functools.partial(_gla_kernel, has_init=s0 is not None, n_chunks=n),
        grid=(batch // ns, n),
        in_specs=in_specs,
        out_specs=[
            pl.BlockSpec((ns, chunk, GLA_VALUE_DIM), lambda b, c: (b, c, 0)),
            state_spec,
        ],
        out_shape=[
            jax.ShapeDtypeStruct((batch, seq, GLA_VALUE_DIM), out_dtype),
            jax.ShapeDtypeStruct((batch, GLA_HEADS, GLA_DK, GLA_DV), F32),
        ],
        scratch_shapes=scratch,
        compiler_params=_params(("parallel", "arbitrary")),
        name=f"gla_chunk{chunk}",
    )(*args)
    return o.reshape(batch * seq, GLA_VALUE_DIM), s_fin


def _alibi_slope(head):
    return 2.0 ** (-8.0 * (head + 1) / SWA_HEADS)


def _attn_prompt_kernel(sinks_ref, q_ref, z_ref, kp_ref, ko_ref, vp_ref, vo_ref, out_ref,
                        bias_ref):
    hd, nkeys = SWA_HEAD_DIM, 2 * WINDOW
    pair_w = 2 * hd
    pairs_per_group = SWA_GROUP // 2
    blk = pl.program_id(1)

    @pl.when((pl.program_id(0) == 0) & (blk == 0))
    def _():
        kj = lax.broadcasted_iota(jnp.int32, (nkeys, WINDOW), 0)
        qi = lax.broadcasted_iota(jnp.int32, (nkeys, WINDOW), 1)
        dist = WINDOW + qi - kj
        ok = (dist >= 0) & (dist <= WINDOW)
        ok_first = ok & (kj >= WINDOW)
        distf = dist.astype(F32)
        for h in range(SWA_HEADS):
            pen = (-_alibi_slope(h) * LOG2E) * distf
            sl = slice((h % 2) * WINDOW, (h % 2 + 1) * WINDOW)
            bias_ref[0, h // 2, :, sl] = jnp.where(ok_first, pen, -jnp.inf)
            bias_ref[1, h // 2, :, sl] = jnp.where(ok, pen, -jnp.inf)

    n_sub = q_ref.shape[0] // WINDOW
    k_rows = jnp.concatenate([kp_ref[...], ko_ref[...]], axis=0)
    v_rows = jnp.concatenate([vp_ref[...], vo_ref[...]], axis=0)
    ones = jnp.ones((16, nkeys), F32)
    lane = lax.broadcasted_iota(jnp.int32, (nkeys, pair_w), 1)
    qlane = lax.broadcasted_iota(jnp.int32, (WINDOW, pair_w), 1)
    qk_scale = (hd ** -0.5) * LOG2E
    quad_pairs = 2
    n_quads = SWA_HEADS // (2 * quad_pairs)
    quads_per_group = pairs_per_group // quad_pairs
    quarter = lax.broadcasted_iota(jnp.int32, (1, 2 * quad_pairs * WINDOW), 1) // WINDOW

    for sub in range(n_sub):
        qrows = slice(sub * WINDOW, (sub + 1) * WINDOW)
        tbl = jnp.minimum(blk, 1) if sub == 0 else 1
        k = k_rows[sub * WINDOW:sub * WINDOW + nkeys]
        v = v_rows[sub * WINDOW:sub * WINDOW + nkeys]
        vt = v.T

        k2, vt1 = [], []
        for g in range(SWA_KV_HEADS):
            kblk = k[:, (g // 2) * pair_w:(g // 2 + 1) * pair_w]
            k_here = jnp.where((lane < hd) if g % 2 == 0 else (lane >= hd), kblk, 0.0)
            k2.append((k_here + pltpu.roll(k_here, hd, axis=1)).astype(BF16))
            vt1.append(jnp.concatenate([vt[g * hd:(g + 1) * hd], ones], axis=0).astype(BF16))

        def scores(quad, k2=k2, qrows=qrows):
            parts = []
            for j in range(quad_pairs):
                col = (quad * quad_pairs + j) * pair_w
                q_pair = q_ref[qrows, col:col + pair_w]
                zero = jnp.zeros_like(q_pair)
                parts += [jnp.where(qlane < hd, q_pair, zero),
                          jnp.where(qlane >= hd, q_pair, zero)]
            return lax.dot_general(k2[quad // quads_per_group], jnp.concatenate(parts, axis=0),
                                   _NT_DIMS, preferred_element_type=F32)

        st_next = scores(0)
        for quad in range(n_quads):
            st = st_next
            if quad + 1 < n_quads:
                st_next = scores(quad + 1)
            pair0 = quad * quad_pairs
            bias = jnp.concatenate([bias_ref[tbl, pair0 + j] for j in range(quad_pairs)], axis=1)
            s2 = st * qk_scale + bias
            sink2 = sinks_ref[2 * pair0]
            for t in range(1, 2 * quad_pairs):
                sink2 = jnp.where(quarter == t, sinks_ref[2 * pair0 + t], sink2)
            sink2 = sink2 * LOG2E
            m = jnp.maximum(jnp.max(s2, axis=0, keepdims=True), sink2)
            p = jnp.exp2(s2 - m).astype(BF16)
            oa = jnp.dot(vt1[quad // quads_per_group], p, preferred_element_type=F32)
            denom = oa[hd:hd + 1] + jnp.exp2(sink2 - m)
            on = oa[0:hd] * (1.0 / denom)
            for j in range(quad_pairs):
                lo = 2 * j * WINDOW
                o_pair = jnp.concatenate([on[:, lo:lo + WINDOW],
                                          on[:, lo + WINDOW:lo + 2 * WINDOW]], axis=0).T
                col = (pair0 + j) * pair_w
                z_pair = z_ref[qrows, col:col + pair_w].astype(F32)
                out_ref[qrows, col:col + pair_w] = (o_pair * _silu(z_pair)).astype(out_ref.dtype)


def _attn_prompt(qz, kv, sinks, *, batch, seq):
    sub = ATTN_BLOCKS_PER_STEP
    tq = sub * WINDOW
    nb = seq // tq
    assert seq % tq == 0
    row = lambda b, i: b * nb + i
    prev = lambda b, i: (b * nb + i) * sub - jnp.minimum(i, 1)
    return pl.pallas_call(
        _attn_prompt_kernel,
        grid=(batch, nb),
        in_specs=[
            pl.BlockSpec(memory_space=pltpu.SMEM),
            pl.BlockSpec((tq, SWA_WIDTH), lambda b, i: (row(b, i), 0)),
            pl.BlockSpec((tq, SWA_WIDTH), lambda b, i: (row(b, i), 1)),
            pl.BlockSpec((WINDOW, SWA_KV_WIDTH), lambda b, i: (prev(b, i), 0)),
            pl.BlockSpec((tq, SWA_KV_WIDTH), lambda b, i: (row(b, i), 0)),
            pl.BlockSpec((WINDOW, SWA_KV_WIDTH), lambda b, i: (prev(b, i), 1)),
            pl.BlockSpec((tq, SWA_KV_WIDTH), lambda b, i: (row(b, i), 1)),
        ],
        out_specs=pl.BlockSpec((tq, SWA_WIDTH), lambda b, i: (row(b, i), 0)),
        out_shape=jax.ShapeDtypeStruct((batch * seq, SWA_WIDTH), BF16),
        scratch_shapes=[pltpu.VMEM((2, SWA_HEADS // 2, 2 * WINDOW, 2 * WINDOW), F32)],
        compiler_params=_params(("arbitrary", "arbitrary")),
        name="attn_prompt",
    )(sinks, qz, qz, kv, kv, kv, kv)


def _attn_sample_kernel(sink_ref, slope_ref, q_ref, z_ref, kn_ref, vn_ref, kc_ref, vc_ref,
                        out_ref, kwin_ref, vwin_ref, *, tq, nb):
    hd, nk = SWA_HEAD_DIM, 2 * WINDOW
    rows = SWA_HEADS * tq
    grows = SWA_GROUP * tq
    seqs = range(nb)
    groups = range(SWA_KV_HEADS)

    lane = lax.broadcasted_iota(jnp.int32, (rows, nk), 1)
    tok = lax.broadcasted_iota(jnp.int32, (rows, nk), 0) % tq
    in_buffer = lane < WINDOW
    dist = jnp.where(in_buffer, WINDOW + tok - lane, (nk - tq) + tok - lane)
    allowed = (dist >= 0) & (dist <= WINDOW) & (in_buffer | (lane >= nk - tq))
    penalty = slope_ref[...] * dist.astype(F32)
    new_lanes = lax.broadcasted_iota(jnp.int32, (hd, WINDOW), 1) >= WINDOW - tq

    def new_rows_t(ref, b):
        x = jnp.concatenate([jnp.zeros((WINDOW - tq, SWA_KV_WIDTH), F32),
                             ref[b * tq:(b + 1) * tq, :]], axis=0)
        xt = [x[:, c * WINDOW:(c + 1) * WINDOW].T for c in range(SWA_KV_WIDTH // WINDOW)]
        per_block = WINDOW // hd
        return [xt[g // per_block][(g % per_block) * hd:(g % per_block + 1) * hd]
                for g in groups]

    k_all, v_all = {}, {}
    for b in seqs:
        kn_t, vn_t = new_rows_t(kn_ref, b), new_rows_t(vn_ref, b)
        for g in groups:
            kc, vc = kc_ref[b, g], vc_ref[b, g]
            k_all[b, g] = jnp.concatenate([kc, kn_t[g]], axis=1).astype(BF16)
            v_all[b, g] = jnp.concatenate([vc, vn_t[g]], axis=1).astype(BF16)
            kwin_ref[b, g] = jnp.where(new_lanes, kn_t[g], pltpu.roll(kc, WINDOW - tq, axis=1))
            vwin_ref[b, g] = jnp.where(new_lanes, vn_t[g], pltpu.roll(vc, WINDOW - tq, axis=1))

    s = []
    for b in seqs:
        q = q_ref[b * tq:(b + 1) * tq, :]
        parts = []
        for g in groups:
            qs = jnp.concatenate([q[:, h * hd:(h + 1) * hd]
                                  for h in range(g * SWA_GROUP, (g + 1) * SWA_GROUP)], axis=0)
            parts.append(jnp.dot(qs.astype(BF16), k_all[b, g], preferred_element_type=F32))
        s.append(jnp.concatenate(parts, axis=0))

    sink = sink_ref[...]
    s = [jnp.where(allowed, sb * (hd ** -0.5) - penalty, -jnp.inf) for sb in s]
    m = [jnp.maximum(jnp.max(sb, axis=-1, keepdims=True), sink) for sb in s]
    p = [jnp.exp(s[b] - m[b]) for b in seqs]
    inv = [1.0 / (jnp.sum(p[b], axis=-1, keepdims=True) + jnp.exp(sink - m[b])) for b in seqs]

    for b in seqs:
        pb = p[b].astype(BF16)
        o = jnp.concatenate(
            [lax.dot_general(pb[g * grows:(g + 1) * grows], v_all[b, g], _NT_DIMS,
                             preferred_element_type=F32) for g in groups], axis=0) * inv[b]
        o = jnp.concatenate([o[h * tq:(h + 1) * tq] for h in range(SWA_HEADS)], axis=1)
        z = z_ref[b * tq:(b + 1) * tq, :]
        out_ref[b * tq:(b + 1) * tq, :] = (o * _silu(z)).astype(out_ref.dtype)


def _attn_sample(qz, kv, cache_k, cache_v, sinks, *, batch, seq):
    nb = SAMPLE_SEQS_PER_STEP
    assert batch % nb == 0
    rows = nb * seq
    sink_col = jnp.repeat(sinks, seq).reshape(SWA_HEADS * seq, 1)
    slope_col = jnp.repeat(jnp.asarray([_alibi_slope(h) for h in range(SWA_HEADS)], F32),
                           seq).reshape(SWA_HEADS * seq, 1)
    col_spec = pl.BlockSpec((SWA_HEADS * seq, 1), lambda i: (0, 0))
    win_spec = pl.BlockSpec((nb, SWA_KV_HEADS, SWA_HEAD_DIM, WINDOW), lambda i: (i, 0, 0, 0))
    win_shape = jax.ShapeDtypeStruct((batch, SWA_KV_HEADS, SWA_HEAD_DIM, WINDOW), F32)
    to_lanes = lambda c: jnp.transpose(c, (0, 2, 3, 1))
    att, k_win, v_win = pl.pallas_call(
        functools.partial(_attn_sample_kernel, tq=seq, nb=nb),
        grid=(batch // nb,),
        in_specs=[
            col_spec,
            col_spec,
            pl.BlockSpec((rows, SWA_WIDTH), lambda i: (i, 0)),
            pl.BlockSpec((rows, SWA_WIDTH), lambda i: (i, 1)),
            pl.BlockSpec((rows, SWA_KV_WIDTH), lambda i: (i, 0)),
            pl.BlockSpec((rows, SWA_KV_WIDTH), lambda i: (i, 1)),
            win_spec,
            win_spec,
        ],
        out_specs=[pl.BlockSpec((rows, SWA_WIDTH), lambda i: (i, 0)), win_spec, win_spec],
        out_shape=[jax.ShapeDtypeStruct((batch * seq, SWA_WIDTH), F32), win_shape, win_shape],
        compiler_params=_params(("parallel",)),
        name="attn_sample",
    )(sink_col, slope_col, qz, qz, kv, kv, to_lanes(cache_k), to_lanes(cache_v))
    return att, jnp.transpose(k_win, (0, 3, 1, 2)), jnp.transpose(v_win, (0, 3, 1, 2))


def _trunk(x, weights, *, batch, seq, row_tile, col_tile, res_tile, chunk, act_dtype,
           gla_init=None, cache=None, emit_weights=False):
    (g_norm_a, w_in_a, w_glow, w_gate_up, b_gate, g_onorm_a, w_out_a,
     g_norm_kv, w_kv, g_norm_b, w_in_b, sinks, w_out_b, g_final) = weights
    h = x.reshape(batch * seq, D_MODEL)

    proj, glow, *w_in_a_bf16 = _norm_matmul(
        h, g_norm_a, w_in_a, tm=row_tile, tn=col_tile, n=GLA_MAIN_COLS, out_dtype=act_dtype,
        w_extra=w_glow, emit_weights=emit_weights, w_transposed=True)
    bcum = _gla_gates(glow, w_gate_up, b_gate, chunk=chunk, tm=min(row_tile, GATE_ROW_TILE))
    o, s_fin = _gla(proj, bcum, g_onorm_a, batch=batch, seq=seq,
                    chunk=chunk, out_dtype=act_dtype, s0=gla_init)
    h = _matmul_residual(o, w_out_a, h, tm=res_tile)

    qz, kv, *w_in_b_bf16 = _norm_matmul(
        h, g_norm_b, w_in_b, tm=row_tile, tn=QZ_COL_TILE, out_dtype=act_dtype,
        w_extra=w_kv, g_extra=g_norm_kv, emit_weights=emit_weights)
    if cache is None:
        att = _attn_prompt(qz, kv, sinks, batch=batch, seq=seq)
        kv_win = kv.reshape(batch, seq, 2 * SWA_KV_WIDTH)[:, seq - WINDOW:]
        kv_win = kv_win.reshape(batch, WINDOW, 2, SWA_KV_HEADS, SWA_HEAD_DIM)
        k_win, v_win = kv_win[:, :, 0], kv_win[:, :, 1]
    else:
        cache_k, cache_v = cache
        att, k_win, v_win = _attn_sample(qz, kv, cache_k, cache_v, sinks,
                                         batch=batch, seq=seq)
    y = _matmul_residual(att, w_out_b, h, tm=res_tile, g_final=g_final)
    outs = (y.reshape(batch, seq, D_MODEL), s_fin[None], k_win, v_win)
    return outs, (w_in_a_bf16 + w_in_b_bf16)


def kernel(x_prompt, x_sample, state_gla, cache_k_win, cache_v_win, g_norm_a, w_in_a,
           w_gate_up, b_gate, g_onorm_a, w_out_a, g_norm_kv, w_kv, g_norm_b, w_in_b, sinks,
           w_out_b, g_final):
    assert w_in_a.shape[0] == 1 and w_in_b.shape[0] == 1, "one GLA layer, one SWA layer"
    assert cache_k_win.shape[1] == WINDOW
    pb, ps, _ = x_prompt.shape
    sb, ss, _ = x_sample.shape

    def weights(w_in_a_any, w_in_b_any):
        return (
            g_norm_a[0], w_in_a_any, w_in_a_t[0, GLA_MAIN_COLS:, :],
            w_gate_up[0], b_gate[0], g_onorm_a[0], w_out_a[0].astype(BF16),
            g_norm_kv, w_kv.astype(BF16),
            g_norm_b[0], w_in_b_any, sinks[0], w_out_b[0].astype(BF16), g_final,
        )

    w_in_a_t = jnp.swapaxes(w_in_a, 1, 2)

    (y_s, gla_s, k_s, v_s), (w_in_a_bf16, w_in_b_bf16) = _trunk(
        x_sample, weights(w_in_a_t, w_in_b), batch=sb, seq=ss, row_tile=sb * ss,
        col_tile=SAMPLE_COL_TILE, res_tile=sb * ss, chunk=ss, act_dtype=F32,
        gla_init=state_gla.reshape(state_gla.shape[1:]),
        cache=(cache_k_win, cache_v_win), emit_weights=True)
    (y_p, gla_p, k_p, v_p), _ = _trunk(
        x_prompt, weights(w_in_a_bf16, w_in_b_bf16), batch=pb, seq=ps,
        row_tile=PROMPT_ROW_TILE, col_tile=PROMPT_COL_TILE, res_tile=RESIDUAL_ROW_TILE,
        chunk=GLA_PROMPT_CHUNK, act_dtype=BF16)
    return (y_p, y_s, gla_p, gla_s, k_p, v_p, k_s, v_s)
```

```python
import functools

import jax
import jax.numpy as jnp
from jax import lax
from jax.experimental import pallas as pl
from jax.experimental.pallas import tpu as pltpu

F32 = jnp.float32
BF16 = jnp.bfloat16

D_MODEL = 2048
GLA_HEADS = 4
GLA_KEY_DIM = D_MODEL // 2
GLA_VALUE_DIM = D_MODEL
GLA_DK = GLA_KEY_DIM // GLA_HEADS
GLA_DV = GLA_VALUE_DIM // GLA_HEADS
GLA_GATE_RANK = 16
GLA_GATE_TEMP = 16.0
GLA_MAIN_COLS = 2 * GLA_KEY_DIM + 2 * GLA_VALUE_DIM
SWA_HEAD_DIM = 64
SWA_HEADS = D_MODEL // SWA_HEAD_DIM
SWA_KV_HEADS = 4
SWA_GROUP = SWA_HEADS // SWA_KV_HEADS
SWA_WIDTH = SWA_HEADS * SWA_HEAD_DIM
SWA_KV_WIDTH = SWA_KV_HEADS * SWA_HEAD_DIM
WINDOW = 128
RMS_EPS = 1e-6
LOG2E = 1.4426950408889634
LN2 = 0.6931471805599453

V7X_VMEM_LIMIT_BYTES = 56 * 1024 * 1024
GLA_PROMPT_CHUNK = 128
GLA_SEQS_PER_STEP = 4
ATTN_BLOCKS_PER_STEP = 4
SAMPLE_SEQS_PER_STEP = 4
NORM_ROW_CHUNK = 256
GATE_CUMSUM_ROWS = 128
GATE_ROW_TILE = 1024
PROMPT_ROW_TILE = 1024
PROMPT_COL_TILE = 2048
SAMPLE_COL_TILE = 1024
QZ_COL_TILE = 1024
SAMPLE_K_TILE = 512
RESIDUAL_ROW_TILE = 512

_NT_DIMS = (((1,), (1,)), ((), ()))
_TN_DIMS = (((0,), (0,)), ((), ()))


def _params(semantics):
    return pltpu.CompilerParams(dimension_semantics=semantics,
                                vmem_limit_bytes=V7X_VMEM_LIMIT_BYTES)


def _silu(x):
    return x / (1.0 + jnp.exp(-x))


def _rms_scale(x):
    return lax.rsqrt(jnp.mean(x * x, axis=-1, keepdims=True) + RMS_EPS)


def _norm_matmul_kernel(x_ref, g_ref, w_ref, *rest, has_extra, extra_gain, emit_weights,
                        w_transposed):
    dims = _NT_DIMS if w_transposed else (((1,), (0,)), ((), ()))

    def mm(a, b):
        return lax.dot_general(a, b, dims, preferred_element_type=F32)

    rest = list(rest)
    g2_ref = rest.pop(0) if extra_gain else None
    w2_ref = rest.pop(0) if has_extra else None
    o_ref = rest.pop(0)
    o2_ref = rest.pop(0) if has_extra else None
    wb_ref = rest.pop(0) if emit_weights else None
    (xn_ref,) = rest

    first = pl.program_id(1) == 0

    def weights():
        w = w_ref[...].astype(BF16)
        if emit_weights:
            wb_ref[...] = w
        return w

    @pl.when(first)
    def _():
        tm = x_ref.shape[0]
        rc = min(tm, NORM_ROW_CHUNK)
        w = weights()
        for c in range(tm // rc):
            rows = slice(c * rc, (c + 1) * rc)
            x = x_ref[rows, :]
            xs = x * _rms_scale(x)
            xn = (xs * g_ref[...]).astype(BF16)
            xn_ref[rows, :] = xn
            o_ref[rows, :] = mm(xn, w).astype(o_ref.dtype)
            if has_extra:
                xn2 = (xs * g2_ref[...]).astype(BF16) if extra_gain else xn
                o2_ref[rows, :] = mm(xn2, w2_ref[...].astype(BF16)).astype(o2_ref.dtype)

    @pl.when(jnp.logical_not(first))
    def _():
        o_ref[...] = mm(xn_ref[...], weights()).astype(o_ref.dtype)


def _norm_matmul(x, g, w, *, tm, tn, out_dtype, n=None, w_extra=None, g_extra=None,
                 emit_weights=False, w_transposed=False):
    m, k = x.shape
    n_axis = -2 if w_transposed else -1
    n = w.shape[n_axis] if n is None else n
    assert n % tn == 0 and m % tm == 0
    assert not emit_weights or m == tm, "each weight block must be visited exactly once"
    grid = (m // tm, n // tn)
    w_block = (tn, k) if w_transposed else (k, tn)
    w_index = (lambda i, j: (j, 0)) if w_transposed else (lambda i, j: (0, j))
    if w.ndim == 3:
        w_spec = pl.BlockSpec((None,) + w_block, lambda i, j: (0,) + w_index(i, j))
    else:
        w_spec = pl.BlockSpec(w_block, w_index)
    in_specs = [
        pl.BlockSpec((tm, k), lambda i, j: (i, 0)),
        pl.BlockSpec((1, k), lambda i, j: (0, 0)),
        w_spec,
    ]
    out_shape = [jax.ShapeDtypeStruct((m, n), out_dtype)]
    out_specs = [pl.BlockSpec((tm, tn), lambda i, j: (i, j))]
    args = [x, g.reshape(1, k), w]
    if g_extra is not None:
        in_specs.append(pl.BlockSpec((1, k), lambda i, j: (0, 0)))
        args.append(g_extra.reshape(1, k))
    if w_extra is not None:
        n2 = w_extra.shape[n_axis]
        in_specs.append(pl.BlockSpec(w_extra.shape, lambda i, j: (0, 0)))
        out_shape.append(jax.ShapeDtypeStruct((m, n2), F32))
        out_specs.append(pl.BlockSpec((tm, n2), lambda i, j: (i, 0)))
        args.append(w_extra)
    if emit_weights:
        out_shape.append(jax.ShapeDtypeStruct((n, k) if w_transposed else (k, n), BF16))
        out_specs.append(pl.BlockSpec(w_block, w_index))
    res = pl.pallas_call(
        functools.partial(_norm_matmul_kernel, has_extra=w_extra is not None,
                          extra_gain=g_extra is not None, emit_weights=emit_weights,
                          w_transposed=w_transposed),
        grid=grid,
        in_specs=in_specs,
        out_specs=out_specs,
        out_shape=out_shape,
        scratch_shapes=[pltpu.VMEM((tm, k), BF16)],
        compiler_params=_params(("parallel", "arbitrary")),
        name=f"norm_matmul_{m}x{n}",
    )(*args)
    return res if len(res) > 1 else res[0]


def _matmul_residual_kernel(a_ref, w_ref, r_ref, *rest, final_norm, emit_weights, nk):
    rest = list(rest)
    g_ref = rest.pop(0) if final_norm else None
    o_ref = rest.pop(0)
    w = w_ref[...].astype(BF16)
    if emit_weights:
        rest[0][...] = w
    part = jnp.dot(a_ref[...].astype(BF16), w, preferred_element_type=F32)

    if nk == 1:
        h = r_ref[...] + part
        if final_norm:
            h = h * _rms_scale(h) * g_ref[...]
        o_ref[...] = h
        return

    kk = pl.program_id(1)

    @pl.when(kk == 0)
    def _():
        o_ref[...] = r_ref[...] + part

    @pl.when(kk > 0)
    def _():
        o_ref[...] += part

    if final_norm:
        @pl.when(kk == nk - 1)
        def _():
            h = o_ref[...]
            o_ref[...] = h * _rms_scale(h) * g_ref[...]


def _matmul_residual(a, w, res, *, tm, tk=None, g_final=None, emit_weights=False):
    m, k = a.shape
    n = w.shape[-1]
    tk = k if tk is None else tk
    assert m % tm == 0 and k % tk == 0
    assert not emit_weights or m == tm, "each weight block must be visited exactly once"
    if w.ndim == 3:
        w_spec = pl.BlockSpec((None, tk, n), lambda i, j: (0, j, 0))
    else:
        w_spec = pl.BlockSpec((tk, n), lambda i, j: (j, 0))
    in_specs = [
        pl.BlockSpec((tm, tk), lambda i, j: (i, j)),
        w_spec,
        pl.BlockSpec((tm, n), lambda i, j: (i, 0)),
    ]
    args = [a, w, res]
    if g_final is not None:
        in_specs.append(pl.BlockSpec((1, n), lambda i, j: (0, 0)))
        args.append(g_final.reshape(1, n))
    out_shape = [jax.ShapeDtypeStruct((m, n), F32)]
    out_specs = [pl.BlockSpec((tm, n), lambda i, j: (i, 0))]
    if emit_weights:
        out_shape.append(jax.ShapeDtypeStruct((k, n), BF16))
        out_specs.append(pl.BlockSpec((tk, n), lambda i, j: (j, 0)))
    res = pl.pallas_call(
        functools.partial(_matmul_residual_kernel, final_norm=g_final is not None,
                          emit_weights=emit_weights, nk=k // tk),
        grid=(m // tm, k // tk),
        in_specs=in_specs,
        out_specs=out_specs,
        out_shape=out_shape,
        compiler_params=_params(("parallel", "arbitrary")),
        name=f"matmul_residual_{m}" + ("_final" if g_final is not None else ""),
    )(*args)
    return res if emit_weights else res[0]


def _split_bf16(x):
    hi = x.astype(BF16)
    return hi, (x - hi.astype(F32)).astype(BF16)


def _gla_gates_kernel(glow_ref, wg_ref, bg_ref, bcum_ref, *, chunk):
    rows = glow_ref.shape[0]
    g_hi, g_lo = _split_bf16(glow_ref[...])
    w_hi, w_lo = _split_bf16(wg_ref[...])
    x = jnp.dot(jnp.concatenate([g_hi, g_lo, g_hi], axis=1),
                jnp.concatenate([w_hi, w_hi, w_lo], axis=0),
                preferred_element_type=F32) + bg_ref[...]
    softplus2 = jnp.log2(1.0 + jnp.exp2(jnp.abs(x) * (-LOG2E)))
    logg = jnp.minimum(x, 0.0) * (1.0 / GLA_GATE_TEMP) - softplus2 * (LN2 / GLA_GATE_TEMP)

    span = GATE_CUMSUM_ROWS
    row = lax.broadcasted_iota(jnp.int32, (span, span), 0)
    col = lax.broadcasted_iota(jnp.int32, (span, span), 1)
    same_chunk = (row // chunk) == (col // chunk) if chunk < span else True
    tril = jnp.where((col <= row) & same_chunk, 1.0, 0.0).astype(BF16)
    tril2 = jnp.concatenate([tril, tril], axis=1)
    for i in range(rows // span):
        hi, lo = _split_bf16(logg[i * span:(i + 1) * span])
        bcum_ref[i * span:(i + 1) * span, :] = jnp.dot(
            tril2, jnp.concatenate([hi, lo], axis=0), preferred_element_type=F32)


def _gla_gates(glow, wg, bg, *, chunk, tm):
    m = glow.shape[0]
    assert m % tm == 0 and tm % GATE_CUMSUM_ROWS == 0
    assert GATE_CUMSUM_ROWS % chunk == 0 or chunk % GATE_CUMSUM_ROWS == 0
    assert chunk <= GATE_CUMSUM_ROWS, "cumulative sums do not cross row spans"
    return pl.pallas_call(
        functools.partial(_gla_gates_kernel, chunk=chunk),
        grid=(m // tm,),
        in_specs=[
            pl.BlockSpec((tm, GLA_GATE_RANK), lambda i: (i, 0)),
            pl.BlockSpec((GLA_GATE_RANK, GLA_KEY_DIM), lambda i: (0, 0)),
            pl.BlockSpec((1, GLA_KEY_DIM), lambda i: (0, 0)),
        ],
        out_specs=pl.BlockSpec((tm, GLA_KEY_DIM), lambda i: (i, 0)),
        out_shape=jax.ShapeDtypeStruct((m, GLA_KEY_DIM), F32),
        compiler_params=_params(("parallel",)),
        name=f"gla_gates_{m}",
    )(glow, wg, bg.reshape(1, -1))


def _gla_kernel(q_ref, k_ref, v_ref, r_ref, bcum_ref, gon_ref, *rest, has_init, n_chunks):
    rest = list(rest)
    s0_ref = rest.pop(0) if has_init else None
    o_ref, sfin_ref = rest[:2]
    s_ref = rest[2] if n_chunks > 1 else None
    c = pl.program_id(1)
    nseq, chunk = q_ref.shape[:2]
    chains = [(s, h) for s in range(nseq) for h in range(GLA_HEADS)]

    if n_chunks > 1:
        @pl.when(c == 0)
        def _():
            if has_init:
                s_ref[...] = s0_ref[...]
            else:
                s_ref[...] = jnp.zeros_like(s_ref)

    row = lax.broadcasted_iota(jnp.int32, (chunk, chunk), 0)
    col = lax.broadcasted_iota(jnp.int32, (chunk, chunk), 1)
    causal = col <= row
    heads = range(GLA_HEADS)
    ks = [slice(h * GLA_DK, (h + 1) * GLA_DK) for h in heads]
    vs = [slice(h * GLA_DV, (h + 1) * GLA_DV) for h in heads]

    def state(s, h):
        if n_chunks > 1:
            return s_ref[s, h]
        return s0_ref[s, h] if has_init else jnp.zeros((GLA_DK, GLA_DV), F32)

    q_inter, k_state, scores, decay = {}, {}, {}, {}
    for s, h in chains:
        b = bcum_ref[s, :, ks[h]]
        b_last = b[chunk - 1:chunk, :]
        b_mid = b[chunk // 2 - 1:chunk // 2, :]
        q = q_ref[s, :, ks[h]].astype(F32) * (GLA_DK ** -0.5)
        k = k_ref[s, :, ks[h]].astype(F32)
        q_inter[s, h] = (q * jnp.exp(b)).astype(BF16)
        q_intra = (q * jnp.exp(b - b_mid)).astype(BF16)
        k_intra = (k * jnp.exp(b_mid - b)).astype(BF16)
        k_state[s, h] = (k * jnp.exp(b_last - b)).astype(BF16)
        scores[s, h] = lax.dot_general(q_intra, k_intra, _NT_DIMS,
                                       preferred_element_type=F32)
        decay[s, h] = jnp.exp(jnp.broadcast_to(b_last, (128, GLA_DK))).T

    o = {}
    for s, h in chains:
        sc = jnp.where(causal, scores[s, h], 0.0).astype(BF16)
        v = v_ref[s, :, vs[h]].astype(BF16)
        o[s, h] = (jnp.dot(q_inter[s, h], state(s, h).astype(BF16),
                           preferred_element_type=F32)
                   + jnp.dot(sc, v, preferred_element_type=F32))
        upd = lax.dot_general(k_state[s, h], v, _TN_DIMS, preferred_element_type=F32)
        s_new = (state(s, h) * jnp.concatenate([decay[s, h]] * (GLA_DV // 128), axis=1)
                 + upd)
        if n_chunks > 1:
            s_ref[s, h] = s_new
        else:
            sfin_ref[s, h] = s_new

    scale = {sh: _rms_scale(o[sh]) for sh in chains}
    for s, h in chains:
        r = r_ref[s, :, vs[h]].astype(F32)
        o_ref[s, :, vs[h]] = (o[s, h] * scale[s, h] * gon_ref[:, vs[h]]
                              * _silu(r)).astype(o_ref.dtype)

    if n_chunks > 1:
        @pl.when(c == n_chunks - 1)
        def _():
            sfin_ref[...] = s_ref[...]


def _gla(proj, bcum, gon, *, batch, seq, chunk, out_dtype, s0=None):
    n = seq // chunk
    ns = GLA_SEQS_PER_STEP
    assert batch % ns == 0
    proj = proj.reshape(batch, seq, proj.shape[-1])
    bcum = bcum.reshape(batch, seq, bcum.shape[-1])
    kb, vb, rb = 1, 2 * GLA_KEY_DIM // GLA_VALUE_DIM, 2 * GLA_KEY_DIM // GLA_VALUE_DIM + 1
    in_specs = [
        pl.BlockSpec((ns, chunk, GLA_KEY_DIM), lambda b, c: (b, c, 0)),
        pl.BlockSpec((ns, chunk, GLA_KEY_DIM), lambda b, c: (b, c, kb)),
        pl.BlockSpec((ns, chunk, GLA_VALUE_DIM), lambda b, c: (b, c, vb)),
        pl.BlockSpec((ns, chunk, GLA_VALUE_DIM), lambda b, c: (b, c, rb)),
        pl.BlockSpec((ns, chunk, GLA_KEY_DIM), lambda b, c: (b, c, 0)),
        pl.BlockSpec((1, GLA_VALUE_DIM), lambda b, c: (0, 0)),
    ]
    args = [proj, proj, proj, proj, bcum, gon.reshape(1, -1)]
    state_spec = pl.BlockSpec((ns, GLA_HEADS, GLA_DK, GLA_DV), lambda b, c: (b, 0, 0, 0))
    if s0 is not None:
        in_specs.append(state_spec)
        args.append(s0)
    scratch = [pltpu.VMEM((ns, GLA_HEADS, GLA_DK, GLA_DV), F32)] if n > 1 else []
    o, s_fin = pl.pallas_call(
        functools.partial(_gla_kernel, has_init=s0 is not None, n_chunks=n),
        grid=(batch // ns, n),
        in_specs=in_specs,
        out_specs=[
            pl.BlockSpec((ns, chunk, GLA_VALUE_DIM), lambda b, c: (b, c, 0)),
            state_spec,
        ],
        out_shape=[
            jax.ShapeDtypeStruct((batch, seq, GLA_VALUE_DIM), out_dtype),
            jax.ShapeDtypeStruct((batch, GLA_HEADS, GLA_DK, GLA_DV), F32),
        ],
        scratch_shapes=scratch,
        compiler_params=_params(("parallel", "arbitrary")),
        name=f"gla_chunk{chunk}",
    )(*args)
    return o.reshape(batch * seq, GLA_VALUE_DIM), s_fin


def _alibi_slope(head):
    return 2.0 ** (-8.0 * (head + 1) / SWA_HEADS)


def _attn_prompt_kernel(sinks_ref, q_ref, z_ref, kp_ref, ko_ref, vp_ref, vo_ref, out_ref,
                        bias_ref):
    hd, nkeys = SWA_HEAD_DIM, 2 * WINDOW
    pair_w = 2 * hd
    pairs_per_group = SWA_GROUP // 2
    blk = pl.program_id(1)

    @pl.when((pl.program_id(0) == 0) & (blk == 0))
    def _():
        kj = lax.broadcasted_iota(jnp.int32, (nkeys, WINDOW), 0)
        qi = lax.broadcasted_iota(jnp.int32, (nkeys, WINDOW), 1)
        dist = WINDOW + qi - kj
        ok = (dist >= 0) & (dist <= WINDOW)
        ok_first = ok & (kj >= WINDOW)
        distf = dist.astype(F32)
        for h in range(SWA_HEADS):
            pen = (-_alibi_slope(h) * LOG2E) * distf
            sl = slice((h % 2) * WINDOW, (h % 2 + 1) * WINDOW)
            bias_ref[0, h // 2, :, sl] = jnp.where(ok_first, pen, -jnp.inf)
            bias_ref[1, h // 2, :, sl] = jnp.where(ok, pen, -jnp.inf)

    n_sub = q_ref.shape[0] // WINDOW
    k_rows = jnp.concatenate([kp_ref[...], ko_ref[...]], axis=0)
    v_rows = jnp.concatenate([vp_ref[...], vo_ref[...]], axis=0)
    ones = jnp.ones((16, nkeys), F32)
    lane = lax.broadcasted_iota(jnp.int32, (nkeys, pair_w), 1)
    qlane = lax.broadcasted_iota(jnp.int32, (WINDOW, pair_w), 1)
    qk_scale = (hd ** -0.5) * LOG2E
    quad_pairs = 2
    n_quads = SWA_HEADS // (2 * quad_pairs)
    quads_per_group = pairs_per_group // quad_pairs
    quarter = lax.broadcasted_iota(jnp.int32, (1, 2 * quad_pairs * WINDOW), 1) // WINDOW

    for sub in range(n_sub):
        qrows = slice(sub * WINDOW, (sub + 1) * WINDOW)
        tbl = jnp.minimum(blk, 1) if sub == 0 else 1
        k = k_rows[sub * WINDOW:sub * WINDOW + nkeys]
        v = v_rows[sub * WINDOW:sub * WINDOW + nkeys]
        vt = v.T

        k2, vt1 = [], []
        for g in range(SWA_KV_HEADS):
            kblk = k[:, (g // 2) * pair_w:(g // 2 + 1) * pair_w]
            k_here = jnp.where((lane < hd) if g % 2 == 0 else (lane >= hd), kblk, 0.0)
            k2.append((k_here + pltpu.roll(k_here, hd, axis=1)).astype(BF16))
            vt1.append(jnp.concatenate([vt[g * hd:(g + 1) * hd], ones], axis=0).astype(BF16))

        def scores(quad, k2=k2, qrows=qrows):
            parts = []
            for j in range(quad_pairs):
                col = (quad * quad_pairs + j) * pair_w
                q_pair = q_ref[qrows, col:col + pair_w]
                zero = jnp.zeros_like(q_pair)
                parts += [jnp.where(qlane < hd, q_pair, zero),
                          jnp.where(qlane >= hd, q_pair, zero)]
            return lax.dot_general(k2[quad // quads_per_group], jnp.concatenate(parts, axis=0),
                                   _NT_DIMS, preferred_element_type=F32)

        st_next = scores(0)
        for quad in range(n_quads):
            st = st_next
            if quad + 1 < n_quads:
                st_next = scores(quad + 1)
            pair0 = quad * quad_pairs
            bias = jnp.concatenate([bias_ref[tbl, pair0 + j] for j in range(quad_pairs)], axis=1)
            s2 = st * qk_scale + bias
            sink2 = sinks_ref[2 * pair0]
            for t in range(1, 2 * quad_pairs):
                sink2 = jnp.where(quarter == t, sinks_ref[2 * pair0 + t], sink2)
            sink2 = sink2 * LOG2E
            m = jnp.maximum(jnp.max(s2, axis=0, keepdims=True), sink2)
            p = jnp.exp2(s2 - m).astype(BF16)
            oa = jnp.dot(vt1[quad // quads_per_group], p, preferred_element_type=F32)
            denom = oa[hd:hd + 1] + jnp.exp2(sink2 - m)
            on = oa[0:hd] * (1.0 / denom)
            for j in range(quad_pairs):
                lo = 2 * j * WINDOW
                o_pair = jnp.concatenate([on[:, lo:lo + WINDOW],
                                          on[:, lo + WINDOW:lo + 2 * WINDOW]], axis=0).T
                col = (pair0 + j) * pair_w
                z_pair = z_ref[qrows, col:col + pair_w].astype(F32)
                out_ref[qrows, col:col + pair_w] = (o_pair * _silu(z_pair)).astype(out_ref.dtype)


def _attn_prompt(qz, kv, sinks, *, batch, seq):
    sub = ATTN_BLOCKS_PER_STEP
    tq = sub * WINDOW
    nb = seq // tq
    assert seq % tq == 0
    row = lambda b, i: b * nb + i
    prev = lambda b, i: (b * nb + i) * sub - jnp.minimum(i, 1)
    return pl.pallas_call(
        _attn_prompt_kernel,
        grid=(batch, nb),
        in_specs=[
            pl.BlockSpec(memory_space=pltpu.SMEM),
            pl.BlockSpec((tq, SWA_WIDTH), lambda b, i: (row(b, i), 0)),
            pl.BlockSpec((tq, SWA_WIDTH), lambda b, i: (row(b, i), 1)),
            pl.BlockSpec((WINDOW, SWA_KV_WIDTH), lambda b, i: (prev(b, i), 0)),
            pl.BlockSpec((tq, SWA_KV_WIDTH), lambda b, i: (row(b, i), 0)),
            pl.BlockSpec((WINDOW, SWA_KV_WIDTH), lambda b, i: (prev(b, i), 1)),
            pl.BlockSpec((tq, SWA_KV_WIDTH), lambda b, i: (row(b, i), 1)),
        ],
        out_specs=pl.BlockSpec((tq, SWA_WIDTH), lambda b, i: (row(b, i), 0)),
        out_shape=jax.ShapeDtypeStruct((batch * seq, SWA_WIDTH), BF16),
        scratch_shapes=[pltpu.VMEM((2, SWA_HEADS // 2, 2 * WINDOW, 2 * WINDOW), F32)],
        compiler_params=_params(("arbitrary", "arbitrary")),
        name="attn_prompt",
    )(sinks, qz, qz, kv, kv, kv, kv)


def _attn_sample_kernel(sink_ref, slope_ref, q_ref, z_ref, kn_ref, vn_ref, kc_ref, vc_ref,
                        out_ref, kwin_ref, vwin_ref, *, tq, nb):
    hd, nk = SWA_HEAD_DIM, 2 * WINDOW
    rows = SWA_HEADS * tq
    grows = SWA_GROUP * tq
    seqs = range(nb)
    groups = range(SWA_KV_HEADS)

    lane = lax.broadcasted_iota(jnp.int32, (rows, nk), 1)
    tok = lax.broadcasted_iota(jnp.int32, (rows, nk), 0) % tq
    in_buffer = lane < WINDOW
    dist = jnp.where(in_buffer, WINDOW + tok - lane, (nk - tq) + tok - lane)
    allowed = (dist >= 0) & (dist <= WINDOW) & (in_buffer | (lane >= nk - tq))
    penalty = slope_ref[...] * dist.astype(F32)
    new_lanes = lax.broadcasted_iota(jnp.int32, (hd, WINDOW), 1) >= WINDOW - tq

    def new_rows_t(ref, b):
        x = jnp.concatenate([jnp.zeros((WINDOW - tq, SWA_KV_WIDTH), F32),
                             ref[b * tq:(b + 1) * tq, :]], axis=0)
        xt = [x[:, c * WINDOW:(c + 1) * WINDOW].T for c in range(SWA_KV_WIDTH // WINDOW)]
        per_block = WINDOW // hd
        return [xt[g // per_block][(g % per_block) * hd:(g % per_block + 1) * hd]
                for g in groups]

    k_all, v_all = {}, {}
    for b in seqs:
        kn_t, vn_t = new_rows_t(kn_ref, b), new_rows_t(vn_ref, b)
        for g in groups:
            kc, vc = kc_ref[b, g], vc_ref[b, g]
            k_all[b, g] = jnp.concatenate([kc, kn_t[g]], axis=1).astype(BF16)
            v_all[b, g] = jnp.concatenate([vc, vn_t[g]], axis=1).astype(BF16)
            kwin_ref[b, g] = jnp.where(new_lanes, kn_t[g], pltpu.roll(kc, WINDOW - tq, axis=1))
            vwin_ref[b, g] = jnp.where(new_lanes, vn_t[g], pltpu.roll(vc, WINDOW - tq, axis=1))

    s = []
    for b in seqs:
        q = q_ref[b * tq:(b + 1) * tq, :]
        parts = []
        for g in groups:
            qs = jnp.concatenate([q[:, h * hd:(h + 1) * hd]
                                  for h in range(g * SWA_GROUP, (g + 1) * SWA_GROUP)], axis=0)
            parts.append(jnp.dot(qs.astype(BF16), k_all[b, g], preferred_element_type=F32))
        s.append(jnp.concatenate(parts, axis=0))

    sink = sink_ref[...]
    s = [jnp.where(allowed, sb * (hd ** -0.5) - penalty, -jnp.inf) for sb in s]
    m = [jnp.maximum(jnp.max(sb, axis=-1, keepdims=True), sink) for sb in s]
    p = [jnp.exp(s[b] - m[b]) for b in seqs]
    inv = [1.0 / (jnp.sum(p[b], axis=-1, keepdims=True) + jnp.exp(sink - m[b])) for b in seqs]

    for b in seqs:
        pb = p[b].astype(BF16)
        o = jnp.concatenate(
            [lax.dot_general(pb[g * grows:(g + 1) * grows], v_all[b, g], _NT_DIMS,
                             preferred_element_type=F32) for g in groups], axis=0) * inv[b]
        o = jnp.concatenate([o[h * tq:(h + 1) * tq] for h in range(SWA_HEADS)], axis=1)
        z = z_ref[b * tq:(b + 1) * tq, :]
        out_ref[b * tq:(b + 1) * tq, :] = (o * _silu(z)).astype(out_ref.dtype)


def _attn_sample(qz, kv, cache_k, cache_v, sinks, *, batch, seq):
    nb = SAMPLE_SEQS_PER_STEP
    assert batch % nb == 0
    rows = nb * seq
    sink_col = jnp.repeat(sinks, seq).reshape(SWA_HEADS * seq, 1)
    slope_col = jnp.repeat(jnp.asarray([_alibi_slope(h) for h in range(SWA_HEADS)], F32),
                           seq).reshape(SWA_HEADS * seq, 1)
    col_spec = pl.BlockSpec((SWA_HEADS * seq, 1), lambda i: (0, 0))
    win_spec = pl.BlockSpec((nb, SWA_KV_HEADS, SWA_HEAD_DIM, WINDOW), lambda i: (i, 0, 0, 0))
    win_shape = jax.ShapeDtypeStruct((batch, SWA_KV_HEADS, SWA_HEAD_DIM, WINDOW), F32)
    to_lanes = lambda c: jnp.transpose(c, (0, 2, 3, 1))
    att, k_win, v_win = pl.pallas_call(
        functools.partial(_attn_sample_kernel, tq=seq, nb=nb),
        grid=(batch // nb,),
        in_specs=[
            col_spec,
            col_spec,
            pl.BlockSpec((rows, SWA_WIDTH), lambda i: (i, 0)),
            pl.BlockSpec((rows, SWA_WIDTH), lambda i: (i, 1)),
            pl.BlockSpec((rows, SWA_KV_WIDTH), lambda i: (i, 0)),
            pl.BlockSpec((rows, SWA_KV_WIDTH), lambda i: (i, 1)),
            win_spec,
            win_spec,
        ],
        out_specs=[pl.BlockSpec((rows, SWA_WIDTH), lambda i: (i, 0)), win_spec, win_spec],
        out_shape=[jax.ShapeDtypeStruct((batch * seq, SWA_WIDTH), F32), win_shape, win_shape],
        compiler_params=_params(("parallel",)),
        name="attn_sample",
    )(sink_col, slope_col, qz, qz, kv, kv, to_lanes(cache_k), to_lanes(cache_v))
    return att, jnp.transpose(k_win, (0, 3, 1, 2)), jnp.transpose(v_win, (0, 3, 1, 2))


def _trunk(x, weights, *, batch, seq, row_tile, col_tile, res_tile, chunk, act_dtype,
           gla_init=None, cache=None, emit_weights=False):
    (g_norm_a, w_in_a, w_glow, w_gate_up, b_gate, g_onorm_a, w_out_a,
     g_norm_kv, w_kv, g_norm_b, w_in_b, sinks, w_out_b, g_final) = weights
    h = x.reshape(batch * seq, D_MODEL)
    res_tk = SAMPLE_K_TILE if emit_weights else None

    proj, glow, *w_in_a_bf16 = _norm_matmul(
        h, g_norm_a, w_in_a, tm=row_tile, tn=col_tile, n=GLA_MAIN_COLS, out_dtype=act_dtype,
        w_extra=w_glow, emit_weights=emit_weights, w_transposed=True)
    bcum = _gla_gates(glow, w_gate_up, b_gate, chunk=chunk, tm=min(row_tile, GATE_ROW_TILE))
    o, s_fin = _gla(proj, bcum, g_onorm_a, batch=batch, seq=seq,
                    chunk=chunk, out_dtype=act_dtype, s0=gla_init)
    h, *w_out_a_bf16 = _as_list(_matmul_residual(o, w_out_a, h, tm=res_tile, tk=res_tk,
                                                 emit_weights=emit_weights))

    qz, kv, *w_in_b_bf16 = _norm_matmul(
        h, g_norm_b, w_in_b, tm=row_tile, tn=QZ_COL_TILE, out_dtype=act_dtype,
        w_extra=w_kv, g_extra=g_norm_kv, emit_weights=emit_weights)
    if cache is None:
        att = _attn_prompt(qz, kv, sinks, batch=batch, seq=seq)
        kv_win = kv.reshape(batch, seq, 2 * SWA_KV_WIDTH)[:, seq - WINDOW:]
        kv_win = kv_win.reshape(batch, WINDOW, 2, SWA_KV_HEADS, SWA_HEAD_DIM)
        k_win, v_win = kv_win[:, :, 0], kv_win[:, :, 1]
    else:
        cache_k, cache_v = cache
        att, k_win, v_win = _attn_sample(qz, kv, cache_k, cache_v, sinks,
                                         batch=batch, seq=seq)
    y, *w_out_b_bf16 = _as_list(_matmul_residual(att, w_out_b, h, tm=res_tile, tk=res_tk,
                                                 g_final=g_final,
                                                 emit_weights=emit_weights))
    outs = (y.reshape(batch, seq, D_MODEL), s_fin[None], k_win, v_win)
    return outs, (w_in_a_bf16 + w_in_b_bf16 + w_out_a_bf16 + w_out_b_bf16)


def _as_list(x):
    return list(x) if isinstance(x, (list, tuple)) else [x]


def kernel(x_prompt, x_sample, state_gla, cache_k_win, cache_v_win, g_norm_a, w_in_a,
           w_gate_up, b_gate, g_onorm_a, w_out_a, g_norm_kv, w_kv, g_norm_b, w_in_b, sinks,
           w_out_b, g_final):
    assert w_in_a.shape[0] == 1 and w_in_b.shape[0] == 1, "one GLA layer, one SWA layer"
    assert cache_k_win.shape[1] == WINDOW
    pb, ps, _ = x_prompt.shape
    sb, ss, _ = x_sample.shape

    def weights(w_in_a_any, w_in_b_any, w_out_a_any, w_out_b_any):
        return (
            g_norm_a[0], w_in_a_any, w_in_a_t[0, GLA_MAIN_COLS:, :],
            w_gate_up[0], b_gate[0], g_onorm_a[0], w_out_a_any,
            g_norm_kv, w_kv.astype(BF16),
            g_norm_b[0], w_in_b_any, sinks[0], w_out_b_any, g_final,
        )

    w_in_a_t = jnp.swapaxes(w_in_a, 1, 2)

    (y_s, gla_s, k_s, v_s), bf16_weights = _trunk(
        x_sample, weights(w_in_a_t, w_in_b, w_out_a, w_out_b), batch=sb, seq=ss,
        row_tile=sb * ss,
        col_tile=SAMPLE_COL_TILE, res_tile=sb * ss, chunk=ss, act_dtype=F32,
        gla_init=state_gla.reshape(state_gla.shape[1:]),
        cache=(cache_k_win, cache_v_win), emit_weights=True)
    (y_p, gla_p, k_p, v_p), _ = _trunk(
        x_prompt, weights(*bf16_weights), batch=pb, seq=ps,
        row_tile=PROMPT_ROW_TILE, col_tile=PROMPT_COL_TILE, res_tile=RESIDUAL_ROW_TILE,
        chunk=GLA_PROMPT_CHUNK, act_dtype=BF16)
    return (y_p, y_s, gla_p, gla_s, k_p, v_p, k_s, v_s)
```

```python
import functools

import jax
import jax.numpy as jnp
from jax import lax
from jax.experimental import pallas as pl
from jax.experimental.pallas import tpu as pltpu

F32 = jnp.float32
BF16 = jnp.bfloat16

D_MODEL = 2048
GLA_HEADS = 4
GLA_KEY_DIM = D_MODEL // 2
GLA_VALUE_DIM = D_MODEL
GLA_DK = GLA_KEY_DIM // GLA_HEADS
GLA_DV = GLA_VALUE_DIM // GLA_HEADS
GLA_GATE_RANK = 16
GLA_GATE_TEMP = 16.0
GLA_MAIN_COLS = 2 * GLA_KEY_DIM + 2 * GLA_VALUE_DIM
SWA_HEAD_DIM = 64
SWA_HEADS = D_MODEL // SWA_HEAD_DIM
SWA_KV_HEADS = 4
SWA_GROUP = SWA_HEADS // SWA_KV_HEADS
SWA_WIDTH = SWA_HEADS * SWA_HEAD_DIM
SWA_KV_WIDTH = SWA_KV_HEADS * SWA_HEAD_DIM
WINDOW = 128
RMS_EPS = 1e-6
LOG2E = 1.4426950408889634
LN2 = 0.6931471805599453

V7X_VMEM_LIMIT_BYTES = 61 * 1024 * 1024
GLA_PROMPT_CHUNK = 128
GLA_SEQS_PER_STEP = 4
ATTN_BLOCKS_PER_STEP = 4
SAMPLE_SEQS_PER_STEP = 4
NORM_ROW_CHUNK = 256
GATE_CUMSUM_ROWS = 128
GATE_ROW_TILE = 1024
PROMPT_ROW_TILE = 1024
PROMPT_COL_TILE = 2048
SAMPLE_COL_TILE = 1024
QZ_COL_TILE = 2048
SAMPLE_K_TILE = 512
RESIDUAL_ROW_TILE = 1024

_NT_DIMS = (((1,), (1,)), ((), ()))
_TN_DIMS = (((0,), (0,)), ((), ()))


def _params(semantics):
    return pltpu.CompilerParams(dimension_semantics=semantics,
                                vmem_limit_bytes=V7X_VMEM_LIMIT_BYTES)


def _silu(x):
    return x / (1.0 + jnp.exp(-x))


def _rms_scale(x):
    return lax.rsqrt(jnp.mean(x * x, axis=-1, keepdims=True) + RMS_EPS)


def _norm_matmul_kernel(x_ref, g_ref, w_ref, *rest, has_extra, extra_gain, emit_weights,
                        w_transposed):
    dims = _NT_DIMS if w_transposed else (((1,), (0,)), ((), ()))

    def mm(a, b):
        return lax.dot_general(a, b, dims, preferred_element_type=F32)

    rest = list(rest)
    g2_ref = rest.pop(0) if extra_gain else None
    w2_ref = rest.pop(0) if has_extra else None
    o_ref = rest.pop(0)
    o2_ref = rest.pop(0) if has_extra else None
    wb_ref = rest.pop(0) if emit_weights else None
    (xn_ref,) = rest

    first = pl.program_id(1) == 0

    def weights():
        w = w_ref[...].astype(BF16)
        if emit_weights:
            wb_ref[...] = w
        return w

    @pl.when(first)
    def _():
        tm = x_ref.shape[0]
        rc = min(tm, NORM_ROW_CHUNK)
        w = weights()
        for c in range(tm // rc):
            rows = slice(c * rc, (c + 1) * rc)
            x = x_ref[rows, :]
            xs = x * _rms_scale(x)
            xn = (xs * g_ref[...]).astype(BF16)
            xn_ref[rows, :] = xn
            o_ref[rows, :] = mm(xn, w).astype(o_ref.dtype)
            if has_extra:
                xn2 = (xs * g2_ref[...]).astype(BF16) if extra_gain else xn
                o2_ref[rows, :] = mm(xn2, w2_ref[...].astype(BF16)).astype(o2_ref.dtype)

    @pl.when(jnp.logical_not(first))
    def _():
        o_ref[...] = mm(xn_ref[...], weights()).astype(o_ref.dtype)


def _norm_matmul(x, g, w, *, tm, tn, out_dtype, n=None, w_extra=None, g_extra=None,
                 emit_weights=False, w_transposed=False):
    m, k = x.shape
    n_axis = -2 if w_transposed else -1
    n = w.shape[n_axis] if n is None else n
    assert n % tn == 0 and m % tm == 0
    assert not emit_weights or m == tm, "each weight block must be visited exactly once"
    grid = (m // tm, n // tn)
    w_block = (tn, k) if w_transposed else (k, tn)
    w_index = (lambda i, j: (j, 0)) if w_transposed else (lambda i, j: (0, j))
    if w.ndim == 3:
        w_spec = pl.BlockSpec((None,) + w_block, lambda i, j: (0,) + w_index(i, j))
    else:
        w_spec = pl.BlockSpec(w_block, w_index)
    in_specs = [
        pl.BlockSpec((tm, k), lambda i, j: (i, 0)),
        pl.BlockSpec((1, k), lambda i, j: (0, 0)),
        w_spec,
    ]
    out_shape = [jax.ShapeDtypeStruct((m, n), out_dtype)]
    out_specs = [pl.BlockSpec((tm, tn), lambda i, j: (i, j))]
    args = [x, g.reshape(1, k), w]
    if g_extra is not None:
        in_specs.append(pl.BlockSpec((1, k), lambda i, j: (0, 0)))
        args.append(g_extra.reshape(1, k))
    if w_extra is not None:
        n2 = w_extra.shape[n_axis]
        in_specs.append(pl.BlockSpec(w_extra.shape, lambda i, j: (0, 0)))
        out_shape.append(jax.ShapeDtypeStruct((m, n2), F32))
        out_specs.append(pl.BlockSpec((tm, n2), lambda i, j: (i, 0)))
        args.append(w_extra)
    if emit_weights:
        out_shape.append(jax.ShapeDtypeStruct((n, k) if w_transposed else (k, n), BF16))
        out_specs.append(pl.BlockSpec(w_block, w_index))
    res = pl.pallas_call(
        functools.partial(_norm_matmul_kernel, has_extra=w_extra is not None,
                          extra_gain=g_extra is not None, emit_weights=emit_weights,
                          w_transposed=w_transposed),
        grid=grid,
        in_specs=in_specs,
        out_specs=out_specs,
        out_shape=out_shape,
        scratch_shapes=[pltpu.VMEM((tm, k), BF16)],
        compiler_params=_params(("parallel", "arbitrary")),
        name=f"norm_matmul_{m}x{n}",
    )(*args)
    return res if len(res) > 1 else res[0]


def _matmul_residual_kernel(a_ref, w_ref, r_ref, *rest, final_norm, emit_weights, nk):
    rest = list(rest)
    g_ref = rest.pop(0) if final_norm else None
    o_ref = rest.pop(0)
    w = w_ref[...].astype(BF16)
    if emit_weights:
        rest[0][...] = w
    part = jnp.dot(a_ref[...].astype(BF16), w, preferred_element_type=F32)

    if nk == 1:
        h = r_ref[...] + part
        if final_norm:
            h = h * _rms_scale(h) * g_ref[...]
        o_ref[...] = h
        return

    kk = pl.program_id(1)

    @pl.when(kk == 0)
    def _():
        o_ref[...] = r_ref[...] + part

    @pl.when(kk > 0)
    def _():
        o_ref[...] += part

    if final_norm:
        @pl.when(kk == nk - 1)
        def _():
            h = o_ref[...]
            o_ref[...] = h * _rms_scale(h) * g_ref[...]


def _matmul_residual(a, w, res, *, tm, tk=None, g_final=None, emit_weights=False):
    m, k = a.shape
    n = w.shape[-1]
    tk = k if tk is None else tk
    assert m % tm == 0 and k % tk == 0
    assert not emit_weights or m == tm, "each weight block must be visited exactly once"
    if w.ndim == 3:
        w_spec = pl.BlockSpec((None, tk, n), lambda i, j: (0, j, 0))
    else:
        w_spec = pl.BlockSpec((tk, n), lambda i, j: (j, 0))
    in_specs = [
        pl.BlockSpec((tm, tk), lambda i, j: (i, j)),
        w_spec,
        pl.BlockSpec((tm, n), lambda i, j: (i, 0)),
    ]
    args = [a, w, res]
    if g_final is not None:
        in_specs.append(pl.BlockSpec((1, n), lambda i, j: (0, 0)))
        args.append(g_final.reshape(1, n))
    out_shape = [jax.ShapeDtypeStruct((m, n), F32)]
    out_specs = [pl.BlockSpec((tm, n), lambda i, j: (i, 0))]
    if emit_weights:
        out_shape.append(jax.ShapeDtypeStruct((k, n), BF16))
        out_specs.append(pl.BlockSpec((tk, n), lambda i, j: (j, 0)))
    res = pl.pallas_call(
        functools.partial(_matmul_residual_kernel, final_norm=g_final is not None,
                          emit_weights=emit_weights, nk=k // tk),
        grid=(m // tm, k // tk),
        in_specs=in_specs,
        out_specs=out_specs,
        out_shape=out_shape,
        compiler_params=_params(("parallel", "arbitrary")),
        name=f"matmul_residual_{m}" + ("_final" if g_final is not None else ""),
    )(*args)
    return res if emit_weights else res[0]


def _split_bf16(x):
    hi = x.astype(BF16)
    return hi, (x - hi.astype(F32)).astype(BF16)


def _gla_gates_kernel(glow_ref, wg_ref, bg_ref, bcum_ref, *, chunk):
    rows = glow_ref.shape[0]
    g_hi, g_lo = _split_bf16(glow_ref[...])
    w_hi, w_lo = _split_bf16(wg_ref[...])
    x = jnp.dot(jnp.concatenate([g_hi, g_lo, g_hi], axis=1),
                jnp.concatenate([w_hi, w_hi, w_lo], axis=0),
                preferred_element_type=F32) + bg_ref[...]
    softplus2 = jnp.log2(1.0 + jnp.exp2(jnp.abs(x) * (-LOG2E)))
    logg = jnp.minimum(x, 0.0) * (1.0 / GLA_GATE_TEMP) - softplus2 * (LN2 / GLA_GATE_TEMP)

    span = GATE_CUMSUM_ROWS
    row = lax.broadcasted_iota(jnp.int32, (span, span), 0)
    col = lax.broadcasted_iota(jnp.int32, (span, span), 1)
    same_chunk = (row // chunk) == (col // chunk) if chunk < span else True
    tril = jnp.where((col <= row) & same_chunk, 1.0, 0.0).astype(BF16)
    tril2 = jnp.concatenate([tril, tril], axis=1)
    for i in range(rows // span):
        hi, lo = _split_bf16(logg[i * span:(i + 1) * span])
        bcum_ref[i * span:(i + 1) * span, :] = jnp.dot(
            tril2, jnp.concatenate([hi, lo], axis=0), preferred_element_type=F32)


def _gla_gates(glow, wg, bg, *, chunk, tm):
    m = glow.shape[0]
    assert m % tm == 0 and tm % GATE_CUMSUM_ROWS == 0
    assert GATE_CUMSUM_ROWS % chunk == 0 or chunk % GATE_CUMSUM_ROWS == 0
    assert chunk <= GATE_CUMSUM_ROWS, "cumulative sums do not cross row spans"
    return pl.pallas_call(
        functools.partial(_gla_gates_kernel, chunk=chunk),
        grid=(m // tm,),
        in_specs=[
            pl.BlockSpec((tm, GLA_GATE_RANK), lambda i: (i, 0)),
            pl.BlockSpec((GLA_GATE_RANK, GLA_KEY_DIM), lambda i: (0, 0)),
            pl.BlockSpec((1, GLA_KEY_DIM), lambda i: (0, 0)),
        ],
        out_specs=pl.BlockSpec((tm, GLA_KEY_DIM), lambda i: (i, 0)),
        out_shape=jax.ShapeDtypeStruct((m, GLA_KEY_DIM), F32),
        compiler_params=_params(("parallel",)),
        name=f"gla_gates_{m}",
    )(glow, wg, bg.reshape(1, -1))


def _gla_kernel(q_ref, k_ref, v_ref, r_ref, bcum_ref, gon_ref, *rest, has_init, n_chunks):
    rest = list(rest)
    s0_ref = rest.pop(0) if has_init else None
    o_ref, sfin_ref = rest[:2]
    s_ref = rest[2] if n_chunks > 1 else None
    c = pl.program_id(1)
    nseq, chunk = q_ref.shape[:2]
    chains = [(s, h) for s in range(nseq) for h in range(GLA_HEADS)]

    if n_chunks > 1:
        @pl.when(c == 0)
        def _():
            if has_init:
                s_ref[...] = s0_ref[...]
            else:
                s_ref[...] = jnp.zeros_like(s_ref)

    row = lax.broadcasted_iota(jnp.int32, (chunk, chunk), 0)
    col = lax.broadcasted_iota(jnp.int32, (chunk, chunk), 1)
    causal = col <= row
    heads = range(GLA_HEADS)
    ks = [slice(h * GLA_DK, (h + 1) * GLA_DK) for h in heads]
    vs = [slice(h * GLA_DV, (h + 1) * GLA_DV) for h in heads]

    def state(s, h):
        if n_chunks > 1:
            return s_ref[s, h]
        return s0_ref[s, h] if has_init else jnp.zeros((GLA_DK, GLA_DV), F32)

    q_inter, k_state, scores, decay = {}, {}, {}, {}
    for s, h in chains:
        b = bcum_ref[s, :, ks[h]]
        b_last = b[chunk - 1:chunk, :]
        b_mid = b[chunk // 2 - 1:chunk // 2, :]
        q = q_ref[s, :, ks[h]].astype(F32) * (GLA_DK ** -0.5)
        k = k_ref[s, :, ks[h]].astype(F32)
        q_inter[s, h] = (q * jnp.exp(b)).astype(BF16)
        q_intra = (q * jnp.exp(b - b_mid)).astype(BF16)
        k_intra = (k * jnp.exp(b_mid - b)).astype(BF16)
        k_state[s, h] = (k * jnp.exp(b_last - b)).astype(BF16)
        scores[s, h] = lax.dot_general(q_intra, k_intra, _NT_DIMS,
                                       preferred_element_type=F32)
        decay[s, h] = jnp.exp(jnp.broadcast_to(b_last, (128, GLA_DK))).T

    o = {}
    for s, h in chains:
        sc = jnp.where(causal, scores[s, h], 0.0).astype(BF16)
        v = v_ref[s, :, vs[h]].astype(BF16)
        o[s, h] = (jnp.dot(q_inter[s, h], state(s, h).astype(BF16),
                           preferred_element_type=F32)
                   + jnp.dot(sc, v, preferred_element_type=F32))
        upd = lax.dot_general(k_state[s, h], v, _TN_DIMS, preferred_element_type=F32)
        s_new = (state(s, h) * jnp.concatenate([decay[s, h]] * (GLA_DV // 128), axis=1)
                 + upd)
        if n_chunks > 1:
            s_ref[s, h] = s_new
        else:
            sfin_ref[s, h] = s_new

    scale = {sh: _rms_scale(o[sh]) for sh in chains}
    for s, h in chains:
        r = r_ref[s, :, vs[h]].astype(F32)
        o_ref[s, :, vs[h]] = (o[s, h] * scale[s, h] * gon_ref[:, vs[h]]
                              * _silu(r)).astype(o_ref.dtype)

    if n_chunks > 1:
        @pl.when(c == n_chunks - 1)
        def _():
            sfin_ref[...] = s_ref[...]


def _gla(proj, bcum, gon, *, batch, seq, chunk, out_dtype, s0=None):
    n = seq // chunk
    ns = GLA_SEQS_PER_STEP
    assert batch % ns == 0
    proj = proj.reshape(batch, seq, proj.shape[-1])
    bcum = bcum.reshape(batch, seq, bcum.shape[-1])
    kb, vb, rb = 1, 2 * GLA_KEY_DIM // GLA_VALUE_DIM, 2 * GLA_KEY_DIM // GLA_VALUE_DIM + 1
    in_specs = [
        pl.BlockSpec((ns, chunk, GLA_KEY_DIM), lambda b, c: (b, c, 0)),
        pl.BlockSpec((ns, chunk, GLA_KEY_DIM), lambda b, c: (b, c, kb)),
        pl.BlockSpec((ns, chunk, GLA_VALUE_DIM), lambda b, c: (b, c, vb)),
        pl.BlockSpec((ns, chunk, GLA_VALUE_DIM), lambda b, c: (b, c, rb)),
        pl.BlockSpec((ns, chunk, GLA_KEY_DIM), lambda b, c: (b, c, 0)),
        pl.BlockSpec((1, GLA_VALUE_DIM), lambda b, c: (0, 0)),
    ]
    args = [proj, proj, proj, proj, bcum, gon.reshape(1, -1)]
    state_spec = pl.BlockSpec((ns, GLA_HEADS, GLA_DK, GLA_DV), lambda b, c: (b, 0, 0, 0))
    if s0 is not None:
        in_specs.append(state_spec)
        args.append(s0)
    scratch = [pltpu.VMEM((ns, GLA_HEADS, GLA_DK, GLA_DV), F32)] if n > 1 else []
    o, s_fin = pl.pallas_call(
        functools.partial(_gla_kernel, has_init=s0 is not None, n_chunks=n),
        grid=(batch // ns, n),
        in_specs=in_specs,
        out_specs=[
            pl.BlockSpec((ns, chunk, GLA_VALUE_DIM), lambda b, c: (b, c, 0)),
            state_spec,
        ],
        out_shape=[
            jax.ShapeDtypeStruct((batch, seq, GLA_VALUE_DIM), out_dtype),
            jax.ShapeDtypeStruct((batch, GLA_HEADS, GLA_DK, GLA_DV), F32),
        ],
        scratch_shapes=scratch,
        compiler_params=_params(("parallel", "arbitrary")),
        name=f"gla_chunk{chunk}",
    )(*args)
    return o.reshape(batch * seq, GLA_VALUE_DIM), s_fin


def _alibi_slope(head):
    return 2.0 ** (-8.0 * (head + 1) / SWA_HEADS)


def _attn_prompt_kernel(sinks_ref, q_ref, z_ref, kp_ref, ko_ref, vp_ref, vo_ref, out_ref,
                        bias_ref):
    hd, nkeys = SWA_HEAD_DIM, 2 * WINDOW
    pair_w = 2 * hd
    pairs_per_group = SWA_GROUP // 2
    blk = pl.program_id(1)

    @pl.when((pl.program_id(0) == 0) & (blk == 0))
    def _():
        kj = lax.broadcasted_iota(jnp.int32, (nkeys, WINDOW), 0)
        qi = lax.broadcasted_iota(jnp.int32, (nkeys, WINDOW), 1)
        dist = WINDOW + qi - kj
        ok = (dist >= 0) & (dist <= WINDOW)
        ok_first = ok & (kj >= WINDOW)
        distf = dist.astype(F32)
        for h in range(SWA_HEADS):
            pen = (-_alibi_slope(h) * LOG2E) * distf
            sl = slice((h % 2) * WINDOW, (h % 2 + 1) * WINDOW)
            bias_ref[0, h // 2, :, sl] = jnp.where(ok_first, pen, -jnp.inf)
            bias_ref[1, h // 2, :, sl] = jnp.where(ok, pen, -jnp.inf)

    n_sub = q_ref.shape[0] // WINDOW
    k_rows = jnp.concatenate([kp_ref[...], ko_ref[...]], axis=0)
    v_rows = jnp.concatenate([vp_ref[...], vo_ref[...]], axis=0)
    ones = jnp.ones((16, nkeys), F32)
    lane = lax.broadcasted_iota(jnp.int32, (nkeys, pair_w), 1)
    qlane = lax.broadcasted_iota(jnp.int32, (WINDOW, pair_w), 1)
    qk_scale = (hd ** -0.5) * LOG2E
    quad_pairs = 2
    n_quads = SWA_HEADS // (2 * quad_pairs)
    quads_per_group = pairs_per_group // quad_pairs
    quarter = lax.broadcasted_iota(jnp.int32, (1, 2 * quad_pairs * WINDOW), 1) // WINDOW

    for sub in range(n_sub):
        qrows = slice(sub * WINDOW, (sub + 1) * WINDOW)
        tbl = jnp.minimum(blk, 1) if sub == 0 else 1
        k = k_rows[sub * WINDOW:sub * WINDOW + nkeys]
        v = v_rows[sub * WINDOW:sub * WINDOW + nkeys]
        vt = v.T

        k2, vt1 = [], []
        for g in range(SWA_KV_HEADS):
            kblk = k[:, (g // 2) * pair_w:(g // 2 + 1) * pair_w]
            k_here = jnp.where((lane < hd) if g % 2 == 0 else (lane >= hd), kblk, 0.0)
            k2.append((k_here + pltpu.roll(k_here, hd, axis=1)).astype(BF16))
            vt1.append(jnp.concatenate([vt[g * hd:(g + 1) * hd], ones], axis=0).astype(BF16))

        def scores(quad, k2=k2, qrows=qrows):
            parts = []
            for j in range(quad_pairs):
                col = (quad * quad_pairs + j) * pair_w
                q_pair = q_ref[qrows, col:col + pair_w]
                zero = jnp.zeros_like(q_pair)
                parts += [jnp.where(qlane < hd, q_pair, zero),
                          jnp.where(qlane >= hd, q_pair, zero)]
            return lax.dot_general(k2[quad // quads_per_group], jnp.concatenate(parts, axis=0),
                                   _NT_DIMS, preferred_element_type=F32)

        st_next = scores(0)
        for quad in range(n_quads):
            st = st_next
            if quad + 1 < n_quads:
                st_next = scores(quad + 1)
            pair0 = quad * quad_pairs
            bias = jnp.concatenate([bias_ref[tbl, pair0 + j] for j in range(quad_pairs)], axis=1)
            s2 = st * qk_scale + bias
            sink2 = sinks_ref[2 * pair0]
            for t in range(1, 2 * quad_pairs):
                sink2 = jnp.where(quarter == t, sinks_ref[2 * pair0 + t], sink2)
            sink2 = sink2 * LOG2E
            m = jnp.maximum(jnp.max(s2, axis=0, keepdims=True), sink2)
            p = jnp.exp2(s2 - m).astype(BF16)
            oa = jnp.dot(vt1[quad // quads_per_group], p, preferred_element_type=F32)
            denom = oa[hd:hd + 1] + jnp.exp2(sink2 - m)
            on = oa[0:hd] * (1.0 / denom)
            for j in range(quad_pairs):
                lo = 2 * j * WINDOW
                o_pair = jnp.concatenate([on[:, lo:lo + WINDOW],
                                          on[:, lo + WINDOW:lo + 2 * WINDOW]], axis=0).T
                col = (pair0 + j) * pair_w
                z_pair = z_ref[qrows, col:col + pair_w].astype(F32)
                out_ref[qrows, col:col + pair_w] = (o_pair * _silu(z_pair)).astype(out_ref.dtype)


def _attn_prompt(qz, kv, sinks, *, batch, seq):
    sub = ATTN_BLOCKS_PER_STEP
    tq = sub * WINDOW
    nb = seq // tq
    assert seq % tq == 0
    row = lambda b, i: b * nb + i
    prev = lambda b, i: (b * nb + i) * sub - jnp.minimum(i, 1)
    return pl.pallas_call(
        _attn_prompt_kernel,
        grid=(batch, nb),
        in_specs=[
            pl.BlockSpec(memory_space=pltpu.SMEM),
            pl.BlockSpec((tq, SWA_WIDTH), lambda b, i: (row(b, i), 0)),
            pl.BlockSpec((tq, SWA_WIDTH), lambda b, i: (row(b, i), 1)),
            pl.BlockSpec((WINDOW, SWA_KV_WIDTH), lambda b, i: (prev(b, i), 0)),
            pl.BlockSpec((tq, SWA_KV_WIDTH), lambda b, i: (row(b, i), 0)),
            pl.BlockSpec((WINDOW, SWA_KV_WIDTH), lambda b, i: (prev(b, i), 1)),
            pl.BlockSpec((tq, SWA_KV_WIDTH), lambda b, i: (row(b, i), 1)),
        ],
        out_specs=pl.BlockSpec((tq, SWA_WIDTH), lambda b, i: (row(b, i), 0)),
        out_shape=jax.ShapeDtypeStruct((batch * seq, SWA_WIDTH), BF16),
        scratch_shapes=[pltpu.VMEM((2, SWA_HEADS // 2, 2 * WINDOW, 2 * WINDOW), F32)],
        compiler_params=_params(("arbitrary", "arbitrary")),
        name="attn_prompt",
    )(sinks, qz, qz, kv, kv, kv, kv)


def _attn_sample_kernel(sink_ref, slope_ref, q_ref, z_ref, kn_ref, vn_ref, kc_ref, vc_ref,
                        out_ref, kwin_ref, vwin_ref, *, tq, nb):
    hd, nk = SWA_HEAD_DIM, 2 * WINDOW
    rows = SWA_HEADS * tq
    grows = SWA_GROUP * tq
    seqs = range(nb)
    groups = range(SWA_KV_HEADS)

    lane = lax.broadcasted_iota(jnp.int32, (rows, nk), 1)
    tok = lax.broadcasted_iota(jnp.int32, (rows, nk), 0) % tq
    in_buffer = lane < WINDOW
    dist = jnp.where(in_buffer, WINDOW + tok - lane, (nk - tq) + tok - lane)
    allowed = (dist >= 0) & (dist <= WINDOW) & (in_buffer | (lane >= nk - tq))
    penalty = slope_ref[...] * dist.astype(F32)
    new_lanes = lax.broadcasted_iota(jnp.int32, (hd, WINDOW), 1) >= WINDOW - tq

    def new_rows_t(ref, b):
        x = jnp.concatenate([jnp.zeros((WINDOW - tq, SWA_KV_WIDTH), F32),
                             ref[b * tq:(b + 1) * tq, :]], axis=0)
        xt = [x[:, c * WINDOW:(c + 1) * WINDOW].T for c in range(SWA_KV_WIDTH // WINDOW)]
        per_block = WINDOW // hd
        return [xt[g // per_block][(g % per_block) * hd:(g % per_block + 1) * hd]
                for g in groups]

    k_all, v_all = {}, {}
    for b in seqs:
        kn_t, vn_t = new_rows_t(kn_ref, b), new_rows_t(vn_ref, b)
        for g in groups:
            kc, vc = kc_ref[b, g], vc_ref[b, g]
            k_all[b, g] = jnp.concatenate([kc, kn_t[g]], axis=1).astype(BF16)
            v_all[b, g] = jnp.concatenate([vc, vn_t[g]], axis=1).astype(BF16)
            kwin_ref[b, g] = jnp.where(new_lanes, kn_t[g], pltpu.roll(kc, WINDOW - tq, axis=1))
            vwin_ref[b, g] = jnp.where(new_lanes, vn_t[g], pltpu.roll(vc, WINDOW - tq, axis=1))

    s = []
    for b in seqs:
        q = q_ref[b * tq:(b + 1) * tq, :]
        parts = []
        for g in groups:
            qs = jnp.concatenate([q[:, h * hd:(h + 1) * hd]
                                  for h in range(g * SWA_GROUP, (g + 1) * SWA_GROUP)], axis=0)
            parts.append(jnp.dot(qs.astype(BF16), k_all[b, g], preferred_element_type=F32))
        s.append(jnp.concatenate(parts, axis=0))

    sink = sink_ref[...]
    s = [jnp.where(allowed, sb * (hd ** -0.5) - penalty, -jnp.inf) for sb in s]
    m = [jnp.maximum(jnp.max(sb, axis=-1, keepdims=True), sink) for sb in s]
    p = [jnp.exp(s[b] - m[b]) for b in seqs]
    inv = [1.0 / (jnp.sum(p[b], axis=-1, keepdims=True) + jnp.exp(sink - m[b])) for b in seqs]

    for b in seqs:
        pb = p[b].astype(BF16)
        o = jnp.concatenate(
            [lax.dot_general(pb[g * grows:(g + 1) * grows], v_all[b, g], _NT_DIMS,
                             preferred_element_type=F32) for g in groups], axis=0) * inv[b]
        o = jnp.concatenate([o[h * tq:(h + 1) * tq] for h in range(SWA_HEADS)], axis=1)
        z = z_ref[b * tq:(b + 1) * tq, :]
        out_ref[b * tq:(b + 1) * tq, :] = (o * _silu(z)).astype(out_ref.dtype)


def _attn_sample(qz, kv, cache_k, cache_v, sinks, *, batch, seq):
    nb = SAMPLE_SEQS_PER_STEP
    assert batch % nb == 0
    rows = nb * seq
    sink_col = jnp.repeat(sinks, seq).reshape(SWA_HEADS * seq, 1)
    slope_col = jnp.repeat(jnp.asarray([_alibi_slope(h) for h in range(SWA_HEADS)], F32),
                           seq).reshape(SWA_HEADS * seq, 1)
    col_spec = pl.BlockSpec((SWA_HEADS * seq, 1), lambda i: (0, 0))
    win_spec = pl.BlockSpec((nb, SWA_KV_HEADS, SWA_HEAD_DIM, WINDOW), lambda i: (i, 0, 0, 0))
    win_shape = jax.ShapeDtypeStruct((batch, SWA_KV_HEADS, SWA_HEAD_DIM, WINDOW), F32)
    to_lanes = lambda c: jnp.transpose(c, (0, 2, 3, 1))
    att, k_win, v_win = pl.pallas_call(
        functools.partial(_attn_sample_kernel, tq=seq, nb=nb),
        grid=(batch // nb,),
        in_specs=[
            col_spec,
            col_spec,
            pl.BlockSpec((rows, SWA_WIDTH), lambda i: (i, 0)),
            pl.BlockSpec((rows, SWA_WIDTH), lambda i: (i, 1)),
            pl.BlockSpec((rows, SWA_KV_WIDTH), lambda i: (i, 0)),
            pl.BlockSpec((rows, SWA_KV_WIDTH), lambda i: (i, 1)),
            win_spec,
            win_spec,
        ],
        out_specs=[pl.BlockSpec((rows, SWA_WIDTH), lambda i: (i, 0)), win_spec, win_spec],
        out_shape=[jax.ShapeDtypeStruct((batch * seq, SWA_WIDTH), F32), win_shape, win_shape],
        compiler_params=_params(("parallel",)),
        name="attn_sample",
    )(sink_col, slope_col, qz, qz, kv, kv, to_lanes(cache_k), to_lanes(cache_v))
    return att, jnp.transpose(k_win, (0, 3, 1, 2)), jnp.transpose(v_win, (0, 3, 1, 2))


def _trunk(x, weights, *, batch, seq, row_tile, col_tile, res_tile, chunk, act_dtype,
           gla_init=None, cache=None, emit_weights=False):
    (g_norm_a, w_in_a, w_glow, w_gate_up, b_gate, g_onorm_a, w_out_a,
     g_norm_kv, w_kv, g_norm_b, w_in_b, sinks, w_out_b, g_final) = weights
    h = x.reshape(batch * seq, D_MODEL)
    res_tk = SAMPLE_K_TILE if emit_weights else None

    proj, glow, *w_in_a_bf16 = _norm_matmul(
        h, g_norm_a, w_in_a, tm=row_tile, tn=col_tile, n=GLA_MAIN_COLS, out_dtype=act_dtype,
        w_extra=w_glow, emit_weights=emit_weights, w_transposed=True)
    bcum = _gla_gates(glow, w_gate_up, b_gate, chunk=chunk, tm=min(row_tile, GATE_ROW_TILE))
    o, s_fin = _gla(proj, bcum, g_onorm_a, batch=batch, seq=seq,
                    chunk=chunk, out_dtype=act_dtype, s0=gla_init)
    h, *w_out_a_bf16 = _as_list(_matmul_residual(o, w_out_a, h, tm=res_tile, tk=res_tk,
                                                 emit_weights=emit_weights))

    qz, kv, *w_in_b_bf16 = _norm_matmul(
        h, g_norm_b, w_in_b, tm=row_tile, tn=QZ_COL_TILE, out_dtype=act_dtype,
        w_extra=w_kv, g_extra=g_norm_kv, emit_weights=emit_weights)
    if cache is None:
        att = _attn_prompt(qz, kv, sinks, batch=batch, seq=seq)
        kv_win = kv.reshape(batch, seq, 2 * SWA_KV_WIDTH)[:, seq - WINDOW:]
        kv_win = kv_win.reshape(batch, WINDOW, 2, SWA_KV_HEADS, SWA_HEAD_DIM)
        k_win, v_win = kv_win[:, :, 0], kv_win[:, :, 1]
    else:
        cache_k, cache_v = cache
        att, k_win, v_win = _attn_sample(qz, kv, cache_k, cache_v, sinks,
                                         batch=batch, seq=seq)
    y, *w_out_b_bf16 = _as_list(_matmul_residual(att, w_out_b, h, tm=res_tile, tk=res_tk,
                                                 g_final=g_final,
                                                 emit_weights=emit_weights))
    outs = (y.reshape(batch, seq, D_MODEL), s_fin[None], k_win, v_win)
    return outs, (w_in_a_bf16 + w_in_b_bf16 + w_out_a_bf16 + w_out_b_bf16)


def _as_list(x):
    return list(x) if isinstance(x, (list, tuple)) else [x]


def kernel(x_prompt, x_sample, state_gla, cache_k_win, cache_v_win, g_norm_a, w_in_a,
           w_gate_up, b_gate, g_onorm_a, w_out_a, g_norm_kv, w_kv, g_norm_b, w_in_b, sinks,
           w_out_b, g_final):
    assert w_in_a.shape[0] == 1 and w_in_b.shape[0] == 1, "one GLA layer, one SWA layer"
    assert cache_k_win.shape[1] == WINDOW
    pb, ps, _ = x_prompt.shape
    sb, ss, _ = x_sample.shape

    def weights(w_in_a_any, w_in_b_any, w_out_a_any, w_out_b_any):
        return (
            g_norm_a[0], w_in_a_any, w_in_a_t[0, GLA_MAIN_COLS:, :],
            w_gate_up[0], b_gate[0], g_onorm_a[0], w_out_a_any,
            g_norm_kv, w_kv.astype(BF16),
            g_norm_b[0], w_in_b_any, sinks[0], w_out_b_any, g_final,
        )

    w_in_a_t = jnp.swapaxes(w_in_a, 1, 2)

    (y_s, gla_s, k_s, v_s), bf16_weights = _trunk(
        x_sample, weights(w_in_a_t, w_in_b, w_out_a, w_out_b), batch=sb, seq=ss,
        row_tile=sb * ss,
        col_tile=SAMPLE_COL_TILE, res_tile=sb * ss, chunk=ss, act_dtype=F32,
        gla_init=state_gla.reshape(state_gla.shape[1:]),
        cache=(cache_k_win, cache_v_win), emit_weights=True)
    (y_p, gla_p, k_p, v_p), _ = _trunk(
        x_prompt, weights(*bf16_weights), batch=pb, seq=ps,
        row_tile=PROMPT_ROW_TILE, col_tile=PROMPT_COL_TILE, res_tile=RESIDUAL_ROW_TILE,
        chunk=GLA_PROMPT_CHUNK, act_dtype=BF16)
    return (y_p, y_s, gla_p, gla_s, k_p, v_p, k_s, v_s)
```

```python
import functools

import jax
import jax.numpy as jnp
from jax import lax
from jax.experimental import pallas as pl
from jax.experimental.pallas import tpu as pltpu

F32 = jnp.float32
BF16 = jnp.bfloat16

D_MODEL = 2048
GLA_HEADS = 4
GLA_KEY_DIM = D_MODEL // 2
GLA_VALUE_DIM = D_MODEL
GLA_DK = GLA_KEY_DIM // GLA_HEADS
GLA_DV = GLA_VALUE_DIM // GLA_HEADS
GLA_GATE_RANK = 16
GLA_GATE_TEMP = 16.0
GLA_MAIN_COLS = 2 * GLA_KEY_DIM + 2 * GLA_VALUE_DIM
SWA_HEAD_DIM = 64
SWA_HEADS = D_MODEL // SWA_HEAD_DIM
SWA_KV_HEADS = 4
SWA_GROUP = SWA_HEADS // SWA_KV_HEADS
SWA_WIDTH = SWA_HEADS * SWA_HEAD_DIM
SWA_KV_WIDTH = SWA_KV_HEADS * SWA_HEAD_DIM
WINDOW = 128
RMS_EPS = 1e-6
LOG2E = 1.4426950408889634

V7X_VMEM_LIMIT_BYTES = 61 * 1024 * 1024
GLA_PROMPT_CHUNK = 128
GLA_SEQS_PER_STEP = 4
ATTN_BLOCKS_PER_STEP = 4
SAMPLE_SEQS_PER_STEP = 4
NORM_ROW_CHUNK = 256
GATE_CUMSUM_ROWS = 128
GATE_ROW_TILE = 1024
PROMPT_ROW_TILE = 1024
PROMPT_COL_TILE = 2048
SAMPLE_COL_TILE = 1024
QZ_COL_TILE = 2048
SAMPLE_K_TILE = 512
RESIDUAL_ROW_TILE = 512

_NT_DIMS = (((1,), (1,)), ((), ()))
_TN_DIMS = (((0,), (0,)), ((), ()))


def _params(semantics):
    return pltpu.CompilerParams(dimension_semantics=semantics,
                                vmem_limit_bytes=V7X_VMEM_LIMIT_BYTES)


def _silu(x):
    return x / (1.0 + jnp.exp(-x))


def _rms_scale(x):
    return lax.rsqrt(jnp.mean(x * x, axis=-1, keepdims=True) + RMS_EPS)


def _norm_matmul_kernel(x_ref, g_ref, w_ref, *rest, has_extra, extra_gain, emit_weights,
                        w_transposed):
    dims = _NT_DIMS if w_transposed else (((1,), (0,)), ((), ()))

    def mm(a, b):
        return lax.dot_general(a, b, dims, preferred_element_type=F32)

    rest = list(rest)
    g2_ref = rest.pop(0) if extra_gain else None
    w2_ref = rest.pop(0) if has_extra else None
    o_ref = rest.pop(0)
    o2_ref = rest.pop(0) if has_extra else None
    wb_ref = rest.pop(0) if emit_weights else None
    (xn_ref,) = rest

    first = pl.program_id(1) == 0

    def weights():
        w = w_ref[...].astype(BF16)
        if emit_weights:
            wb_ref[...] = w
        return w

    @pl.when(first)
    def _():
        tm = x_ref.shape[0]
        rc = min(tm, NORM_ROW_CHUNK)
        w = weights()
        for c in range(tm // rc):
            rows = slice(c * rc, (c + 1) * rc)
            x = x_ref[rows, :]
            xs = x * _rms_scale(x)
            xn = (xs * g_ref[...]).astype(BF16)
            xn_ref[rows, :] = xn
            o_ref[rows, :] = mm(xn, w).astype(o_ref.dtype)
            if has_extra:
                xn2 = (xs * g2_ref[...]).astype(BF16) if extra_gain else xn
                o2_ref[rows, :] = mm(xn2, w2_ref[...].astype(BF16)).astype(o2_ref.dtype)

    @pl.when(jnp.logical_not(first))
    def _():
        o_ref[...] = mm(xn_ref[...], weights()).astype(o_ref.dtype)


def _norm_matmul(x, g, w, *, tm, tn, out_dtype, n=None, w_extra=None, g_extra=None,
                 emit_weights=False, w_transposed=False):
    m, k = x.shape
    n_axis = -2 if w_transposed else -1
    n = w.shape[n_axis] if n is None else n
    assert n % tn == 0 and m % tm == 0
    assert not emit_weights or m == tm, "each weight block must be visited exactly once"
    grid = (m // tm, n // tn)
    w_block = (tn, k) if w_transposed else (k, tn)
    w_index = (lambda i, j: (j, 0)) if w_transposed else (lambda i, j: (0, j))
    if w.ndim == 3:
        w_spec = pl.BlockSpec((None,) + w_block, lambda i, j: (0,) + w_index(i, j))
    else:
        w_spec = pl.BlockSpec(w_block, w_index)
    in_specs = [
        pl.BlockSpec((tm, k), lambda i, j: (i, 0)),
        pl.BlockSpec((1, k), lambda i, j: (0, 0)),
        w_spec,
    ]
    out_shape = [jax.ShapeDtypeStruct((m, n), out_dtype)]
    out_specs = [pl.BlockSpec((tm, tn), lambda i, j: (i, j))]
    args = [x, g.reshape(1, k), w]
    if g_extra is not None:
        in_specs.append(pl.BlockSpec((1, k), lambda i, j: (0, 0)))
        args.append(g_extra.reshape(1, k))
    if w_extra is not None:
        n2 = w_extra.shape[n_axis]
        in_specs.append(pl.BlockSpec(w_extra.shape, lambda i, j: (0, 0)))
        out_shape.append(jax.ShapeDtypeStruct((m, n2), F32))
        out_specs.append(pl.BlockSpec((tm, n2), lambda i, j: (i, 0)))
        args.append(w_extra)
    if emit_weights:
        out_shape.append(jax.ShapeDtypeStruct((n, k) if w_transposed else (k, n), BF16))
        out_specs.append(pl.BlockSpec(w_block, w_index))
    res = pl.pallas_call(
        functools.partial(_norm_matmul_kernel, has_extra=w_extra is not None,
                          extra_gain=g_extra is not None, emit_weights=emit_weights,
                          w_transposed=w_transposed),
        grid=grid,
        in_specs=in_specs,
        out_specs=out_specs,
        out_shape=out_shape,
        scratch_shapes=[pltpu.VMEM((tm, k), BF16)],
        compiler_params=_params(("parallel", "arbitrary")),
        name=f"norm_matmul_{m}x{n}",
    )(*args)
    return res if len(res) > 1 else res[0]


def _matmul_residual_kernel(a_ref, w_ref, r_ref, *rest, final_norm, emit_weights, nk):
    rest = list(rest)
    g_ref = rest.pop(0) if final_norm else None
    o_ref = rest.pop(0)
    w = w_ref[...].astype(BF16)
    if emit_weights:
        rest[0][...] = w
    part = jnp.dot(a_ref[...].astype(BF16), w, preferred_element_type=F32)

    if nk == 1:
        h = r_ref[...] + part
        if final_norm:
            h = h * _rms_scale(h) * g_ref[...]
        o_ref[...] = h
        return

    kk = pl.program_id(1)

    @pl.when(kk == 0)
    def _():
        o_ref[...] = r_ref[...] + part

    @pl.when(kk > 0)
    def _():
        o_ref[...] += part

    if final_norm:
        @pl.when(kk == nk - 1)
        def _():
            h = o_ref[...]
            o_ref[...] = h * _rms_scale(h) * g_ref[...]


def _matmul_residual(a, w, res, *, tm, tk=None, g_final=None, emit_weights=False):
    m, k = a.shape
    n = w.shape[-1]
    tk = k if tk is None else tk
    assert m % tm == 0 and k % tk == 0
    assert not emit_weights or m == tm, "each weight block must be visited exactly once"
    if w.ndim == 3:
        w_spec = pl.BlockSpec((None, tk, n), lambda i, j: (0, j, 0))
    else:
        w_spec = pl.BlockSpec((tk, n), lambda i, j: (j, 0))
    in_specs = [
        pl.BlockSpec((tm, tk), lambda i, j: (i, j)),
        w_spec,
        pl.BlockSpec((tm, n), lambda i, j: (i, 0)),
    ]
    args = [a, w, res]
    if g_final is not None:
        in_specs.append(pl.BlockSpec((1, n), lambda i, j: (0, 0)))
        args.append(g_final.reshape(1, n))
    out_shape = [jax.ShapeDtypeStruct((m, n), F32)]
    out_specs = [pl.BlockSpec((tm, n), lambda i, j: (i, 0))]
    if emit_weights:
        out_shape.append(jax.ShapeDtypeStruct((k, n), BF16))
        out_specs.append(pl.BlockSpec((tk, n), lambda i, j: (j, 0)))
    res = pl.pallas_call(
        functools.partial(_matmul_residual_kernel, final_norm=g_final is not None,
                          emit_weights=emit_weights, nk=k // tk),
        grid=(m // tm, k // tk),
        in_specs=in_specs,
        out_specs=out_specs,
        out_shape=out_shape,
        compiler_params=_params(("parallel", "arbitrary")),
        name=f"matmul_residual_{m}" + ("_final" if g_final is not None else ""),
    )(*args)
    return res if emit_weights else res[0]


def _split_bf16(x):
    hi = x.astype(BF16)
    return hi, (x - hi.astype(F32)).astype(BF16)


def _gla_gates_kernel(glow_ref, wg_ref, bg_ref, bcum_ref, *, chunk):
    rows = glow_ref.shape[0]
    g_hi, g_lo = _split_bf16(glow_ref[...])
    w_hi, w_lo = _split_bf16(wg_ref[...])
    x = jnp.dot(jnp.concatenate([g_hi, g_lo, g_hi], axis=1),
                jnp.concatenate([w_hi, w_hi, w_lo], axis=0),
                preferred_element_type=F32) + bg_ref[...]
    softplus2 = jnp.log2(1.0 + jnp.exp2(jnp.abs(x) * (-LOG2E)))
    logg = jnp.minimum(x, 0.0) * (LOG2E / GLA_GATE_TEMP) - softplus2 * (1.0 / GLA_GATE_TEMP)

    span = GATE_CUMSUM_ROWS
    row = lax.broadcasted_iota(jnp.int32, (span, span), 0)
    col = lax.broadcasted_iota(jnp.int32, (span, span), 1)
    same_chunk = (row // chunk) == (col // chunk) if chunk < span else True
    tril = jnp.where((col <= row) & same_chunk, 1.0, 0.0).astype(BF16)
    tril2 = jnp.concatenate([tril, tril], axis=1)
    for i in range(rows // span):
        hi, lo = _split_bf16(logg[i * span:(i + 1) * span])
        bcum_ref[i * span:(i + 1) * span, :] = jnp.dot(
            tril2, jnp.concatenate([hi, lo], axis=0), preferred_element_type=F32)


def _gla_gates(glow, wg, bg, *, chunk, tm):
    m = glow.shape[0]
    assert m % tm == 0 and tm % GATE_CUMSUM_ROWS == 0
    assert GATE_CUMSUM_ROWS % chunk == 0 or chunk % GATE_CUMSUM_ROWS == 0
    assert chunk <= GATE_CUMSUM_ROWS, "cumulative sums do not cross row spans"
    return pl.pallas_call(
        functools.partial(_gla_gates_kernel, chunk=chunk),
        grid=(m // tm,),
        in_specs=[
            pl.BlockSpec((tm, GLA_GATE_RANK), lambda i: (i, 0)),
            pl.BlockSpec((GLA_GATE_RANK, GLA_KEY_DIM), lambda i: (0, 0)),
            pl.BlockSpec((1, GLA_KEY_DIM), lambda i: (0, 0)),
        ],
        out_specs=pl.BlockSpec((tm, GLA_KEY_DIM), lambda i: (i, 0)),
        out_shape=jax.ShapeDtypeStruct((m, GLA_KEY_DIM), F32),
        compiler_params=_params(("parallel",)),
        name=f"gla_gates_{m}",
    )(glow, wg, bg.reshape(1, -1))


def _gla_kernel(q_ref, k_ref, v_ref, r_ref, bcum_ref, gon_ref, *rest, has_init, n_chunks):
    rest = list(rest)
    s0_ref = rest.pop(0) if has_init else None
    o_ref, sfin_ref = rest[:2]
    s_ref = rest[2] if n_chunks > 1 else None
    c = pl.program_id(1)
    nseq, chunk = q_ref.shape[:2]
    chains = [(s, h) for s in range(nseq) for h in range(GLA_HEADS)]

    if n_chunks > 1:
        @pl.when(c == 0)
        def _():
            if has_init:
                s_ref[...] = s0_ref[...]
            else:
                s_ref[...] = jnp.zeros_like(s_ref)

    row = lax.broadcasted_iota(jnp.int32, (chunk, chunk), 0)
    col = lax.broadcasted_iota(jnp.int32, (chunk, chunk), 1)
    causal = col <= row
    heads = range(GLA_HEADS)
    ks = [slice(h * GLA_DK, (h + 1) * GLA_DK) for h in heads]
    vs = [slice(h * GLA_DV, (h + 1) * GLA_DV) for h in heads]

    def state(s, h):
        if n_chunks > 1:
            return s_ref[s, h]
        return s0_ref[s, h] if has_init else jnp.zeros((GLA_DK, GLA_DV), F32)

    q_inter, k_state, scores, decay = {}, {}, {}, {}
    for s, h in chains:
        b = bcum_ref[s, :, ks[h]]
        b_last = b[chunk - 1:chunk, :]
        b_mid = b[chunk // 2 - 1:chunk // 2, :]
        q = q_ref[s, :, ks[h]].astype(F32)
        k = k_ref[s, :, ks[h]].astype(F32)
        q_inter[s, h] = (q * jnp.exp2(b)).astype(BF16)
        q_intra = (q * jnp.exp2(b - b_mid)).astype(BF16)
        k_intra = (k * jnp.exp2(b_mid - b)).astype(BF16)
        k_state[s, h] = (k * jnp.exp2(b_last - b)).astype(BF16)
        scores[s, h] = lax.dot_general(q_intra, k_intra, _NT_DIMS,
                                       preferred_element_type=F32)
        decay[s, h] = jnp.exp2(jnp.broadcast_to(b_last, (128, GLA_DK))).T

    o = {}
    for s, h in chains:
        sc = jnp.where(causal, scores[s, h], 0.0).astype(BF16)
        v = v_ref[s, :, vs[h]].astype(BF16)
        o[s, h] = (jnp.dot(q_inter[s, h], state(s, h).astype(BF16),
                           preferred_element_type=F32)
                   + jnp.dot(sc, v, preferred_element_type=F32))
        upd = lax.dot_general(k_state[s, h], v, _TN_DIMS, preferred_element_type=F32)
        s_new = (state(s, h) * jnp.concatenate([decay[s, h]] * (GLA_DV // 128), axis=1)
                 + upd)
        if n_chunks > 1:
            s_ref[s, h] = s_new
        else:
            sfin_ref[s, h] = s_new

    qs = GLA_DK ** -0.5
    scale = {sh: qs * lax.rsqrt(jnp.mean(o[sh] * o[sh], axis=-1, keepdims=True) * (qs * qs)
                                + RMS_EPS) for sh in chains}
    for s, h in chains:
        r = r_ref[s, :, vs[h]].astype(F32)
        o_ref[s, :, vs[h]] = (o[s, h] * scale[s, h] * gon_ref[:, vs[h]]
                              * _silu(r)).astype(o_ref.dtype)

    if n_chunks > 1:
        @pl.when(c == n_chunks - 1)
        def _():
            sfin_ref[...] = s_ref[...]


def _gla(proj, bcum, gon, *, batch, seq, chunk, out_dtype, s0=None):
    n = seq // chunk
    ns = GLA_SEQS_PER_STEP
    assert batch % ns == 0
    proj = proj.reshape(batch, seq, proj.shape[-1])
    bcum = bcum.reshape(batch, seq, bcum.shape[-1])
    kb, vb, rb = 1, 2 * GLA_KEY_DIM // GLA_VALUE_DIM, 2 * GLA_KEY_DIM // GLA_VALUE_DIM + 1
    in_specs = [
        pl.BlockSpec((ns, chunk, GLA_KEY_DIM), lambda b, c: (b, c, 0)),
        pl.BlockSpec((ns, chunk, GLA_KEY_DIM), lambda b, c: (b, c, kb)),
        pl.BlockSpec((ns, chunk, GLA_VALUE_DIM), lambda b, c: (b, c, vb)),
        pl.BlockSpec((ns, chunk, GLA_VALUE_DIM), lambda b, c: (b, c, rb)),
        pl.BlockSpec((ns, chunk, GLA_KEY_DIM), lambda b, c: (b, c, 0)),
        pl.BlockSpec((1, GLA_VALUE_DIM), lambda b, c: (0, 0)),
    ]
    args = [proj, proj, proj, proj, bcum, gon.reshape(1, -1)]
    state_spec = pl.BlockSpec((ns, GLA_HEADS, GLA_DK, GLA_DV), lambda b, c: (b, 0, 0, 0))
    if s0 is not None:
        in_specs.append(state_spec)
        args.append(s0)
    scratch = [pltpu.VMEM((ns, GLA_HEADS, GLA_DK, GLA_DV), F32)] if n > 1 else []
    o, s_fin = pl.pallas_call(
        functools.partial(_gla_kernel, has_init=s0 is not None, n_chunks=n),
        grid=(batch // ns, n),
        in_specs=in_specs,
        out_specs=[
            pl.BlockSpec((ns, chunk, GLA_VALUE_DIM), lambda b, c: (b, c, 0)),
            state_spec,
        ],
        out_shape=[
            jax.ShapeDtypeStruct((batch, seq, GLA_VALUE_DIM), out_dtype),
            jax.ShapeDtypeStruct((batch, GLA_HEADS, GLA_DK, GLA_DV), F32),
        ],
        scratch_shapes=scratch,
        compiler_params=_params(("parallel", "arbitrary")),
        name=f"gla_chunk{chunk}",
    )(*args)
    return o.reshape(batch * seq, GLA_VALUE_DIM), s_fin


def _alibi_slope(head):
    return 2.0 ** (-8.0 * (head + 1) / SWA_HEADS)


def _attn_prompt_kernel(sinks_ref, q_ref, z_ref, kp_ref, ko_ref, vp_ref, vo_ref, out_ref,
                        bias_ref):
    hd, nkeys = SWA_HEAD_DIM, 2 * WINDOW
    pair_w = 2 * hd
    pairs_per_group = SWA_GROUP // 2
    blk = pl.program_id(1)

    @pl.when((pl.program_id(0) == 0) & (blk == 0))
    def _():
        kj = lax.broadcasted_iota(jnp.int32, (nkeys, WINDOW), 0)
        qi = lax.broadcasted_iota(jnp.int32, (nkeys, WINDOW), 1)
        dist = WINDOW + qi - kj
        ok = (dist >= 0) & (dist <= WINDOW)
        ok_first = ok & (kj >= WINDOW)
        distf = dist.astype(F32)
        for h in range(SWA_HEADS):
            pen = (-_alibi_slope(h) * LOG2E) * distf
            sl = slice((h % 2) * WINDOW, (h % 2 + 1) * WINDOW)
            bias_ref[0, h // 2, :, sl] = jnp.where(ok_first, pen, -jnp.inf)
            bias_ref[1, h // 2, :, sl] = jnp.where(ok, pen, -jnp.inf)

    n_sub = q_ref.shape[0] // WINDOW
    k_rows = jnp.concatenate([kp_ref[...], ko_ref[...]], axis=0)
    v_rows = jnp.concatenate([vp_ref[...], vo_ref[...]], axis=0)
    ones = jnp.ones((16, nkeys), F32)
    lane = lax.broadcasted_iota(jnp.int32, (nkeys, pair_w), 1)
    qlane = lax.broadcasted_iota(jnp.int32, (WINDOW, pair_w), 1)
    qk_scale = (hd ** -0.5) * LOG2E
    quad_pairs = 2
    n_quads = SWA_HEADS // (2 * quad_pairs)
    quads_per_group = pairs_per_group // quad_pairs
    quarter = lax.broadcasted_iota(jnp.int32, (1, 2 * quad_pairs * WINDOW), 1) // WINDOW

    for sub in range(n_sub):
        qrows = slice(sub * WINDOW, (sub + 1) * WINDOW)
        tbl = jnp.minimum(blk, 1) if sub == 0 else 1
        k = k_rows[sub * WINDOW:sub * WINDOW + nkeys]
        v = v_rows[sub * WINDOW:sub * WINDOW + nkeys]
        vt = v.T

        k2, vt1 = [], []
        for g in range(SWA_KV_HEADS):
            kblk = k[:, (g // 2) * pair_w:(g // 2 + 1) * pair_w]
            k_here = jnp.where((lane < hd) if g % 2 == 0 else (lane >= hd), kblk, 0.0)
            k2.append((k_here + pltpu.roll(k_here, hd, axis=1)).astype(BF16))
            vt1.append(jnp.concatenate([vt[g * hd:(g + 1) * hd], ones], axis=0).astype(BF16))

        def scores(quad, k2=k2, qrows=qrows):
            parts = []
            for j in range(quad_pairs):
                col = (quad * quad_pairs + j) * pair_w
                q_pair = q_ref[qrows, col:col + pair_w]
                zero = jnp.zeros_like(q_pair)
                parts += [jnp.where(qlane < hd, q_pair, zero),
                          jnp.where(qlane >= hd, q_pair, zero)]
            return lax.dot_general(k2[quad // quads_per_group], jnp.concatenate(parts, axis=0),
                                   _NT_DIMS, preferred_element_type=F32)

        st_next = scores(0)
        for quad in range(n_quads):
            st = st_next
            if quad + 1 < n_quads:
                st_next = scores(quad + 1)
            pair0 = quad * quad_pairs
            bias = jnp.concatenate([bias_ref[tbl, pair0 + j] for j in range(quad_pairs)], axis=1)
            s2 = st * qk_scale + bias
            sink2 = sinks_ref[2 * pair0]
            for t in range(1, 2 * quad_pairs):
                sink2 = jnp.where(quarter == t, sinks_ref[2 * pair0 + t], sink2)
            sink2 = sink2 * LOG2E
            m = jnp.maximum(jnp.max(s2, axis=0, keepdims=True), sink2)
            p = jnp.exp2(s2 - m).astype(BF16)
            oa = jnp.dot(vt1[quad // quads_per_group], p, preferred_element_type=F32)
            denom = oa[hd:hd + 1] + jnp.exp2(sink2 - m)
            on = oa[0:hd] * (1.0 / denom)
            for j in range(quad_pairs):
                lo = 2 * j * WINDOW
                o_pair = jnp.concatenate([on[:, lo:lo + WINDOW],
                                          on[:, lo + WINDOW:lo + 2 * WINDOW]], axis=0).T
                col = (pair0 + j) * pair_w
                z_pair = z_ref[qrows, col:col + pair_w].astype(F32)
                out_ref[qrows, col:col + pair_w] = (o_pair * _silu(z_pair)).astype(out_ref.dtype)


def _attn_prompt(qz, kv, sinks, *, batch, seq):
    sub = ATTN_BLOCKS_PER_STEP
    tq = sub * WINDOW
    nb = seq // tq
    assert seq % tq == 0
    row = lambda b, i: b * nb + i
    prev = lambda b, i: (b * nb + i) * sub - jnp.minimum(i, 1)
    return pl.pallas_call(
        _attn_prompt_kernel,
        grid=(batch, nb),
        in_specs=[
            pl.BlockSpec(memory_space=pltpu.SMEM),
            pl.BlockSpec((tq, SWA_WIDTH), lambda b, i: (row(b, i), 0)),
            pl.BlockSpec((tq, SWA_WIDTH), lambda b, i: (row(b, i), 1)),
            pl.BlockSpec((WINDOW, SWA_KV_WIDTH), lambda b, i: (prev(b, i), 0)),
            pl.BlockSpec((tq, SWA_KV_WIDTH), lambda b, i: (row(b, i), 0)),
            pl.BlockSpec((WINDOW, SWA_KV_WIDTH), lambda b, i: (prev(b, i), 1)),
            pl.BlockSpec((tq, SWA_KV_WIDTH), lambda b, i: (row(b, i), 1)),
        ],
        out_specs=pl.BlockSpec((tq, SWA_WIDTH), lambda b, i: (row(b, i), 0)),
        out_shape=jax.ShapeDtypeStruct((batch * seq, SWA_WIDTH), BF16),
        scratch_shapes=[pltpu.VMEM((2, SWA_HEADS // 2, 2 * WINDOW, 2 * WINDOW), F32)],
        compiler_params=_params(("arbitrary", "arbitrary")),
        name="attn_prompt",
    )(sinks, qz, qz, kv, kv, kv, kv)


def _attn_sample_kernel(sink_ref, slope_ref, q_ref, z_ref, kn_ref, vn_ref, kc_ref, vc_ref,
                        out_ref, kwin_ref, vwin_ref, *, tq, nb):
    hd, nk = SWA_HEAD_DIM, 2 * WINDOW
    rows = SWA_HEADS * tq
    grows = SWA_GROUP * tq
    seqs = range(nb)
    groups = range(SWA_KV_HEADS)

    lane = lax.broadcasted_iota(jnp.int32, (rows, nk), 1)
    tok = lax.broadcasted_iota(jnp.int32, (rows, nk), 0) % tq
    in_buffer = lane < WINDOW
    dist = jnp.where(in_buffer, WINDOW + tok - lane, (nk - tq) + tok - lane)
    allowed = (dist >= 0) & (dist <= WINDOW) & (in_buffer | (lane >= nk - tq))
    penalty = slope_ref[...] * dist.astype(F32)
    new_lanes = lax.broadcasted_iota(jnp.int32, (hd, WINDOW), 1) >= WINDOW - tq

    def new_rows_t(ref, b):
        x = jnp.concatenate([jnp.zeros((WINDOW - tq, SWA_KV_WIDTH), F32),
                             ref[b * tq:(b + 1) * tq, :]], axis=0)
        xt = [x[:, c * WINDOW:(c + 1) * WINDOW].T for c in range(SWA_KV_WIDTH // WINDOW)]
        per_block = WINDOW // hd
        return [xt[g // per_block][(g % per_block) * hd:(g % per_block + 1) * hd]
                for g in groups]

    k_all, v_all = {}, {}
    for b in seqs:
        kn_t, vn_t = new_rows_t(kn_ref, b), new_rows_t(vn_ref, b)
        for g in groups:
            kc, vc = kc_ref[b, g], vc_ref[b, g]
            k_all[b, g] = jnp.concatenate([kc, kn_t[g]], axis=1).astype(BF16)
            v_all[b, g] = jnp.concatenate([vc, vn_t[g]], axis=1).astype(BF16)
            kwin_ref[b, g] = jnp.where(new_lanes, kn_t[g], pltpu.roll(kc, WINDOW - tq, axis=1))
            vwin_ref[b, g] = jnp.where(new_lanes, vn_t[g], pltpu.roll(vc, WINDOW - tq, axis=1))

    s = []
    for b in seqs:
        q = q_ref[b * tq:(b + 1) * tq, :]
        parts = []
        for g in groups:
            qs = jnp.concatenate([q[:, h * hd:(h + 1) * hd]
                                  for h in range(g * SWA_GROUP, (g + 1) * SWA_GROUP)], axis=0)
            parts.append(jnp.dot(qs.astype(BF16), k_all[b, g], preferred_element_type=F32))
        s.append(jnp.concatenate(parts, axis=0))

    sink = sink_ref[...]
    s = [jnp.where(allowed, sb * (hd ** -0.5) - penalty, -jnp.inf) for sb in s]
    m = [jnp.maximum(jnp.max(sb, axis=-1, keepdims=True), sink) for sb in s]
    p = [jnp.exp(s[b] - m[b]) for b in seqs]
    inv = [1.0 / (jnp.sum(p[b], axis=-1, keepdims=True) + jnp.exp(sink - m[b])) for b in seqs]

    for b in seqs:
        pb = p[b].astype(BF16)
        o = jnp.concatenate(
            [lax.dot_general(pb[g * grows:(g + 1) * grows], v_all[b, g], _NT_DIMS,
                             preferred_element_type=F32) for g in groups], axis=0) * inv[b]
        o = jnp.concatenate([o[h * tq:(h + 1) * tq] for h in range(SWA_HEADS)], axis=1)
        z = z_ref[b * tq:(b + 1) * tq, :]
        out_ref[b * tq:(b + 1) * tq, :] = (o * _silu(z)).astype(out_ref.dtype)


def _attn_sample(qz, kv, cache_k, cache_v, sinks, *, batch, seq):
    nb = SAMPLE_SEQS_PER_STEP
    assert batch % nb == 0
    rows = nb * seq
    sink_col = jnp.repeat(sinks, seq).reshape(SWA_HEADS * seq, 1)
    slope_col = jnp.repeat(jnp.asarray([_alibi_slope(h) for h in range(SWA_HEADS)], F32),
                           seq).reshape(SWA_HEADS * seq, 1)
    col_spec = pl.BlockSpec((SWA_HEADS * seq, 1), lambda i: (0, 0))
    win_spec = pl.BlockSpec((nb, SWA_KV_HEADS, SWA_HEAD_DIM, WINDOW), lambda i: (i, 0, 0, 0))
    win_shape = jax.ShapeDtypeStruct((batch, SWA_KV_HEADS, SWA_HEAD_DIM, WINDOW), F32)
    to_lanes = lambda c: jnp.transpose(c, (0, 2, 3, 1))
    att, k_win, v_win = pl.pallas_call(
        functools.partial(_attn_sample_kernel, tq=seq, nb=nb),
        grid=(batch // nb,),
        in_specs=[
            col_spec,
            col_spec,
            pl.BlockSpec((rows, SWA_WIDTH), lambda i: (i, 0)),
            pl.BlockSpec((rows, SWA_WIDTH), lambda i: (i, 1)),
            pl.BlockSpec((rows, SWA_KV_WIDTH), lambda i: (i, 0)),
            pl.BlockSpec((rows, SWA_KV_WIDTH), lambda i: (i, 1)),
            win_spec,
            win_spec,
        ],
        out_specs=[pl.BlockSpec((rows, SWA_WIDTH), lambda i: (i, 0)), win_spec, win_spec],
        out_shape=[jax.ShapeDtypeStruct((batch * seq, SWA_WIDTH), F32), win_shape, win_shape],
        compiler_params=_params(("parallel",)),
        name="attn_sample",
    )(sink_col, slope_col, qz, qz, kv, kv, to_lanes(cache_k), to_lanes(cache_v))
    return att, jnp.transpose(k_win, (0, 3, 1, 2)), jnp.transpose(v_win, (0, 3, 1, 2))


def _trunk(x, weights, *, batch, seq, row_tile, col_tile, res_tile, chunk, act_dtype,
           gla_init=None, cache=None, emit_weights=False):
    (g_norm_a, w_in_a, w_glow, w_gate_up, b_gate, g_onorm_a, w_out_a,
     g_norm_kv, w_kv, g_norm_b, w_in_b, sinks, w_out_b, g_final) = weights
    h = x.reshape(batch * seq, D_MODEL)
    res_tk = SAMPLE_K_TILE if emit_weights else None

    proj, glow, *w_in_a_bf16 = _norm_matmul(
        h, g_norm_a, w_in_a, tm=row_tile, tn=col_tile, n=GLA_MAIN_COLS, out_dtype=act_dtype,
        w_extra=w_glow, emit_weights=emit_weights, w_transposed=True)
    bcum = _gla_gates(glow, w_gate_up, b_gate, chunk=chunk, tm=min(row_tile, GATE_ROW_TILE))
    o, s_fin = _gla(proj, bcum, g_onorm_a, batch=batch, seq=seq,
                    chunk=chunk, out_dtype=act_dtype, s0=gla_init)
    h, *w_out_a_bf16 = _as_list(_matmul_residual(o, w_out_a, h, tm=res_tile, tk=res_tk,
                                                 emit_weights=emit_weights))

    qz, kv, *w_in_b_bf16 = _norm_matmul(
        h, g_norm_b, w_in_b, tm=row_tile, tn=QZ_COL_TILE, out_dtype=act_dtype,
        w_extra=w_kv, g_extra=g_norm_kv, emit_weights=emit_weights)
    if cache is None:
        att = _attn_prompt(qz, kv, sinks, batch=batch, seq=seq)
        kv_win = kv.reshape(batch, seq, 2 * SWA_KV_WIDTH)[:, seq - WINDOW:]
        kv_win = kv_win.reshape(batch, WINDOW, 2, SWA_KV_HEADS, SWA_HEAD_DIM)
        k_win, v_win = kv_win[:, :, 0], kv_win[:, :, 1]
    else:
        cache_k, cache_v = cache
        att, k_win, v_win = _attn_sample(qz, kv, cache_k, cache_v, sinks,
                                         batch=batch, seq=seq)
    y, *w_out_b_bf16 = _as_list(_matmul_residual(att, w_out_b, h, tm=res_tile, tk=res_tk,
                                                 g_final=g_final,
                                                 emit_weights=emit_weights))
    outs = (y.reshape(batch, seq, D_MODEL), s_fin[None], k_win, v_win)
    return outs, (w_in_a_bf16 + w_in_b_bf16 + w_out_a_bf16 + w_out_b_bf16)


def _as_list(x):
    return list(x) if isinstance(x, (list, tuple)) else [x]


def kernel(x_prompt, x_sample, state_gla, cache_k_win, cache_v_win, g_norm_a, w_in_a,
           w_gate_up, b_gate, g_onorm_a, w_out_a, g_norm_kv, w_kv, g_norm_b, w_in_b, sinks,
           w_out_b, g_final):
    assert w_in_a.shape[0] == 1 and w_in_b.shape[0] == 1, "one GLA layer, one SWA layer"
    assert cache_k_win.shape[1] == WINDOW
    pb, ps, _ = x_prompt.shape
    sb, ss, _ = x_sample.shape

    def weights(w_in_a_any, w_in_b_any, w_out_a_any, w_out_b_any):
        return (
            g_norm_a[0], w_in_a_any, w_in_a_t[0, GLA_MAIN_COLS:, :],
            w_gate_up[0], b_gate[0], g_onorm_a[0], w_out_a_any,
            g_norm_kv, w_kv.astype(BF16),
            g_norm_b[0], w_in_b_any, sinks[0], w_out_b_any, g_final,
        )

    w_in_a_t = jnp.swapaxes(w_in_a, 1, 2)

    (y_s, gla_s, k_s, v_s), bf16_weights = _trunk(
        x_sample, weights(w_in_a_t, w_in_b, w_out_a, w_out_b), batch=sb, seq=ss,
        row_tile=sb * ss,
        col_tile=SAMPLE_COL_TILE, res_tile=sb * ss, chunk=ss, act_dtype=F32,
        gla_init=state_gla.reshape(state_gla.shape[1:]),
        cache=(cache_k_win, cache_v_win), emit_weights=True)
    (y_p, gla_p, k_p, v_p), _ = _trunk(
        x_prompt, weights(*bf16_weights), batch=pb, seq=ps,
        row_tile=PROMPT_ROW_TILE, col_tile=PROMPT_COL_TILE, res_tile=RESIDUAL_ROW_TILE,
        chunk=GLA_PROMPT_CHUNK, act_dtype=BF16)
    return (y_p, y_s, gla_p, gla_s, k_p, v_p, k_s, v_s)
```

```python
import functools

import jax
import jax.numpy as jnp
from jax import lax
from jax.experimental import pallas as pl
from jax.experimental.pallas import tpu as pltpu

F32 = jnp.float32
BF16 = jnp.bfloat16

D_MODEL = 2048
GLA_HEADS = 4
GLA_KEY_DIM = D_MODEL // 2
GLA_VALUE_DIM = D_MODEL
GLA_DK = GLA_KEY_DIM // GLA_HEADS
GLA_DV = GLA_VALUE_DIM // GLA_HEADS
GLA_GATE_RANK = 16
GLA_GATE_TEMP = 16.0
GLA_MAIN_COLS = 2 * GLA_KEY_DIM + 2 * GLA_VALUE_DIM
SWA_HEAD_DIM = 64
SWA_HEADS = D_MODEL // SWA_HEAD_DIM
SWA_KV_HEADS = 4
SWA_GROUP = SWA_HEADS // SWA_KV_HEADS
SWA_WIDTH = SWA_HEADS * SWA_HEAD_DIM
SWA_KV_WIDTH = SWA_KV_HEADS * SWA_HEAD_DIM
WINDOW = 128
RMS_EPS = 1e-6
LOG2E = 1.4426950408889634

V7X_VMEM_LIMIT_BYTES = 61 * 1024 * 1024
GLA_PROMPT_CHUNK = 128
GLA_SEQS_PER_STEP = 4
ATTN_BLOCKS_PER_STEP = 4
ATTN_BLOCKS_IN_FLIGHT = 2
SAMPLE_SEQS_PER_STEP = 4
NORM_ROW_CHUNK = 256
GATE_CUMSUM_ROWS = 128
GATE_ROW_TILE = 1024
PROMPT_ROW_TILE = 1024
PROMPT_COL_TILE = 2048
SAMPLE_COL_TILE = 1024
QZ_COL_TILE = 2048
SAMPLE_K_TILE = 512
RESIDUAL_ROW_TILE = 512

_NT_DIMS = (((1,), (1,)), ((), ()))
_TN_DIMS = (((0,), (0,)), ((), ()))


def _params(semantics):
    return pltpu.CompilerParams(dimension_semantics=semantics,
                                vmem_limit_bytes=V7X_VMEM_LIMIT_BYTES)


def _silu(x):
    return x / (1.0 + jnp.exp(-x))


def _rms_scale(x):
    return lax.rsqrt(jnp.mean(x * x, axis=-1, keepdims=True) + RMS_EPS)


def _norm_matmul_kernel(x_ref, g_ref, w_ref, *rest, has_extra, extra_gain, emit_weights,
                        w_transposed):
    dims = _NT_DIMS if w_transposed else (((1,), (0,)), ((), ()))

    def mm(a, b):
        return lax.dot_general(a, b, dims, preferred_element_type=F32)

    rest = list(rest)
    g2_ref = rest.pop(0) if extra_gain else None
    w2_ref = rest.pop(0) if has_extra else None
    o_ref = rest.pop(0)
    o2_ref = rest.pop(0) if has_extra else None
    wb_ref = rest.pop(0) if emit_weights else None
    (xn_ref,) = rest

    first = pl.program_id(1) == 0

    def weights():
        w = w_ref[...].astype(BF16)
        if emit_weights:
            wb_ref[...] = w
        return w

    @pl.when(first)
    def _():
        tm = x_ref.shape[0]
        rc = min(tm, NORM_ROW_CHUNK)
        w = weights()
        for c in range(tm // rc):
            rows = slice(c * rc, (c + 1) * rc)
            x = x_ref[rows, :]
            xs = x * _rms_scale(x)
            xn = (xs * g_ref[...]).astype(BF16)
            xn_ref[rows, :] = xn
            o_ref[rows, :] = mm(xn, w).astype(o_ref.dtype)
            if has_extra:
                xn2 = (xs * g2_ref[...]).astype(BF16) if extra_gain else xn
                o2_ref[rows, :] = mm(xn2, w2_ref[...].astype(BF16)).astype(o2_ref.dtype)

    @pl.when(jnp.logical_not(first))
    def _():
        o_ref[...] = mm(xn_ref[...], weights()).astype(o_ref.dtype)


def _norm_matmul(x, g, w, *, tm, tn, out_dtype, n=None, w_extra=None, g_extra=None,
                 emit_weights=False, w_transposed=False):
    m, k = x.shape
    n_axis = -2 if w_transposed else -1
    n = w.shape[n_axis] if n is None else n
    assert n % tn == 0 and m % tm == 0
    assert not emit_weights or m == tm, "each weight block must be visited exactly once"
    grid = (m // tm, n // tn)
    w_block = (tn, k) if w_transposed else (k, tn)
    w_index = (lambda i, j: (j, 0)) if w_transposed else (lambda i, j: (0, j))
    if w.ndim == 3:
        w_spec = pl.BlockSpec((None,) + w_block, lambda i, j: (0,) + w_index(i, j))
    else:
        w_spec = pl.BlockSpec(w_block, w_index)
    in_specs = [
        pl.BlockSpec((tm, k), lambda i, j: (i, 0)),
        pl.BlockSpec((1, k), lambda i, j: (0, 0)),
        w_spec,
    ]
    out_shape = [jax.ShapeDtypeStruct((m, n), out_dtype)]
    out_specs = [pl.BlockSpec((tm, tn), lambda i, j: (i, j))]
    args = [x, g.reshape(1, k), w]
    if g_extra is not None:
        in_specs.append(pl.BlockSpec((1, k), lambda i, j: (0, 0)))
        args.append(g_extra.reshape(1, k))
    if w_extra is not None:
        n2 = w_extra.shape[n_axis]
        in_specs.append(pl.BlockSpec(w_extra.shape, lambda i, j: (0, 0)))
        out_shape.append(jax.ShapeDtypeStruct((m, n2), F32))
        out_specs.append(pl.BlockSpec((tm, n2), lambda i, j: (i, 0)))
        args.append(w_extra)
    if emit_weights:
        out_shape.append(jax.ShapeDtypeStruct((n, k) if w_transposed else (k, n), BF16))
        out_specs.append(pl.BlockSpec(w_block, w_index))
    res = pl.pallas_call(
        functools.partial(_norm_matmul_kernel, has_extra=w_extra is not None,
                          extra_gain=g_extra is not None, emit_weights=emit_weights,
                          w_transposed=w_transposed),
        grid=grid,
        in_specs=in_specs,
        out_specs=out_specs,
        out_shape=out_shape,
        scratch_shapes=[pltpu.VMEM((tm, k), BF16)],
        compiler_params=_params(("parallel", "arbitrary")),
        name=f"norm_matmul_{m}x{n}",
    )(*args)
    return res if len(res) > 1 else res[0]


def _matmul_residual_kernel(a_ref, w_ref, r_ref, *rest, final_norm, emit_weights, nk):
    rest = list(rest)
    g_ref = rest.pop(0) if final_norm else None
    o_ref = rest.pop(0)
    w = w_ref[...].astype(BF16)
    if emit_weights:
        rest[0][...] = w
    part = jnp.dot(a_ref[...].astype(BF16), w, preferred_element_type=F32)

    if nk == 1:
        h = r_ref[...] + part
        if final_norm:
            h = h * _rms_scale(h) * g_ref[...]
        o_ref[...] = h
        return

    kk = pl.program_id(1)

    @pl.when(kk == 0)
    def _():
        o_ref[...] = r_ref[...] + part

    @pl.when(kk > 0)
    def _():
        o_ref[...] += part

    if final_norm:
        @pl.when(kk == nk - 1)
        def _():
            h = o_ref[...]
            o_ref[...] = h * _rms_scale(h) * g_ref[...]


def _matmul_residual(a, w, res, *, tm, tk=None, g_final=None, emit_weights=False):
    m, k = a.shape
    n = w.shape[-1]
    tk = k if tk is None else tk
    assert m % tm == 0 and k % tk == 0
    assert not emit_weights or m == tm, "each weight block must be visited exactly once"
    if w.ndim == 3:
        w_spec = pl.BlockSpec((None, tk, n), lambda i, j: (0, j, 0))
    else:
        w_spec = pl.BlockSpec((tk, n), lambda i, j: (j, 0))
    in_specs = [
        pl.BlockSpec((tm, tk), lambda i, j: (i, j)),
        w_spec,
        pl.BlockSpec((tm, n), lambda i, j: (i, 0)),
    ]
    args = [a, w, res]
    if g_final is not None:
        in_specs.append(pl.BlockSpec((1, n), lambda i, j: (0, 0)))
        args.append(g_final.reshape(1, n))
    out_shape = [jax.ShapeDtypeStruct((m, n), F32)]
    out_specs = [pl.BlockSpec((tm, n), lambda i, j: (i, 0))]
    if emit_weights:
        out_shape.append(jax.ShapeDtypeStruct((k, n), BF16))
        out_specs.append(pl.BlockSpec((tk, n), lambda i, j: (j, 0)))
    res = pl.pallas_call(
        functools.partial(_matmul_residual_kernel, final_norm=g_final is not None,
                          emit_weights=emit_weights, nk=k // tk),
        grid=(m // tm, k // tk),
        in_specs=in_specs,
        out_specs=out_specs,
        out_shape=out_shape,
        compiler_params=_params(("parallel", "arbitrary")),
        name=f"matmul_residual_{m}" + ("_final" if g_final is not None else ""),
    )(*args)
    return res if emit_weights else res[0]


def _split_bf16(x):
    hi = x.astype(BF16)
    return hi, (x - hi.astype(F32)).astype(BF16)


def _gla_gates_kernel(glow_ref, wg_ref, bg_ref, bcum_ref, *, chunk):
    rows = glow_ref.shape[0]
    g_hi, g_lo = _split_bf16(glow_ref[...])
    w_hi, w_lo = _split_bf16(wg_ref[...])
    x = jnp.dot(jnp.concatenate([g_hi, g_lo, g_hi], axis=1),
                jnp.concatenate([w_hi, w_hi, w_lo], axis=0),
                preferred_element_type=F32) + bg_ref[...]
    softplus2 = jnp.log2(1.0 + jnp.exp2(jnp.abs(x) * (-LOG2E)))
    logg = jnp.minimum(x, 0.0) * (LOG2E / GLA_GATE_TEMP) - softplus2 * (1.0 / GLA_GATE_TEMP)

    span = GATE_CUMSUM_ROWS
    row = lax.broadcasted_iota(jnp.int32, (span, span), 0)
    col = lax.broadcasted_iota(jnp.int32, (span, span), 1)
    same_chunk = (row // chunk) == (col // chunk) if chunk < span else True
    tril = jnp.where((col <= row) & same_chunk, 1.0, 0.0).astype(BF16)
    tril2 = jnp.concatenate([tril, tril], axis=1)
    for i in range(rows // span):
        hi, lo = _split_bf16(logg[i * span:(i + 1) * span])
        bcum_ref[i * span:(i + 1) * span, :] = jnp.dot(
            tril2, jnp.concatenate([hi, lo], axis=0), preferred_element_type=F32)


def _gla_gates(glow, wg, bg, *, chunk, tm):
    m = glow.shape[0]
    assert m % tm == 0 and tm % GATE_CUMSUM_ROWS == 0
    assert GATE_CUMSUM_ROWS % chunk == 0 or chunk % GATE_CUMSUM_ROWS == 0
    assert chunk <= GATE_CUMSUM_ROWS, "cumulative sums do not cross row spans"
    return pl.pallas_call(
        functools.partial(_gla_gates_kernel, chunk=chunk),
        grid=(m // tm,),
        in_specs=[
            pl.BlockSpec((tm, GLA_GATE_RANK), lambda i: (i, 0)),
            pl.BlockSpec((GLA_GATE_RANK, GLA_KEY_DIM), lambda i: (0, 0)),
            pl.BlockSpec((1, GLA_KEY_DIM), lambda i: (0, 0)),
        ],
        out_specs=pl.BlockSpec((tm, GLA_KEY_DIM), lambda i: (i, 0)),
        out_shape=jax.ShapeDtypeStruct((m, GLA_KEY_DIM), F32),
        compiler_params=_params(("parallel",)),
        name=f"gla_gates_{m}",
    )(glow, wg, bg.reshape(1, -1))


def _gla_kernel(q_ref, k_ref, v_ref, r_ref, bcum_ref, gon_ref, *rest, has_init, n_chunks):
    rest = list(rest)
    s0_ref = rest.pop(0) if has_init else None
    o_ref, sfin_ref = rest[:2]
    s_ref = rest[2] if n_chunks > 1 else None
    c = pl.program_id(1)
    nseq, chunk = q_ref.shape[:2]
    chains = [(s, h) for s in range(nseq) for h in range(GLA_HEADS)]

    if n_chunks > 1:
        @pl.when(c == 0)
        def _():
            if has_init:
                s_ref[...] = s0_ref[...]
            else:
                s_ref[...] = jnp.zeros_like(s_ref)

    row = lax.broadcasted_iota(jnp.int32, (chunk, chunk), 0)
    col = lax.broadcasted_iota(jnp.int32, (chunk, chunk), 1)
    causal = col <= row
    heads = range(GLA_HEADS)
    ks = [slice(h * GLA_DK, (h + 1) * GLA_DK) for h in heads]
    vs = [slice(h * GLA_DV, (h + 1) * GLA_DV) for h in heads]

    def state(s, h):
        if n_chunks > 1:
            return s_ref[s, h]
        return s0_ref[s, h] if has_init else jnp.zeros((GLA_DK, GLA_DV), F32)

    q_inter, k_state, scores, decay = {}, {}, {}, {}
    for s, h in chains:
        b = bcum_ref[s, :, ks[h]]
        b_last = b[chunk - 1:chunk, :]
        b_mid = b[chunk // 2 - 1:chunk // 2, :]
        q = q_ref[s, :, ks[h]].astype(F32)
        k = k_ref[s, :, ks[h]].astype(F32)
        q_inter[s, h] = (q * jnp.exp2(b)).astype(BF16)
        q_intra = (q * jnp.exp2(b - b_mid)).astype(BF16)
        k_intra = (k * jnp.exp2(b_mid - b)).astype(BF16)
        k_state[s, h] = (k * jnp.exp2(b_last - b)).astype(BF16)
        scores[s, h] = lax.dot_general(q_intra, k_intra, _NT_DIMS,
                                       preferred_element_type=F32)
        decay[s, h] = jnp.exp2(jnp.broadcast_to(b_last, (128, GLA_DK))).T

    o = {}
    for s, h in chains:
        sc = jnp.where(causal, scores[s, h], 0.0).astype(BF16)
        v = v_ref[s, :, vs[h]].astype(BF16)
        o[s, h] = (jnp.dot(q_inter[s, h], state(s, h).astype(BF16),
                           preferred_element_type=F32)
                   + jnp.dot(sc, v, preferred_element_type=F32))
        upd = lax.dot_general(k_state[s, h], v, _TN_DIMS, preferred_element_type=F32)
        s_new = (state(s, h) * jnp.concatenate([decay[s, h]] * (GLA_DV // 128), axis=1)
                 + upd)
        if n_chunks > 1:
            s_ref[s, h] = s_new
        else:
            sfin_ref[s, h] = s_new

    qs = GLA_DK ** -0.5
    scale = {sh: qs * lax.rsqrt(jnp.mean(o[sh] * o[sh], axis=-1, keepdims=True) * (qs * qs)
                                + RMS_EPS) for sh in chains}
    for s, h in chains:
        r = r_ref[s, :, vs[h]].astype(F32)
        o_ref[s, :, vs[h]] = (o[s, h] * scale[s, h] * gon_ref[:, vs[h]]
                              * _silu(r)).astype(o_ref.dtype)

    if n_chunks > 1:
        @pl.when(c == n_chunks - 1)
        def _():
            sfin_ref[...] = s_ref[...]


def _gla(proj, bcum, gon, *, batch, seq, chunk, out_dtype, s0=None):
    n = seq // chunk
    ns = GLA_SEQS_PER_STEP
    assert batch % ns == 0
    proj = proj.reshape(batch, seq, proj.shape[-1])
    bcum = bcum.reshape(batch, seq, bcum.shape[-1])
    kb, vb, rb = 1, 2 * GLA_KEY_DIM // GLA_VALUE_DIM, 2 * GLA_KEY_DIM // GLA_VALUE_DIM + 1
    in_specs = [
        pl.BlockSpec((ns, chunk, GLA_KEY_DIM), lambda b, c: (b, c, 0)),
        pl.BlockSpec((ns, chunk, GLA_KEY_DIM), lambda b, c: (b, c, kb)),
        pl.BlockSpec((ns, chunk, GLA_VALUE_DIM), lambda b, c: (b, c, vb)),
        pl.BlockSpec((ns, chunk, GLA_VALUE_DIM), lambda b, c: (b, c, rb)),
        pl.BlockSpec((ns, chunk, GLA_KEY_DIM), lambda b, c: (b, c, 0)),
        pl.BlockSpec((1, GLA_VALUE_DIM), lambda b, c: (0, 0)),
    ]
    args = [proj, proj, proj, proj, bcum, gon.reshape(1, -1)]
    state_spec = pl.BlockSpec((ns, GLA_HEADS, GLA_DK, GLA_DV), lambda b, c: (b, 0, 0, 0))
    if s0 is not None:
        in_specs.append(state_spec)
        args.append(s0)
    scratch = [pltpu.VMEM((ns, GLA_HEADS, GLA_DK, GLA_DV), F32)] if n > 1 else []
    o, s_fin = pl.pallas_call(
        functools.partial(_gla_kernel, has_init=s0 is not None, n_chunks=n),
        grid=(batch // ns, n),
        in_specs=in_specs,
        out_specs=[
            pl.BlockSpec((ns, chunk, GLA_VALUE_DIM), lambda b, c: (b, c, 0)),
            state_spec,
        ],
        out_shape=[
            jax.ShapeDtypeStruct((batch, seq, GLA_VALUE_DIM), out_dtype),
            jax.ShapeDtypeStruct((batch, GLA_HEADS, GLA_DK, GLA_DV), F32),
        ],
        scratch_shapes=scratch,
        compiler_params=_params(("parallel", "arbitrary")),
        name=f"gla_chunk{chunk}",
    )(*args)
    return o.reshape(batch * seq, GLA_VALUE_DIM), s_fin


def _alibi_slope(head):
    return 2.0 ** (-8.0 * (head + 1) / SWA_HEADS)


def _attn_prompt_kernel(sinks_ref, q_ref, z_ref, kp_ref, ko_ref, vp_ref, vo_ref, out_ref,
                        bias_ref):
    hd, nkeys = SWA_HEAD_DIM, 2 * WINDOW
    pair_w = 2 * hd
    pairs_per_group = SWA_GROUP // 2
    blk = pl.program_id(1)

    @pl.when((pl.program_id(0) == 0) & (blk == 0))
    def _():
        kj = lax.broadcasted_iota(jnp.int32, (nkeys, WINDOW), 0)
        qi = lax.broadcasted_iota(jnp.int32, (nkeys, WINDOW), 1)
        dist = WINDOW + qi - kj
        ok = (dist >= 0) & (dist <= WINDOW)
        ok_first = ok & (kj >= WINDOW)
        distf = dist.astype(F32)
        for h in range(SWA_HEADS):
            pen = (-_alibi_slope(h) * LOG2E) * distf
            sl = slice((h % 2) * WINDOW, (h % 2 + 1) * WINDOW)
            bias_ref[0, h // 2, :, sl] = jnp.where(ok_first, pen, -jnp.inf)
            bias_ref[1, h // 2, :, sl] = jnp.where(ok, pen, -jnp.inf)

    n_sub = q_ref.shape[0] // WINDOW
    k_rows = jnp.concatenate([kp_ref[...], ko_ref[...]], axis=0)
    v_rows = jnp.concatenate([vp_ref[...], vo_ref[...]], axis=0)
    ones = jnp.ones((16, nkeys), F32)
    lane = lax.broadcasted_iota(jnp.int32, (nkeys, pair_w), 1)
    qlane = lax.broadcasted_iota(jnp.int32, (WINDOW, pair_w), 1)
    qk_scale = (hd ** -0.5) * LOG2E
    quad_pairs = 2
    n_quads = SWA_HEADS // (2 * quad_pairs)
    quads_per_group = pairs_per_group // quad_pairs
    quarter = lax.broadcasted_iota(jnp.int32, (1, 2 * quad_pairs * WINDOW), 1) // WINDOW

    def prepare(sub):
        k = k_rows[sub * WINDOW:sub * WINDOW + nkeys]
        vt = v_rows[sub * WINDOW:sub * WINDOW + nkeys].T
        k2, vt1 = [], []
        for g in range(SWA_KV_HEADS):
            kblk = k[:, (g // 2) * pair_w:(g // 2 + 1) * pair_w]
            k_here = jnp.where((lane < hd) if g % 2 == 0 else (lane >= hd), kblk, 0.0)
            k2.append((k_here + pltpu.roll(k_here, hd, axis=1)).astype(BF16))
            vt1.append(jnp.concatenate([vt[g * hd:(g + 1) * hd], ones], axis=0).astype(BF16))
        return dict(qrows=slice(sub * WINDOW, (sub + 1) * WINDOW), k2=k2, vt1=vt1,
                    tbl=jnp.minimum(blk, 1) if sub == 0 else 1)

    def scores(blkst, quad):
        parts = []
        for j in range(quad_pairs):
            col = (quad * quad_pairs + j) * pair_w
            q_pair = q_ref[blkst["qrows"], col:col + pair_w]
            zero = jnp.zeros_like(q_pair)
            parts += [jnp.where(qlane < hd, q_pair, zero), jnp.where(qlane >= hd, q_pair, zero)]
        return lax.dot_general(blkst["k2"][quad // quads_per_group],
                               jnp.concatenate(parts, axis=0),
                               _NT_DIMS, preferred_element_type=F32)

    def finish(blkst, quad, st):
        pair0 = quad * quad_pairs
        bias = jnp.concatenate([bias_ref[blkst["tbl"], pair0 + j] for j in range(quad_pairs)],
                               axis=1)
        s2 = st * qk_scale + bias
        sink2 = sinks_ref[2 * pair0]
        for t in range(1, 2 * quad_pairs):
            sink2 = jnp.where(quarter == t, sinks_ref[2 * pair0 + t], sink2)
        sink2 = sink2 * LOG2E
        m = jnp.maximum(jnp.max(s2, axis=0, keepdims=True), sink2)
        p = jnp.exp2(s2 - m).astype(BF16)
        oa = jnp.dot(blkst["vt1"][quad // quads_per_group], p, preferred_element_type=F32)
        denom = oa[hd:hd + 1] + jnp.exp2(sink2 - m)
        on = oa[0:hd] * (1.0 / denom)
        for j in range(quad_pairs):
            lo = 2 * j * WINDOW
            o_pair = jnp.concatenate([on[:, lo:lo + WINDOW],
                                      on[:, lo + WINDOW:lo + 2 * WINDOW]], axis=0).T
            col = (pair0 + j) * pair_w
            z_pair = z_ref[blkst["qrows"], col:col + pair_w].astype(F32)
            out_ref[blkst["qrows"], col:col + pair_w] = (
                o_pair * _silu(z_pair)).astype(out_ref.dtype)

    for first in range(0, n_sub, ATTN_BLOCKS_IN_FLIGHT):
        blocks = [prepare(sub) for sub in range(first, min(first + ATTN_BLOCKS_IN_FLIGHT, n_sub))]
        st_next = [scores(b, 0) for b in blocks]
        for quad in range(n_quads):
            for i, b in enumerate(blocks):
                st = st_next[i]
                if quad + 1 < n_quads:
                    st_next[i] = scores(b, quad + 1)
                finish(b, quad, st)


def _attn_prompt(qz, kv, sinks, *, batch, seq):
    sub = ATTN_BLOCKS_PER_STEP
    tq = sub * WINDOW
    nb = seq // tq
    assert seq % tq == 0
    row = lambda b, i: b * nb + i
    prev = lambda b, i: (b * nb + i) * sub - jnp.minimum(i, 1)
    return pl.pallas_call(
        _attn_prompt_kernel,
        grid=(batch, nb),
        in_specs=[
            pl.BlockSpec(memory_space=pltpu.SMEM),
            pl.BlockSpec((tq, SWA_WIDTH), lambda b, i: (row(b, i), 0)),
            pl.BlockSpec((tq, SWA_WIDTH), lambda b, i: (row(b, i), 1)),
            pl.BlockSpec((WINDOW, SWA_KV_WIDTH), lambda b, i: (prev(b, i), 0)),
            pl.BlockSpec((tq, SWA_KV_WIDTH), lambda b, i: (row(b, i), 0)),
            pl.BlockSpec((WINDOW, SWA_KV_WIDTH), lambda b, i: (prev(b, i), 1)),
            pl.BlockSpec((tq, SWA_KV_WIDTH), lambda b, i: (row(b, i), 1)),
        ],
        out_specs=pl.BlockSpec((tq, SWA_WIDTH), lambda b, i: (row(b, i), 0)),
        out_shape=jax.ShapeDtypeStruct((batch * seq, SWA_WIDTH), BF16),
        scratch_shapes=[pltpu.VMEM((2, SWA_HEADS // 2, 2 * WINDOW, 2 * WINDOW), F32)],
        compiler_params=_params(("arbitrary", "arbitrary")),
        name="attn_prompt",
    )(sinks, qz, qz, kv, kv, kv, kv)


def _attn_sample_kernel(sink_ref, slope_ref, q_ref, z_ref, kn_ref, vn_ref, kc_ref, vc_ref,
                        out_ref, kwin_ref, vwin_ref, *, tq, nb):
    hd, nk = SWA_HEAD_DIM, 2 * WINDOW
    rows = SWA_HEADS * tq
    grows = SWA_GROUP * tq
    seqs = range(nb)
    groups = range(SWA_KV_HEADS)

    lane = lax.broadcasted_iota(jnp.int32, (rows, nk), 1)
    tok = lax.broadcasted_iota(jnp.int32, (rows, nk), 0) % tq
    in_buffer = lane < WINDOW
    dist = jnp.where(in_buffer, WINDOW + tok - lane, (nk - tq) + tok - lane)
    allowed = (dist >= 0) & (dist <= WINDOW) & (in_buffer | (lane >= nk - tq))
    penalty = slope_ref[...] * dist.astype(F32)
    new_lanes = lax.broadcasted_iota(jnp.int32, (hd, WINDOW), 1) >= WINDOW - tq

    def new_rows_t(ref, b):
        x = jnp.concatenate([jnp.zeros((WINDOW - tq, SWA_KV_WIDTH), F32),
                             ref[b * tq:(b + 1) * tq, :]], axis=0)
        xt = [x[:, c * WINDOW:(c + 1) * WINDOW].T for c in range(SWA_KV_WIDTH // WINDOW)]
        per_block = WINDOW // hd
        return [xt[g // per_block][(g % per_block) * hd:(g % per_block + 1) * hd]
                for g in groups]

    k_all, v_all = {}, {}
    for b in seqs:
        kn_t, vn_t = new_rows_t(kn_ref, b), new_rows_t(vn_ref, b)
        for g in groups:
            kc, vc = kc_ref[b, g], vc_ref[b, g]
            k_all[b, g] = jnp.concatenate([kc, kn_t[g]], axis=1).astype(BF16)
            v_all[b, g] = jnp.concatenate([vc, vn_t[g]], axis=1).astype(BF16)
            kwin_ref[b, g] = jnp.where(new_lanes, kn_t[g], pltpu.roll(kc, WINDOW - tq, axis=1))
            vwin_ref[b, g] = jnp.where(new_lanes, vn_t[g], pltpu.roll(vc, WINDOW - tq, axis=1))

    s = []
    for b in seqs:
        q = q_ref[b * tq:(b + 1) * tq, :]
        parts = []
        for g in groups:
            qs = jnp.concatenate([q[:, h * hd:(h + 1) * hd]
                                  for h in range(g * SWA_GROUP, (g + 1) * SWA_GROUP)], axis=0)
            parts.append(jnp.dot(qs.astype(BF16), k_all[b, g], preferred_element_type=F32))
        s.append(jnp.concatenate(parts, axis=0))

    sink = sink_ref[...]
    s = [jnp.where(allowed, sb * (hd ** -0.5) - penalty, -jnp.inf) for sb in s]
    m = [jnp.maximum(jnp.max(sb, axis=-1, keepdims=True), sink) for sb in s]
    p = [jnp.exp(s[b] - m[b]) for b in seqs]
    inv = [1.0 / (jnp.sum(p[b], axis=-1, keepdims=True) + jnp.exp(sink - m[b])) for b in seqs]

    for b in seqs:
        pb = p[b].astype(BF16)
        o = jnp.concatenate(
            [lax.dot_general(pb[g * grows:(g + 1) * grows], v_all[b, g], _NT_DIMS,
                             preferred_element_type=F32) for g in groups], axis=0) * inv[b]
        o = jnp.concatenate([o[h * tq:(h + 1) * tq] for h in range(SWA_HEADS)], axis=1)
        z = z_ref[b * tq:(b + 1) * tq, :]
        out_ref[b * tq:(b + 1) * tq, :] = (o * _silu(z)).astype(out_ref.dtype)


def _attn_sample(qz, kv, cache_k, cache_v, sinks, *, batch, seq):
    nb = SAMPLE_SEQS_PER_STEP
    assert batch % nb == 0
    rows = nb * seq
    sink_col = jnp.repeat(sinks, seq).reshape(SWA_HEADS * seq, 1)
    slope_col = jnp.repeat(jnp.asarray([_alibi_slope(h) for h in range(SWA_HEADS)], F32),
                           seq).reshape(SWA_HEADS * seq, 1)
    col_spec = pl.BlockSpec((SWA_HEADS * seq, 1), lambda i: (0, 0))
    win_spec = pl.BlockSpec((nb, SWA_KV_HEADS, SWA_HEAD_DIM, WINDOW), lambda i: (i, 0, 0, 0))
    win_shape = jax.ShapeDtypeStruct((batch, SWA_KV_HEADS, SWA_HEAD_DIM, WINDOW), F32)
    to_lanes = lambda c: jnp.transpose(c, (0, 2, 3, 1))
    att, k_win, v_win = pl.pallas_call(
        functools.partial(_attn_sample_kernel, tq=seq, nb=nb),
        grid=(batch // nb,),
        in_specs=[
            col_spec,
            col_spec,
            pl.BlockSpec((rows, SWA_WIDTH), lambda i: (i, 0)),
            pl.BlockSpec((rows, SWA_WIDTH), lambda i: (i, 1)),
            pl.BlockSpec((rows, SWA_KV_WIDTH), lambda i: (i, 0)),
            pl.BlockSpec((rows, SWA_KV_WIDTH), lambda i: (i, 1)),
            win_spec,
            win_spec,
        ],
        out_specs=[pl.BlockSpec((rows, SWA_WIDTH), lambda i: (i, 0)), win_spec, win_spec],
        out_shape=[jax.ShapeDtypeStruct((batch * seq, SWA_WIDTH), F32), win_shape, win_shape],
        compiler_params=_params(("parallel",)),
        name="attn_sample",
    )(sink_col, slope_col, qz, qz, kv, kv, to_lanes(cache_k), to_lanes(cache_v))
    return att, jnp.transpose(k_win, (0, 3, 1, 2)), jnp.transpose(v_win, (0, 3, 1, 2))


def _trunk(x, weights, *, batch, seq, row_tile, col_tile, res_tile, chunk, act_dtype,
           gla_init=None, cache=None, emit_weights=False):
    (g_norm_a, w_in_a, w_glow, w_gate_up, b_gate, g_onorm_a, w_out_a,
     g_norm_kv, w_kv, g_norm_b, w_in_b, sinks, w_out_b, g_final) = weights
    h = x.reshape(batch * seq, D_MODEL)
    res_tk = SAMPLE_K_TILE if emit_weights else None

    proj, glow, *w_in_a_bf16 = _norm_matmul(
        h, g_norm_a, w_in_a, tm=row_tile, tn=col_tile, n=GLA_MAIN_COLS, out_dtype=act_dtype,
        w_extra=w_glow, emit_weights=emit_weights, w_transposed=True)
    bcum = _gla_gates(glow, w_gate_up, b_gate, chunk=chunk, tm=min(row_tile, GATE_ROW_TILE))
    o, s_fin = _gla(proj, bcum, g_onorm_a, batch=batch, seq=seq,
                    chunk=chunk, out_dtype=act_dtype, s0=gla_init)
    h, *w_out_a_bf16 = _as_list(_matmul_residual(o, w_out_a, h, tm=res_tile, tk=res_tk,
                                                 emit_weights=emit_weights))

    qz, kv, *w_in_b_bf16 = _norm_matmul(
        h, g_norm_b, w_in_b, tm=row_tile, tn=QZ_COL_TILE, out_dtype=act_dtype,
        w_extra=w_kv, g_extra=g_norm_kv, emit_weights=emit_weights)
    if cache is None:
        att = _attn_prompt(qz, kv, sinks, batch=batch, seq=seq)
        kv_win = kv.reshape(batch, seq, 2 * SWA_KV_WIDTH)[:, seq - WINDOW:]
        kv_win = kv_win.reshape(batch, WINDOW, 2, SWA_KV_HEADS, SWA_HEAD_DIM)
        k_win, v_win = kv_win[:, :, 0], kv_win[:, :, 1]
    else:
        cache_k, cache_v = cache
        att, k_win, v_win = _attn_sample(qz, kv, cache_k, cache_v, sinks,
                                         batch=batch, seq=seq)
    y, *w_out_b_bf16 = _as_list(_matmul_residual(att, w_out_b, h, tm=res_tile, tk=res_tk,
                                                 g_final=g_final,
                                                 emit_weights=emit_weights))
    outs = (y.reshape(batch, seq, D_MODEL), s_fin[None], k_win, v_win)
    return outs, (w_in_a_bf16 + w_in_b_bf16 + w_out_a_bf16 + w_out_b_bf16)


def _as_list(x):
    return list(x) if isinstance(x, (list, tuple)) else [x]


def kernel(x_prompt, x_sample, state_gla, cache_k_win, cache_v_win, g_norm_a, w_in_a,
           w_gate_up, b_gate, g_onorm_a, w_out_a, g_norm_kv, w_kv, g_norm_b, w_in_b, sinks,
           w_out_b, g_final):
    assert w_in_a.shape[0] == 1 and w_in_b.shape[0] == 1, "one GLA layer, one SWA layer"
    assert cache_k_win.shape[1] == WINDOW
    pb, ps, _ = x_prompt.shape
    sb, ss, _ = x_sample.shape

    def weights(w_in_a_any, w_in_b_any, w_out_a_any, w_out_b_any):
        return (
            g_norm_a[0], w_in_a_any, w_in_a_t[0, GLA_MAIN_COLS:, :],
            w_gate_up[0], b_gate[0], g_onorm_a[0], w_out_a_any,
            g_norm_kv, w_kv.astype(BF16),
            g_norm_b[0], w_in_b_any, sinks[0], w_out_b_any, g_final,
        )

    w_in_a_t = jnp.swapaxes(w_in_a, 1, 2)

    (y_s, gla_s, k_s, v_s), bf16_weights = _trunk(
        x_sample, weights(w_in_a_t, w_in_b, w_out_a, w_out_b), batch=sb, seq=ss,
        row_tile=sb * ss,
        col_tile=SAMPLE_COL_TILE, res_tile=sb * ss, chunk=ss, act_dtype=F32,
        gla_init=state_gla.reshape(state_gla.shape[1:]),
        cache=(cache_k_win, cache_v_win), emit_weights=True)
    (y_p, gla_p, k_p, v_p), _ = _trunk(
        x_prompt, weights(*bf16_weights), batch=pb, seq=ps,
        row_tile=PROMPT_ROW_TILE, col_tile=PROMPT_COL_TILE, res_tile=RESIDUAL_ROW_TILE,
        chunk=GLA_PROMPT_CHUNK, act_dtype=BF16)
    return (y_p, y_s, gla_p, gla_s, k_p, v_p, k_s, v_s)
```

```python
import functools

import jax
import jax.numpy as jnp
from jax import lax
from jax.experimental import pallas as pl
from jax.experimental.pallas import tpu as pltpu

F32 = jnp.float32
BF16 = jnp.bfloat16

D_MODEL = 2048
GLA_HEADS = 4
GLA_KEY_DIM = D_MODEL // 2
GLA_VALUE_DIM = D_MODEL
GLA_DK = GLA_KEY_DIM // GLA_HEADS
GLA_DV = GLA_VALUE_DIM // GLA_HEADS
GLA_GATE_RANK = 16
GLA_GATE_TEMP = 16.0
GLA_MAIN_COLS = 2 * GLA_KEY_DIM + 2 * GLA_VALUE_DIM
SWA_HEAD_DIM = 64
SWA_HEADS = D_MODEL // SWA_HEAD_DIM
SWA_KV_HEADS = 4
SWA_GROUP = SWA_HEADS // SWA_KV_HEADS
SWA_WIDTH = SWA_HEADS * SWA_HEAD_DIM
SWA_KV_WIDTH = SWA_KV_HEADS * SWA_HEAD_DIM
WINDOW = 128
RMS_EPS = 1e-6
LOG2E = 1.4426950408889634

V7X_VMEM_LIMIT_BYTES = 61 * 1024 * 1024
GLA_PROMPT_CHUNK = 128
GLA_SEQS_PER_STEP = 4
ATTN_BLOCKS_PER_STEP = 4
ATTN_BLOCKS_IN_FLIGHT = 2
SAMPLE_SEQS_PER_STEP = 4
NORM_ROW_CHUNK = 256
GATE_CUMSUM_ROWS = 128
GATE_ROW_TILE = 1024
PROMPT_ROW_TILE = 1024
PROMPT_COL_TILE = 2048
SAMPLE_COL_TILE = 1024
QZ_COL_TILE = 2048
SAMPLE_K_TILE = 512
RESIDUAL_ROW_TILE = 512

_NT_DIMS = (((1,), (1,)), ((), ()))
_TN_DIMS = (((0,), (0,)), ((), ()))


def _params(semantics):
    return pltpu.CompilerParams(dimension_semantics=semantics,
                                vmem_limit_bytes=V7X_VMEM_LIMIT_BYTES)


def _silu(x):
    return x / (1.0 + jnp.exp(-x))


def _rms_scale(x):
    return lax.rsqrt(jnp.mean(x * x, axis=-1, keepdims=True) + RMS_EPS)


def _norm_matmul_kernel(x_ref, g_ref, w_ref, *rest, has_extra, extra_gain, emit_weights,
                        w_transposed):
    dims = _NT_DIMS if w_transposed else (((1,), (0,)), ((), ()))

    def mm(a, b):
        return lax.dot_general(a, b, dims, preferred_element_type=F32)

    rest = list(rest)
    g2_ref = rest.pop(0) if extra_gain else None
    w2_ref = rest.pop(0) if has_extra else None
    o_ref = rest.pop(0)
    o2_ref = rest.pop(0) if has_extra else None
    wb_ref = rest.pop(0) if emit_weights else None
    (xn_ref,) = rest

    first = pl.program_id(1) == 0

    def weights():
        w = w_ref[...].astype(BF16)
        if emit_weights:
            wb_ref[...] = w
        return w

    @pl.when(first)
    def _():
        tm = x_ref.shape[0]
        rc = min(tm, NORM_ROW_CHUNK)
        w = weights()
        for c in range(tm // rc):
            rows = slice(c * rc, (c + 1) * rc)
            x = x_ref[rows, :]
            xs = x * _rms_scale(x)
            xn = (xs * g_ref[...]).astype(BF16)
            xn_ref[rows, :] = xn
            o_ref[rows, :] = mm(xn, w).astype(o_ref.dtype)
            if has_extra:
                xn2 = (xs * g2_ref[...]).astype(BF16) if extra_gain else xn
                o2_ref[rows, :] = mm(xn2, w2_ref[...].astype(BF16)).astype(o2_ref.dtype)

    @pl.when(jnp.logical_not(first))
    def _():
        o_ref[...] = mm(xn_ref[...], weights()).astype(o_ref.dtype)


def _norm_matmul(x, g, w, *, tm, tn, out_dtype, n=None, w_extra=None, g_extra=None,
                 emit_weights=False, w_transposed=False):
    m, k = x.shape
    n_axis = -2 if w_transposed else -1
    n = w.shape[n_axis] if n is None else n
    assert n % tn == 0 and m % tm == 0
    assert not emit_weights or m == tm, "each weight block must be visited exactly once"
    grid = (m // tm, n // tn)
    w_block = (tn, k) if w_transposed else (k, tn)
    w_index = (lambda i, j: (j, 0)) if w_transposed else (lambda i, j: (0, j))
    if w.ndim == 3:
        w_spec = pl.BlockSpec((None,) + w_block, lambda i, j: (0,) + w_index(i, j))
    else:
        w_spec = pl.BlockSpec(w_block, w_index)
    in_specs = [
        pl.BlockSpec((tm, k), lambda i, j: (i, 0)),
        pl.BlockSpec((1, k), lambda i, j: (0, 0)),
        w_spec,
    ]
    out_shape = [jax.ShapeDtypeStruct((m, n), out_dtype)]
    out_specs = [pl.BlockSpec((tm, tn), lambda i, j: (i, j))]
    args = [x, g.reshape(1, k), w]
    if g_extra is not None:
        in_specs.append(pl.BlockSpec((1, k), lambda i, j: (0, 0)))
        args.append(g_extra.reshape(1, k))
    if w_extra is not None:
        n2 = w_extra.shape[n_axis]
        in_specs.append(pl.BlockSpec(w_extra.shape, lambda i, j: (0, 0)))
        out_shape.append(jax.ShapeDtypeStruct((m, n2), F32))
        out_specs.append(pl.BlockSpec((tm, n2), lambda i, j: (i, 0)))
        args.append(w_extra)
    if emit_weights:
        out_shape.append(jax.ShapeDtypeStruct((n, k) if w_transposed else (k, n), BF16))
        out_specs.append(pl.BlockSpec(w_block, w_index))
    res = pl.pallas_call(
        functools.partial(_norm_matmul_kernel, has_extra=w_extra is not None,
                          extra_gain=g_extra is not None, emit_weights=emit_weights,
                          w_transposed=w_transposed),
        grid=grid,
        in_specs=in_specs,
        out_specs=out_specs,
        out_shape=out_shape,
        scratch_shapes=[pltpu.VMEM((tm, k), BF16)],
        compiler_params=_params(("parallel", "arbitrary")),
        name=f"norm_matmul_{m}x{n}",
    )(*args)
    return res if len(res) > 1 else res[0]


def _matmul_residual_kernel(a_ref, w_ref, r_ref, *rest, final_norm, emit_weights, nk):
    rest = list(rest)
    g_ref = rest.pop(0) if final_norm else None
    o_ref = rest.pop(0)
    w = w_ref[...].astype(BF16)
    if emit_weights:
        rest[0][...] = w
    part = jnp.dot(a_ref[...].astype(BF16), w, preferred_element_type=F32)

    if nk == 1:
        h = r_ref[...] + part
        if final_norm:
            h = h * _rms_scale(h) * g_ref[...]
        o_ref[...] = h
        return

    kk = pl.program_id(1)

    @pl.when(kk == 0)
    def _():
        o_ref[...] = r_ref[...] + part

    @pl.when(kk > 0)
    def _():
        o_ref[...] += part

    if final_norm:
        @pl.when(kk == nk - 1)
        def _():
            h = o_ref[...]
            o_ref[...] = h * _rms_scale(h) * g_ref[...]


def _matmul_residual(a, w, res, *, tm, tk=None, g_final=None, emit_weights=False):
    m, k = a.shape
    n = w.shape[-1]
    tk = k if tk is None else tk
    assert m % tm == 0 and k % tk == 0
    assert not emit_weights or m == tm, "each weight block must be visited exactly once"
    if w.ndim == 3:
        w_spec = pl.BlockSpec((None, tk, n), lambda i, j: (0, j, 0))
    else:
        w_spec = pl.BlockSpec((tk, n), lambda i, j: (j, 0))
    in_specs = [
        pl.BlockSpec((tm, tk), lambda i, j: (i, j)),
        w_spec,
        pl.BlockSpec((tm, n), lambda i, j: (i, 0)),
    ]
    args = [a, w, res]
    if g_final is not None:
        in_specs.append(pl.BlockSpec((1, n), lambda i, j: (0, 0)))
        args.append(g_final.reshape(1, n))
    out_shape = [jax.ShapeDtypeStruct((m, n), F32)]
    out_specs = [pl.BlockSpec((tm, n), lambda i, j: (i, 0))]
    if emit_weights:
        out_shape.append(jax.ShapeDtypeStruct((k, n), BF16))
        out_specs.append(pl.BlockSpec((tk, n), lambda i, j: (j, 0)))
    res = pl.pallas_call(
        functools.partial(_matmul_residual_kernel, final_norm=g_final is not None,
                          emit_weights=emit_weights, nk=k // tk),
        grid=(m // tm, k // tk),
        in_specs=in_specs,
        out_specs=out_specs,
        out_shape=out_shape,
        compiler_params=_params(("parallel", "arbitrary")),
        name=f"matmul_residual_{m}" + ("_final" if g_final is not None else ""),
    )(*args)
    return res if emit_weights else res[0]


def _split_bf16(x):
    hi = x.astype(BF16)
    return hi, (x - hi.astype(F32)).astype(BF16)


def _gla_gates_kernel(glow_ref, wg_ref, bg_ref, bcum_ref, *, chunk):
    rows = glow_ref.shape[0]
    g_hi, g_lo = _split_bf16(glow_ref[...])
    w_hi, w_lo = _split_bf16(wg_ref[...])
    x = jnp.dot(jnp.concatenate([g_hi, g_lo, g_hi], axis=1),
                jnp.concatenate([w_hi, w_hi, w_lo], axis=0),
                preferred_element_type=F32) + bg_ref[...]
    softplus2 = jnp.log2(1.0 + jnp.exp2(jnp.abs(x) * (-LOG2E)))
    logg = jnp.minimum(x, 0.0) * (LOG2E / GLA_GATE_TEMP) - softplus2 * (1.0 / GLA_GATE_TEMP)

    span = GATE_CUMSUM_ROWS
    row = lax.broadcasted_iota(jnp.int32, (span, span), 0)
    col = lax.broadcasted_iota(jnp.int32, (span, span), 1)
    same_chunk = (row // chunk) == (col // chunk) if chunk < span else True
    tril = jnp.where((col <= row) & same_chunk, 1.0, 0.0).astype(BF16)
    tril2 = jnp.concatenate([tril, tril], axis=1)
    for i in range(rows // span):
        hi, lo = _split_bf16(logg[i * span:(i + 1) * span])
        bcum_ref[i * span:(i + 1) * span, :] = jnp.dot(
            tril2, jnp.concatenate([hi, lo], axis=0), preferred_element_type=F32)


def _gates_with_gla_kernel(glow_ref, wg_ref, bg_ref, *rest, chunk, n_gla_in):
    gla_in, (bcum_ref, o_ref, sfin_ref) = rest[:n_gla_in], rest[n_gla_in:]
    _gla_kernel(*gla_in, o_ref, sfin_ref, has_init=True, n_chunks=1)
    _gla_gates_kernel(glow_ref, wg_ref, bg_ref, bcum_ref, chunk=chunk)


def _gla_gates(glow, wg, bg, *, chunk, tm, rider=None):
    m = glow.shape[0]
    assert m % tm == 0 and tm % GATE_CUMSUM_ROWS == 0
    assert GATE_CUMSUM_ROWS % chunk == 0 or chunk % GATE_CUMSUM_ROWS == 0
    assert chunk <= GATE_CUMSUM_ROWS, "cumulative sums do not cross row spans"
    steps = m // tm
    in_specs = [
        pl.BlockSpec((tm, GLA_GATE_RANK), lambda i: (i, 0)),
        pl.BlockSpec((GLA_GATE_RANK, GLA_KEY_DIM), lambda i: (0, 0)),
        pl.BlockSpec((1, GLA_KEY_DIM), lambda i: (0, 0)),
    ]
    args = [glow, wg, bg.reshape(1, -1)]
    out_specs = [pl.BlockSpec((tm, GLA_KEY_DIM), lambda i: (i, 0))]
    out_shape = [jax.ShapeDtypeStruct((m, GLA_KEY_DIM), F32)]
    if rider is None:
        body = functools.partial(_gla_gates_kernel, chunk=chunk)
    else:
        batch, seq = rider["batch"], rider["seq"]
        assert batch % steps == 0
        g_in, g_args, g_out, g_shape = _gla_operands(
            rider["proj"], rider["bcum"], rider["gon"], rider["s0"], batch=batch, seq=seq,
            chunk=seq, ns=batch // steps, out_dtype=F32, index=lambda i: (i, 0))
        body = functools.partial(_gates_with_gla_kernel, chunk=chunk, n_gla_in=len(g_in))
        in_specs += g_in
        args += g_args
        out_specs += g_out
        out_shape += g_shape
    res = pl.pallas_call(
        body,
        grid=(steps,),
        in_specs=in_specs,
        out_specs=out_specs,
        out_shape=out_shape,
        compiler_params=_params(("parallel",)),
        name=f"gla_gates_{m}",
    )(*args)
    if rider is None:
        return res[0]
    bcum, o, s_fin = res
    return bcum, o.reshape(rider["batch"] * rider["seq"], GLA_VALUE_DIM), s_fin


def _gla_kernel(q_ref, k_ref, v_ref, r_ref, bcum_ref, gon_ref, *rest, has_init, n_chunks):
    rest = list(rest)
    s0_ref = rest.pop(0) if has_init else None
    o_ref, sfin_ref = rest[:2]
    s_ref = rest[2] if n_chunks > 1 else None
    c = pl.program_id(1) if n_chunks > 1 else None
    nseq, chunk = q_ref.shape[:2]
    chains = [(s, h) for s in range(nseq) for h in range(GLA_HEADS)]

    if n_chunks > 1:
        @pl.when(c == 0)
        def _():
            if has_init:
                s_ref[...] = s0_ref[...]
            else:
                s_ref[...] = jnp.zeros_like(s_ref)

    row = lax.broadcasted_iota(jnp.int32, (chunk, chunk), 0)
    col = lax.broadcasted_iota(jnp.int32, (chunk, chunk), 1)
    causal = col <= row
    heads = range(GLA_HEADS)
    ks = [slice(h * GLA_DK, (h + 1) * GLA_DK) for h in heads]
    vs = [slice(h * GLA_DV, (h + 1) * GLA_DV) for h in heads]

    def state(s, h):
        if n_chunks > 1:
            return s_ref[s, h]
        return s0_ref[s, h] if has_init else jnp.zeros((GLA_DK, GLA_DV), F32)

    q_inter, k_state, scores, decay = {}, {}, {}, {}
    for s, h in chains:
        b = bcum_ref[s, :, ks[h]]
        b_last = b[chunk - 1:chunk, :]
        b_mid = b[chunk // 2 - 1:chunk // 2, :]
        q = q_ref[s, :, ks[h]].astype(F32)
        k = k_ref[s, :, ks[h]].astype(F32)
        q_inter[s, h] = (q * jnp.exp2(b)).astype(BF16)
        q_intra = (q * jnp.exp2(b - b_mid)).astype(BF16)
        k_intra = (k * jnp.exp2(b_mid - b)).astype(BF16)
        k_state[s, h] = (k * jnp.exp2(b_last - b)).astype(BF16)
        scores[s, h] = lax.dot_general(q_intra, k_intra, _NT_DIMS,
                                       preferred_element_type=F32)
        decay[s, h] = jnp.exp2(jnp.broadcast_to(b_last, (128, GLA_DK))).T

    o = {}
    for s, h in chains:
        sc = jnp.where(causal, scores[s, h], 0.0).astype(BF16)
        v = v_ref[s, :, vs[h]].astype(BF16)
        o[s, h] = (jnp.dot(q_inter[s, h], state(s, h).astype(BF16),
                           preferred_element_type=F32)
                   + jnp.dot(sc, v, preferred_element_type=F32))
        upd = lax.dot_general(k_state[s, h], v, _TN_DIMS, preferred_element_type=F32)
        s_new = (state(s, h) * jnp.concatenate([decay[s, h]] * (GLA_DV // 128), axis=1)
                 + upd)
        if n_chunks > 1:
            s_ref[s, h] = s_new
        else:
            sfin_ref[s, h] = s_new

    qs = GLA_DK ** -0.5
    scale = {sh: qs * lax.rsqrt(jnp.mean(o[sh] * o[sh], axis=-1, keepdims=True) * (qs * qs)
                                + RMS_EPS) for sh in chains}
    for s, h in chains:
        r = r_ref[s, :, vs[h]].astype(F32)
        o_ref[s, :, vs[h]] = (o[s, h] * scale[s, h] * gon_ref[:, vs[h]]
                              * _silu(r)).astype(o_ref.dtype)

    if n_chunks > 1:
        @pl.when(c == n_chunks - 1)
        def _():
            sfin_ref[...] = s_ref[...]


def _gla_operands(proj, bcum, gon, s0, *, batch, seq, chunk, ns, out_dtype, index):
    assert batch % ns == 0 and seq % chunk == 0
    proj = proj.reshape(batch, seq, proj.shape[-1])
    bcum = bcum.reshape(batch, seq, bcum.shape[-1])
    kb, vb, rb = 1, 2 * GLA_KEY_DIM // GLA_VALUE_DIM, 2 * GLA_KEY_DIM // GLA_VALUE_DIM + 1

    def rows(width, col):
        return pl.BlockSpec((ns, chunk, width), lambda *g: (*index(*g), col))

    state_spec = pl.BlockSpec((ns, GLA_HEADS, GLA_DK, GLA_DV),
                              lambda *g: (index(*g)[0], 0, 0, 0))
    in_specs = [rows(GLA_KEY_DIM, 0), rows(GLA_KEY_DIM, kb), rows(GLA_VALUE_DIM, vb),
                rows(GLA_VALUE_DIM, rb), rows(GLA_KEY_DIM, 0),
                pl.BlockSpec((1, GLA_VALUE_DIM), lambda *g: (0, 0))]
    args = [proj, proj, proj, proj, bcum, gon.reshape(1, -1)]
    if s0 is not None:
        in_specs.append(state_spec)
        args.append(s0)
    out_specs = [rows(GLA_VALUE_DIM, 0), state_spec]
    out_shape = [jax.ShapeDtypeStruct((batch, seq, GLA_VALUE_DIM), out_dtype),
                 jax.ShapeDtypeStruct((batch, GLA_HEADS, GLA_DK, GLA_DV), F32)]
    return in_specs, args, out_specs, out_shape


def _gla(proj, bcum, gon, *, batch, seq, chunk, out_dtype, s0=None):
    n = seq // chunk
    ns = GLA_SEQS_PER_STEP
    in_specs, args, out_specs, out_shape = _gla_operands(
        proj, bcum, gon, s0, batch=batch, seq=seq, chunk=chunk, ns=ns, out_dtype=out_dtype,
        index=lambda b, c: (b, c))
    scratch = [pltpu.VMEM((ns, GLA_HEADS, GLA_DK, GLA_DV), F32)] if n > 1 else []
    o, s_fin = pl.pallas_call(
        functools.partial(_gla_kernel, has_init=s0 is not None, n_chunks=n),
        grid=(batch // ns, n),
        in_specs=in_specs,
        out_specs=out_specs,
        out_shape=out_shape,
        scratch_shapes=scratch,
        compiler_params=_params(("parallel", "arbitrary")),
        name=f"gla_chunk{chunk}",
    )(*args)
    return o.reshape(batch * seq, GLA_VALUE_DIM), s_fin


def _alibi_slope(head):
    return 2.0 ** (-8.0 * (head + 1) / SWA_HEADS)


def _attn_prompt_kernel(sinks_ref, q_ref, z_ref, kp_ref, ko_ref, vp_ref, vo_ref, out_ref,
                        bias_ref):
    hd, nkeys = SWA_HEAD_DIM, 2 * WINDOW
    pair_w = 2 * hd
    pairs_per_group = SWA_GROUP // 2
    blk = pl.program_id(1)

    @pl.when((pl.program_id(0) == 0) & (blk == 0))
    def _():
        kj = lax.broadcasted_iota(jnp.int32, (nkeys, WINDOW), 0)
        qi = lax.broadcasted_iota(jnp.int32, (nkeys, WINDOW), 1)
        dist = WINDOW + qi - kj
        ok = (dist >= 0) & (dist <= WINDOW)
        ok_first = ok & (kj >= WINDOW)
        distf = dist.astype(F32)
        for h in range(SWA_HEADS):
            pen = (-_alibi_slope(h) * LOG2E) * distf
            sl = slice((h % 2) * WINDOW, (h % 2 + 1) * WINDOW)
            bias_ref[0, h // 2, :, sl] = jnp.where(ok_first, pen, -jnp.inf)
            bias_ref[1, h // 2, :, sl] = jnp.where(ok, pen, -jnp.inf)

    n_sub = q_ref.shape[0] // WINDOW
    k_rows = jnp.concatenate([kp_ref[...], ko_ref[...]], axis=0)
    v_rows = jnp.concatenate([vp_ref[...], vo_ref[...]], axis=0)
    ones = jnp.ones((16, nkeys), F32)
    lane = lax.broadcasted_iota(jnp.int32, (nkeys, pair_w), 1)
    qlane = lax.broadcasted_iota(jnp.int32, (WINDOW, pair_w), 1)
    qk_scale = (hd ** -0.5) * LOG2E
    quad_pairs = 2
    n_quads = SWA_HEADS // (2 * quad_pairs)
    quads_per_group = pairs_per_group // quad_pairs
    quarter = lax.broadcasted_iota(jnp.int32, (1, 2 * quad_pairs * WINDOW), 1) // WINDOW

    def prepare(sub):
        k = k_rows[sub * WINDOW:sub * WINDOW + nkeys]
        vt = v_rows[sub * WINDOW:sub * WINDOW + nkeys].T
        k2, vt1 = [], []
        for g in range(SWA_KV_HEADS):
            kblk = k[:, (g // 2) * pair_w:(g // 2 + 1) * pair_w]
            k_here = jnp.where((lane < hd) if g % 2 == 0 else (lane >= hd), kblk, 0.0)
            k2.append((k_here + pltpu.roll(k_here, hd, axis=1)).astype(BF16))
            vt1.append(jnp.concatenate([vt[g * hd:(g + 1) * hd], ones], axis=0).astype(BF16))
        return dict(qrows=slice(sub * WINDOW, (sub + 1) * WINDOW), k2=k2, vt1=vt1,
                    tbl=jnp.minimum(blk, 1) if sub == 0 else 1)

    def scores(blkst, quad):
        parts = []
        for j in range(quad_pairs):
            col = (quad * quad_pairs + j) * pair_w
            q_pair = q_ref[blkst["qrows"], col:col + pair_w]
            zero = jnp.zeros_like(q_pair)
            parts += [jnp.where(qlane < hd, q_pair, zero), jnp.where(qlane >= hd, q_pair, zero)]
        return lax.dot_general(blkst["k2"][quad // quads_per_group],
                               jnp.concatenate(parts, axis=0),
                               _NT_DIMS, preferred_element_type=F32)

    def finish(blkst, quad, st):
        pair0 = quad * quad_pairs
        bias = jnp.concatenate([bias_ref[blkst["tbl"], pair0 + j] for j in range(quad_pairs)],
                               axis=1)
        s2 = st * qk_scale + bias
        sink2 = sinks_ref[2 * pair0]
        for t in range(1, 2 * quad_pairs):
            sink2 = jnp.where(quarter == t, sinks_ref[2 * pair0 + t], sink2)
        sink2 = sink2 * LOG2E
        m = jnp.maximum(jnp.max(s2, axis=0, keepdims=True), sink2)
        p = jnp.exp2(s2 - m).astype(BF16)
        oa = jnp.dot(blkst["vt1"][quad // quads_per_group], p, preferred_element_type=F32)
        denom = oa[hd:hd + 1] + jnp.exp2(sink2 - m)
        on = oa[0:hd] * (1.0 / denom)
        for j in range(quad_pairs):
            lo = 2 * j * WINDOW
            o_pair = jnp.concatenate([on[:, lo:lo + WINDOW],
                                      on[:, lo + WINDOW:lo + 2 * WINDOW]], axis=0).T
            col = (pair0 + j) * pair_w
            z_pair = z_ref[blkst["qrows"], col:col + pair_w].astype(F32)
            out_ref[blkst["qrows"], col:col + pair_w] = (
                o_pair * _silu(z_pair)).astype(out_ref.dtype)

    for first in range(0, n_sub, ATTN_BLOCKS_IN_FLIGHT):
        blocks = [prepare(sub) for sub in range(first, min(first + ATTN_BLOCKS_IN_FLIGHT, n_sub))]
        st_next = [scores(b, 0) for b in blocks]
        for quad in range(n_quads):
            for i, b in enumerate(blocks):
                st = st_next[i]
                if quad + 1 < n_quads:
                    st_next[i] = scores(b, quad + 1)
                finish(b, quad, st)


def _attn_prompt(qz, kv, sinks, *, batch, seq):
    sub = ATTN_BLOCKS_PER_STEP
    tq = sub * WINDOW
    nb = seq // tq
    assert seq % tq == 0
    row = lambda b, i: b * nb + i
    prev = lambda b, i: (b * nb + i) * sub - jnp.minimum(i, 1)
    return pl.pallas_call(
        _attn_prompt_kernel,
        grid=(batch, nb),
        in_specs=[
            pl.BlockSpec(memory_space=pltpu.SMEM),
            pl.BlockSpec((tq, SWA_WIDTH), lambda b, i: (row(b, i), 0)),
            pl.BlockSpec((tq, SWA_WIDTH), lambda b, i: (row(b, i), 1)),
            pl.BlockSpec((WINDOW, SWA_KV_WIDTH), lambda b, i: (prev(b, i), 0)),
            pl.BlockSpec((tq, SWA_KV_WIDTH), lambda b, i: (row(b, i), 0)),
            pl.BlockSpec((WINDOW, SWA_KV_WIDTH), lambda b, i: (prev(b, i), 1)),
            pl.BlockSpec((tq, SWA_KV_WIDTH), lambda b, i: (row(b, i), 1)),
        ],
        out_specs=pl.BlockSpec((tq, SWA_WIDTH), lambda b, i: (row(b, i), 0)),
        out_shape=jax.ShapeDtypeStruct((batch * seq, SWA_WIDTH), BF16),
        scratch_shapes=[pltpu.VMEM((2, SWA_HEADS // 2, 2 * WINDOW, 2 * WINDOW), F32)],
        compiler_params=_params(("arbitrary", "arbitrary")),
        name="attn_prompt",
    )(sinks, qz, qz, kv, kv, kv, kv)


def _attn_sample_kernel(sink_ref, slope_ref, q_ref, z_ref, kn_ref, vn_ref, kc_ref, vc_ref,
                        out_ref, kwin_ref, vwin_ref, *, tq, nb):
    hd, nk = SWA_HEAD_DIM, 2 * WINDOW
    rows = SWA_HEADS * tq
    grows = SWA_GROUP * tq
    seqs = range(nb)
    groups = range(SWA_KV_HEADS)

    lane = lax.broadcasted_iota(jnp.int32, (rows, nk), 1)
    tok = lax.broadcasted_iota(jnp.int32, (rows, nk), 0) % tq
    in_buffer = lane < WINDOW
    dist = jnp.where(in_buffer, WINDOW + tok - lane, (nk - tq) + tok - lane)
    allowed = (dist >= 0) & (dist <= WINDOW) & (in_buffer | (lane >= nk - tq))
    penalty = slope_ref[...] * dist.astype(F32)
    new_lanes = lax.broadcasted_iota(jnp.int32, (hd, WINDOW), 1) >= WINDOW - tq

    def new_rows_t(ref, b):
        x = jnp.concatenate([jnp.zeros((WINDOW - tq, SWA_KV_WIDTH), F32),
                             ref[b * tq:(b + 1) * tq, :]], axis=0)
        xt = [x[:, c * WINDOW:(c + 1) * WINDOW].T for c in range(SWA_KV_WIDTH // WINDOW)]
        per_block = WINDOW // hd
        return [xt[g // per_block][(g % per_block) * hd:(g % per_block + 1) * hd]
                for g in groups]

    k_all, v_all = {}, {}
    for b in seqs:
        kn_t, vn_t = new_rows_t(kn_ref, b), new_rows_t(vn_ref, b)
        for g in groups:
            kc, vc = kc_ref[b, g], vc_ref[b, g]
            k_all[b, g] = jnp.concatenate([kc, kn_t[g]], axis=1).astype(BF16)
            v_all[b, g] = jnp.concatenate([vc, vn_t[g]], axis=1).astype(BF16)
            kwin_ref[b, g] = jnp.where(new_lanes, kn_t[g], pltpu.roll(kc, WINDOW - tq, axis=1))
            vwin_ref[b, g] = jnp.where(new_lanes, vn_t[g], pltpu.roll(vc, WINDOW - tq, axis=1))

    s = []
    for b in seqs:
        q = q_ref[b * tq:(b + 1) * tq, :]
        parts = []
        for g in groups:
            qs = jnp.concatenate([q[:, h * hd:(h + 1) * hd]
                                  for h in range(g * SWA_GROUP, (g + 1) * SWA_GROUP)], axis=0)
            parts.append(jnp.dot(qs.astype(BF16), k_all[b, g], preferred_element_type=F32))
        s.append(jnp.concatenate(parts, axis=0))

    sink = sink_ref[...]
    s = [jnp.where(allowed, sb * (hd ** -0.5) - penalty, -jnp.inf) for sb in s]
    m = [jnp.maximum(jnp.max(sb, axis=-1, keepdims=True), sink) for sb in s]
    p = [jnp.exp(s[b] - m[b]) for b in seqs]
    inv = [1.0 / (jnp.sum(p[b], axis=-1, keepdims=True) + jnp.exp(sink - m[b])) for b in seqs]

    for b in seqs:
        pb = p[b].astype(BF16)
        o = jnp.concatenate(
            [lax.dot_general(pb[g * grows:(g + 1) * grows], v_all[b, g], _NT_DIMS,
                             preferred_element_type=F32) for g in groups], axis=0) * inv[b]
        o = jnp.concatenate([o[h * tq:(h + 1) * tq] for h in range(SWA_HEADS)], axis=1)
        z = z_ref[b * tq:(b + 1) * tq, :]
        out_ref[b * tq:(b + 1) * tq, :] = (o * _silu(z)).astype(out_ref.dtype)


def _attn_sample(qz, kv, cache_k, cache_v, sinks, *, batch, seq):
    nb = SAMPLE_SEQS_PER_STEP
    assert batch % nb == 0
    rows = nb * seq
    sink_col = jnp.repeat(sinks, seq).reshape(SWA_HEADS * seq, 1)
    slope_col = jnp.repeat(jnp.asarray([_alibi_slope(h) for h in range(SWA_HEADS)], F32),
                           seq).reshape(SWA_HEADS * seq, 1)
    col_spec = pl.BlockSpec((SWA_HEADS * seq, 1), lambda i: (0, 0))
    win_spec = pl.BlockSpec((nb, SWA_KV_HEADS, SWA_HEAD_DIM, WINDOW), lambda i: (i, 0, 0, 0))
    win_shape = jax.ShapeDtypeStruct((batch, SWA_KV_HEADS, SWA_HEAD_DIM, WINDOW), F32)
    to_lanes = lambda c: jnp.transpose(c, (0, 2, 3, 1))
    att, k_win, v_win = pl.pallas_call(
        functools.partial(_attn_sample_kernel, tq=seq, nb=nb),
        grid=(batch // nb,),
        in_specs=[
            col_spec,
            col_spec,
            pl.BlockSpec((rows, SWA_WIDTH), lambda i: (i, 0)),
            pl.BlockSpec((rows, SWA_WIDTH), lambda i: (i, 1)),
            pl.BlockSpec((rows, SWA_KV_WIDTH), lambda i: (i, 0)),
            pl.BlockSpec((rows, SWA_KV_WIDTH), lambda i: (i, 1)),
            win_spec,
            win_spec,
        ],
        out_specs=[pl.BlockSpec((rows, SWA_WIDTH), lambda i: (i, 0)), win_spec, win_spec],
        out_shape=[jax.ShapeDtypeStruct((batch * seq, SWA_WIDTH), F32), win_shape, win_shape],
        compiler_params=_params(("parallel",)),
        name="attn_sample",
    )(sink_col, slope_col, qz, qz, kv, kv, to_lanes(cache_k), to_lanes(cache_v))
    return att, jnp.transpose(k_win, (0, 3, 1, 2)), jnp.transpose(v_win, (0, 3, 1, 2))


def kernel(x_prompt, x_sample, state_gla, cache_k_win, cache_v_win, g_norm_a, w_in_a,
           w_gate_up, b_gate, g_onorm_a, w_out_a, g_norm_kv, w_kv, g_norm_b, w_in_b, sinks,
           w_out_b, g_final):
    assert w_in_a.shape[0] == 1 and w_in_b.shape[0] == 1, "one GLA layer, one SWA layer"
    assert cache_k_win.shape[1] == WINDOW
    pb, ps, _ = x_prompt.shape
    sb, ss, _ = x_sample.shape
    pm, sm = pb * ps, sb * ss
    g_norm_a, w_gate_up, b_gate, g_onorm_a = g_norm_a[0], w_gate_up[0], b_gate[0], g_onorm_a[0]
    g_norm_b, sinks = g_norm_b[0], sinks[0]
    w_kv = w_kv.astype(BF16)
    w_in_a_t = jnp.swapaxes(w_in_a, 1, 2)
    w_glow = w_in_a_t[0, GLA_MAIN_COLS:, :]
    h_p = x_prompt.reshape(pm, D_MODEL)
    h_s = x_sample.reshape(sm, D_MODEL)

    proj_s, glow_s, w_in_a_bf16 = _norm_matmul(
        h_s, g_norm_a, w_in_a_t, tm=sm, tn=SAMPLE_COL_TILE, n=GLA_MAIN_COLS, out_dtype=F32,
        w_extra=w_glow, emit_weights=True, w_transposed=True)
    bcum_s = _gla_gates(glow_s, w_gate_up, b_gate, chunk=ss, tm=sm)
    proj_p, glow_p = _norm_matmul(
        h_p, g_norm_a, w_in_a_bf16, tm=PROMPT_ROW_TILE, tn=PROMPT_COL_TILE, n=GLA_MAIN_COLS,
        out_dtype=BF16, w_extra=w_glow, w_transposed=True)
    bcum_p, o_s, gla_s = _gla_gates(
        glow_p, w_gate_up, b_gate, chunk=GLA_PROMPT_CHUNK, tm=GATE_ROW_TILE,
        rider=dict(proj=proj_s, bcum=bcum_s, gon=g_onorm_a, batch=sb, seq=ss,
                   s0=state_gla.reshape(state_gla.shape[1:])))
    h_s, w_out_a_bf16 = _matmul_residual(o_s, w_out_a, h_s, tm=sm, tk=SAMPLE_K_TILE,
                                         emit_weights=True)
    o_p, gla_p = _gla(proj_p, bcum_p, g_onorm_a, batch=pb, seq=ps, chunk=GLA_PROMPT_CHUNK,
                      out_dtype=BF16)
    h_p = _matmul_residual(o_p, w_out_a_bf16, h_p, tm=RESIDUAL_ROW_TILE)

    qz_s, kv_s, w_in_b_bf16 = _norm_matmul(
        h_s, g_norm_b, w_in_b, tm=sm, tn=QZ_COL_TILE, out_dtype=F32,
        w_extra=w_kv, g_extra=g_norm_kv, emit_weights=True)
    att_s, k_s, v_s = _attn_sample(qz_s, kv_s, cache_k_win, cache_v_win, sinks,
                                   batch=sb, seq=ss)
    y_s, w_out_b_bf16 = _matmul_residual(att_s, w_out_b, h_s, tm=sm, tk=SAMPLE_K_TILE,
                                         g_final=g_final, emit_weights=True)
    qz_p, kv_p = _norm_matmul(
        h_p, g_norm_b, w_in_b_bf16, tm=PROMPT_ROW_TILE, tn=QZ_COL_TILE, out_dtype=BF16,
        w_extra=w_kv, g_extra=g_norm_kv)
    att_p = _attn_prompt(qz_p, kv_p, sinks, batch=pb, seq=ps)
    y_p = _matmul_residual(att_p, w_out_b_bf16, h_p, tm=RESIDUAL_ROW_TILE, g_final=g_final)

    kv_win = kv_p.reshape(pb, ps, 2 * SWA_KV_WIDTH)[:, ps - WINDOW:]
    kv_win = kv_win.reshape(pb, WINDOW, 2, SWA_KV_HEADS, SWA_HEAD_DIM)
    return (y_p.reshape(pb, ps, D_MODEL), y_s.reshape(sb, ss, D_MODEL), gla_p[None],
            gla_s[None], kv_win[:, :, 0], kv_win[:, :, 1], k_s, v_s)
```

```python
import functools

import jax
import jax.numpy as jnp
from jax import lax
from jax.experimental import pallas as pl
from jax.experimental.pallas import tpu as pltpu

F32 = jnp.float32
BF16 = jnp.bfloat16

D_MODEL = 2048
GLA_HEADS = 4
GLA_KEY_DIM = D_MODEL // 2
GLA_VALUE_DIM = D_MODEL
GLA_DK = GLA_KEY_DIM // GLA_HEADS
GLA_DV = GLA_VALUE_DIM // GLA_HEADS
GLA_GATE_RANK = 16
GLA_GATE_TEMP = 16.0
GLA_MAIN_COLS = 2 * GLA_KEY_DIM + 2 * GLA_VALUE_DIM
SWA_HEAD_DIM = 64
SWA_HEADS = D_MODEL // SWA_HEAD_DIM
SWA_KV_HEADS = 4
SWA_GROUP = SWA_HEADS // SWA_KV_HEADS
SWA_WIDTH = SWA_HEADS * SWA_HEAD_DIM
SWA_KV_WIDTH = SWA_KV_HEADS * SWA_HEAD_DIM
WINDOW = 128
RMS_EPS = 1e-6
LOG2E = 1.4426950408889634

V7X_VMEM_LIMIT_BYTES = 61 * 1024 * 1024
GLA_PROMPT_CHUNK = 128
GLA_SEQS_PER_STEP = 4
ATTN_BLOCKS_PER_STEP = 4
ATTN_BLOCKS_IN_FLIGHT = 2
SAMPLE_SEQS_PER_STEP = 4
NORM_ROW_CHUNK = 256
GATE_CUMSUM_ROWS = 128
GATE_ROW_TILE = 1024
PROMPT_ROW_TILE = 1024
PROMPT_COL_TILE = 2048
SAMPLE_COL_TILE = 1024
QZ_COL_TILE = 2048
SAMPLE_K_TILE = 512
RESIDUAL_ROW_TILE = 512

_NT_DIMS = (((1,), (1,)), ((), ()))
_TN_DIMS = (((0,), (0,)), ((), ()))


def _params(semantics):
    return pltpu.CompilerParams(dimension_semantics=semantics,
                                vmem_limit_bytes=V7X_VMEM_LIMIT_BYTES)


def _silu(x):
    return x / (1.0 + jnp.exp(-x))


def _rms_scale(x):
    return lax.rsqrt(jnp.mean(x * x, axis=-1, keepdims=True) + RMS_EPS)


def _norm_matmul_kernel(x_ref, g_ref, w_ref, *rest, has_extra, extra_gain, emit_weights,
                        w_transposed):
    dims = _NT_DIMS if w_transposed else (((1,), (0,)), ((), ()))

    def mm(a, b):
        return lax.dot_general(a, b, dims, preferred_element_type=F32)

    rest = list(rest)
    g2_ref = rest.pop(0) if extra_gain else None
    w2_ref = rest.pop(0) if has_extra else None
    o_ref = rest.pop(0)
    o2_ref = rest.pop(0) if has_extra else None
    wb_ref = rest.pop(0) if emit_weights else None
    (xn_ref,) = rest

    first = pl.program_id(1) == 0

    def weights():
        w = w_ref[...].astype(BF16)
        if emit_weights:
            wb_ref[...] = w
        return w

    @pl.when(first)
    def _():
        tm = x_ref.shape[0]
        rc = min(tm, NORM_ROW_CHUNK)
        w = weights()
        for c in range(tm // rc):
            rows = slice(c * rc, (c + 1) * rc)
            x = x_ref[rows, :]
            xs = x * _rms_scale(x)
            xn = (xs * g_ref[...]).astype(BF16)
            xn_ref[rows, :] = xn
            o_ref[rows, :] = mm(xn, w).astype(o_ref.dtype)
            if has_extra:
                xn2 = (xs * g2_ref[...]).astype(BF16) if extra_gain else xn
                o2_ref[rows, :] = mm(xn2, w2_ref[...].astype(BF16)).astype(o2_ref.dtype)

    @pl.when(jnp.logical_not(first))
    def _():
        o_ref[...] = mm(xn_ref[...], weights()).astype(o_ref.dtype)


def _norm_matmul_operands(x, g, w, *, tm, tn, out_dtype, n=None, w_extra=None, g_extra=None,
                          emit_weights=False, w_transposed=False):
    m, k = x.shape
    n_axis = -2 if w_transposed else -1
    n = w.shape[n_axis] if n is None else n
    assert n % tn == 0 and m % tm == 0
    assert not emit_weights or m == tm, "each weight block must be visited exactly once"
    grid = (m // tm, n // tn)
    w_block = (tn, k) if w_transposed else (k, tn)
    w_index = (lambda i, j: (j, 0)) if w_transposed else (lambda i, j: (0, j))
    if w.ndim == 3:
        w_spec = pl.BlockSpec((None,) + w_block, lambda i, j: (0,) + w_index(i, j))
    else:
        w_spec = pl.BlockSpec(w_block, w_index)
    in_specs = [
        pl.BlockSpec((tm, k), lambda i, j: (i, 0)),
        pl.BlockSpec((1, k), lambda i, j: (0, 0)),
        w_spec,
    ]
    out_shape = [jax.ShapeDtypeStruct((m, n), out_dtype)]
    out_specs = [pl.BlockSpec((tm, tn), lambda i, j: (i, j))]
    args = [x, g.reshape(1, k), w]
    if g_extra is not None:
        in_specs.append(pl.BlockSpec((1, k), lambda i, j: (0, 0)))
        args.append(g_extra.reshape(1, k))
    if w_extra is not None:
        n2 = w_extra.shape[n_axis]
        in_specs.append(pl.BlockSpec(w_extra.shape, lambda i, j: (0, 0)))
        out_shape.append(jax.ShapeDtypeStruct((m, n2), F32))
        out_specs.append(pl.BlockSpec((tm, n2), lambda i, j: (i, 0)))
        args.append(w_extra)
    if emit_weights:
        out_shape.append(jax.ShapeDtypeStruct((n, k) if w_transposed else (k, n), BF16))
        out_specs.append(pl.BlockSpec(w_block, w_index))
    return dict(
        kernel=functools.partial(_norm_matmul_kernel, has_extra=w_extra is not None,
                                 extra_gain=g_extra is not None, emit_weights=emit_weights,
                                 w_transposed=w_transposed),
        grid=grid, in_specs=in_specs, args=args, out_specs=out_specs, out_shape=out_shape,
        scratch_shapes=[pltpu.VMEM((tm, k), BF16)], name=f"norm_matmul_{m}x{n}")


def _norm_matmul(x, g, w, **kwargs):
    ops = _norm_matmul_operands(x, g, w, **kwargs)
    res = pl.pallas_call(
        ops["kernel"],
        grid=ops["grid"],
        in_specs=ops["in_specs"],
        out_specs=ops["out_specs"],
        out_shape=ops["out_shape"],
        scratch_shapes=ops["scratch_shapes"],
        compiler_params=_params(("parallel", "arbitrary")),
        name=ops["name"],
    )(*ops["args"])
    return res if len(res) > 1 else res[0]


def _matmul_residual_kernel(a_ref, w_ref, r_ref, *rest, final_norm, emit_weights, nk):
    rest = list(rest)
    g_ref = rest.pop(0) if final_norm else None
    o_ref = rest.pop(0)
    w = w_ref[...].astype(BF16)
    if emit_weights:
        rest[0][...] = w
    part = jnp.dot(a_ref[...].astype(BF16), w, preferred_element_type=F32)

    if nk == 1:
        h = r_ref[...] + part
        if final_norm:
            h = h * _rms_scale(h) * g_ref[...]
        o_ref[...] = h
        return

    kk = pl.program_id(1)

    @pl.when(kk == 0)
    def _():
        o_ref[...] = r_ref[...] + part

    @pl.when(kk > 0)
    def _():
        o_ref[...] += part

    if final_norm:
        @pl.when(kk == nk - 1)
        def _():
            h = o_ref[...]
            o_ref[...] = h * _rms_scale(h) * g_ref[...]


def _matmul_residual(a, w, res, *, tm, tk=None, g_final=None, emit_weights=False):
    m, k = a.shape
    n = w.shape[-1]
    tk = k if tk is None else tk
    assert m % tm == 0 and k % tk == 0
    assert not emit_weights or m == tm, "each weight block must be visited exactly once"
    if w.ndim == 3:
        w_spec = pl.BlockSpec((None, tk, n), lambda i, j: (0, j, 0))
    else:
        w_spec = pl.BlockSpec((tk, n), lambda i, j: (j, 0))
    in_specs = [
        pl.BlockSpec((tm, tk), lambda i, j: (i, j)),
        w_spec,
        pl.BlockSpec((tm, n), lambda i, j: (i, 0)),
    ]
    args = [a, w, res]
    if g_final is not None:
        in_specs.append(pl.BlockSpec((1, n), lambda i, j: (0, 0)))
        args.append(g_final.reshape(1, n))
    out_shape = [jax.ShapeDtypeStruct((m, n), F32)]
    out_specs = [pl.BlockSpec((tm, n), lambda i, j: (i, 0))]
    if emit_weights:
        out_shape.append(jax.ShapeDtypeStruct((k, n), BF16))
        out_specs.append(pl.BlockSpec((tk, n), lambda i, j: (j, 0)))
    res = pl.pallas_call(
        functools.partial(_matmul_residual_kernel, final_norm=g_final is not None,
                          emit_weights=emit_weights, nk=k // tk),
        grid=(m // tm, k // tk),
        in_specs=in_specs,
        out_specs=out_specs,
        out_shape=out_shape,
        compiler_params=_params(("parallel", "arbitrary")),
        name=f"matmul_residual_{m}" + ("_final" if g_final is not None else ""),
    )(*args)
    return res if emit_weights else res[0]


def _split_bf16(x):
    hi = x.astype(BF16)
    return hi, (x - hi.astype(F32)).astype(BF16)


def _gla_gates_kernel(glow_ref, wg_ref, bg_ref, bcum_ref, *, chunk):
    rows = glow_ref.shape[0]
    g_hi, g_lo = _split_bf16(glow_ref[...])
    w_hi, w_lo = _split_bf16(wg_ref[...])
    x = jnp.dot(jnp.concatenate([g_hi, g_lo, g_hi], axis=1),
                jnp.concatenate([w_hi, w_hi, w_lo], axis=0),
                preferred_element_type=F32) + bg_ref[...]
    softplus2 = jnp.log2(1.0 + jnp.exp2(jnp.abs(x) * (-LOG2E)))
    logg = jnp.minimum(x, 0.0) * (LOG2E / GLA_GATE_TEMP) - softplus2 * (1.0 / GLA_GATE_TEMP)

    span = GATE_CUMSUM_ROWS
    row = lax.broadcasted_iota(jnp.int32, (span, span), 0)
    col = lax.broadcasted_iota(jnp.int32, (span, span), 1)
    same_chunk = (row // chunk) == (col // chunk) if chunk < span else True
    tril = jnp.where((col <= row) & same_chunk, 1.0, 0.0).astype(BF16)
    tril2 = jnp.concatenate([tril, tril], axis=1)
    for i in range(rows // span):
        hi, lo = _split_bf16(logg[i * span:(i + 1) * span])
        bcum_ref[i * span:(i + 1) * span, :] = jnp.dot(
            tril2, jnp.concatenate([hi, lo], axis=0), preferred_element_type=F32)


def _gates_with_gla_kernel(glow_ref, wg_ref, bg_ref, *rest, chunk, n_gla_in):
    gla_in, (bcum_ref, o_ref, sfin_ref) = rest[:n_gla_in], rest[n_gla_in:]
    _gla_kernel(*gla_in, o_ref, sfin_ref, has_init=True, n_chunks=1)
    _gla_gates_kernel(glow_ref, wg_ref, bg_ref, bcum_ref, chunk=chunk)


def _gla_gates(glow, wg, bg, *, chunk, tm, rider=None):
    m = glow.shape[0]
    assert m % tm == 0 and tm % GATE_CUMSUM_ROWS == 0
    assert GATE_CUMSUM_ROWS % chunk == 0 or chunk % GATE_CUMSUM_ROWS == 0
    assert chunk <= GATE_CUMSUM_ROWS, "cumulative sums do not cross row spans"
    steps = m // tm
    in_specs = [
        pl.BlockSpec((tm, GLA_GATE_RANK), lambda i: (i, 0)),
        pl.BlockSpec((GLA_GATE_RANK, GLA_KEY_DIM), lambda i: (0, 0)),
        pl.BlockSpec((1, GLA_KEY_DIM), lambda i: (0, 0)),
    ]
    args = [glow, wg, bg.reshape(1, -1)]
    out_specs = [pl.BlockSpec((tm, GLA_KEY_DIM), lambda i: (i, 0))]
    out_shape = [jax.ShapeDtypeStruct((m, GLA_KEY_DIM), F32)]
    if rider is None:
        body = functools.partial(_gla_gates_kernel, chunk=chunk)
    else:
        batch, seq = rider["batch"], rider["seq"]
        assert batch % steps == 0
        g_in, g_args, g_out, g_shape = _gla_operands(
            rider["proj"], rider["bcum"], rider["gon"], rider["s0"], batch=batch, seq=seq,
            chunk=seq, ns=batch // steps, out_dtype=F32, index=lambda i: (i, 0))
        body = functools.partial(_gates_with_gla_kernel, chunk=chunk, n_gla_in=len(g_in))
        in_specs += g_in
        args += g_args
        out_specs += g_out
        out_shape += g_shape
    res = pl.pallas_call(
        body,
        grid=(steps,),
        in_specs=in_specs,
        out_specs=out_specs,
        out_shape=out_shape,
        compiler_params=_params(("parallel",)),
        name=f"gla_gates_{m}",
    )(*args)
    if rider is None:
        return res[0]
    bcum, o, s_fin = res
    return bcum, o.reshape(rider["batch"] * rider["seq"], GLA_VALUE_DIM), s_fin


def _gla_kernel(q_ref, k_ref, v_ref, r_ref, bcum_ref, gon_ref, *rest, has_init, n_chunks):
    rest = list(rest)
    s0_ref = rest.pop(0) if has_init else None
    o_ref, sfin_ref = rest[:2]
    s_ref = rest[2] if n_chunks > 1 else None
    c = pl.program_id(1) if n_chunks > 1 else None
    nseq, chunk = q_ref.shape[:2]
    chains = [(s, h) for s in range(nseq) for h in range(GLA_HEADS)]

    if n_chunks > 1:
        @pl.when(c == 0)
        def _():
            if has_init:
                s_ref[...] = s0_ref[...]
            else:
                s_ref[...] = jnp.zeros_like(s_ref)

    row = lax.broadcasted_iota(jnp.int32, (chunk, chunk), 0)
    col = lax.broadcasted_iota(jnp.int32, (chunk, chunk), 1)
    causal = col <= row
    heads = range(GLA_HEADS)
    ks = [slice(h * GLA_DK, (h + 1) * GLA_DK) for h in heads]
    vs = [slice(h * GLA_DV, (h + 1) * GLA_DV) for h in heads]

    def state(s, h):
        if n_chunks > 1:
            return s_ref[s, h]
        return s0_ref[s, h] if has_init else jnp.zeros((GLA_DK, GLA_DV), F32)

    q_inter, k_state, scores, decay = {}, {}, {}, {}
    for s, h in chains:
        b = bcum_ref[s, :, ks[h]]
        b_last = b[chunk - 1:chunk, :]
        b_mid = b[chunk // 2 - 1:chunk // 2, :]
        q = q_ref[s, :, ks[h]].astype(F32)
        k = k_ref[s, :, ks[h]].astype(F32)
        q_inter[s, h] = (q * jnp.exp2(b)).astype(BF16)
        q_intra = (q * jnp.exp2(b - b_mid)).astype(BF16)
        k_intra = (k * jnp.exp2(b_mid - b)).astype(BF16)
        k_state[s, h] = (k * jnp.exp2(b_last - b)).astype(BF16)
        scores[s, h] = lax.dot_general(q_intra, k_intra, _NT_DIMS,
                                       preferred_element_type=F32)
        decay[s, h] = jnp.exp2(jnp.broadcast_to(b_last, (128, GLA_DK))).T

    o = {}
    for s, h in chains:
        sc = jnp.where(causal, scores[s, h], 0.0).astype(BF16)
        v = v_ref[s, :, vs[h]].astype(BF16)
        o[s, h] = (jnp.dot(q_inter[s, h], state(s, h).astype(BF16),
                           preferred_element_type=F32)
                   + jnp.dot(sc, v, preferred_element_type=F32))
        upd = lax.dot_general(k_state[s, h], v, _TN_DIMS, preferred_element_type=F32)
        s_new = (state(s, h) * jnp.concatenate([decay[s, h]] * (GLA_DV // 128), axis=1)
                 + upd)
        if n_chunks > 1:
            s_ref[s, h] = s_new
        else:
            sfin_ref[s, h] = s_new

    qs = GLA_DK ** -0.5
    scale = {sh: qs * lax.rsqrt(jnp.mean(o[sh] * o[sh], axis=-1, keepdims=True) * (qs * qs)
                                + RMS_EPS) for sh in chains}
    for s, h in chains:
        r = r_ref[s, :, vs[h]].astype(F32)
        o_ref[s, :, vs[h]] = (o[s, h] * scale[s, h] * gon_ref[:, vs[h]]
                              * _silu(r)).astype(o_ref.dtype)

    if n_chunks > 1:
        @pl.when(c == n_chunks - 1)
        def _():
            sfin_ref[...] = s_ref[...]


def _gla_operands(proj, bcum, gon, s0, *, batch, seq, chunk, ns, out_dtype, index):
    assert batch % ns == 0 and seq % chunk == 0
    proj = proj.reshape(batch, seq, proj.shape[-1])
    bcum = bcum.reshape(batch, seq, bcum.shape[-1])
    kb, vb, rb = 1, 2 * GLA_KEY_DIM // GLA_VALUE_DIM, 2 * GLA_KEY_DIM // GLA_VALUE_DIM + 1

    def rows(width, col):
        return pl.BlockSpec((ns, chunk, width), lambda *g: (*index(*g), col))

    state_spec = pl.BlockSpec((ns, GLA_HEADS, GLA_DK, GLA_DV),
                              lambda *g: (index(*g)[0], 0, 0, 0))
    in_specs = [rows(GLA_KEY_DIM, 0), rows(GLA_KEY_DIM, kb), rows(GLA_VALUE_DIM, vb),
                rows(GLA_VALUE_DIM, rb), rows(GLA_KEY_DIM, 0),
                pl.BlockSpec((1, GLA_VALUE_DIM), lambda *g: (0, 0))]
    args = [proj, proj, proj, proj, bcum, gon.reshape(1, -1)]
    if s0 is not None:
        in_specs.append(state_spec)
        args.append(s0)
    out_specs = [rows(GLA_VALUE_DIM, 0), state_spec]
    out_shape = [jax.ShapeDtypeStruct((batch, seq, GLA_VALUE_DIM), out_dtype),
                 jax.ShapeDtypeStruct((batch, GLA_HEADS, GLA_DK, GLA_DV), F32)]
    return in_specs, args, out_specs, out_shape


def _gla_with_rider_kernel(*refs, gla_kernel, rider_kernel, n_in, n_out, n_scratch):
    n_total_in = len(refs) - n_out - n_scratch - rider_kernel.n_out - rider_kernel.n_scratch
    ins, outs, scr = (refs[:n_total_in],
                      refs[n_total_in:n_total_in + n_out + rider_kernel.n_out],
                      refs[n_total_in + n_out + rider_kernel.n_out:])
    rider_kernel.fn(*ins[n_in:], *outs[n_out:], *scr[n_scratch:])
    gla_kernel(*ins[:n_in], *outs[:n_out], *scr[:n_scratch])


class _Rider:
    def __init__(self, fn, n_out, n_scratch):
        self.fn, self.n_out, self.n_scratch = fn, n_out, n_scratch


def _gla(proj, bcum, gon, *, batch, seq, chunk, out_dtype, s0=None, rider=None):
    n = seq // chunk
    ns = GLA_SEQS_PER_STEP
    in_specs, args, out_specs, out_shape = _gla_operands(
        proj, bcum, gon, s0, batch=batch, seq=seq, chunk=chunk, ns=ns, out_dtype=out_dtype,
        index=lambda b, c: (b, c))
    scratch = [pltpu.VMEM((ns, GLA_HEADS, GLA_DK, GLA_DV), F32)] if n > 1 else []
    body = functools.partial(_gla_kernel, has_init=s0 is not None, n_chunks=n)
    grid = (batch // ns, n)
    if rider is not None:
        assert rider["grid"] == grid, (rider["grid"], grid)
        body = functools.partial(
            _gla_with_rider_kernel, gla_kernel=body, n_in=len(in_specs), n_out=len(out_specs),
            n_scratch=len(scratch),
            rider_kernel=_Rider(rider["kernel"], len(rider["out_specs"]),
                                len(rider["scratch_shapes"])))
        in_specs, args = in_specs + rider["in_specs"], args + rider["args"]
        out_specs, out_shape = out_specs + rider["out_specs"], out_shape + rider["out_shape"]
        scratch = scratch + rider["scratch_shapes"]
    o, s_fin, *extra = pl.pallas_call(
        body,
        grid=grid,
        in_specs=in_specs,
        out_specs=out_specs,
        out_shape=out_shape,
        scratch_shapes=scratch,
        compiler_params=_params(("parallel", "arbitrary")),
        name=f"gla_chunk{chunk}",
    )(*args)
    return (o.reshape(batch * seq, GLA_VALUE_DIM), s_fin, *extra)


def _alibi_slope(head):
    return 2.0 ** (-8.0 * (head + 1) / SWA_HEADS)


def _attn_prompt_kernel(sinks_ref, q_ref, z_ref, kp_ref, ko_ref, vp_ref, vo_ref, out_ref,
                        bias_ref):
    hd, nkeys = SWA_HEAD_DIM, 2 * WINDOW
    pair_w = 2 * hd
    pairs_per_group = SWA_GROUP // 2
    blk = pl.program_id(1)

    @pl.when((pl.program_id(0) == 0) & (blk == 0))
    def _():
        kj = lax.broadcasted_iota(jnp.int32, (nkeys, WINDOW), 0)
        qi = lax.broadcasted_iota(jnp.int32, (nkeys, WINDOW), 1)
        dist = WINDOW + qi - kj
        ok = (dist >= 0) & (dist <= WINDOW)
        ok_first = ok & (kj >= WINDOW)
        distf = dist.astype(F32)
        for h in range(SWA_HEADS):
            pen = (-_alibi_slope(h) * LOG2E) * distf
            sl = slice((h % 2) * WINDOW, (h % 2 + 1) * WINDOW)
            bias_ref[0, h // 2, :, sl] = jnp.where(ok_first, pen, -jnp.inf)
            bias_ref[1, h // 2, :, sl] = jnp.where(ok, pen, -jnp.inf)

    n_sub = q_ref.shape[0] // WINDOW
    k_rows = jnp.concatenate([kp_ref[...], ko_ref[...]], axis=0)
    v_rows = jnp.concatenate([vp_ref[...], vo_ref[...]], axis=0)
    ones = jnp.ones((16, nkeys), F32)
    lane = lax.broadcasted_iota(jnp.int32, (nkeys, pair_w), 1)
    qlane = lax.broadcasted_iota(jnp.int32, (WINDOW, pair_w), 1)
    qk_scale = (hd ** -0.5) * LOG2E
    quad_pairs = 2
    n_quads = SWA_HEADS // (2 * quad_pairs)
    quads_per_group = pairs_per_group // quad_pairs
    quarter = lax.broadcasted_iota(jnp.int32, (1, 2 * quad_pairs * WINDOW), 1) // WINDOW

    def prepare(sub):
        k = k_rows[sub * WINDOW:sub * WINDOW + nkeys]
        vt = v_rows[sub * WINDOW:sub * WINDOW + nkeys].T
        k2, vt1 = [], []
        for g in range(SWA_KV_HEADS):
            kblk = k[:, (g // 2) * pair_w:(g // 2 + 1) * pair_w]
            k_here = jnp.where((lane < hd) if g % 2 == 0 else (lane >= hd), kblk, 0.0)
            k2.append((k_here + pltpu.roll(k_here, hd, axis=1)).astype(BF16))
            vt1.append(jnp.concatenate([vt[g * hd:(g + 1) * hd], ones], axis=0).astype(BF16))
        return dict(qrows=slice(sub * WINDOW, (sub + 1) * WINDOW), k2=k2, vt1=vt1,
                    tbl=jnp.minimum(blk, 1) if sub == 0 else 1)

    def scores(blkst, quad):
        parts = []
        for j in range(quad_pairs):
            col = (quad * quad_pairs + j) * pair_w
            q_pair = q_ref[blkst["qrows"], col:col + pair_w]
            zero = jnp.zeros_like(q_pair)
            parts += [jnp.where(qlane < hd, q_pair, zero), jnp.where(qlane >= hd, q_pair, zero)]
        return lax.dot_general(blkst["k2"][quad // quads_per_group],
                               jnp.concatenate(parts, axis=0),
                               _NT_DIMS, preferred_element_type=F32)

    def finish(blkst, quad, st):
        pair0 = quad * quad_pairs
        bias = jnp.concatenate([bias_ref[blkst["tbl"], pair0 + j] for j in range(quad_pairs)],
                               axis=1)
        s2 = st * qk_scale + bias
        sink2 = sinks_ref[2 * pair0]
        for t in range(1, 2 * quad_pairs):
            sink2 = jnp.where(quarter == t, sinks_ref[2 * pair0 + t], sink2)
        sink2 = sink2 * LOG2E
        m = jnp.maximum(jnp.max(s2, axis=0, keepdims=True), sink2)
        p = jnp.exp2(s2 - m).astype(BF16)
        oa = jnp.dot(blkst["vt1"][quad // quads_per_group], p, preferred_element_type=F32)
        denom = oa[hd:hd + 1] + jnp.exp2(sink2 - m)
        on = oa[0:hd] * (1.0 / denom)
        for j in range(quad_pairs):
            lo = 2 * j * WINDOW
            o_pair = jnp.concatenate([on[:, lo:lo + WINDOW],
                                      on[:, lo + WINDOW:lo + 2 * WINDOW]], axis=0).T
            col = (pair0 + j) * pair_w
            z_pair = z_ref[blkst["qrows"], col:col + pair_w].astype(F32)
            out_ref[blkst["qrows"], col:col + pair_w] = (
                o_pair * _silu(z_pair)).astype(out_ref.dtype)

    for first in range(0, n_sub, ATTN_BLOCKS_IN_FLIGHT):
        blocks = [prepare(sub) for sub in range(first, min(first + ATTN_BLOCKS_IN_FLIGHT, n_sub))]
        st_next = [scores(b, 0) for b in blocks]
        for quad in range(n_quads):
            for i, b in enumerate(blocks):
                st = st_next[i]
                if quad + 1 < n_quads:
                    st_next[i] = scores(b, quad + 1)
                finish(b, quad, st)


def _attn_prompt(qz, kv, sinks, *, batch, seq):
    sub = ATTN_BLOCKS_PER_STEP
    tq = sub * WINDOW
    nb = seq // tq
    assert seq % tq == 0
    row = lambda b, i: b * nb + i
    prev = lambda b, i: (b * nb + i) * sub - jnp.minimum(i, 1)
    return pl.pallas_call(
        _attn_prompt_kernel,
        grid=(batch, nb),
        in_specs=[
            pl.BlockSpec(memory_space=pltpu.SMEM),
            pl.BlockSpec((tq, SWA_WIDTH), lambda b, i: (row(b, i), 0)),
            pl.BlockSpec((tq, SWA_WIDTH), lambda b, i: (row(b, i), 1)),
            pl.BlockSpec((WINDOW, SWA_KV_WIDTH), lambda b, i: (prev(b, i), 0)),
            pl.BlockSpec((tq, SWA_KV_WIDTH), lambda b, i: (row(b, i), 0)),
            pl.BlockSpec((WINDOW, SWA_KV_WIDTH), lambda b, i: (prev(b, i), 1)),
            pl.BlockSpec((tq, SWA_KV_WIDTH), lambda b, i: (row(b, i), 1)),
        ],
        out_specs=pl.BlockSpec((tq, SWA_WIDTH), lambda b, i: (row(b, i), 0)),
        out_shape=jax.ShapeDtypeStruct((batch * seq, SWA_WIDTH), BF16),
        scratch_shapes=[pltpu.VMEM((2, SWA_HEADS // 2, 2 * WINDOW, 2 * WINDOW), F32)],
        compiler_params=_params(("arbitrary", "arbitrary")),
        name="attn_prompt",
    )(sinks, qz, qz, kv, kv, kv, kv)


def _attn_sample_kernel(sink_ref, slope_ref, q_ref, z_ref, kn_ref, vn_ref, kc_ref, vc_ref,
                        out_ref, kwin_ref, vwin_ref, *, tq, nb):
    hd, nk = SWA_HEAD_DIM, 2 * WINDOW
    rows = SWA_HEADS * tq
    grows = SWA_GROUP * tq
    seqs = range(nb)
    groups = range(SWA_KV_HEADS)

    lane = lax.broadcasted_iota(jnp.int32, (rows, nk), 1)
    tok = lax.broadcasted_iota(jnp.int32, (rows, nk), 0) % tq
    in_buffer = lane < WINDOW
    dist = jnp.where(in_buffer, WINDOW + tok - lane, (nk - tq) + tok - lane)
    allowed = (dist >= 0) & (dist <= WINDOW) & (in_buffer | (lane >= nk - tq))
    penalty = slope_ref[...] * dist.astype(F32)
    new_lanes = lax.broadcasted_iota(jnp.int32, (hd, WINDOW), 1) >= WINDOW - tq

    def new_rows_t(ref, b):
        x = jnp.concatenate([jnp.zeros((WINDOW - tq, SWA_KV_WIDTH), F32),
                             ref[b * tq:(b + 1) * tq, :]], axis=0)
        xt = [x[:, c * WINDOW:(c + 1) * WINDOW].T for c in range(SWA_KV_WIDTH // WINDOW)]
        per_block = WINDOW // hd
        return [xt[g // per_block][(g % per_block) * hd:(g % per_block + 1) * hd]
                for g in groups]

    k_all, v_all = {}, {}
    for b in seqs:
        kn_t, vn_t = new_rows_t(kn_ref, b), new_rows_t(vn_ref, b)
        for g in groups:
            kc, vc = kc_ref[b, g], vc_ref[b, g]
            k_all[b, g] = jnp.concatenate([kc, kn_t[g]], axis=1).astype(BF16)
            v_all[b, g] = jnp.concatenate([vc, vn_t[g]], axis=1).astype(BF16)
            kwin_ref[b, g] = jnp.where(new_lanes, kn_t[g], pltpu.roll(kc, WINDOW - tq, axis=1))
            vwin_ref[b, g] = jnp.where(new_lanes, vn_t[g], pltpu.roll(vc, WINDOW - tq, axis=1))

    s = []
    for b in seqs:
        q = q_ref[b * tq:(b + 1) * tq, :]
        parts = []
        for g in groups:
            qs = jnp.concatenate([q[:, h * hd:(h + 1) * hd]
                                  for h in range(g * SWA_GROUP, (g + 1) * SWA_GROUP)], axis=0)
            parts.append(jnp.dot(qs.astype(BF16), k_all[b, g], preferred_element_type=F32))
        s.append(jnp.concatenate(parts, axis=0))

    sink = sink_ref[...]
    s = [jnp.where(allowed, sb * (hd ** -0.5) - penalty, -jnp.inf) for sb in s]
    m = [jnp.maximum(jnp.max(sb, axis=-1, keepdims=True), sink) for sb in s]
    p = [jnp.exp(s[b] - m[b]) for b in seqs]
    inv = [1.0 / (jnp.sum(p[b], axis=-1, keepdims=True) + jnp.exp(sink - m[b])) for b in seqs]

    for b in seqs:
        pb = p[b].astype(BF16)
        o = jnp.concatenate(
            [lax.dot_general(pb[g * grows:(g + 1) * grows], v_all[b, g], _NT_DIMS,
                             preferred_element_type=F32) for g in groups], axis=0) * inv[b]
        o = jnp.concatenate([o[h * tq:(h + 1) * tq] for h in range(SWA_HEADS)], axis=1)
        z = z_ref[b * tq:(b + 1) * tq, :]
        out_ref[b * tq:(b + 1) * tq, :] = (o * _silu(z)).astype(out_ref.dtype)


def _attn_sample(qz, kv, cache_k, cache_v, sinks, *, batch, seq):
    nb = SAMPLE_SEQS_PER_STEP
    assert batch % nb == 0
    rows = nb * seq
    sink_col = jnp.repeat(sinks, seq).reshape(SWA_HEADS * seq, 1)
    slope_col = jnp.repeat(jnp.asarray([_alibi_slope(h) for h in range(SWA_HEADS)], F32),
                           seq).reshape(SWA_HEADS * seq, 1)
    col_spec = pl.BlockSpec((SWA_HEADS * seq, 1), lambda i: (0, 0))
    win_spec = pl.BlockSpec((nb, SWA_KV_HEADS, SWA_HEAD_DIM, WINDOW), lambda i: (i, 0, 0, 0))
    win_shape = jax.ShapeDtypeStruct((batch, SWA_KV_HEADS, SWA_HEAD_DIM, WINDOW), F32)
    to_lanes = lambda c: jnp.transpose(c, (0, 2, 3, 1))
    att, k_win, v_win = pl.pallas_call(
        functools.partial(_attn_sample_kernel, tq=seq, nb=nb),
        grid=(batch // nb,),
        in_specs=[
            col_spec,
            col_spec,
            pl.BlockSpec((rows, SWA_WIDTH), lambda i: (i, 0)),
            pl.BlockSpec((rows, SWA_WIDTH), lambda i: (i, 1)),
            pl.BlockSpec((rows, SWA_KV_WIDTH), lambda i: (i, 0)),
            pl.BlockSpec((rows, SWA_KV_WIDTH), lambda i: (i, 1)),
            win_spec,
            win_spec,
        ],
        out_specs=[pl.BlockSpec((rows, SWA_WIDTH), lambda i: (i, 0)), win_spec, win_spec],
        out_shape=[jax.ShapeDtypeStruct((batch * seq, SWA_WIDTH), F32), win_shape, win_shape],
        compiler_params=_params(("parallel",)),
        name="attn_sample",
    )(sink_col, slope_col, qz, qz, kv, kv, to_lanes(cache_k), to_lanes(cache_v))
    return att, jnp.transpose(k_win, (0, 3, 1, 2)), jnp.transpose(v_win, (0, 3, 1, 2))


def kernel(x_prompt, x_sample, state_gla, cache_k_win, cache_v_win, g_norm_a, w_in_a,
           w_gate_up, b_gate, g_onorm_a, w_out_a, g_norm_kv, w_kv, g_norm_b, w_in_b, sinks,
           w_out_b, g_final):
    assert w_in_a.shape[0] == 1 and w_in_b.shape[0] == 1, "one GLA layer, one SWA layer"
    assert cache_k_win.shape[1] == WINDOW
    pb, ps, _ = x_prompt.shape
    sb, ss, _ = x_sample.shape
    pm, sm = pb * ps, sb * ss
    g_norm_a, w_gate_up, b_gate, g_onorm_a = g_norm_a[0], w_gate_up[0], b_gate[0], g_onorm_a[0]
    g_norm_b, sinks = g_norm_b[0], sinks[0]
    w_kv = w_kv.astype(BF16)
    w_in_a_t = jnp.swapaxes(w_in_a, 1, 2)
    w_glow = w_in_a_t[0, GLA_MAIN_COLS:, :]
    h_p = x_prompt.reshape(pm, D_MODEL)
    h_s = x_sample.reshape(sm, D_MODEL)

    proj_s, glow_s, w_in_a_bf16 = _norm_matmul(
        h_s, g_norm_a, w_in_a_t, tm=sm, tn=SAMPLE_COL_TILE, n=GLA_MAIN_COLS, out_dtype=F32,
        w_extra=w_glow, emit_weights=True, w_transposed=True)
    bcum_s = _gla_gates(glow_s, w_gate_up, b_gate, chunk=ss, tm=sm)
    proj_p, glow_p = _norm_matmul(
        h_p, g_norm_a, w_in_a_bf16, tm=PROMPT_ROW_TILE, tn=PROMPT_COL_TILE, n=GLA_MAIN_COLS,
        out_dtype=BF16, w_extra=w_glow, w_transposed=True)
    bcum_p, o_s, gla_s = _gla_gates(
        glow_p, w_gate_up, b_gate, chunk=GLA_PROMPT_CHUNK, tm=GATE_ROW_TILE,
        rider=dict(proj=proj_s, bcum=bcum_s, gon=g_onorm_a, batch=sb, seq=ss,
                   s0=state_gla.reshape(state_gla.shape[1:])))
    h_s, w_out_a_bf16 = _matmul_residual(o_s, w_out_a, h_s, tm=sm, tk=SAMPLE_K_TILE,
                                         emit_weights=True)
    qz_s_ops = _norm_matmul_operands(
        h_s, g_norm_b, w_in_b, tm=sm, tn=2 * SWA_WIDTH // (ps // GLA_PROMPT_CHUNK),
        out_dtype=F32, w_extra=w_kv, g_extra=g_norm_kv, emit_weights=True)
    o_p, gla_p, qz_s, kv_s, w_in_b_bf16 = _gla(
        proj_p, bcum_p, g_onorm_a, batch=pb, seq=ps, chunk=GLA_PROMPT_CHUNK, out_dtype=BF16,
        rider=qz_s_ops)
    h_p = _matmul_residual(o_p, w_out_a_bf16, h_p, tm=RESIDUAL_ROW_TILE)

    att_s, k_s, v_s = _attn_sample(qz_s, kv_s, cache_k_win, cache_v_win, sinks,
                                   batch=sb, seq=ss)
    y_s, w_out_b_bf16 = _matmul_residual(att_s, w_out_b, h_s, tm=sm, tk=SAMPLE_K_TILE,
                                         g_final=g_final, emit_weights=True)
    qz_p, kv_p = _norm_matmul(
        h_p, g_norm_b, w_in_b_bf16, tm=PROMPT_ROW_TILE, tn=QZ_COL_TILE, out_dtype=BF16,
        w_extra=w_kv, g_extra=g_norm_kv)
    att_p = _attn_prompt(qz_p, kv_p, sinks, batch=pb, seq=ps)
    y_p = _matmul_residual(att_p, w_out_b_bf16, h_p, tm=RESIDUAL_ROW_TILE, g_final=g_final)

    kv_win = kv_p.reshape(pb, ps, 2 * SWA_KV_WIDTH)[:, ps - WINDOW:]
    kv_win = kv_win.reshape(pb, WINDOW, 2, SWA_KV_HEADS, SWA_HEAD_DIM)
    return (y_p.reshape(pb, ps, D_MODEL), y_s.reshape(sb, ss, D_MODEL), gla_p[None],
            gla_s[None], kv_win[:, :, 0], kv_win[:, :, 1], k_s, v_s)
```

```python
import functools

import jax
import jax.numpy as jnp
from jax import lax
from jax.experimental import pallas as pl
from jax.experimental.pallas import tpu as pltpu

F32 = jnp.float32
BF16 = jnp.bfloat16

D_MODEL = 2048
GLA_HEADS = 4
GLA_KEY_DIM = D_MODEL // 2
GLA_VALUE_DIM = D_MODEL
GLA_DK = GLA_KEY_DIM // GLA_HEADS
GLA_DV = GLA_VALUE_DIM // GLA_HEADS
GLA_GATE_RANK = 16
GLA_GATE_TEMP = 16.0
GLA_MAIN_COLS = 2 * GLA_KEY_DIM + 2 * GLA_VALUE_DIM
SWA_HEAD_DIM = 64
SWA_HEADS = D_MODEL // SWA_HEAD_DIM
SWA_KV_HEADS = 4
SWA_GROUP = SWA_HEADS // SWA_KV_HEADS
SWA_WIDTH = SWA_HEADS * SWA_HEAD_DIM
SWA_KV_WIDTH = SWA_KV_HEADS * SWA_HEAD_DIM
WINDOW = 128
RMS_EPS = 1e-6
LOG2E = 1.4426950408889634

V7X_VMEM_LIMIT_BYTES = 61 * 1024 * 1024
GLA_PROMPT_CHUNK = 128
GLA_SEQS_PER_STEP = 4
ATTN_BLOCKS_PER_STEP = 4
ATTN_BLOCKS_IN_FLIGHT = 2
SAMPLE_SEQS_PER_STEP = 8
NORM_ROW_CHUNK = 256
GATE_CUMSUM_ROWS = 128
GATE_ROW_TILE = 1024
PROMPT_ROW_TILE = 1024
PROMPT_COL_TILE = 2048
SAMPLE_COL_TILE = 1024
QZ_COL_TILE = 2048
SAMPLE_K_TILE = 1024
RESIDUAL_ROW_TILE = 512

_NT_DIMS = (((1,), (1,)), ((), ()))
_TN_DIMS = (((0,), (0,)), ((), ()))


def _params(semantics):
    return pltpu.CompilerParams(dimension_semantics=semantics,
                                vmem_limit_bytes=V7X_VMEM_LIMIT_BYTES)


def _silu(x):
    return x / (1.0 + jnp.exp(-x))


def _rms_scale(x):
    return lax.rsqrt(jnp.mean(x * x, axis=-1, keepdims=True) + RMS_EPS)


def _norm_matmul_kernel(x_ref, g_ref, w_ref, *rest, has_extra, extra_gain, emit_weights,
                        w_transposed):
    dims = _NT_DIMS if w_transposed else (((1,), (0,)), ((), ()))

    def mm(a, b):
        return lax.dot_general(a, b, dims, preferred_element_type=F32)

    rest = list(rest)
    g2_ref = rest.pop(0) if extra_gain else None
    w2_ref = rest.pop(0) if has_extra else None
    o_ref = rest.pop(0)
    o2_ref = rest.pop(0) if has_extra else None
    wb_ref = rest.pop(0) if emit_weights else None
    (xn_ref,) = rest

    first = pl.program_id(1) == 0

    def weights():
        w = w_ref[...].astype(BF16)
        if emit_weights:
            wb_ref[...] = w
        return w

    @pl.when(first)
    def _():
        tm = x_ref.shape[0]
        rc = min(tm, NORM_ROW_CHUNK)
        w = weights()
        for c in range(tm // rc):
            rows = slice(c * rc, (c + 1) * rc)
            x = x_ref[rows, :]
            xs = x * _rms_scale(x)
            xn = (xs * g_ref[...]).astype(BF16)
            xn_ref[rows, :] = xn
            o_ref[rows, :] = mm(xn, w).astype(o_ref.dtype)
            if has_extra:
                xn2 = (xs * g2_ref[...]).astype(BF16) if extra_gain else xn
                o2_ref[rows, :] = mm(xn2, w2_ref[...].astype(BF16)).astype(o2_ref.dtype)

    @pl.when(jnp.logical_not(first))
    def _():
        o_ref[...] = mm(xn_ref[...], weights()).astype(o_ref.dtype)


def _norm_matmul(x, g, w, *, tm, tn, out_dtype, n=None, w_extra=None, g_extra=None,
                 emit_weights=False, w_transposed=False):
    m, k = x.shape
    n_axis = -2 if w_transposed else -1
    n = w.shape[n_axis] if n is None else n
    assert n % tn == 0 and m % tm == 0
    assert not emit_weights or m == tm, "each weight block must be visited exactly once"
    grid = (m // tm, n // tn)
    w_block = (tn, k) if w_transposed else (k, tn)
    w_index = (lambda i, j: (j, 0)) if w_transposed else (lambda i, j: (0, j))
    if w.ndim == 3:
        w_spec = pl.BlockSpec((None,) + w_block, lambda i, j: (0,) + w_index(i, j))
    else:
        w_spec = pl.BlockSpec(w_block, w_index)
    in_specs = [
        pl.BlockSpec((tm, k), lambda i, j: (i, 0)),
        pl.BlockSpec((1, k), lambda i, j: (0, 0)),
        w_spec,
    ]
    out_shape = [jax.ShapeDtypeStruct((m, n), out_dtype)]
    out_specs = [pl.BlockSpec((tm, tn), lambda i, j: (i, j))]
    args = [x, g.reshape(1, k), w]
    if g_extra is not None:
        in_specs.append(pl.BlockSpec((1, k), lambda i, j: (0, 0)))
        args.append(g_extra.reshape(1, k))
    if w_extra is not None:
        n2 = w_extra.shape[n_axis]
        in_specs.append(pl.BlockSpec(w_extra.shape, lambda i, j: (0, 0)))
        out_shape.append(jax.ShapeDtypeStruct((m, n2), F32))
        out_specs.append(pl.BlockSpec((tm, n2), lambda i, j: (i, 0)))
        args.append(w_extra)
    if emit_weights:
        out_shape.append(jax.ShapeDtypeStruct((n, k) if w_transposed else (k, n), BF16))
        out_specs.append(pl.BlockSpec(w_block, w_index))
    res = pl.pallas_call(
        functools.partial(_norm_matmul_kernel, has_extra=w_extra is not None,
                          extra_gain=g_extra is not None, emit_weights=emit_weights,
                          w_transposed=w_transposed),
        grid=grid,
        in_specs=in_specs,
        out_specs=out_specs,
        out_shape=out_shape,
        scratch_shapes=[pltpu.VMEM((tm, k), BF16)],
        compiler_params=_params(("parallel", "arbitrary")),
        name=f"norm_matmul_{m}x{n}",
    )(*args)
    return res if len(res) > 1 else res[0]


def _matmul_residual_kernel(a_ref, w_ref, r_ref, *rest, final_norm, emit_weights, nk):
    rest = list(rest)
    g_ref = rest.pop(0) if final_norm else None
    o_ref = rest.pop(0)
    w = w_ref[...].astype(BF16)
    if emit_weights:
        rest[0][...] = w
    part = jnp.dot(a_ref[...].astype(BF16), w, preferred_element_type=F32)

    if nk == 1:
        h = r_ref[...] + part
        if final_norm:
            h = h * _rms_scale(h) * g_ref[...]
        o_ref[...] = h
        return

    kk = pl.program_id(1)

    @pl.when(kk == 0)
    def _():
        o_ref[...] = r_ref[...] + part

    @pl.when(kk > 0)
    def _():
        o_ref[...] += part

    if final_norm:
        @pl.when(kk == nk - 1)
        def _():
            h = o_ref[...]
            o_ref[...] = h * _rms_scale(h) * g_ref[...]


def _matmul_residual(a, w, res, *, tm, tk=None, g_final=None, emit_weights=False):
    m, k = a.shape
    n = w.shape[-1]
    tk = k if tk is None else tk
    assert m % tm == 0 and k % tk == 0
    assert not emit_weights or m == tm, "each weight block must be visited exactly once"
    if w.ndim == 3:
        w_spec = pl.BlockSpec((None, tk, n), lambda i, j: (0, j, 0))
    else:
        w_spec = pl.BlockSpec((tk, n), lambda i, j: (j, 0))
    in_specs = [
        pl.BlockSpec((tm, tk), lambda i, j: (i, j)),
        w_spec,
        pl.BlockSpec((tm, n), lambda i, j: (i, 0)),
    ]
    args = [a, w, res]
    if g_final is not None:
        in_specs.append(pl.BlockSpec((1, n), lambda i, j: (0, 0)))
        args.append(g_final.reshape(1, n))
    out_shape = [jax.ShapeDtypeStruct((m, n), F32)]
    out_specs = [pl.BlockSpec((tm, n), lambda i, j: (i, 0))]
    if emit_weights:
        out_shape.append(jax.ShapeDtypeStruct((k, n), BF16))
        out_specs.append(pl.BlockSpec((tk, n), lambda i, j: (j, 0)))
    res = pl.pallas_call(
        functools.partial(_matmul_residual_kernel, final_norm=g_final is not None,
                          emit_weights=emit_weights, nk=k // tk),
        grid=(m // tm, k // tk),
        in_specs=in_specs,
        out_specs=out_specs,
        out_shape=out_shape,
        compiler_params=_params(("parallel", "arbitrary")),
        name=f"matmul_residual_{m}" + ("_final" if g_final is not None else ""),
    )(*args)
    return res if emit_weights else res[0]


def _split_bf16(x):
    hi = x.astype(BF16)
    return hi, (x - hi.astype(F32)).astype(BF16)


def _gla_gates_kernel(glow_ref, wg_ref, bg_ref, bcum_ref, *, chunk):
    rows = glow_ref.shape[0]
    g_hi, g_lo = _split_bf16(glow_ref[...])
    w_hi, w_lo = _split_bf16(wg_ref[...])
    x = jnp.dot(jnp.concatenate([g_hi, g_lo, g_hi], axis=1),
                jnp.concatenate([w_hi, w_hi, w_lo], axis=0),
                preferred_element_type=F32) + bg_ref[...]
    softplus2 = jnp.log2(1.0 + jnp.exp2(jnp.abs(x) * (-LOG2E)))
    logg = jnp.minimum(x, 0.0) * (LOG2E / GLA_GATE_TEMP) - softplus2 * (1.0 / GLA_GATE_TEMP)

    span = GATE_CUMSUM_ROWS
    row = lax.broadcasted_iota(jnp.int32, (span, span), 0)
    col = lax.broadcasted_iota(jnp.int32, (span, span), 1)
    same_chunk = (row // chunk) == (col // chunk) if chunk < span else True
    tril = jnp.where((col <= row) & same_chunk, 1.0, 0.0).astype(BF16)
    tril2 = jnp.concatenate([tril, tril], axis=1)
    for i in range(rows // span):
        hi, lo = _split_bf16(logg[i * span:(i + 1) * span])
        bcum_ref[i * span:(i + 1) * span, :] = jnp.dot(
            tril2, jnp.concatenate([hi, lo], axis=0), preferred_element_type=F32)


def _gates_with_gla_kernel(glow_ref, wg_ref, bg_ref, *rest, chunk, n_gla_in):
    gla_in, (bcum_ref, o_ref, sfin_ref) = rest[:n_gla_in], rest[n_gla_in:]
    _gla_kernel(*gla_in, o_ref, sfin_ref, has_init=True, n_chunks=1)
    _gla_gates_kernel(glow_ref, wg_ref, bg_ref, bcum_ref, chunk=chunk)


def _gla_gates(glow, wg, bg, *, chunk, tm, rider=None):
    m = glow.shape[0]
    assert m % tm == 0 and tm % GATE_CUMSUM_ROWS == 0
    assert GATE_CUMSUM_ROWS % chunk == 0 or chunk % GATE_CUMSUM_ROWS == 0
    assert chunk <= GATE_CUMSUM_ROWS, "cumulative sums do not cross row spans"
    steps = m // tm
    in_specs = [
        pl.BlockSpec((tm, GLA_GATE_RANK), lambda i: (i, 0)),
        pl.BlockSpec((GLA_GATE_RANK, GLA_KEY_DIM), lambda i: (0, 0)),
        pl.BlockSpec((1, GLA_KEY_DIM), lambda i: (0, 0)),
    ]
    args = [glow, wg, bg.reshape(1, -1)]
    out_specs = [pl.BlockSpec((tm, GLA_KEY_DIM), lambda i: (i, 0))]
    out_shape = [jax.ShapeDtypeStruct((m, GLA_KEY_DIM), F32)]
    if rider is None:
        body = functools.partial(_gla_gates_kernel, chunk=chunk)
    else:
        batch, seq = rider["batch"], rider["seq"]
        assert batch % steps == 0
        g_in, g_args, g_out, g_shape = _gla_operands(
            rider["proj"], rider["bcum"], rider["gon"], rider["s0"], batch=batch, seq=seq,
            chunk=seq, ns=batch // steps, out_dtype=F32, index=lambda i: (i, 0))
        body = functools.partial(_gates_with_gla_kernel, chunk=chunk, n_gla_in=len(g_in))
        in_specs += g_in
        args += g_args
        out_specs += g_out
        out_shape += g_shape
    res = pl.pallas_call(
        body,
        grid=(steps,),
        in_specs=in_specs,
        out_specs=out_specs,
        out_shape=out_shape,
        compiler_params=_params(("parallel",)),
        name=f"gla_gates_{m}",
    )(*args)
    if rider is None:
        return res[0]
    bcum, o, s_fin = res
    return bcum, o.reshape(rider["batch"] * rider["seq"], GLA_VALUE_DIM), s_fin


def _gla_kernel(q_ref, k_ref, v_ref, r_ref, bcum_ref, gon_ref, *rest, has_init, n_chunks):
    rest = list(rest)
    s0_ref = rest.pop(0) if has_init else None
    o_ref, sfin_ref = rest[:2]
    s_ref = rest[2] if n_chunks > 1 else None
    c = pl.program_id(1) if n_chunks > 1 else None
    nseq, chunk = q_ref.shape[:2]
    chains = [(s, h) for s in range(nseq) for h in range(GLA_HEADS)]

    if n_chunks > 1:
        @pl.when(c == 0)
        def _():
            if has_init:
                s_ref[...] = s0_ref[...]
            else:
                s_ref[...] = jnp.zeros_like(s_ref)

    row = lax.broadcasted_iota(jnp.int32, (chunk, chunk), 0)
    col = lax.broadcasted_iota(jnp.int32, (chunk, chunk), 1)
    causal = col <= row
    heads = range(GLA_HEADS)
    ks = [slice(h * GLA_DK, (h + 1) * GLA_DK) for h in heads]
    vs = [slice(h * GLA_DV, (h + 1) * GLA_DV) for h in heads]

    def state(s, h):
        if n_chunks > 1:
            return s_ref[s, h]
        return s0_ref[s, h] if has_init else jnp.zeros((GLA_DK, GLA_DV), F32)

    q_inter, k_state, scores, decay = {}, {}, {}, {}
    for s, h in chains:
        b = bcum_ref[s, :, ks[h]]
        b_last = b[chunk - 1:chunk, :]
        b_mid = b[chunk // 2 - 1:chunk // 2, :]
        q = q_ref[s, :, ks[h]].astype(F32)
        k = k_ref[s, :, ks[h]].astype(F32)
        q_inter[s, h] = (q * jnp.exp2(b)).astype(BF16)
        q_intra = (q * jnp.exp2(b - b_mid)).astype(BF16)
        k_intra = (k * jnp.exp2(b_mid - b)).astype(BF16)
        k_state[s, h] = (k * jnp.exp2(b_last - b)).astype(BF16)
        scores[s, h] = lax.dot_general(q_intra, k_intra, _NT_DIMS,
                                       preferred_element_type=F32)
        decay[s, h] = jnp.exp2(jnp.broadcast_to(b_last, (128, GLA_DK))).T

    o = {}
    for s, h in chains:
        sc = jnp.where(causal, scores[s, h], 0.0).astype(BF16)
        v = v_ref[s, :, vs[h]].astype(BF16)
        o[s, h] = (jnp.dot(q_inter[s, h], state(s, h).astype(BF16),
                           preferred_element_type=F32)
                   + jnp.dot(sc, v, preferred_element_type=F32))
        upd = lax.dot_general(k_state[s, h], v, _TN_DIMS, preferred_element_type=F32)
        s_new = (state(s, h) * jnp.concatenate([decay[s, h]] * (GLA_DV // 128), axis=1)
                 + upd)
        if n_chunks > 1:
            s_ref[s, h] = s_new
        else:
            sfin_ref[s, h] = s_new

    qs = GLA_DK ** -0.5
    scale = {sh: qs * lax.rsqrt(jnp.mean(o[sh] * o[sh], axis=-1, keepdims=True) * (qs * qs)
                                + RMS_EPS) for sh in chains}
    for s, h in chains:
        r = r_ref[s, :, vs[h]].astype(F32)
        o_ref[s, :, vs[h]] = (o[s, h] * scale[s, h] * gon_ref[:, vs[h]]
                              * _silu(r)).astype(o_ref.dtype)

    if n_chunks > 1:
        @pl.when(c == n_chunks - 1)
        def _():
            sfin_ref[...] = s_ref[...]


def _gla_operands(proj, bcum, gon, s0, *, batch, seq, chunk, ns, out_dtype, index):
    assert batch % ns == 0 and seq % chunk == 0
    proj = proj.reshape(batch, seq, proj.shape[-1])
    bcum = bcum.reshape(batch, seq, bcum.shape[-1])
    kb, vb, rb = 1, 2 * GLA_KEY_DIM // GLA_VALUE_DIM, 2 * GLA_KEY_DIM // GLA_VALUE_DIM + 1

    def rows(width, col):
        return pl.BlockSpec((ns, chunk, width), lambda *g: (*index(*g), col))

    state_spec = pl.BlockSpec((ns, GLA_HEADS, GLA_DK, GLA_DV),
                              lambda *g: (index(*g)[0], 0, 0, 0))
    in_specs = [rows(GLA_KEY_DIM, 0), rows(GLA_KEY_DIM, kb), rows(GLA_VALUE_DIM, vb),
                rows(GLA_VALUE_DIM, rb), rows(GLA_KEY_DIM, 0),
                pl.BlockSpec((1, GLA_VALUE_DIM), lambda *g: (0, 0))]
    args = [proj, proj, proj, proj, bcum, gon.reshape(1, -1)]
    if s0 is not None:
        in_specs.append(state_spec)
        args.append(s0)
    out_specs = [rows(GLA_VALUE_DIM, 0), state_spec]
    out_shape = [jax.ShapeDtypeStruct((batch, seq, GLA_VALUE_DIM), out_dtype),
                 jax.ShapeDtypeStruct((batch, GLA_HEADS, GLA_DK, GLA_DV), F32)]
    return in_specs, args, out_specs, out_shape


def _gla(proj, bcum, gon, *, batch, seq, chunk, out_dtype, s0=None):
    n = seq // chunk
    ns = GLA_SEQS_PER_STEP
    in_specs, args, out_specs, out_shape = _gla_operands(
        proj, bcum, gon, s0, batch=batch, seq=seq, chunk=chunk, ns=ns, out_dtype=out_dtype,
        index=lambda b, c: (b, c))
    scratch = [pltpu.VMEM((ns, GLA_HEADS, GLA_DK, GLA_DV), F32)] if n > 1 else []
    o, s_fin = pl.pallas_call(
        functools.partial(_gla_kernel, has_init=s0 is not None, n_chunks=n),
        grid=(batch // ns, n),
        in_specs=in_specs,
        out_specs=out_specs,
        out_shape=out_shape,
        scratch_shapes=scratch,
        compiler_params=_params(("parallel", "arbitrary")),
        name=f"gla_chunk{chunk}",
    )(*args)
    return o.reshape(batch * seq, GLA_VALUE_DIM), s_fin


def _alibi_slope(head):
    return 2.0 ** (-8.0 * (head + 1) / SWA_HEADS)


def _attn_prompt_kernel(sinks_ref, q_ref, z_ref, kp_ref, ko_ref, vp_ref, vo_ref, out_ref,
                        bias_ref):
    hd, nkeys = SWA_HEAD_DIM, 2 * WINDOW
    pair_w = 2 * hd
    pairs_per_group = SWA_GROUP // 2
    blk = pl.program_id(1)

    @pl.when((pl.program_id(0) == 0) & (blk == 0))
    def _():
        kj = lax.broadcasted_iota(jnp.int32, (nkeys, WINDOW), 0)
        qi = lax.broadcasted_iota(jnp.int32, (nkeys, WINDOW), 1)
        dist = WINDOW + qi - kj
        ok = (dist >= 0) & (dist <= WINDOW)
        ok_first = ok & (kj >= WINDOW)
        distf = dist.astype(F32)
        for h in range(SWA_HEADS):
            pen = (-_alibi_slope(h) * LOG2E) * distf
            sl = slice((h % 2) * WINDOW, (h % 2 + 1) * WINDOW)
            bias_ref[0, h // 2, :, sl] = jnp.where(ok_first, pen, -jnp.inf)
            bias_ref[1, h // 2, :, sl] = jnp.where(ok, pen, -jnp.inf)

    n_sub = q_ref.shape[0] // WINDOW
    k_rows = jnp.concatenate([kp_ref[...], ko_ref[...]], axis=0)
    v_rows = jnp.concatenate([vp_ref[...], vo_ref[...]], axis=0)
    ones = jnp.ones((16, nkeys), F32)
    lane = lax.broadcasted_iota(jnp.int32, (nkeys, pair_w), 1)
    qlane = lax.broadcasted_iota(jnp.int32, (WINDOW, pair_w), 1)
    qk_scale = (hd ** -0.5) * LOG2E
    quad_pairs = 2
    n_quads = SWA_HEADS // (2 * quad_pairs)
    quads_per_group = pairs_per_group // quad_pairs
    quarter = lax.broadcasted_iota(jnp.int32, (1, 2 * quad_pairs * WINDOW), 1) // WINDOW

    def prepare(sub):
        k = k_rows[sub * WINDOW:sub * WINDOW + nkeys]
        vt = v_rows[sub * WINDOW:sub * WINDOW + nkeys].T
        k2, vt1 = [], []
        for g in range(SWA_KV_HEADS):
            kblk = k[:, (g // 2) * pair_w:(g // 2 + 1) * pair_w]
            k_here = jnp.where((lane < hd) if g % 2 == 0 else (lane >= hd), kblk, 0.0)
            k2.append((k_here + pltpu.roll(k_here, hd, axis=1)).astype(BF16))
            vt1.append(jnp.concatenate([vt[g * hd:(g + 1) * hd], ones], axis=0).astype(BF16))
        return dict(qrows=slice(sub * WINDOW, (sub + 1) * WINDOW), k2=k2, vt1=vt1,
                    tbl=jnp.minimum(blk, 1) if sub == 0 else 1)

    def scores(blkst, quad):
        parts = []
        for j in range(quad_pairs):
            col = (quad * quad_pairs + j) * pair_w
            q_pair = q_ref[blkst["qrows"], col:col + pair_w]
            zero = jnp.zeros_like(q_pair)
            parts += [jnp.where(qlane < hd, q_pair, zero), jnp.where(qlane >= hd, q_pair, zero)]
        return lax.dot_general(blkst["k2"][quad // quads_per_group],
                               jnp.concatenate(parts, axis=0),
                               _NT_DIMS, preferred_element_type=F32)

    def finish(blkst, quad, st):
        pair0 = quad * quad_pairs
        bias = jnp.concatenate([bias_ref[blkst["tbl"], pair0 + j] for j in range(quad_pairs)],
                               axis=1)
        s2 = st * qk_scale + bias
        sink2 = sinks_ref[2 * pair0]
        for t in range(1, 2 * quad_pairs):
            sink2 = jnp.where(quarter == t, sinks_ref[2 * pair0 + t], sink2)
        sink2 = sink2 * LOG2E
        m = jnp.maximum(jnp.max(s2, axis=0, keepdims=True), sink2)
        p = jnp.exp2(s2 - m).astype(BF16)
        oa = jnp.dot(blkst["vt1"][quad // quads_per_group], p, preferred_element_type=F32)
        denom = oa[hd:hd + 1] + jnp.exp2(sink2 - m)
        on = oa[0:hd] * (1.0 / denom)
        for j in range(quad_pairs):
            lo = 2 * j * WINDOW
            o_pair = jnp.concatenate([on[:, lo:lo + WINDOW],
                                      on[:, lo + WINDOW:lo + 2 * WINDOW]], axis=0).T
            col = (pair0 + j) * pair_w
            z_pair = z_ref[blkst["qrows"], col:col + pair_w].astype(F32)
            out_ref[blkst["qrows"], col:col + pair_w] = (
                o_pair * _silu(z_pair)).astype(out_ref.dtype)

    for first in range(0, n_sub, ATTN_BLOCKS_IN_FLIGHT):
        blocks = [prepare(sub) for sub in range(first, min(first + ATTN_BLOCKS_IN_FLIGHT, n_sub))]
        st_next = [scores(b, 0) for b in blocks]
        for quad in range(n_quads):
            for i, b in enumerate(blocks):
                st = st_next[i]
                if quad + 1 < n_quads:
                    st_next[i] = scores(b, quad + 1)
                finish(b, quad, st)


def _attn_prompt(qz, kv, sinks, *, batch, seq):
    sub = ATTN_BLOCKS_PER_STEP
    tq = sub * WINDOW
    nb = seq // tq
    assert seq % tq == 0
    row = lambda b, i: b * nb + i
    prev = lambda b, i: (b * nb + i) * sub - jnp.minimum(i, 1)
    return pl.pallas_call(
        _attn_prompt_kernel,
        grid=(batch, nb),
        in_specs=[
            pl.BlockSpec(memory_space=pltpu.SMEM),
            pl.BlockSpec((tq, SWA_WIDTH), lambda b, i: (row(b, i), 0)),
            pl.BlockSpec((tq, SWA_WIDTH), lambda b, i: (row(b, i), 1)),
            pl.BlockSpec((WINDOW, SWA_KV_WIDTH), lambda b, i: (prev(b, i), 0)),
            pl.BlockSpec((tq, SWA_KV_WIDTH), lambda b, i: (row(b, i), 0)),
            pl.BlockSpec((WINDOW, SWA_KV_WIDTH), lambda b, i: (prev(b, i), 1)),
            pl.BlockSpec((tq, SWA_KV_WIDTH), lambda b, i: (row(b, i), 1)),
        ],
        out_specs=pl.BlockSpec((tq, SWA_WIDTH), lambda b, i: (row(b, i), 0)),
        out_shape=jax.ShapeDtypeStruct((batch * seq, SWA_WIDTH), BF16),
        scratch_shapes=[pltpu.VMEM((2, SWA_HEADS // 2, 2 * WINDOW, 2 * WINDOW), F32)],
        compiler_params=_params(("arbitrary", "arbitrary")),
        name="attn_prompt",
    )(sinks, qz, qz, kv, kv, kv, kv)


def _attn_sample_kernel(sink_ref, slope_ref, q_ref, z_ref, kn_ref, vn_ref, kc_ref, vc_ref,
                        out_ref, kwin_ref, vwin_ref, *, tq, nb):
    hd, nk = SWA_HEAD_DIM, 2 * WINDOW
    rows = SWA_HEADS * tq
    grows = SWA_GROUP * tq
    seqs = range(nb)
    groups = range(SWA_KV_HEADS)

    lane = lax.broadcasted_iota(jnp.int32, (rows, nk), 1)
    tok = lax.broadcasted_iota(jnp.int32, (rows, nk), 0) % tq
    in_buffer = lane < WINDOW
    dist = jnp.where(in_buffer, WINDOW + tok - lane, (nk - tq) + tok - lane)
    allowed = (dist >= 0) & (dist <= WINDOW) & (in_buffer | (lane >= nk - tq))
    penalty = slope_ref[...] * dist.astype(F32)
    new_lanes = lax.broadcasted_iota(jnp.int32, (hd, WINDOW), 1) >= WINDOW - tq

    def new_rows_t(ref, b):
        x = jnp.concatenate([jnp.zeros((WINDOW - tq, SWA_KV_WIDTH), F32),
                             ref[b * tq:(b + 1) * tq, :]], axis=0)
        xt = [x[:, c * WINDOW:(c + 1) * WINDOW].T for c in range(SWA_KV_WIDTH // WINDOW)]
        per_block = WINDOW // hd
        return [xt[g // per_block][(g % per_block) * hd:(g % per_block + 1) * hd]
                for g in groups]

    k_all, v_all = {}, {}
    for b in seqs:
        kn_t, vn_t = new_rows_t(kn_ref, b), new_rows_t(vn_ref, b)
        for g in groups:
            kc, vc = kc_ref[b, g], vc_ref[b, g]
            k_all[b, g] = jnp.concatenate([kc, kn_t[g]], axis=1).astype(BF16)
            v_all[b, g] = jnp.concatenate([vc, vn_t[g]], axis=1).astype(BF16)
            kwin_ref[b, g] = jnp.where(new_lanes, kn_t[g], pltpu.roll(kc, WINDOW - tq, axis=1))
            vwin_ref[b, g] = jnp.where(new_lanes, vn_t[g], pltpu.roll(vc, WINDOW - tq, axis=1))

    s = []
    for b in seqs:
        q = q_ref[b * tq:(b + 1) * tq, :]
        parts = []
        for g in groups:
            qs = jnp.concatenate([q[:, h * hd:(h + 1) * hd]
                                  for h in range(g * SWA_GROUP, (g + 1) * SWA_GROUP)], axis=0)
            parts.append(jnp.dot(qs.astype(BF16), k_all[b, g], preferred_element_type=F32))
        s.append(jnp.concatenate(parts, axis=0))

    sink = sink_ref[...]
    s = [jnp.where(allowed, sb * (hd ** -0.5) - penalty, -jnp.inf) for sb in s]
    m = [jnp.maximum(jnp.max(sb, axis=-1, keepdims=True), sink) for sb in s]
    p = [jnp.exp(s[b] - m[b]) for b in seqs]
    inv = [1.0 / (jnp.sum(p[b], axis=-1, keepdims=True) + jnp.exp(sink - m[b])) for b in seqs]

    for b in seqs:
        pb = p[b].astype(BF16)
        o = jnp.concatenate(
            [lax.dot_general(pb[g * grows:(g + 1) * grows], v_all[b, g], _NT_DIMS,
                             preferred_element_type=F32) for g in groups], axis=0) * inv[b]
        o = jnp.concatenate([o[h * tq:(h + 1) * tq] for h in range(SWA_HEADS)], axis=1)
        z = z_ref[b * tq:(b + 1) * tq, :]
        out_ref[b * tq:(b + 1) * tq, :] = (o * _silu(z)).astype(out_ref.dtype)


def _attn_sample(qz, kv, cache_k, cache_v, sinks, *, batch, seq):
    nb = SAMPLE_SEQS_PER_STEP
    assert batch % nb == 0
    rows = nb * seq
    sink_col = jnp.repeat(sinks, seq).reshape(SWA_HEADS * seq, 1)
    slope_col = jnp.repeat(jnp.asarray([_alibi_slope(h) for h in range(SWA_HEADS)], F32),
                           seq).reshape(SWA_HEADS * seq, 1)
    col_spec = pl.BlockSpec((SWA_HEADS * seq, 1), lambda i: (0, 0))
    win_spec = pl.BlockSpec((nb, SWA_KV_HEADS, SWA_HEAD_DIM, WINDOW), lambda i: (i, 0, 0, 0))
    win_shape = jax.ShapeDtypeStruct((batch, SWA_KV_HEADS, SWA_HEAD_DIM, WINDOW), F32)
    to_lanes = lambda c: jnp.transpose(c, (0, 2, 3, 1))
    att, k_win, v_win = pl.pallas_call(
        functools.partial(_attn_sample_kernel, tq=seq, nb=nb),
        grid=(batch // nb,),
        in_specs=[
            col_spec,
            col_spec,
            pl.BlockSpec((rows, SWA_WIDTH), lambda i: (i, 0)),
            pl.BlockSpec((rows, SWA_WIDTH), lambda i: (i, 1)),
            pl.BlockSpec((rows, SWA_KV_WIDTH), lambda i: (i, 0)),
            pl.BlockSpec((rows, SWA_KV_WIDTH), lambda i: (i, 1)),
            win_spec,
            win_spec,
        ],
        out_specs=[pl.BlockSpec((rows, SWA_WIDTH), lambda i: (i, 0)), win_spec, win_spec],
        out_shape=[jax.ShapeDtypeStruct((batch * seq, SWA_WIDTH), F32), win_shape, win_shape],
        compiler_params=_params(("parallel",)),
        name="attn_sample",
    )(sink_col, slope_col, qz, qz, kv, kv, to_lanes(cache_k), to_lanes(cache_v))
    return att, jnp.transpose(k_win, (0, 3, 1, 2)), jnp.transpose(v_win, (0, 3, 1, 2))


def kernel(x_prompt, x_sample, state_gla, cache_k_win, cache_v_win, g_norm_a, w_in_a,
           w_gate_up, b_gate, g_onorm_a, w_out_a, g_norm_kv, w_kv, g_norm_b, w_in_b, sinks,
           w_out_b, g_final):
    assert w_in_a.shape[0] == 1 and w_in_b.shape[0] == 1, "one GLA layer, one SWA layer"
    assert cache_k_win.shape[1] == WINDOW
    pb, ps, _ = x_prompt.shape
    sb, ss, _ = x_sample.shape
    pm, sm = pb * ps, sb * ss
    g_norm_a, w_gate_up, b_gate, g_onorm_a = g_norm_a[0], w_gate_up[0], b_gate[0], g_onorm_a[0]
    g_norm_b, sinks = g_norm_b[0], sinks[0]
    w_kv = w_kv.astype(BF16)
    w_in_a_t = jnp.swapaxes(w_in_a, 1, 2)
    w_glow = w_in_a_t[0, GLA_MAIN_COLS:, :]
    h_p = x_prompt.reshape(pm, D_MODEL)
    h_s = x_sample.reshape(sm, D_MODEL)

    proj_s, glow_s, w_in_a_bf16 = _norm_matmul(
        h_s, g_norm_a, w_in_a_t, tm=sm, tn=SAMPLE_COL_TILE, n=GLA_MAIN_COLS, out_dtype=F32,
        w_extra=w_glow, emit_weights=True, w_transposed=True)
    bcum_s = _gla_gates(glow_s, w_gate_up, b_gate, chunk=ss, tm=sm)
    proj_p, glow_p = _norm_matmul(
        h_p, g_norm_a, w_in_a_bf16, tm=PROMPT_ROW_TILE, tn=PROMPT_COL_TILE, n=GLA_MAIN_COLS,
        out_dtype=BF16, w_extra=w_glow, w_transposed=True)
    bcum_p, o_s, gla_s = _gla_gates(
        glow_p, w_gate_up, b_gate, chunk=GLA_PROMPT_CHUNK, tm=GATE_ROW_TILE,
        rider=dict(proj=proj_s, bcum=bcum_s, gon=g_onorm_a, batch=sb, seq=ss,
                   s0=state_gla.reshape(state_gla.shape[1:])))
    h_s, w_out_a_bf16 = _matmul_residual(o_s, w_out_a, h_s, tm=sm, tk=SAMPLE_K_TILE,
                                         emit_weights=True)
    o_p, gla_p = _gla(proj_p, bcum_p, g_onorm_a, batch=pb, seq=ps, chunk=GLA_PROMPT_CHUNK,
                      out_dtype=BF16)
    h_p = _matmul_residual(o_p, w_out_a_bf16, h_p, tm=RESIDUAL_ROW_TILE)

    qz_s, kv_s, w_in_b_bf16 = _norm_matmul(
        h_s, g_norm_b, w_in_b, tm=sm, tn=QZ_COL_TILE, out_dtype=F32,
        w_extra=w_kv, g_extra=g_norm_kv, emit_weights=True)
    att_s, k_s, v_s = _attn_sample(qz_s, kv_s, cache_k_win, cache_v_win, sinks,
                                   batch=sb, seq=ss)
    y_s, w_out_b_bf16 = _matmul_residual(att_s, w_out_b, h_s, tm=sm, tk=SAMPLE_K_TILE,
                                         g_final=g_final, emit_weights=True)
    qz_p, kv_p = _norm_matmul(
        h_p, g_norm_b, w_in_b_bf16, tm=PROMPT_ROW_TILE, tn=QZ_COL_TILE, out_dtype=BF16,
        w_extra=w_kv, g_extra=g_norm_kv)
    att_p = _attn_prompt(qz_p, kv_p, sinks, batch=pb, seq=ps)
    y_p = _matmul_residual(att_p, w_out_b_bf16, h_p, tm=RESIDUAL_ROW_TILE, g_final=g_final)

    kv_win = kv_p.reshape(pb, ps, 2 * SWA_KV_WIDTH)[:, ps - WINDOW:]
    kv_win = kv_win.reshape(pb, WINDOW, 2, SWA_KV_HEADS, SWA_HEAD_DIM)
    return (y_p.reshape(pb, ps, D_MODEL), y_s.reshape(sb, ss, D_MODEL), gla_p[None],
            gla_s[None], kv_win[:, :, 0], kv_win[:, :, 1], k_s, v_s)
```

```python
import functools

import jax
import jax.numpy as jnp
from jax import lax
from jax.experimental import pallas as pl
from jax.experimental.pallas import tpu as pltpu

F32 = jnp.float32
BF16 = jnp.bfloat16

D_MODEL = 2048
GLA_HEADS = 4
GLA_KEY_DIM = D_MODEL // 2
GLA_VALUE_DIM = D_MODEL
GLA_DK = GLA_KEY_DIM // GLA_HEADS
GLA_DV = GLA_VALUE_DIM // GLA_HEADS
GLA_GATE_RANK = 16
GLA_GATE_TEMP = 16.0
GLA_MAIN_COLS = 2 * GLA_KEY_DIM + 2 * GLA_VALUE_DIM
SWA_HEAD_DIM = 64
SWA_HEADS = D_MODEL // SWA_HEAD_DIM
SWA_KV_HEADS = 4
SWA_GROUP = SWA_HEADS // SWA_KV_HEADS
SWA_WIDTH = SWA_HEADS * SWA_HEAD_DIM
SWA_KV_WIDTH = SWA_KV_HEADS * SWA_HEAD_DIM
WINDOW = 128
RMS_EPS = 1e-6
LOG2E = 1.4426950408889634

V7X_VMEM_LIMIT_BYTES = 61 * 1024 * 1024
GLA_PROMPT_CHUNK = 128
GLA_SEQS_PER_STEP = 4
ATTN_BLOCKS_PER_STEP = 4
ATTN_BLOCKS_IN_FLIGHT = 2
SAMPLE_SEQS_PER_STEP = 8
NORM_ROW_CHUNK = 256
GATE_CUMSUM_ROWS = 128
GATE_ROW_TILE = 512
PROMPT_ROW_TILE = 1024
PROMPT_COL_TILE = 2048
SAMPLE_COL_TILE = 1024
QZ_COL_TILE = 2048
SAMPLE_K_TILE = 1024
RESIDUAL_ROW_TILE = 512

_NT_DIMS = (((1,), (1,)), ((), ()))
_TN_DIMS = (((0,), (0,)), ((), ()))


def _params(semantics):
    return pltpu.CompilerParams(dimension_semantics=semantics,
                                vmem_limit_bytes=V7X_VMEM_LIMIT_BYTES)


def _silu(x):
    return x / (1.0 + jnp.exp(-x))


def _rms_scale(x):
    return lax.rsqrt(jnp.mean(x * x, axis=-1, keepdims=True) + RMS_EPS)


def _norm_matmul_kernel(x_ref, g_ref, w_ref, *rest, has_extra, extra_gain, emit_weights,
                        w_transposed):
    dims = _NT_DIMS if w_transposed else (((1,), (0,)), ((), ()))

    def mm(a, b):
        return lax.dot_general(a, b, dims, preferred_element_type=F32)

    rest = list(rest)
    g2_ref = rest.pop(0) if extra_gain else None
    w2_ref = rest.pop(0) if has_extra else None
    o_ref = rest.pop(0)
    o2_ref = rest.pop(0) if has_extra else None
    wb_ref = rest.pop(0) if emit_weights else None
    (xn_ref,) = rest

    first = pl.program_id(1) == 0

    def weights():
        w = w_ref[...].astype(BF16)
        if emit_weights:
            wb_ref[...] = w
        return w

    @pl.when(first)
    def _():
        tm = x_ref.shape[0]
        rc = min(tm, NORM_ROW_CHUNK)
        w = weights()
        for c in range(tm // rc):
            rows = slice(c * rc, (c + 1) * rc)
            x = x_ref[rows, :]
            xs = x * _rms_scale(x)
            xn = (xs * g_ref[...]).astype(BF16)
            xn_ref[rows, :] = xn
            o_ref[rows, :] = mm(xn, w).astype(o_ref.dtype)
            if has_extra:
                xn2 = (xs * g2_ref[...]).astype(BF16) if extra_gain else xn
                o2_ref[rows, :] = mm(xn2, w2_ref[...].astype(BF16)).astype(o2_ref.dtype)

    @pl.when(jnp.logical_not(first))
    def _():
        o_ref[...] = mm(xn_ref[...], weights()).astype(o_ref.dtype)


def _norm_matmul(x, g, w, *, tm, tn, out_dtype, n=None, w_extra=None, g_extra=None,
                 emit_weights=False, w_transposed=False):
    m, k = x.shape
    n_axis = -2 if w_transposed else -1
    n = w.shape[n_axis] if n is None else n
    assert n % tn == 0 and m % tm == 0
    assert not emit_weights or m == tm, "each weight block must be visited exactly once"
    grid = (m // tm, n // tn)
    w_block = (tn, k) if w_transposed else (k, tn)
    w_index = (lambda i, j: (j, 0)) if w_transposed else (lambda i, j: (0, j))
    if w.ndim == 3:
        w_spec = pl.BlockSpec((None,) + w_block, lambda i, j: (0,) + w_index(i, j))
    else:
        w_spec = pl.BlockSpec(w_block, w_index)
    in_specs = [
        pl.BlockSpec((tm, k), lambda i, j: (i, 0)),
        pl.BlockSpec((1, k), lambda i, j: (0, 0)),
        w_spec,
    ]
    out_shape = [jax.ShapeDtypeStruct((m, n), out_dtype)]
    out_specs = [pl.BlockSpec((tm, tn), lambda i, j: (i, j))]
    args = [x, g.reshape(1, k), w]
    if g_extra is not None:
        in_specs.append(pl.BlockSpec((1, k), lambda i, j: (0, 0)))
        args.append(g_extra.reshape(1, k))
    if w_extra is not None:
        n2 = w_extra.shape[n_axis]
        in_specs.append(pl.BlockSpec(w_extra.shape, lambda i, j: (0, 0)))
        out_shape.append(jax.ShapeDtypeStruct((m, n2), F32))
        out_specs.append(pl.BlockSpec((tm, n2), lambda i, j: (i, 0)))
        args.append(w_extra)
    if emit_weights:
        out_shape.append(jax.ShapeDtypeStruct((n, k) if w_transposed else (k, n), BF16))
        out_specs.append(pl.BlockSpec(w_block, w_index))
    res = pl.pallas_call(
        functools.partial(_norm_matmul_kernel, has_extra=w_extra is not None,
                          extra_gain=g_extra is not None, emit_weights=emit_weights,
                          w_transposed=w_transposed),
        grid=grid,
        in_specs=in_specs,
        out_specs=out_specs,
        out_shape=out_shape,
        scratch_shapes=[pltpu.VMEM((tm, k), BF16)],
        compiler_params=_params(("parallel", "arbitrary")),
        name=f"norm_matmul_{m}x{n}",
    )(*args)
    return res if len(res) > 1 else res[0]


def _matmul_residual_kernel(a_ref, w_ref, r_ref, *rest, final_norm, emit_weights, nk):
    rest = list(rest)
    g_ref = rest.pop(0) if final_norm else None
    o_ref = rest.pop(0)
    w = w_ref[...].astype(BF16)
    if emit_weights:
        rest[0][...] = w
    part = jnp.dot(a_ref[...].astype(BF16), w, preferred_element_type=F32)

    if nk == 1:
        h = r_ref[...] + part
        if final_norm:
            h = h * _rms_scale(h) * g_ref[...]
        o_ref[...] = h
        return

    kk = pl.program_id(1)

    @pl.when(kk == 0)
    def _():
        o_ref[...] = r_ref[...] + part

    @pl.when(kk > 0)
    def _():
        o_ref[...] += part

    if final_norm:
        @pl.when(kk == nk - 1)
        def _():
            h = o_ref[...]
            o_ref[...] = h * _rms_scale(h) * g_ref[...]


def _matmul_residual(a, w, res, *, tm, tk=None, g_final=None, emit_weights=False):
    m, k = a.shape
    n = w.shape[-1]
    tk = k if tk is None else tk
    assert m % tm == 0 and k % tk == 0
    assert not emit_weights or m == tm, "each weight block must be visited exactly once"
    if w.ndim == 3:
        w_spec = pl.BlockSpec((None, tk, n), lambda i, j: (0, j, 0))
    else:
        w_spec = pl.BlockSpec((tk, n), lambda i, j: (j, 0))
    in_specs = [
        pl.BlockSpec((tm, tk), lambda i, j: (i, j)),
        w_spec,
        pl.BlockSpec((tm, n), lambda i, j: (i, 0)),
    ]
    args = [a, w, res]
    if g_final is not None:
        in_specs.append(pl.BlockSpec((1, n), lambda i, j: (0, 0)))
        args.append(g_final.reshape(1, n))
    out_shape = [jax.ShapeDtypeStruct((m, n), F32)]
    out_specs = [pl.BlockSpec((tm, n), lambda i, j: (i, 0))]
    if emit_weights:
        out_shape.append(jax.ShapeDtypeStruct((k, n), BF16))
        out_specs.append(pl.BlockSpec((tk, n), lambda i, j: (j, 0)))
    res = pl.pallas_call(
        functools.partial(_matmul_residual_kernel, final_norm=g_final is not None,
                          emit_weights=emit_weights, nk=k // tk),
        grid=(m // tm, k // tk),
        in_specs=in_specs,
        out_specs=out_specs,
        out_shape=out_shape,
        compiler_params=_params(("parallel", "arbitrary")),
        name=f"matmul_residual_{m}" + ("_final" if g_final is not None else ""),
    )(*args)
    return res if emit_weights else res[0]


def _split_bf16(x):
    hi = x.astype(BF16)
    return hi, (x - hi.astype(F32)).astype(BF16)


def _gla_gates_kernel(glow_ref, wg_ref, bg_ref, bcum_ref, *, chunk):
    rows = glow_ref.shape[0]
    g_hi, g_lo = _split_bf16(glow_ref[...])
    w_hi, w_lo = _split_bf16(wg_ref[...])
    x = jnp.dot(jnp.concatenate([g_hi, g_lo, g_hi], axis=1),
                jnp.concatenate([w_hi, w_hi, w_lo], axis=0),
                preferred_element_type=F32) + bg_ref[...]
    softplus2 = jnp.log2(1.0 + jnp.exp2(jnp.abs(x) * (-LOG2E)))
    logg = jnp.minimum(x, 0.0) * (LOG2E / GLA_GATE_TEMP) - softplus2 * (1.0 / GLA_GATE_TEMP)

    span = GATE_CUMSUM_ROWS
    row = lax.broadcasted_iota(jnp.int32, (span, span), 0)
    col = lax.broadcasted_iota(jnp.int32, (span, span), 1)
    same_chunk = (row // chunk) == (col // chunk) if chunk < span else True
    tril = jnp.where((col <= row) & same_chunk, 1.0, 0.0).astype(BF16)
    tril2 = jnp.concatenate([tril, tril], axis=1)
    for i in range(rows // span):
        hi, lo = _split_bf16(logg[i * span:(i + 1) * span])
        bcum_ref[i * span:(i + 1) * span, :] = jnp.dot(
            tril2, jnp.concatenate([hi, lo], axis=0), preferred_element_type=F32)


def _gates_with_gla_kernel(glow_ref, wg_ref, bg_ref, *rest, chunk, n_gla_in):
    gla_in, (bcum_ref, o_ref, sfin_ref) = rest[:n_gla_in], rest[n_gla_in:]
    _gla_kernel(*gla_in, o_ref, sfin_ref, has_init=True, n_chunks=1)
    _gla_gates_kernel(glow_ref, wg_ref, bg_ref, bcum_ref, chunk=chunk)


def _gla_gates(glow, wg, bg, *, chunk, tm, rider=None):
    m = glow.shape[0]
    assert m % tm == 0 and tm % GATE_CUMSUM_ROWS == 0
    assert GATE_CUMSUM_ROWS % chunk == 0 or chunk % GATE_CUMSUM_ROWS == 0
    assert chunk <= GATE_CUMSUM_ROWS, "cumulative sums do not cross row spans"
    steps = m // tm
    in_specs = [
        pl.BlockSpec((tm, GLA_GATE_RANK), lambda i: (i, 0)),
        pl.BlockSpec((GLA_GATE_RANK, GLA_KEY_DIM), lambda i: (0, 0)),
        pl.BlockSpec((1, GLA_KEY_DIM), lambda i: (0, 0)),
    ]
    args = [glow, wg, bg.reshape(1, -1)]
    out_specs = [pl.BlockSpec((tm, GLA_KEY_DIM), lambda i: (i, 0))]
    out_shape = [jax.ShapeDtypeStruct((m, GLA_KEY_DIM), F32)]
    if rider is None:
        body = functools.partial(_gla_gates_kernel, chunk=chunk)
    else:
        batch, seq = rider["batch"], rider["seq"]
        assert batch % steps == 0
        g_in, g_args, g_out, g_shape = _gla_operands(
            rider["proj"], rider["bcum"], rider["gon"], rider["s0"], batch=batch, seq=seq,
            chunk=seq, ns=batch // steps, out_dtype=F32, index=lambda i: (i, 0))
        body = functools.partial(_gates_with_gla_kernel, chunk=chunk, n_gla_in=len(g_in))
        in_specs += g_in
        args += g_args
        out_specs += g_out
        out_shape += g_shape
    res = pl.pallas_call(
        body,
        grid=(steps,),
        in_specs=in_specs,
        out_specs=out_specs,
        out_shape=out_shape,
        compiler_params=_params(("parallel",)),
        name=f"gla_gates_{m}",
    )(*args)
    if rider is None:
        return res[0]
    bcum, o, s_fin = res
    return bcum, o.reshape(rider["batch"] * rider["seq"], GLA_VALUE_DIM), s_fin


def _gla_kernel(q_ref, k_ref, v_ref, r_ref, bcum_ref, gon_ref, *rest, has_init, n_chunks):
    rest = list(rest)
    s0_ref = rest.pop(0) if has_init else None
    o_ref, sfin_ref = rest[:2]
    s_ref = rest[2] if n_chunks > 1 else None
    c = pl.program_id(1) if n_chunks > 1 else None
    nseq, chunk = q_ref.shape[:2]
    chains = [(s, h) for s in range(nseq) for h in range(GLA_HEADS)]

    if n_chunks > 1:
        @pl.when(c == 0)
        def _():
            if has_init:
                s_ref[...] = s0_ref[...]
            else:
                s_ref[...] = jnp.zeros_like(s_ref)

    row = lax.broadcasted_iota(jnp.int32, (chunk, chunk), 0)
    col = lax.broadcasted_iota(jnp.int32, (chunk, chunk), 1)
    causal = col <= row
    heads = range(GLA_HEADS)
    ks = [slice(h * GLA_DK, (h + 1) * GLA_DK) for h in heads]
    vs = [slice(h * GLA_DV, (h + 1) * GLA_DV) for h in heads]

    def state(s, h):
        if n_chunks > 1:
            return s_ref[s, h]
        return s0_ref[s, h] if has_init else jnp.zeros((GLA_DK, GLA_DV), F32)

    q_inter, k_state, scores, decay = {}, {}, {}, {}
    for s, h in chains:
        b = bcum_ref[s, :, ks[h]]
        b_last = b[chunk - 1:chunk, :]
        b_mid = b[chunk // 2 - 1:chunk // 2, :]
        q = q_ref[s, :, ks[h]].astype(F32)
        k = k_ref[s, :, ks[h]].astype(F32)
        q_inter[s, h] = (q * jnp.exp2(b)).astype(BF16)
        q_intra = (q * jnp.exp2(b - b_mid)).astype(BF16)
        k_intra = (k * jnp.exp2(b_mid - b)).astype(BF16)
        k_state[s, h] = (k * jnp.exp2(b_last - b)).astype(BF16)
        scores[s, h] = lax.dot_general(q_intra, k_intra, _NT_DIMS,
                                       preferred_element_type=F32)
        decay[s, h] = jnp.exp2(jnp.broadcast_to(b_last, (128, GLA_DK))).T

    o = {}
    for s, h in chains:
        sc = jnp.where(causal, scores[s, h], 0.0).astype(BF16)
        v = v_ref[s, :, vs[h]].astype(BF16)
        o[s, h] = (jnp.dot(q_inter[s, h], state(s, h).astype(BF16),
                           preferred_element_type=F32)
                   + jnp.dot(sc, v, preferred_element_type=F32))
        upd = lax.dot_general(k_state[s, h], v, _TN_DIMS, preferred_element_type=F32)
        s_new = (state(s, h) * jnp.concatenate([decay[s, h]] * (GLA_DV // 128), axis=1)
                 + upd)
        if n_chunks > 1:
            s_ref[s, h] = s_new
        else:
            sfin_ref[s, h] = s_new

    qs = GLA_DK ** -0.5
    scale = {sh: qs * lax.rsqrt(jnp.mean(o[sh] * o[sh], axis=-1, keepdims=True) * (qs * qs)
                                + RMS_EPS) for sh in chains}
    for s, h in chains:
        r = r_ref[s, :, vs[h]].astype(F32)
        o_ref[s, :, vs[h]] = (o[s, h] * scale[s, h] * gon_ref[:, vs[h]]
                              * _silu(r)).astype(o_ref.dtype)

    if n_chunks > 1:
        @pl.when(c == n_chunks - 1)
        def _():
            sfin_ref[...] = s_ref[...]


def _gla_operands(proj, bcum, gon, s0, *, batch, seq, chunk, ns, out_dtype, index):
    assert batch % ns == 0 and seq % chunk == 0
    proj = proj.reshape(batch, seq, proj.shape[-1])
    bcum = bcum.reshape(batch, seq, bcum.shape[-1])
    kb, vb, rb = 1, 2 * GLA_KEY_DIM // GLA_VALUE_DIM, 2 * GLA_KEY_DIM // GLA_VALUE_DIM + 1

    def rows(width, col):
        return pl.BlockSpec((ns, chunk, width), lambda *g: (*index(*g), col))

    state_spec = pl.BlockSpec((ns, GLA_HEADS, GLA_DK, GLA_DV),
                              lambda *g: (index(*g)[0], 0, 0, 0))
    in_specs = [rows(GLA_KEY_DIM, 0), rows(GLA_KEY_DIM, kb), rows(GLA_VALUE_DIM, vb),
                rows(GLA_VALUE_DIM, rb), rows(GLA_KEY_DIM, 0),
                pl.BlockSpec((1, GLA_VALUE_DIM), lambda *g: (0, 0))]
    args = [proj, proj, proj, proj, bcum, gon.reshape(1, -1)]
    if s0 is not None:
        in_specs.append(state_spec)
        args.append(s0)
    out_specs = [rows(GLA_VALUE_DIM, 0), state_spec]
    out_shape = [jax.ShapeDtypeStruct((batch, seq, GLA_VALUE_DIM), out_dtype),
                 jax.ShapeDtypeStruct((batch, GLA_HEADS, GLA_DK, GLA_DV), F32)]
    return in_specs, args, out_specs, out_shape


def _gla(proj, bcum, gon, *, batch, seq, chunk, out_dtype, s0=None):
    n = seq // chunk
    ns = GLA_SEQS_PER_STEP
    in_specs, args, out_specs, out_shape = _gla_operands(
        proj, bcum, gon, s0, batch=batch, seq=seq, chunk=chunk, ns=ns, out_dtype=out_dtype,
        index=lambda b, c: (b, c))
    scratch = [pltpu.VMEM((ns, GLA_HEADS, GLA_DK, GLA_DV), F32)] if n > 1 else []
    o, s_fin = pl.pallas_call(
        functools.partial(_gla_kernel, has_init=s0 is not None, n_chunks=n),
        grid=(batch // ns, n),
        in_specs=in_specs,
        out_specs=out_specs,
        out_shape=out_shape,
        scratch_shapes=scratch,
        compiler_params=_params(("parallel", "arbitrary")),
        name=f"gla_chunk{chunk}",
    )(*args)
    return o.reshape(batch * seq, GLA_VALUE_DIM), s_fin


def _alibi_slope(head):
    return 2.0 ** (-8.0 * (head + 1) / SWA_HEADS)


def _attn_prompt_kernel(sinks_ref, q_ref, z_ref, kp_ref, ko_ref, vp_ref, vo_ref, out_ref,
                        bias_ref):
    hd, nkeys = SWA_HEAD_DIM, 2 * WINDOW
    pair_w = 2 * hd
    pairs_per_group = SWA_GROUP // 2
    blk = pl.program_id(1)

    @pl.when((pl.program_id(0) == 0) & (blk == 0))
    def _():
        kj = lax.broadcasted_iota(jnp.int32, (nkeys, WINDOW), 0)
        qi = lax.broadcasted_iota(jnp.int32, (nkeys, WINDOW), 1)
        dist = WINDOW + qi - kj
        ok = (dist >= 0) & (dist <= WINDOW)
        ok_first = ok & (kj >= WINDOW)
        distf = dist.astype(F32)
        for h in range(SWA_HEADS):
            pen = (-_alibi_slope(h) * LOG2E) * distf
            sl = slice((h % 2) * WINDOW, (h % 2 + 1) * WINDOW)
            bias_ref[0, h // 2, :, sl] = jnp.where(ok_first, pen, -jnp.inf)
            bias_ref[1, h // 2, :, sl] = jnp.where(ok, pen, -jnp.inf)

    n_sub = q_ref.shape[0] // WINDOW
    k_rows = jnp.concatenate([kp_ref[...], ko_ref[...]], axis=0)
    v_rows = jnp.concatenate([vp_ref[...], vo_ref[...]], axis=0)
    ones = jnp.ones((16, nkeys), F32)
    lane = lax.broadcasted_iota(jnp.int32, (nkeys, pair_w), 1)
    qlane = lax.broadcasted_iota(jnp.int32, (WINDOW, pair_w), 1)
    qk_scale = (hd ** -0.5) * LOG2E
    quad_pairs = 2
    n_quads = SWA_HEADS // (2 * quad_pairs)
    quads_per_group = pairs_per_group // quad_pairs
    quarter = lax.broadcasted_iota(jnp.int32, (1, 2 * quad_pairs * WINDOW), 1) // WINDOW

    def prepare(sub):
        k = k_rows[sub * WINDOW:sub * WINDOW + nkeys]
        vt = v_rows[sub * WINDOW:sub * WINDOW + nkeys].T
        k2, vt1 = [], []
        for g in range(SWA_KV_HEADS):
            kblk = k[:, (g // 2) * pair_w:(g // 2 + 1) * pair_w]
            k_here = jnp.where((lane < hd) if g % 2 == 0 else (lane >= hd), kblk, 0.0)
            k2.append((k_here + pltpu.roll(k_here, hd, axis=1)).astype(BF16))
            vt1.append(jnp.concatenate([vt[g * hd:(g + 1) * hd], ones], axis=0).astype(BF16))
        return dict(qrows=slice(sub * WINDOW, (sub + 1) * WINDOW), k2=k2, vt1=vt1,
                    tbl=jnp.minimum(blk, 1) if sub == 0 else 1)

    def scores(blkst, quad):
        parts = []
        for j in range(quad_pairs):
            col = (quad * quad_pairs + j) * pair_w
            q_pair = q_ref[blkst["qrows"], col:col + pair_w]
            zero = jnp.zeros_like(q_pair)
            parts += [jnp.where(qlane < hd, q_pair, zero), jnp.where(qlane >= hd, q_pair, zero)]
        return lax.dot_general(blkst["k2"][quad // quads_per_group],
                               jnp.concatenate(parts, axis=0),
                               _NT_DIMS, preferred_element_type=F32)

    def finish(blkst, quad, st):
        pair0 = quad * quad_pairs
        bias = jnp.concatenate([bias_ref[blkst["tbl"], pair0 + j] for j in range(quad_pairs)],
                               axis=1)
        s2 = st * qk_scale + bias
        sink2 = sinks_ref[2 * pair0]
        for t in range(1, 2 * quad_pairs):
            sink2 = jnp.where(quarter == t, sinks_ref[2 * pair0 + t], sink2)
        sink2 = sink2 * LOG2E
        m = jnp.maximum(jnp.max(s2, axis=0, keepdims=True), sink2)
        p = jnp.exp2(s2 - m).astype(BF16)
        oa = jnp.dot(blkst["vt1"][quad // quads_per_group], p, preferred_element_type=F32)
        denom = oa[hd:hd + 1] + jnp.exp2(sink2 - m)
        on = oa[0:hd] * (1.0 / denom)
        for j in range(quad_pairs):
            lo = 2 * j * WINDOW
            o_pair = jnp.concatenate([on[:, lo:lo + WINDOW],
                                      on[:, lo + WINDOW:lo + 2 * WINDOW]], axis=0).T
            col = (pair0 + j) * pair_w
            z_pair = z_ref[blkst["qrows"], col:col + pair_w].astype(F32)
            out_ref[blkst["qrows"], col:col + pair_w] = (
                o_pair * _silu(z_pair)).astype(out_ref.dtype)

    for first in range(0, n_sub, ATTN_BLOCKS_IN_FLIGHT):
        blocks = [prepare(sub) for sub in range(first, min(first + ATTN_BLOCKS_IN_FLIGHT, n_sub))]
        st_next = [scores(b, 0) for b in blocks]
        for quad in range(n_quads):
            for i, b in enumerate(blocks):
                st = st_next[i]
                if quad + 1 < n_quads:
                    st_next[i] = scores(b, quad + 1)
                finish(b, quad, st)


def _attn_prompt(qz, kv, sinks, *, batch, seq):
    sub = ATTN_BLOCKS_PER_STEP
    tq = sub * WINDOW
    nb = seq // tq
    assert seq % tq == 0
    row = lambda b, i: b * nb + i
    prev = lambda b, i: (b * nb + i) * sub - jnp.minimum(i, 1)
    return pl.pallas_call(
        _attn_prompt_kernel,
        grid=(batch, nb),
        in_specs=[
            pl.BlockSpec(memory_space=pltpu.SMEM),
            pl.BlockSpec((tq, SWA_WIDTH), lambda b, i: (row(b, i), 0)),
            pl.BlockSpec((tq, SWA_WIDTH), lambda b, i: (row(b, i), 1)),
            pl.BlockSpec((WINDOW, SWA_KV_WIDTH), lambda b, i: (prev(b, i), 0)),
            pl.BlockSpec((tq, SWA_KV_WIDTH), lambda b, i: (row(b, i), 0)),
            pl.BlockSpec((WINDOW, SWA_KV_WIDTH), lambda b, i: (prev(b, i), 1)),
            pl.BlockSpec((tq, SWA_KV_WIDTH), lambda b, i: (row(b, i), 1)),
        ],
        out_specs=pl.BlockSpec((tq, SWA_WIDTH), lambda b, i: (row(b, i), 0)),
        out_shape=jax.ShapeDtypeStruct((batch * seq, SWA_WIDTH), BF16),
        scratch_shapes=[pltpu.VMEM((2, SWA_HEADS // 2, 2 * WINDOW, 2 * WINDOW), F32)],
        compiler_params=_params(("arbitrary", "arbitrary")),
        name="attn_prompt",
    )(sinks, qz, qz, kv, kv, kv, kv)


def _attn_sample_kernel(sink_ref, slope_ref, q_ref, z_ref, kn_ref, vn_ref, kc_ref, vc_ref,
                        out_ref, kwin_ref, vwin_ref, *, tq, nb):
    hd, nk = SWA_HEAD_DIM, 2 * WINDOW
    rows = SWA_HEADS * tq
    grows = SWA_GROUP * tq
    seqs = range(nb)
    groups = range(SWA_KV_HEADS)

    lane = lax.broadcasted_iota(jnp.int32, (rows, nk), 1)
    tok = lax.broadcasted_iota(jnp.int32, (rows, nk), 0) % tq
    in_buffer = lane < WINDOW
    dist = jnp.where(in_buffer, WINDOW + tok - lane, (nk - tq) + tok - lane)
    allowed = (dist >= 0) & (dist <= WINDOW) & (in_buffer | (lane >= nk - tq))
    penalty = slope_ref[...] * dist.astype(F32)
    new_lanes = lax.broadcasted_iota(jnp.int32, (hd, WINDOW), 1) >= WINDOW - tq

    def new_rows_t(ref, b):
        x = jnp.concatenate([jnp.zeros((WINDOW - tq, SWA_KV_WIDTH), F32),
                             ref[b * tq:(b + 1) * tq, :]], axis=0)
        xt = [x[:, c * WINDOW:(c + 1) * WINDOW].T for c in range(SWA_KV_WIDTH // WINDOW)]
        per_block = WINDOW // hd
        return [xt[g // per_block][(g % per_block) * hd:(g % per_block + 1) * hd]
                for g in groups]

    k_all, v_all = {}, {}
    for b in seqs:
        kn_t, vn_t = new_rows_t(kn_ref, b), new_rows_t(vn_ref, b)
        for g in groups:
            kc, vc = kc_ref[b, g], vc_ref[b, g]
            k_all[b, g] = jnp.concatenate([kc, kn_t[g]], axis=1).astype(BF16)
            v_all[b, g] = jnp.concatenate([vc, vn_t[g]], axis=1).astype(BF16)
            kwin_ref[b, g] = jnp.where(new_lanes, kn_t[g], pltpu.roll(kc, WINDOW - tq, axis=1))
            vwin_ref[b, g] = jnp.where(new_lanes, vn_t[g], pltpu.roll(vc, WINDOW - tq, axis=1))

    s = []
    for b in seqs:
        q = q_ref[b * tq:(b + 1) * tq, :]
        parts = []
        for g in groups:
            qs = jnp.concatenate([q[:, h * hd:(h + 1) * hd]
                                  for h in range(g * SWA_GROUP, (g + 1) * SWA_GROUP)], axis=0)
            parts.append(jnp.dot(qs.astype(BF16), k_all[b, g], preferred_element_type=F32))
        s.append(jnp.concatenate(parts, axis=0))

    sink = sink_ref[...]
    s = [jnp.where(allowed, sb * (hd ** -0.5) - penalty, -jnp.inf) for sb in s]
    m = [jnp.maximum(jnp.max(sb, axis=-1, keepdims=True), sink) for sb in s]
    p = [jnp.exp(s[b] - m[b]) for b in seqs]
    inv = [1.0 / (jnp.sum(p[b], axis=-1, keepdims=True) + jnp.exp(sink - m[b])) for b in seqs]

    for b in seqs:
        pb = p[b].astype(BF16)
        o = jnp.concatenate(
            [lax.dot_general(pb[g * grows:(g + 1) * grows], v_all[b, g], _NT_DIMS,
                             preferred_element_type=F32) for g in groups], axis=0) * inv[b]
        o = jnp.concatenate([o[h * tq:(h + 1) * tq] for h in range(SWA_HEADS)], axis=1)
        z = z_ref[b * tq:(b + 1) * tq, :]
        out_ref[b * tq:(b + 1) * tq, :] = (o * _silu(z)).astype(out_ref.dtype)


def _attn_sample(qz, kv, cache_k, cache_v, sinks, *, batch, seq):
    nb = SAMPLE_SEQS_PER_STEP
    assert batch % nb == 0
    rows = nb * seq
    sink_col = jnp.repeat(sinks, seq).reshape(SWA_HEADS * seq, 1)
    slope_col = jnp.repeat(jnp.asarray([_alibi_slope(h) for h in range(SWA_HEADS)], F32),
                           seq).reshape(SWA_HEADS * seq, 1)
    col_spec = pl.BlockSpec((SWA_HEADS * seq, 1), lambda i: (0, 0))
    win_spec = pl.BlockSpec((nb, SWA_KV_HEADS, SWA_HEAD_DIM, WINDOW), lambda i: (i, 0, 0, 0))
    win_shape = jax.ShapeDtypeStruct((batch, SWA_KV_HEADS, SWA_HEAD_DIM, WINDOW), F32)
    to_lanes = lambda c: jnp.transpose(c, (0, 2, 3, 1))
    att, k_win, v_win = pl.pallas_call(
        functools.partial(_attn_sample_kernel, tq=seq, nb=nb),
        grid=(batch // nb,),
        in_specs=[
            col_spec,
            col_spec,
            pl.BlockSpec((rows, SWA_WIDTH), lambda i: (i, 0)),
            pl.BlockSpec((rows, SWA_WIDTH), lambda i: (i, 1)),
            pl.BlockSpec((rows, SWA_KV_WIDTH), lambda i: (i, 0)),
            pl.BlockSpec((rows, SWA_KV_WIDTH), lambda i: (i, 1)),
            win_spec,
            win_spec,
        ],
        out_specs=[pl.BlockSpec((rows, SWA_WIDTH), lambda i: (i, 0)), win_spec, win_spec],
        out_shape=[jax.ShapeDtypeStruct((batch * seq, SWA_WIDTH), F32), win_shape, win_shape],
        compiler_params=_params(("parallel",)),
        name="attn_sample",
    )(sink_col, slope_col, qz, qz, kv, kv, to_lanes(cache_k), to_lanes(cache_v))
    return att, jnp.transpose(k_win, (0, 3, 1, 2)), jnp.transpose(v_win, (0, 3, 1, 2))


def kernel(x_prompt, x_sample, state_gla, cache_k_win, cache_v_win, g_norm_a, w_in_a,
           w_gate_up, b_gate, g_onorm_a, w_out_a, g_norm_kv, w_kv, g_norm_b, w_in_b, sinks,
           w_out_b, g_final):
    assert w_in_a.shape[0] == 1 and w_in_b.shape[0] == 1, "one GLA layer, one SWA layer"
    assert cache_k_win.shape[1] == WINDOW
    pb, ps, _ = x_prompt.shape
    sb, ss, _ = x_sample.shape
    pm, sm = pb * ps, sb * ss
    g_norm_a, w_gate_up, b_gate, g_onorm_a = g_norm_a[0], w_gate_up[0], b_gate[0], g_onorm_a[0]
    g_norm_b, sinks = g_norm_b[0], sinks[0]
    w_kv = w_kv.astype(BF16)
    w_in_a_t = jnp.swapaxes(w_in_a, 1, 2)
    w_glow = w_in_a_t[0, GLA_MAIN_COLS:, :]
    h_p = x_prompt.reshape(pm, D_MODEL)
    h_s = x_sample.reshape(sm, D_MODEL)

    proj_s, glow_s, w_in_a_bf16 = _norm_matmul(
        h_s, g_norm_a, w_in_a_t, tm=sm, tn=SAMPLE_COL_TILE, n=GLA_MAIN_COLS, out_dtype=F32,
        w_extra=w_glow, emit_weights=True, w_transposed=True)
    bcum_s = _gla_gates(glow_s, w_gate_up, b_gate, chunk=ss, tm=sm)
    proj_p, glow_p = _norm_matmul(
        h_p, g_norm_a, w_in_a_bf16, tm=PROMPT_ROW_TILE, tn=PROMPT_COL_TILE, n=GLA_MAIN_COLS,
        out_dtype=BF16, w_extra=w_glow, w_transposed=True)
    bcum_p, o_s, gla_s = _gla_gates(
        glow_p, w_gate_up, b_gate, chunk=GLA_PROMPT_CHUNK, tm=GATE_ROW_TILE,
        rider=dict(proj=proj_s, bcum=bcum_s, gon=g_onorm_a, batch=sb, seq=ss,
                   s0=state_gla.reshape(state_gla.shape[1:])))
    h_s, w_out_a_bf16 = _matmul_residual(o_s, w_out_a, h_s, tm=sm, tk=SAMPLE_K_TILE,
                                         emit_weights=True)
    o_p, gla_p = _gla(proj_p, bcum_p, g_onorm_a, batch=pb, seq=ps, chunk=GLA_PROMPT_CHUNK,
                      out_dtype=BF16)
    h_p = _matmul_residual(o_p, w_out_a_bf16, h_p, tm=RESIDUAL_ROW_TILE)

    qz_s, kv_s, w_in_b_bf16 = _norm_matmul(
        h_s, g_norm_b, w_in_b, tm=sm, tn=QZ_COL_TILE, out_dtype=F32,
        w_extra=w_kv, g_extra=g_norm_kv, emit_weights=True)
    att_s, k_s, v_s = _attn_sample(qz_s, kv_s, cache_k_win, cache_v_win, sinks,
                                   batch=sb, seq=ss)
    y_s, w_out_b_bf16 = _matmul_residual(att_s, w_out_b, h_s, tm=sm, tk=SAMPLE_K_TILE,
                                         g_final=g_final, emit_weights=True)
    qz_p, kv_p = _norm_matmul(
        h_p, g_norm_b, w_in_b_bf16, tm=PROMPT_ROW_TILE, tn=QZ_COL_TILE, out_dtype=BF16,
        w_extra=w_kv, g_extra=g_norm_kv)
    att_p = _attn_prompt(qz_p, kv_p, sinks, batch=pb, seq=ps)
    y_p = _matmul_residual(att_p, w_out_b_bf16, h_p, tm=RESIDUAL_ROW_TILE, g_final=g_final)

    kv_win = kv_p.reshape(pb, ps, 2 * SWA_KV_WIDTH)[:, ps - WINDOW:]
    kv_win = kv_win.reshape(pb, WINDOW, 2, SWA_KV_HEADS, SWA_HEAD_DIM)
    return (y_p.reshape(pb, ps, D_MODEL), y_s.reshape(sb, ss, D_MODEL), gla_p[None],
            gla_s[None], kv_win[:, :, 0], kv_win[:, :, 1], k_s, v_s)
```

```python
import functools

import jax
import jax.numpy as jnp
from jax import lax
from jax.experimental import pallas as pl
from jax.experimental.pallas import tpu as pltpu

F32 = jnp.float32
BF16 = jnp.bfloat16

D_MODEL = 2048
GLA_HEADS = 4
GLA_KEY_DIM = D_MODEL // 2
GLA_VALUE_DIM = D_MODEL
GLA_DK = GLA_KEY_DIM // GLA_HEADS
GLA_DV = GLA_VALUE_DIM // GLA_HEADS
GLA_GATE_RANK = 16
GLA_GATE_TEMP = 16.0
GLA_MAIN_COLS = 2 * GLA_KEY_DIM + 2 * GLA_VALUE_DIM
SWA_HEAD_DIM = 64
SWA_HEADS = D_MODEL // SWA_HEAD_DIM
SWA_KV_HEADS = 4
SWA_GROUP = SWA_HEADS // SWA_KV_HEADS
SWA_WIDTH = SWA_HEADS * SWA_HEAD_DIM
SWA_KV_WIDTH = SWA_KV_HEADS * SWA_HEAD_DIM
WINDOW = 128
RMS_EPS = 1e-6
LOG2E = 1.4426950408889634

V7X_VMEM_LIMIT_BYTES = 61 * 1024 * 1024
GLA_PROMPT_CHUNK = 128
GLA_SEQS_PER_STEP = 4
ATTN_BLOCKS_PER_STEP = 4
ATTN_BLOCKS_IN_FLIGHT = 2
SAMPLE_SEQS_PER_STEP = 8
NORM_ROW_CHUNK = 256
GATE_CUMSUM_ROWS = 128
GATE_ROW_TILE = 1024
PROMPT_ROW_TILE = 1024
PROMPT_COL_TILE = 2048
SAMPLE_COL_TILE = 1024
QZ_COL_TILE = 2048
SAMPLE_K_TILE = 1024
RESIDUAL_ROW_TILE = 512

_NT_DIMS = (((1,), (1,)), ((), ()))
_TN_DIMS = (((0,), (0,)), ((), ()))


def _params(semantics):
    return pltpu.CompilerParams(dimension_semantics=semantics,
                                vmem_limit_bytes=V7X_VMEM_LIMIT_BYTES)


def _silu(x):
    return x / (1.0 + jnp.exp(-x))


def _rms_scale(x):
    return lax.rsqrt(jnp.mean(x * x, axis=-1, keepdims=True) + RMS_EPS)


def _norm_matmul_kernel(x_ref, g_ref, w_ref, *rest, has_extra, extra_gain, emit_weights,
                        w_transposed):
    dims = _NT_DIMS if w_transposed else (((1,), (0,)), ((), ()))

    def mm(a, b):
        return lax.dot_general(a, b, dims, preferred_element_type=F32)

    rest = list(rest)
    g2_ref = rest.pop(0) if extra_gain else None
    w2_ref = rest.pop(0) if has_extra else None
    o_ref = rest.pop(0)
    o2_ref = rest.pop(0) if has_extra else None
    wb_ref = rest.pop(0) if emit_weights else None
    (xn_ref,) = rest

    first = pl.program_id(1) == 0

    def weights():
        w = w_ref[...].astype(BF16)
        if emit_weights:
            wb_ref[...] = w
        return w

    @pl.when(first)
    def _():
        tm = x_ref.shape[0]
        rc = min(tm, NORM_ROW_CHUNK)
        w = weights()
        for c in range(tm // rc):
            rows = slice(c * rc, (c + 1) * rc)
            x = x_ref[rows, :]
            xs = x * _rms_scale(x)
            xn = (xs * g_ref[...]).astype(BF16)
            xn_ref[rows, :] = xn
            o_ref[rows, :] = mm(xn, w).astype(o_ref.dtype)
            if has_extra:
                xn2 = (xs * g2_ref[...]).astype(BF16) if extra_gain else xn
                o2_ref[rows, :] = mm(xn2, w2_ref[...].astype(BF16)).astype(o2_ref.dtype)

    @pl.when(jnp.logical_not(first))
    def _():
        o_ref[...] = mm(xn_ref[...], weights()).astype(o_ref.dtype)


def _norm_matmul(x, g, w, *, tm, tn, out_dtype, n=None, w_extra=None, g_extra=None,
                 emit_weights=False, w_transposed=False):
    m, k = x.shape
    n_axis = -2 if w_transposed else -1
    n = w.shape[n_axis] if n is None else n
    assert n % tn == 0 and m % tm == 0
    assert not emit_weights or m == tm, "each weight block must be visited exactly once"
    grid = (m // tm, n // tn)
    w_block = (tn, k) if w_transposed else (k, tn)
    w_index = (lambda i, j: (j, 0)) if w_transposed else (lambda i, j: (0, j))
    if w.ndim == 3:
        w_spec = pl.BlockSpec((None,) + w_block, lambda i, j: (0,) + w_index(i, j))
    else:
        w_spec = pl.BlockSpec(w_block, w_index)
    in_specs = [
        pl.BlockSpec((tm, k), lambda i, j: (i, 0)),
        pl.BlockSpec((1, k), lambda i, j: (0, 0)),
        w_spec,
    ]
    out_shape = [jax.ShapeDtypeStruct((m, n), out_dtype)]
    out_specs = [pl.BlockSpec((tm, tn), lambda i, j: (i, j))]
    args = [x, g.reshape(1, k), w]
    if g_extra is not None:
        in_specs.append(pl.BlockSpec((1, k), lambda i, j: (0, 0)))
        args.append(g_extra.reshape(1, k))
    if w_extra is not None:
        n2 = w_extra.shape[n_axis]
        in_specs.append(pl.BlockSpec(w_extra.shape, lambda i, j: (0, 0)))
        out_shape.append(jax.ShapeDtypeStruct((m, n2), F32))
        out_specs.append(pl.BlockSpec((tm, n2), lambda i, j: (i, 0)))
        args.append(w_extra)
    if emit_weights:
        out_shape.append(jax.ShapeDtypeStruct((n, k) if w_transposed else (k, n), BF16))
        out_specs.append(pl.BlockSpec(w_block, w_index))
    res = pl.pallas_call(
        functools.partial(_norm_matmul_kernel, has_extra=w_extra is not None,
                          extra_gain=g_extra is not None, emit_weights=emit_weights,
                          w_transposed=w_transposed),
        grid=grid,
        in_specs=in_specs,
        out_specs=out_specs,
        out_shape=out_shape,
        scratch_shapes=[pltpu.VMEM((tm, k), BF16)],
        compiler_params=_params(("parallel", "arbitrary")),
        name=f"norm_matmul_{m}x{n}",
    )(*args)
    return res if len(res) > 1 else res[0]


def _matmul_residual_kernel(a_ref, w_ref, r_ref, *rest, final_norm, emit_weights, nk):
    rest = list(rest)
    g_ref = rest.pop(0) if final_norm else None
    o_ref = rest.pop(0)
    w = w_ref[...].astype(BF16)
    if emit_weights:
        rest[0][...] = w
    part = jnp.dot(a_ref[...].astype(BF16), w, preferred_element_type=F32)

    if nk == 1:
        h = r_ref[...] + part
        if final_norm:
            h = h * _rms_scale(h) * g_ref[...]
        o_ref[...] = h
        return

    kk = pl.program_id(1)

    @pl.when(kk == 0)
    def _():
        o_ref[...] = r_ref[...] + part

    @pl.when(kk > 0)
    def _():
        o_ref[...] += part

    if final_norm:
        @pl.when(kk == nk - 1)
        def _():
            h = o_ref[...]
            o_ref[...] = h * _rms_scale(h) * g_ref[...]


def _matmul_residual(a, w, res, *, tm, tk=None, g_final=None, emit_weights=False):
    m, k = a.shape
    n = w.shape[-1]
    tk = k if tk is None else tk
    assert m % tm == 0 and k % tk == 0
    assert not emit_weights or m == tm, "each weight block must be visited exactly once"
    if w.ndim == 3:
        w_spec = pl.BlockSpec((None, tk, n), lambda i, j: (0, j, 0))
    else:
        w_spec = pl.BlockSpec((tk, n), lambda i, j: (j, 0))
    in_specs = [
        pl.BlockSpec((tm, tk), lambda i, j: (i, j)),
        w_spec,
        pl.BlockSpec((tm, n), lambda i, j: (i, 0)),
    ]
    args = [a, w, res]
    if g_final is not None:
        in_specs.append(pl.BlockSpec((1, n), lambda i, j: (0, 0)))
        args.append(g_final.reshape(1, n))
    out_shape = [jax.ShapeDtypeStruct((m, n), F32)]
    out_specs = [pl.BlockSpec((tm, n), lambda i, j: (i, 0))]
    if emit_weights:
        out_shape.append(jax.ShapeDtypeStruct((k, n), BF16))
        out_specs.append(pl.BlockSpec((tk, n), lambda i, j: (j, 0)))
    res = pl.pallas_call(
        functools.partial(_matmul_residual_kernel, final_norm=g_final is not None,
                          emit_weights=emit_weights, nk=k // tk),
        grid=(m // tm, k // tk),
        in_specs=in_specs,
        out_specs=out_specs,
        out_shape=out_shape,
        compiler_params=_params(("parallel", "arbitrary")),
        name=f"matmul_residual_{m}" + ("_final" if g_final is not None else ""),
    )(*args)
    return res if emit_weights else res[0]


def _split_bf16(x):
    hi = x.astype(BF16)
    return hi, (x - hi.astype(F32)).astype(BF16)


def _gla_gates_kernel(glow_ref, wg_ref, bg_ref, bcum_ref, *, chunk):
    rows = glow_ref.shape[0]
    g_hi, g_lo = _split_bf16(glow_ref[...])
    w_hi, w_lo = _split_bf16(wg_ref[...])
    x = jnp.dot(jnp.concatenate([g_hi, g_lo, g_hi], axis=1),
                jnp.concatenate([w_hi, w_hi, w_lo], axis=0),
                preferred_element_type=F32) + bg_ref[...]
    softplus2 = jnp.log2(1.0 + jnp.exp2(jnp.abs(x) * (-LOG2E)))
    logg = jnp.minimum(x, 0.0) * (LOG2E / GLA_GATE_TEMP) - softplus2 * (1.0 / GLA_GATE_TEMP)

    span = GATE_CUMSUM_ROWS
    row = lax.broadcasted_iota(jnp.int32, (span, span), 0)
    col = lax.broadcasted_iota(jnp.int32, (span, span), 1)
    same_chunk = (row // chunk) == (col // chunk) if chunk < span else True
    tril = jnp.where((col <= row) & same_chunk, 1.0, 0.0).astype(BF16)
    tril2 = jnp.concatenate([tril, tril], axis=1)
    for i in range(rows // span):
        hi, lo = _split_bf16(logg[i * span:(i + 1) * span])
        bcum_ref[i * span:(i + 1) * span, :] = jnp.dot(
            tril2, jnp.concatenate([hi, lo], axis=0), preferred_element_type=F32)


def _gates_with_gla_kernel(glow_ref, wg_ref, bg_ref, *rest, chunk, n_gla_in):
    gla_in, (bcum_ref, o_ref, sfin_ref) = rest[:n_gla_in], rest[n_gla_in:]
    _gla_kernel(*gla_in, o_ref, sfin_ref, has_init=True, n_chunks=1)
    _gla_gates_kernel(glow_ref, wg_ref, bg_ref, bcum_ref, chunk=chunk)


def _gla_gates(glow, wg, bg, *, chunk, tm, rider=None):
    m = glow.shape[0]
    assert m % tm == 0 and tm % GATE_CUMSUM_ROWS == 0
    assert GATE_CUMSUM_ROWS % chunk == 0 or chunk % GATE_CUMSUM_ROWS == 0
    assert chunk <= GATE_CUMSUM_ROWS, "cumulative sums do not cross row spans"
    steps = m // tm
    in_specs = [
        pl.BlockSpec((tm, GLA_GATE_RANK), lambda i: (i, 0)),
        pl.BlockSpec((GLA_GATE_RANK, GLA_KEY_DIM), lambda i: (0, 0)),
        pl.BlockSpec((1, GLA_KEY_DIM), lambda i: (0, 0)),
    ]
    args = [glow, wg, bg.reshape(1, -1)]
    out_specs = [pl.BlockSpec((tm, GLA_KEY_DIM), lambda i: (i, 0))]
    out_shape = [jax.ShapeDtypeStruct((m, GLA_KEY_DIM), F32)]
    if rider is None:
        body = functools.partial(_gla_gates_kernel, chunk=chunk)
    else:
        batch, seq = rider["batch"], rider["seq"]
        assert batch % steps == 0
        g_in, g_args, g_out, g_shape = _gla_operands(
            rider["proj"], rider["bcum"], rider["gon"], rider["s0"], batch=batch, seq=seq,
            chunk=seq, ns=batch // steps, out_dtype=F32, index=lambda i: (i, 0))
        body = functools.partial(_gates_with_gla_kernel, chunk=chunk, n_gla_in=len(g_in))
        in_specs += g_in
        args += g_args
        out_specs += g_out
        out_shape += g_shape
    res = pl.pallas_call(
        body,
        grid=(steps,),
        in_specs=in_specs,
        out_specs=out_specs,
        out_shape=out_shape,
        compiler_params=_params(("parallel",)),
        name=f"gla_gates_{m}",
    )(*args)
    if rider is None:
        return res[0]
    bcum, o, s_fin = res
    return bcum, o.reshape(rider["batch"] * rider["seq"], GLA_VALUE_DIM), s_fin


def _gla_kernel(q_ref, k_ref, v_ref, r_ref, bcum_ref, gon_ref, *rest, has_init, n_chunks):
    rest = list(rest)
    s0_ref = rest.pop(0) if has_init else None
    o_ref, sfin_ref = rest[:2]
    s_ref = rest[2] if n_chunks > 1 else None
    c = pl.program_id(1) if n_chunks > 1 else None
    nseq, chunk = q_ref.shape[:2]
    chains = [(s, h) for s in range(nseq) for h in range(GLA_HEADS)]

    if n_chunks > 1:
        @pl.when(c == 0)
        def _():
            if has_init:
                s_ref[...] = s0_ref[...]
            else:
                s_ref[...] = jnp.zeros_like(s_ref)

    row = lax.broadcasted_iota(jnp.int32, (chunk, chunk), 0)
    col = lax.broadcasted_iota(jnp.int32, (chunk, chunk), 1)
    causal = col <= row
    heads = range(GLA_HEADS)
    ks = [slice(h * GLA_DK, (h + 1) * GLA_DK) for h in heads]
    vs = [slice(h * GLA_DV, (h + 1) * GLA_DV) for h in heads]

    def state(s, h):
        if n_chunks > 1:
            return s_ref[s, h]
        return s0_ref[s, h] if has_init else jnp.zeros((GLA_DK, GLA_DV), F32)

    q_inter, k_state, scores, decay = {}, {}, {}, {}
    for s, h in chains:
        b = bcum_ref[s, :, ks[h]]
        b_last = b[chunk - 1:chunk, :]
        b_mid = b[chunk // 2 - 1:chunk // 2, :]
        q = q_ref[s, :, ks[h]].astype(F32)
        k = k_ref[s, :, ks[h]].astype(F32)
        q_inter[s, h] = (q * jnp.exp2(b)).astype(BF16)
        q_intra = (q * jnp.exp2(b - b_mid)).astype(BF16)
        k_intra = (k * jnp.exp2(b_mid - b)).astype(BF16)
        k_state[s, h] = (k * jnp.exp2(b_last - b)).astype(BF16)
        scores[s, h] = lax.dot_general(q_intra, k_intra, _NT_DIMS,
                                       preferred_element_type=F32)
        decay[s, h] = jnp.exp2(jnp.broadcast_to(b_last, (128, GLA_DK))).T

    o = {}
    for s, h in chains:
        sc = jnp.where(causal, scores[s, h], 0.0).astype(BF16)
        v = v_ref[s, :, vs[h]].astype(BF16)
        o[s, h] = (jnp.dot(q_inter[s, h], state(s, h).astype(BF16),
                           preferred_element_type=F32)
                   + jnp.dot(sc, v, preferred_element_type=F32))
        upd = lax.dot_general(k_state[s, h], v, _TN_DIMS, preferred_element_type=F32)
        s_new = (state(s, h) * jnp.concatenate([decay[s, h]] * (GLA_DV // 128), axis=1)
                 + upd)
        if n_chunks > 1:
            s_ref[s, h] = s_new
        else:
            sfin_ref[s, h] = s_new

    qs = GLA_DK ** -0.5
    scale = {sh: qs * lax.rsqrt(jnp.mean(o[sh] * o[sh], axis=-1, keepdims=True) * (qs * qs)
                                + RMS_EPS) for sh in chains}
    for s, h in chains:
        r = r_ref[s, :, vs[h]].astype(F32)
        o_ref[s, :, vs[h]] = (o[s, h] * scale[s, h] * gon_ref[:, vs[h]]
                              * _silu(r)).astype(o_ref.dtype)

    if n_chunks > 1:
        @pl.when(c == n_chunks - 1)
        def _():
            sfin_ref[...] = s_ref[...]


def _gla_operands(proj, bcum, gon, s0, *, batch, seq, chunk, ns, out_dtype, index):
    assert batch % ns == 0 and seq % chunk == 0
    proj = proj.reshape(batch, seq, proj.shape[-1])
    bcum = bcum.reshape(batch, seq, bcum.shape[-1])
    kb, vb, rb = 1, 2 * GLA_KEY_DIM // GLA_VALUE_DIM, 2 * GLA_KEY_DIM // GLA_VALUE_DIM + 1

    def rows(width, col):
        return pl.BlockSpec((ns, chunk, width), lambda *g: (*index(*g), col))

    state_spec = pl.BlockSpec((ns, GLA_HEADS, GLA_DK, GLA_DV),
                              lambda *g: (index(*g)[0], 0, 0, 0))
    in_specs = [rows(GLA_KEY_DIM, 0), rows(GLA_KEY_DIM, kb), rows(GLA_VALUE_DIM, vb),
                rows(GLA_VALUE_DIM, rb), rows(GLA_KEY_DIM, 0),
                pl.BlockSpec((1, GLA_VALUE_DIM), lambda *g: (0, 0))]
    args = [proj, proj, proj, proj, bcum, gon.reshape(1, -1)]
    if s0 is not None:
        in_specs.append(state_spec)
        args.append(s0)
    out_specs = [rows(GLA_VALUE_DIM, 0), state_spec]
    out_shape = [jax.ShapeDtypeStruct((batch, seq, GLA_VALUE_DIM), out_dtype),
                 jax.ShapeDtypeStruct((batch, GLA_HEADS, GLA_DK, GLA_DV), F32)]
    return in_specs, args, out_specs, out_shape


def _gla(proj, bcum, gon, *, batch, seq, chunk, out_dtype, s0=None):
    n = seq // chunk
    ns = GLA_SEQS_PER_STEP
    in_specs, args, out_specs, out_shape = _gla_operands(
        proj, bcum, gon, s0, batch=batch, seq=seq, chunk=chunk, ns=ns, out_dtype=out_dtype,
        index=lambda b, c: (b, c))
    scratch = [pltpu.VMEM((ns, GLA_HEADS, GLA_DK, GLA_DV), F32)] if n > 1 else []
    o, s_fin = pl.pallas_call(
        functools.partial(_gla_kernel, has_init=s0 is not None, n_chunks=n),
        grid=(batch // ns, n),
        in_specs=in_specs,
        out_specs=out_specs,
        out_shape=out_shape,
        scratch_shapes=scratch,
        compiler_params=_params(("parallel", "arbitrary")),
        name=f"gla_chunk{chunk}",
    )(*args)
    return o.reshape(batch * seq, GLA_VALUE_DIM), s_fin


def _alibi_slope(head):
    return 2.0 ** (-8.0 * (head + 1) / SWA_HEADS)


def _attn_prompt_kernel(sinks_ref, q_ref, z_ref, kp_ref, ko_ref, vp_ref, vo_ref, out_ref,
                        kwin_ref, vwin_ref, bias_ref):
    hd, nkeys = SWA_HEAD_DIM, 2 * WINDOW
    pair_w = 2 * hd
    pairs_per_group = SWA_GROUP // 2
    blk = pl.program_id(1)

    @pl.when((pl.program_id(0) == 0) & (blk == 0))
    def _():
        kj = lax.broadcasted_iota(jnp.int32, (nkeys, WINDOW), 0)
        qi = lax.broadcasted_iota(jnp.int32, (nkeys, WINDOW), 1)
        dist = WINDOW + qi - kj
        ok = (dist >= 0) & (dist <= WINDOW)
        ok_first = ok & (kj >= WINDOW)
        distf = dist.astype(F32)
        for h in range(SWA_HEADS):
            pen = (-_alibi_slope(h) * LOG2E) * distf
            sl = slice((h % 2) * WINDOW, (h % 2 + 1) * WINDOW)
            bias_ref[0, h // 2, :, sl] = jnp.where(ok_first, pen, -jnp.inf)
            bias_ref[1, h // 2, :, sl] = jnp.where(ok, pen, -jnp.inf)

    n_sub = q_ref.shape[0] // WINDOW
    k_rows = jnp.concatenate([kp_ref[...], ko_ref[...]], axis=0)
    v_rows = jnp.concatenate([vp_ref[...], vo_ref[...]], axis=0)
    ones = jnp.ones((16, nkeys), F32)
    lane = lax.broadcasted_iota(jnp.int32, (nkeys, pair_w), 1)
    qlane = lax.broadcasted_iota(jnp.int32, (WINDOW, pair_w), 1)
    qk_scale = (hd ** -0.5) * LOG2E
    quad_pairs = 2
    n_quads = SWA_HEADS // (2 * quad_pairs)
    quads_per_group = pairs_per_group // quad_pairs
    quarter = lax.broadcasted_iota(jnp.int32, (1, 2 * quad_pairs * WINDOW), 1) // WINDOW

    def prepare(sub):
        k = k_rows[sub * WINDOW:sub * WINDOW + nkeys]
        vt = v_rows[sub * WINDOW:sub * WINDOW + nkeys].T
        k2, vt1 = [], []
        for g in range(SWA_KV_HEADS):
            kblk = k[:, (g // 2) * pair_w:(g // 2 + 1) * pair_w]
            k_here = jnp.where((lane < hd) if g % 2 == 0 else (lane >= hd), kblk, 0.0)
            k2.append((k_here + pltpu.roll(k_here, hd, axis=1)).astype(BF16))
            vt1.append(jnp.concatenate([vt[g * hd:(g + 1) * hd], ones], axis=0).astype(BF16))
        return dict(qrows=slice(sub * WINDOW, (sub + 1) * WINDOW), k2=k2, vt1=vt1,
                    tbl=jnp.minimum(blk, 1) if sub == 0 else 1)

    def scores(blkst, quad):
        parts = []
        for j in range(quad_pairs):
            col = (quad * quad_pairs + j) * pair_w
            q_pair = q_ref[blkst["qrows"], col:col + pair_w]
            zero = jnp.zeros_like(q_pair)
            parts += [jnp.where(qlane < hd, q_pair, zero), jnp.where(qlane >= hd, q_pair, zero)]
        return lax.dot_general(blkst["k2"][quad // quads_per_group],
                               jnp.concatenate(parts, axis=0),
                               _NT_DIMS, preferred_element_type=F32)

    def finish(blkst, quad, st):
        pair0 = quad * quad_pairs
        bias = jnp.concatenate([bias_ref[blkst["tbl"], pair0 + j] for j in range(quad_pairs)],
                               axis=1)
        s2 = st * qk_scale + bias
        sink2 = sinks_ref[2 * pair0]
        for t in range(1, 2 * quad_pairs):
            sink2 = jnp.where(quarter == t, sinks_ref[2 * pair0 + t], sink2)
        sink2 = sink2 * LOG2E
        m = jnp.maximum(jnp.max(s2, axis=0, keepdims=True), sink2)
        p = jnp.exp2(s2 - m).astype(BF16)
        oa = jnp.dot(blkst["vt1"][quad // quads_per_group], p, preferred_element_type=F32)
        denom = oa[hd:hd + 1] + jnp.exp2(sink2 - m)
        on = oa[0:hd] * (1.0 / denom)
        for j in range(quad_pairs):
            lo = 2 * j * WINDOW
            o_pair = jnp.concatenate([on[:, lo:lo + WINDOW],
                                      on[:, lo + WINDOW:lo + 2 * WINDOW]], axis=0).T
            col = (pair0 + j) * pair_w
            z_pair = z_ref[blkst["qrows"], col:col + pair_w].astype(F32)
            out_ref[blkst["qrows"], col:col + pair_w] = (
                o_pair * _silu(z_pair)).astype(out_ref.dtype)

    for first in range(0, n_sub, ATTN_BLOCKS_IN_FLIGHT):
        blocks = [prepare(sub) for sub in range(first, min(first + ATTN_BLOCKS_IN_FLIGHT, n_sub))]
        st_next = [scores(b, 0) for b in blocks]
        for quad in range(n_quads):
            for i, b in enumerate(blocks):
                st = st_next[i]
                if quad + 1 < n_quads:
                    st_next[i] = scores(b, quad + 1)
                finish(b, quad, st)

    @pl.when(blk == pl.num_programs(1) - 1)
    def _():
        last = slice((n_sub - 1) * WINDOW, n_sub * WINDOW)
        kwin_ref[...] = ko_ref[last, :].T
        vwin_ref[...] = vo_ref[last, :].T


def _attn_prompt(qz, kv, sinks, *, batch, seq):
    sub = ATTN_BLOCKS_PER_STEP
    tq = sub * WINDOW
    nb = seq // tq
    assert seq % tq == 0
    row = lambda b, i: b * nb + i
    prev = lambda b, i: (b * nb + i) * sub - jnp.minimum(i, 1)
    win_spec = pl.BlockSpec((None, SWA_KV_WIDTH, WINDOW), lambda b, i: (b, 0, 0))
    win_shape = jax.ShapeDtypeStruct((batch, SWA_KV_WIDTH, WINDOW), F32)
    att, k_win, v_win = pl.pallas_call(
        _attn_prompt_kernel,
        grid=(batch, nb),
        in_specs=[
            pl.BlockSpec(memory_space=pltpu.SMEM),
            pl.BlockSpec((tq, SWA_WIDTH), lambda b, i: (row(b, i), 0)),
            pl.BlockSpec((tq, SWA_WIDTH), lambda b, i: (row(b, i), 1)),
            pl.BlockSpec((WINDOW, SWA_KV_WIDTH), lambda b, i: (prev(b, i), 0)),
            pl.BlockSpec((tq, SWA_KV_WIDTH), lambda b, i: (row(b, i), 0)),
            pl.BlockSpec((WINDOW, SWA_KV_WIDTH), lambda b, i: (prev(b, i), 1)),
            pl.BlockSpec((tq, SWA_KV_WIDTH), lambda b, i: (row(b, i), 1)),
        ],
        out_specs=[pl.BlockSpec((tq, SWA_WIDTH), lambda b, i: (row(b, i), 0)),
                   win_spec, win_spec],
        out_shape=[jax.ShapeDtypeStruct((batch * seq, SWA_WIDTH), BF16), win_shape, win_shape],
        scratch_shapes=[pltpu.VMEM((2, SWA_HEADS // 2, 2 * WINDOW, 2 * WINDOW), F32)],
        compiler_params=_params(("arbitrary", "arbitrary")),
        name="attn_prompt",
    )(sinks, qz, qz, kv, kv, kv, kv)
    to_rows = lambda w: jnp.transpose(
        w.reshape(batch, SWA_KV_HEADS, SWA_HEAD_DIM, WINDOW), (0, 3, 1, 2))
    return att, to_rows(k_win), to_rows(v_win)


def _attn_sample_kernel(sink_ref, slope_ref, q_ref, z_ref, kn_ref, vn_ref, kc_ref, vc_ref,
                        out_ref, kwin_ref, vwin_ref, *, tq, nb):
    hd, nk = SWA_HEAD_DIM, 2 * WINDOW
    rows = SWA_HEADS * tq
    grows = SWA_GROUP * tq
    seqs = range(nb)
    groups = range(SWA_KV_HEADS)

    lane = lax.broadcasted_iota(jnp.int32, (rows, nk), 1)
    tok = lax.broadcasted_iota(jnp.int32, (rows, nk), 0) % tq
    in_buffer = lane < WINDOW
    dist = jnp.where(in_buffer, WINDOW + tok - lane, (nk - tq) + tok - lane)
    allowed = (dist >= 0) & (dist <= WINDOW) & (in_buffer | (lane >= nk - tq))
    penalty = slope_ref[...] * dist.astype(F32)
    new_lanes = lax.broadcasted_iota(jnp.int32, (hd, WINDOW), 1) >= WINDOW - tq

    def new_rows_t(ref, b):
        x = jnp.concatenate([jnp.zeros((WINDOW - tq, SWA_KV_WIDTH), F32),
                             ref[b * tq:(b + 1) * tq, :]], axis=0)
        xt = [x[:, c * WINDOW:(c + 1) * WINDOW].T for c in range(SWA_KV_WIDTH // WINDOW)]
        per_block = WINDOW // hd
        return [xt[g // per_block][(g % per_block) * hd:(g % per_block + 1) * hd]
                for g in groups]

    k_all, v_all = {}, {}
    for b in seqs:
        kn_t, vn_t = new_rows_t(kn_ref, b), new_rows_t(vn_ref, b)
        for g in groups:
            kc, vc = kc_ref[b, g], vc_ref[b, g]
            k_all[b, g] = jnp.concatenate([kc, kn_t[g]], axis=1).astype(BF16)
            v_all[b, g] = jnp.concatenate([vc, vn_t[g]], axis=1).astype(BF16)
            kwin_ref[b, g] = jnp.where(new_lanes, kn_t[g], pltpu.roll(kc, WINDOW - tq, axis=1))
            vwin_ref[b, g] = jnp.where(new_lanes, vn_t[g], pltpu.roll(vc, WINDOW - tq, axis=1))

    s = []
    for b in seqs:
        q = q_ref[b * tq:(b + 1) * tq, :]
        parts = []
        for g in groups:
            qs = jnp.concatenate([q[:, h * hd:(h + 1) * hd]
                                  for h in range(g * SWA_GROUP, (g + 1) * SWA_GROUP)], axis=0)
            parts.append(jnp.dot(qs.astype(BF16), k_all[b, g], preferred_element_type=F32))
        s.append(jnp.concatenate(parts, axis=0))

    sink = sink_ref[...]
    s = [jnp.where(allowed, sb * (hd ** -0.5) - penalty, -jnp.inf) for sb in s]
    m = [jnp.maximum(jnp.max(sb, axis=-1, keepdims=True), sink) for sb in s]
    p = [jnp.exp(s[b] - m[b]) for b in seqs]
    inv = [1.0 / (jnp.sum(p[b], axis=-1, keepdims=True) + jnp.exp(sink - m[b])) for b in seqs]

    for b in seqs:
        pb = p[b].astype(BF16)
        o = jnp.concatenate(
            [lax.dot_general(pb[g * grows:(g + 1) * grows], v_all[b, g], _NT_DIMS,
                             preferred_element_type=F32) for g in groups], axis=0) * inv[b]
        o = jnp.concatenate([o[h * tq:(h + 1) * tq] for h in range(SWA_HEADS)], axis=1)
        z = z_ref[b * tq:(b + 1) * tq, :]
        out_ref[b * tq:(b + 1) * tq, :] = (o * _silu(z)).astype(out_ref.dtype)


def _attn_sample(qz, kv, cache_k, cache_v, sinks, *, batch, seq):
    nb = SAMPLE_SEQS_PER_STEP
    assert batch % nb == 0
    rows = nb * seq
    sink_col = jnp.repeat(sinks, seq).reshape(SWA_HEADS * seq, 1)
    slope_col = jnp.repeat(jnp.asarray([_alibi_slope(h) for h in range(SWA_HEADS)], F32),
                           seq).reshape(SWA_HEADS * seq, 1)
    col_spec = pl.BlockSpec((SWA_HEADS * seq, 1), lambda i: (0, 0))
    win_spec = pl.BlockSpec((nb, SWA_KV_HEADS, SWA_HEAD_DIM, WINDOW), lambda i: (i, 0, 0, 0))
    win_shape = jax.ShapeDtypeStruct((batch, SWA_KV_HEADS, SWA_HEAD_DIM, WINDOW), F32)
    to_lanes = lambda c: jnp.transpose(c, (0, 2, 3, 1))
    att, k_win, v_win = pl.pallas_call(
        functools.partial(_attn_sample_kernel, tq=seq, nb=nb),
        grid=(batch // nb,),
        in_specs=[
            col_spec,
            col_spec,
            pl.BlockSpec((rows, SWA_WIDTH), lambda i: (i, 0)),
            pl.BlockSpec((rows, SWA_WIDTH), lambda i: (i, 1)),
            pl.BlockSpec((rows, SWA_KV_WIDTH), lambda i: (i, 0)),
            pl.BlockSpec((rows, SWA_KV_WIDTH), lambda i: (i, 1)),
            win_spec,
            win_spec,
        ],
        out_specs=[pl.BlockSpec((rows, SWA_WIDTH), lambda i: (i, 0)), win_spec, win_spec],
        out_shape=[jax.ShapeDtypeStruct((batch * seq, SWA_WIDTH), F32), win_shape, win_shape],
        compiler_params=_params(("parallel",)),
        name="attn_sample",
    )(sink_col, slope_col, qz, qz, kv, kv, to_lanes(cache_k), to_lanes(cache_v))
    return att, jnp.transpose(k_win, (0, 3, 1, 2)), jnp.transpose(v_win, (0, 3, 1, 2))


def kernel(x_prompt, x_sample, state_gla, cache_k_win, cache_v_win, g_norm_a, w_in_a,
           w_gate_up, b_gate, g_onorm_a, w_out_a, g_norm_kv, w_kv, g_norm_b, w_in_b, sinks,
           w_out_b, g_final):
    assert w_in_a.shape[0] == 1 and w_in_b.shape[0] == 1, "one GLA layer, one SWA layer"
    assert cache_k_win.shape[1] == WINDOW
    pb, ps, _ = x_prompt.shape
    sb, ss, _ = x_sample.shape
    pm, sm = pb * ps, sb * ss
    g_norm_a, w_gate_up, b_gate, g_onorm_a = g_norm_a[0], w_gate_up[0], b_gate[0], g_onorm_a[0]
    g_norm_b, sinks = g_norm_b[0], sinks[0]
    w_kv = w_kv.astype(BF16)
    w_in_a_t = jnp.swapaxes(w_in_a, 1, 2)
    w_glow = w_in_a_t[0, GLA_MAIN_COLS:, :]
    h_p = x_prompt.reshape(pm, D_MODEL)
    h_s = x_sample.reshape(sm, D_MODEL)

    proj_s, glow_s, w_in_a_bf16 = _norm_matmul(
        h_s, g_norm_a, w_in_a_t, tm=sm, tn=SAMPLE_COL_TILE, n=GLA_MAIN_COLS, out_dtype=F32,
        w_extra=w_glow, emit_weights=True, w_transposed=True)
    bcum_s = _gla_gates(glow_s, w_gate_up, b_gate, chunk=ss, tm=sm)
    proj_p, glow_p = _norm_matmul(
        h_p, g_norm_a, w_in_a_bf16, tm=PROMPT_ROW_TILE, tn=PROMPT_COL_TILE, n=GLA_MAIN_COLS,
        out_dtype=BF16, w_extra=w_glow, w_transposed=True)
    bcum_p, o_s, gla_s = _gla_gates(
        glow_p, w_gate_up, b_gate, chunk=GLA_PROMPT_CHUNK, tm=GATE_ROW_TILE,
        rider=dict(proj=proj_s, bcum=bcum_s, gon=g_onorm_a, batch=sb, seq=ss,
                   s0=state_gla.reshape(state_gla.shape[1:])))
    h_s, w_out_a_bf16 = _matmul_residual(o_s, w_out_a, h_s, tm=sm, tk=SAMPLE_K_TILE,
                                         emit_weights=True)
    o_p, gla_p = _gla(proj_p, bcum_p, g_onorm_a, batch=pb, seq=ps, chunk=GLA_PROMPT_CHUNK,
                      out_dtype=BF16)
    h_p = _matmul_residual(o_p, w_out_a_bf16, h_p, tm=RESIDUAL_ROW_TILE)

    qz_s, kv_s, w_in_b_bf16 = _norm_matmul(
        h_s, g_norm_b, w_in_b, tm=sm, tn=QZ_COL_TILE, out_dtype=F32,
        w_extra=w_kv, g_extra=g_norm_kv, emit_weights=True)
    att_s, k_s, v_s = _attn_sample(qz_s, kv_s, cache_k_win, cache_v_win, sinks,
                                   batch=sb, seq=ss)
    y_s, w_out_b_bf16 = _matmul_residual(att_s, w_out_b, h_s, tm=sm, tk=SAMPLE_K_TILE,
                                         g_final=g_final, emit_weights=True)
    qz_p, kv_p = _norm_matmul(
        h_p, g_norm_b, w_in_b_bf16, tm=PROMPT_ROW_TILE, tn=QZ_COL_TILE, out_dtype=BF16,
        w_extra=w_kv, g_extra=g_norm_kv)
    att_p, k_p, v_p = _attn_prompt(qz_p, kv_p, sinks, batch=pb, seq=ps)
    y_p = _matmul_residual(att_p, w_out_b_bf16, h_p, tm=RESIDUAL_ROW_TILE, g_final=g_final)

    return (y_p.reshape(pb, ps, D_MODEL), y_s.reshape(sb, ss, D_MODEL), gla_p[None],
            gla_s[None], k_p, v_p, k_s, v_s)
```

```python
import functools

import jax
import jax.numpy as jnp
from jax import lax
from jax.experimental import pallas as pl
from jax.experimental.pallas import tpu as pltpu

F32 = jnp.float32
BF16 = jnp.bfloat16

D_MODEL = 2048
GLA_HEADS = 4
GLA_KEY_DIM = D_MODEL // 2
GLA_VALUE_DIM = D_MODEL
GLA_DK = GLA_KEY_DIM // GLA_HEADS
GLA_DV = GLA_VALUE_DIM // GLA_HEADS
GLA_GATE_RANK = 16
GLA_GATE_TEMP = 16.0
GLA_MAIN_COLS = 2 * GLA_KEY_DIM + 2 * GLA_VALUE_DIM
SWA_HEAD_DIM = 64
SWA_HEADS = D_MODEL // SWA_HEAD_DIM
SWA_KV_HEADS = 4
SWA_GROUP = SWA_HEADS // SWA_KV_HEADS
SWA_WIDTH = SWA_HEADS * SWA_HEAD_DIM
SWA_KV_WIDTH = SWA_KV_HEADS * SWA_HEAD_DIM
WINDOW = 128
RMS_EPS = 1e-6
LOG2E = 1.4426950408889634

V7X_VMEM_LIMIT_BYTES = 61 * 1024 * 1024
GLA_PROMPT_CHUNK = 128
GLA_SEQS_PER_STEP = 4
ATTN_BLOCKS_PER_STEP = 4
ATTN_BLOCKS_IN_FLIGHT = 2
SAMPLE_SEQS_PER_STEP = 8
NORM_ROW_CHUNK = 256
GATE_CUMSUM_ROWS = 128
GATE_ROW_TILE = 1024
PROMPT_ROW_TILE = 1024
PROMPT_COL_TILE = 2048
SAMPLE_COL_TILE = 1024
QZ_COL_TILE = 2048
SAMPLE_K_TILE = 1024
RESIDUAL_ROW_TILE = 512

_NT_DIMS = (((1,), (1,)), ((), ()))
_TN_DIMS = (((0,), (0,)), ((), ()))


def _params(semantics):
    return pltpu.CompilerParams(dimension_semantics=semantics,
                                vmem_limit_bytes=V7X_VMEM_LIMIT_BYTES)


def _silu(x):
    return x / (1.0 + jnp.exp(-x))


def _rms_scale(x):
    return lax.rsqrt(jnp.mean(x * x, axis=-1, keepdims=True) + RMS_EPS)


def _norm_matmul_kernel(x_ref, g_ref, w_ref, *rest, has_extra, extra_gain, emit_weights,
                        emit_extra, w_transposed):
    dims = _NT_DIMS if w_transposed else (((1,), (0,)), ((), ()))

    def mm(a, b):
        return lax.dot_general(a, b, dims, preferred_element_type=F32)

    rest = list(rest)
    g2_ref = rest.pop(0) if extra_gain else None
    w2_ref = rest.pop(0) if has_extra else None
    o_ref = rest.pop(0)
    o2_ref = rest.pop(0) if has_extra else None
    wb_ref = rest.pop(0) if emit_weights else None
    w2b_ref = rest.pop(0) if emit_extra else None
    (xn_ref,) = rest

    first = pl.program_id(1) == 0

    def weights():
        w = w_ref[...].astype(BF16)
        if emit_weights:
            wb_ref[...] = w
        return w

    @pl.when(first)
    def _():
        tm = x_ref.shape[0]
        rc = min(tm, NORM_ROW_CHUNK)
        w = weights()
        if has_extra:
            w2 = w2_ref[...].astype(BF16)
            if emit_extra:
                w2b_ref[...] = w2
        for c in range(tm // rc):
            rows = slice(c * rc, (c + 1) * rc)
            x = x_ref[rows, :]
            xs = x * _rms_scale(x)
            xn = (xs * g_ref[...]).astype(BF16)
            xn_ref[rows, :] = xn
            o_ref[rows, :] = mm(xn, w).astype(o_ref.dtype)
            if has_extra:
                xn2 = (xs * g2_ref[...]).astype(BF16) if extra_gain else xn
                o2_ref[rows, :] = mm(xn2, w2).astype(o2_ref.dtype)

    @pl.when(jnp.logical_not(first))
    def _():
        o_ref[...] = mm(xn_ref[...], weights()).astype(o_ref.dtype)


def _norm_matmul(x, g, w, *, tm, tn, out_dtype, n=None, w_extra=None, g_extra=None,
                 emit_weights=False, w_transposed=False):
    m, k = x.shape
    n_axis = -2 if w_transposed else -1
    n = w.shape[n_axis] if n is None else n
    assert n % tn == 0 and m % tm == 0
    assert not emit_weights or m == tm, "each weight block must be visited exactly once"
    grid = (m // tm, n // tn)
    w_block = (tn, k) if w_transposed else (k, tn)
    w_index = (lambda i, j: (j, 0)) if w_transposed else (lambda i, j: (0, j))
    if w.ndim == 3:
        w_spec = pl.BlockSpec((None,) + w_block, lambda i, j: (0,) + w_index(i, j))
    else:
        w_spec = pl.BlockSpec(w_block, w_index)
    in_specs = [
        pl.BlockSpec((tm, k), lambda i, j: (i, 0)),
        pl.BlockSpec((1, k), lambda i, j: (0, 0)),
        w_spec,
    ]
    out_shape = [jax.ShapeDtypeStruct((m, n), out_dtype)]
    out_specs = [pl.BlockSpec((tm, tn), lambda i, j: (i, j))]
    args = [x, g.reshape(1, k), w]
    if g_extra is not None:
        in_specs.append(pl.BlockSpec((1, k), lambda i, j: (0, 0)))
        args.append(g_extra.reshape(1, k))
    if w_extra is not None:
        n2 = w_extra.shape[n_axis]
        in_specs.append(pl.BlockSpec(w_extra.shape, lambda i, j: (0, 0)))
        out_shape.append(jax.ShapeDtypeStruct((m, n2), F32))
        out_specs.append(pl.BlockSpec((tm, n2), lambda i, j: (i, 0)))
        args.append(w_extra)
    emit_extra = emit_weights and w_extra is not None and w_extra.dtype == F32
    if emit_weights:
        out_shape.append(jax.ShapeDtypeStruct((n, k) if w_transposed else (k, n), BF16))
        out_specs.append(pl.BlockSpec(w_block, w_index))
    if emit_extra:
        out_shape.append(jax.ShapeDtypeStruct(w_extra.shape, BF16))
        out_specs.append(pl.BlockSpec(w_extra.shape, lambda i, j: (0, 0)))
    res = pl.pallas_call(
        functools.partial(_norm_matmul_kernel, has_extra=w_extra is not None,
                          extra_gain=g_extra is not None, emit_weights=emit_weights,
                          emit_extra=emit_extra, w_transposed=w_transposed),
        grid=grid,
        in_specs=in_specs,
        out_specs=out_specs,
        out_shape=out_shape,
        scratch_shapes=[pltpu.VMEM((tm, k), BF16)],
        compiler_params=_params(("parallel", "arbitrary")),
        name=f"norm_matmul_{m}x{n}",
    )(*args)
    return res if len(res) > 1 else res[0]


def _matmul_residual_kernel(a_ref, w_ref, r_ref, *rest, final_norm, emit_weights, nk):
    rest = list(rest)
    g_ref = rest.pop(0) if final_norm else None
    o_ref = rest.pop(0)
    w = w_ref[...].astype(BF16)
    if emit_weights:
        rest[0][...] = w
    part = jnp.dot(a_ref[...].astype(BF16), w, preferred_element_type=F32)

    if nk == 1:
        h = r_ref[...] + part
        if final_norm:
            h = h * _rms_scale(h) * g_ref[...]
        o_ref[...] = h
        return

    kk = pl.program_id(1)

    @pl.when(kk == 0)
    def _():
        o_ref[...] = r_ref[...] + part

    @pl.when(kk > 0)
    def _():
        o_ref[...] += part

    if final_norm:
        @pl.when(kk == nk - 1)
        def _():
            h = o_ref[...]
            o_ref[...] = h * _rms_scale(h) * g_ref[...]


def _matmul_residual(a, w, res, *, tm, tk=None, g_final=None, emit_weights=False):
    m, k = a.shape
    n = w.shape[-1]
    tk = k if tk is None else tk
    assert m % tm == 0 and k % tk == 0
    assert not emit_weights or m == tm, "each weight block must be visited exactly once"
    if w.ndim == 3:
        w_spec = pl.BlockSpec((None, tk, n), lambda i, j: (0, j, 0))
    else:
        w_spec = pl.BlockSpec((tk, n), lambda i, j: (j, 0))
    in_specs = [
        pl.BlockSpec((tm, tk), lambda i, j: (i, j)),
        w_spec,
        pl.BlockSpec((tm, n), lambda i, j: (i, 0)),
    ]
    args = [a, w, res]
    if g_final is not None:
        in_specs.append(pl.BlockSpec((1, n), lambda i, j: (0, 0)))
        args.append(g_final.reshape(1, n))
    out_shape = [jax.ShapeDtypeStruct((m, n), F32)]
    out_specs = [pl.BlockSpec((tm, n), lambda i, j: (i, 0))]
    if emit_weights:
        out_shape.append(jax.ShapeDtypeStruct((k, n), BF16))
        out_specs.append(pl.BlockSpec((tk, n), lambda i, j: (j, 0)))
    res = pl.pallas_call(
        functools.partial(_matmul_residual_kernel, final_norm=g_final is not None,
                          emit_weights=emit_weights, nk=k // tk),
        grid=(m // tm, k // tk),
        in_specs=in_specs,
        out_specs=out_specs,
        out_shape=out_shape,
        compiler_params=_params(("parallel", "arbitrary")),
        name=f"matmul_residual_{m}" + ("_final" if g_final is not None else ""),
    )(*args)
    return res if emit_weights else res[0]


def _split_bf16(x):
    hi = x.astype(BF16)
    return hi, (x - hi.astype(F32)).astype(BF16)


def _gla_gates_kernel(glow_ref, wg_ref, bg_ref, bcum_ref, *, chunk):
    rows = glow_ref.shape[0]
    g_hi, g_lo = _split_bf16(glow_ref[...])
    w_hi, w_lo = _split_bf16(wg_ref[...])
    x = jnp.dot(jnp.concatenate([g_hi, g_lo, g_hi], axis=1),
                jnp.concatenate([w_hi, w_hi, w_lo], axis=0),
                preferred_element_type=F32) + bg_ref[...]
    softplus2 = jnp.log2(1.0 + jnp.exp2(jnp.abs(x) * (-LOG2E)))
    logg = jnp.minimum(x, 0.0) * (LOG2E / GLA_GATE_TEMP) - softplus2 * (1.0 / GLA_GATE_TEMP)

    span = GATE_CUMSUM_ROWS
    row = lax.broadcasted_iota(jnp.int32, (span, span), 0)
    col = lax.broadcasted_iota(jnp.int32, (span, span), 1)
    same_chunk = (row // chunk) == (col // chunk) if chunk < span else True
    tril = jnp.where((col <= row) & same_chunk, 1.0, 0.0).astype(BF16)
    tril2 = jnp.concatenate([tril, tril], axis=1)
    for i in range(rows // span):
        hi, lo = _split_bf16(logg[i * span:(i + 1) * span])
        bcum_ref[i * span:(i + 1) * span, :] = jnp.dot(
            tril2, jnp.concatenate([hi, lo], axis=0), preferred_element_type=F32)


def _gates_with_gla_kernel(glow_ref, wg_ref, bg_ref, *rest, chunk, n_gla_in):
    gla_in, (bcum_ref, o_ref, sfin_ref) = rest[:n_gla_in], rest[n_gla_in:]
    _gla_kernel(*gla_in, o_ref, sfin_ref, has_init=True, n_chunks=1)
    _gla_gates_kernel(glow_ref, wg_ref, bg_ref, bcum_ref, chunk=chunk)


def _gla_gates(glow, wg, bg, *, chunk, tm, rider=None):
    m = glow.shape[0]
    assert m % tm == 0 and tm % GATE_CUMSUM_ROWS == 0
    assert GATE_CUMSUM_ROWS % chunk == 0 or chunk % GATE_CUMSUM_ROWS == 0
    assert chunk <= GATE_CUMSUM_ROWS, "cumulative sums do not cross row spans"
    steps = m // tm
    in_specs = [
        pl.BlockSpec((tm, GLA_GATE_RANK), lambda i: (i, 0)),
        pl.BlockSpec((GLA_GATE_RANK, GLA_KEY_DIM), lambda i: (0, 0)),
        pl.BlockSpec((1, GLA_KEY_DIM), lambda i: (0, 0)),
    ]
    args = [glow, wg, bg.reshape(1, -1)]
    out_specs = [pl.BlockSpec((tm, GLA_KEY_DIM), lambda i: (i, 0))]
    out_shape = [jax.ShapeDtypeStruct((m, GLA_KEY_DIM), F32)]
    if rider is None:
        body = functools.partial(_gla_gates_kernel, chunk=chunk)
    else:
        batch, seq = rider["batch"], rider["seq"]
        assert batch % steps == 0
        g_in, g_args, g_out, g_shape = _gla_operands(
            rider["proj"], rider["bcum"], rider["gon"], rider["s0"], batch=batch, seq=seq,
            chunk=seq, ns=batch // steps, out_dtype=F32, index=lambda i: (i, 0))
        body = functools.partial(_gates_with_gla_kernel, chunk=chunk, n_gla_in=len(g_in))
        in_specs += g_in
        args += g_args
        out_specs += g_out
        out_shape += g_shape
    res = pl.pallas_call(
        body,
        grid=(steps,),
        in_specs=in_specs,
        out_specs=out_specs,
        out_shape=out_shape,
        compiler_params=_params(("parallel",)),
        name=f"gla_gates_{m}",
    )(*args)
    if rider is None:
        return res[0]
    bcum, o, s_fin = res
    return bcum, o.reshape(rider["batch"] * rider["seq"], GLA_VALUE_DIM), s_fin


def _gla_kernel(q_ref, k_ref, v_ref, r_ref, bcum_ref, gon_ref, *rest, has_init, n_chunks):
    rest = list(rest)
    s0_ref = rest.pop(0) if has_init else None
    o_ref, sfin_ref = rest[:2]
    s_ref = rest[2] if n_chunks > 1 else None
    c = pl.program_id(1) if n_chunks > 1 else None
    nseq, chunk = q_ref.shape[:2]
    chains = [(s, h) for s in range(nseq) for h in range(GLA_HEADS)]

    if n_chunks > 1:
        @pl.when(c == 0)
        def _():
            if has_init:
                s_ref[...] = s0_ref[...]
            else:
                s_ref[...] = jnp.zeros_like(s_ref)

    row = lax.broadcasted_iota(jnp.int32, (chunk, chunk), 0)
    col = lax.broadcasted_iota(jnp.int32, (chunk, chunk), 1)
    causal = col <= row
    heads = range(GLA_HEADS)
    ks = [slice(h * GLA_DK, (h + 1) * GLA_DK) for h in heads]
    vs = [slice(h * GLA_DV, (h + 1) * GLA_DV) for h in heads]

    def state(s, h):
        if n_chunks > 1:
            return s_ref[s, h]
        return s0_ref[s, h] if has_init else jnp.zeros((GLA_DK, GLA_DV), F32)

    q_inter, k_state, scores, decay = {}, {}, {}, {}
    for s, h in chains:
        b = bcum_ref[s, :, ks[h]]
        b_last = b[chunk - 1:chunk, :]
        b_mid = b[chunk // 2 - 1:chunk // 2, :]
        q = q_ref[s, :, ks[h]].astype(F32)
        k = k_ref[s, :, ks[h]].astype(F32)
        q_inter[s, h] = (q * jnp.exp2(b)).astype(BF16)
        q_intra = (q * jnp.exp2(b - b_mid)).astype(BF16)
        k_intra = (k * jnp.exp2(b_mid - b)).astype(BF16)
        k_state[s, h] = (k * jnp.exp2(b_last - b)).astype(BF16)
        scores[s, h] = lax.dot_general(q_intra, k_intra, _NT_DIMS,
                                       preferred_element_type=F32)
        decay[s, h] = jnp.exp2(jnp.broadcast_to(b_last, (128, GLA_DK))).T

    o = {}
    for s, h in chains:
        sc = jnp.where(causal, scores[s, h], 0.0).astype(BF16)
        v = v_ref[s, :, vs[h]].astype(BF16)
        o[s, h] = (jnp.dot(q_inter[s, h], state(s, h).astype(BF16),
                           preferred_element_type=F32)
                   + jnp.dot(sc, v, preferred_element_type=F32))
        upd = lax.dot_general(k_state[s, h], v, _TN_DIMS, preferred_element_type=F32)
        s_new = (state(s, h) * jnp.concatenate([decay[s, h]] * (GLA_DV // 128), axis=1)
                 + upd)
        if n_chunks > 1:
            s_ref[s, h] = s_new
        else:
            sfin_ref[s, h] = s_new

    qs = GLA_DK ** -0.5
    scale = {sh: qs * lax.rsqrt(jnp.mean(o[sh] * o[sh], axis=-1, keepdims=True) * (qs * qs)
                                + RMS_EPS) for sh in chains}
    for s, h in chains:
        r = r_ref[s, :, vs[h]].astype(F32)
        o_ref[s, :, vs[h]] = (o[s, h] * scale[s, h] * gon_ref[:, vs[h]]
                              * _silu(r)).astype(o_ref.dtype)

    if n_chunks > 1:
        @pl.when(c == n_chunks - 1)
        def _():
            sfin_ref[...] = s_ref[...]


def _gla_operands(proj, bcum, gon, s0, *, batch, seq, chunk, ns, out_dtype, index):
    assert batch % ns == 0 and seq % chunk == 0
    proj = proj.reshape(batch, seq, proj.shape[-1])
    bcum = bcum.reshape(batch, seq, bcum.shape[-1])
    kb, vb, rb = 1, 2 * GLA_KEY_DIM // GLA_VALUE_DIM, 2 * GLA_KEY_DIM // GLA_VALUE_DIM + 1

    def rows(width, col):
        return pl.BlockSpec((ns, chunk, width), lambda *g: (*index(*g), col))

    state_spec = pl.BlockSpec((ns, GLA_HEADS, GLA_DK, GLA_DV),
                              lambda *g: (index(*g)[0], 0, 0, 0))
    in_specs = [rows(GLA_KEY_DIM, 0), rows(GLA_KEY_DIM, kb), rows(GLA_VALUE_DIM, vb),
                rows(GLA_VALUE_DIM, rb), rows(GLA_KEY_DIM, 0),
                pl.BlockSpec((1, GLA_VALUE_DIM), lambda *g: (0, 0))]
    args = [proj, proj, proj, proj, bcum, gon.reshape(1, -1)]
    if s0 is not None:
        in_specs.append(state_spec)
        args.append(s0)
    out_specs = [rows(GLA_VALUE_DIM, 0), state_spec]
    out_shape = [jax.ShapeDtypeStruct((batch, seq, GLA_VALUE_DIM), out_dtype),
                 jax.ShapeDtypeStruct((batch, GLA_HEADS, GLA_DK, GLA_DV), F32)]
    return in_specs, args, out_specs, out_shape


def _gla(proj, bcum, gon, *, batch, seq, chunk, out_dtype, s0=None):
    n = seq // chunk
    ns = GLA_SEQS_PER_STEP
    in_specs, args, out_specs, out_shape = _gla_operands(
        proj, bcum, gon, s0, batch=batch, seq=seq, chunk=chunk, ns=ns, out_dtype=out_dtype,
        index=lambda b, c: (b, c))
    scratch = [pltpu.VMEM((ns, GLA_HEADS, GLA_DK, GLA_DV), F32)] if n > 1 else []
    o, s_fin = pl.pallas_call(
        functools.partial(_gla_kernel, has_init=s0 is not None, n_chunks=n),
        grid=(batch // ns, n),
        in_specs=in_specs,
        out_specs=out_specs,
        out_shape=out_shape,
        scratch_shapes=scratch,
        compiler_params=_params(("parallel", "arbitrary")),
        name=f"gla_chunk{chunk}",
    )(*args)
    return o.reshape(batch * seq, GLA_VALUE_DIM), s_fin


def _alibi_slope(head):
    return 2.0 ** (-8.0 * (head + 1) / SWA_HEADS)


def _attn_prompt_kernel(sinks_ref, q_ref, z_ref, kp_ref, ko_ref, vp_ref, vo_ref, out_ref,
                        kwin_ref, vwin_ref, bias_ref):
    hd, nkeys = SWA_HEAD_DIM, 2 * WINDOW
    pair_w = 2 * hd
    pairs_per_group = SWA_GROUP // 2
    blk = pl.program_id(1)

    @pl.when((pl.program_id(0) == 0) & (blk == 0))
    def _():
        kj = lax.broadcasted_iota(jnp.int32, (nkeys, WINDOW), 0)
        qi = lax.broadcasted_iota(jnp.int32, (nkeys, WINDOW), 1)
        dist = WINDOW + qi - kj
        ok = (dist >= 0) & (dist <= WINDOW)
        ok_first = ok & (kj >= WINDOW)
        distf = dist.astype(F32)
        for h in range(SWA_HEADS):
            pen = (-_alibi_slope(h) * LOG2E) * distf
            sl = slice((h % 2) * WINDOW, (h % 2 + 1) * WINDOW)
            bias_ref[0, h // 2, :, sl] = jnp.where(ok_first, pen, -jnp.inf)
            bias_ref[1, h // 2, :, sl] = jnp.where(ok, pen, -jnp.inf)

    n_sub = q_ref.shape[0] // WINDOW
    k_rows = jnp.concatenate([kp_ref[...], ko_ref[...]], axis=0)
    v_rows = jnp.concatenate([vp_ref[...], vo_ref[...]], axis=0)
    ones = jnp.ones((16, nkeys), F32)
    lane = lax.broadcasted_iota(jnp.int32, (nkeys, pair_w), 1)
    qlane = lax.broadcasted_iota(jnp.int32, (WINDOW, pair_w), 1)
    qk_scale = (hd ** -0.5) * LOG2E
    quad_pairs = 2
    n_quads = SWA_HEADS // (2 * quad_pairs)
    quads_per_group = pairs_per_group // quad_pairs
    quarter = lax.broadcasted_iota(jnp.int32, (1, 2 * quad_pairs * WINDOW), 1) // WINDOW

    def prepare(sub):
        k = k_rows[sub * WINDOW:sub * WINDOW + nkeys]
        vt = v_rows[sub * WINDOW:sub * WINDOW + nkeys].T
        k2, vt1 = [], []
        for g in range(SWA_KV_HEADS):
            kblk = k[:, (g // 2) * pair_w:(g // 2 + 1) * pair_w]
            k_here = jnp.where((lane < hd) if g % 2 == 0 else (lane >= hd), kblk, 0.0)
            k2.append((k_here + pltpu.roll(k_here, hd, axis=1)).astype(BF16))
            vt1.append(jnp.concatenate([vt[g * hd:(g + 1) * hd], ones], axis=0).astype(BF16))
        return dict(qrows=slice(sub * WINDOW, (sub + 1) * WINDOW), k2=k2, vt1=vt1,
                    tbl=jnp.minimum(blk, 1) if sub == 0 else 1)

    def scores(blkst, quad):
        parts = []
        for j in range(quad_pairs):
            col = (quad * quad_pairs + j) * pair_w
            q_pair = q_ref[blkst["qrows"], col:col + pair_w]
            zero = jnp.zeros_like(q_pair)
            parts += [jnp.where(qlane < hd, q_pair, zero), jnp.where(qlane >= hd, q_pair, zero)]
        return lax.dot_general(blkst["k2"][quad // quads_per_group],
                               jnp.concatenate(parts, axis=0),
                               _NT_DIMS, preferred_element_type=F32)

    def finish(blkst, quad, st):
        pair0 = quad * quad_pairs
        bias = jnp.concatenate([bias_ref[blkst["tbl"], pair0 + j] for j in range(quad_pairs)],
                               axis=1)
        s2 = st * qk_scale + bias
        sink2 = sinks_ref[2 * pair0]
        for t in range(1, 2 * quad_pairs):
            sink2 = jnp.where(quarter == t, sinks_ref[2 * pair0 + t], sink2)
        sink2 = sink2 * LOG2E
        m = jnp.maximum(jnp.max(s2, axis=0, keepdims=True), sink2)
        p = jnp.exp2(s2 - m).astype(BF16)
        oa = jnp.dot(blkst["vt1"][quad // quads_per_group], p, preferred_element_type=F32)
        denom = oa[hd:hd + 1] + jnp.exp2(sink2 - m)
        on = oa[0:hd] * (1.0 / denom)
        for j in range(quad_pairs):
            lo = 2 * j * WINDOW
            o_pair = jnp.concatenate([on[:, lo:lo + WINDOW],
                                      on[:, lo + WINDOW:lo + 2 * WINDOW]], axis=0).T
            col = (pair0 + j) * pair_w
            z_pair = z_ref[blkst["qrows"], col:col + pair_w].astype(F32)
            out_ref[blkst["qrows"], col:col + pair_w] = (
                o_pair * _silu(z_pair)).astype(out_ref.dtype)

    for first in range(0, n_sub, ATTN_BLOCKS_IN_FLIGHT):
        blocks = [prepare(sub) for sub in range(first, min(first + ATTN_BLOCKS_IN_FLIGHT, n_sub))]
        st_next = [scores(b, 0) for b in blocks]
        for quad in range(n_quads):
            for i, b in enumerate(blocks):
                st = st_next[i]
                if quad + 1 < n_quads:
                    st_next[i] = scores(b, quad + 1)
                finish(b, quad, st)

    @pl.when(blk == pl.num_programs(1) - 1)
    def _():
        last = slice((n_sub - 1) * WINDOW, n_sub * WINDOW)
        kwin_ref[...] = ko_ref[last, :].T
        vwin_ref[...] = vo_ref[last, :].T


def _attn_prompt(qz, kv, sinks, *, batch, seq):
    sub = ATTN_BLOCKS_PER_STEP
    tq = sub * WINDOW
    nb = seq // tq
    assert seq % tq == 0
    row = lambda b, i: b * nb + i
    prev = lambda b, i: (b * nb + i) * sub - jnp.minimum(i, 1)
    win_spec = pl.BlockSpec((None, SWA_KV_WIDTH, WINDOW), lambda b, i: (b, 0, 0))
    win_shape = jax.ShapeDtypeStruct((batch, SWA_KV_WIDTH, WINDOW), F32)
    att, k_win, v_win = pl.pallas_call(
        _attn_prompt_kernel,
        grid=(batch, nb),
        in_specs=[
            pl.BlockSpec(memory_space=pltpu.SMEM),
            pl.BlockSpec((tq, SWA_WIDTH), lambda b, i: (row(b, i), 0)),
            pl.BlockSpec((tq, SWA_WIDTH), lambda b, i: (row(b, i), 1)),
            pl.BlockSpec((WINDOW, SWA_KV_WIDTH), lambda b, i: (prev(b, i), 0)),
            pl.BlockSpec((tq, SWA_KV_WIDTH), lambda b, i: (row(b, i), 0)),
            pl.BlockSpec((WINDOW, SWA_KV_WIDTH), lambda b, i: (prev(b, i), 1)),
            pl.BlockSpec((tq, SWA_KV_WIDTH), lambda b, i: (row(b, i), 1)),
        ],
        out_specs=[pl.BlockSpec((tq, SWA_WIDTH), lambda b, i: (row(b, i), 0)),
                   win_spec, win_spec],
        out_shape=[jax.ShapeDtypeStruct((batch * seq, SWA_WIDTH), BF16), win_shape, win_shape],
        scratch_shapes=[pltpu.VMEM((2, SWA_HEADS // 2, 2 * WINDOW, 2 * WINDOW), F32)],
        compiler_params=_params(("arbitrary", "arbitrary")),
        name="attn_prompt",
    )(sinks, qz, qz, kv, kv, kv, kv)
    to_rows = lambda w: jnp.transpose(
        w.reshape(batch, SWA_KV_HEADS, SWA_HEAD_DIM, WINDOW), (0, 3, 1, 2))
    return att, to_rows(k_win), to_rows(v_win)


def _attn_sample_kernel(sink_ref, slope_ref, q_ref, z_ref, kn_ref, vn_ref, kc_ref, vc_ref,
                        out_ref, kwin_ref, vwin_ref, *, tq, nb):
    hd, nk = SWA_HEAD_DIM, 2 * WINDOW
    rows = SWA_HEADS * tq
    grows = SWA_GROUP * tq
    seqs = range(nb)
    groups = range(SWA_KV_HEADS)

    lane = lax.broadcasted_iota(jnp.int32, (rows, nk), 1)
    tok = lax.broadcasted_iota(jnp.int32, (rows, nk), 0) % tq
    in_buffer = lane < WINDOW
    dist = jnp.where(in_buffer, WINDOW + tok - lane, (nk - tq) + tok - lane)
    allowed = (dist >= 0) & (dist <= WINDOW) & (in_buffer | (lane >= nk - tq))
    penalty = slope_ref[...] * dist.astype(F32)
    new_lanes = lax.broadcasted_iota(jnp.int32, (hd, WINDOW), 1) >= WINDOW - tq

    def new_rows_t(ref, b):
        x = jnp.concatenate([jnp.zeros((WINDOW - tq, SWA_KV_WIDTH), F32),
                             ref[b * tq:(b + 1) * tq, :]], axis=0)
        xt = [x[:, c * WINDOW:(c + 1) * WINDOW].T for c in range(SWA_KV_WIDTH // WINDOW)]
        per_block = WINDOW // hd
        return [xt[g // per_block][(g % per_block) * hd:(g % per_block + 1) * hd]
                for g in groups]

    k_all, v_all = {}, {}
    for b in seqs:
        kn_t, vn_t = new_rows_t(kn_ref, b), new_rows_t(vn_ref, b)
        for g in groups:
            kc, vc = kc_ref[b, g], vc_ref[b, g]
            k_all[b, g] = jnp.concatenate([kc, kn_t[g]], axis=1).astype(BF16)
            v_all[b, g] = jnp.concatenate([vc, vn_t[g]], axis=1).astype(BF16)
            kwin_ref[b, g] = jnp.where(new_lanes, kn_t[g], pltpu.roll(kc, WINDOW - tq, axis=1))
            vwin_ref[b, g] = jnp.where(new_lanes, vn_t[g], pltpu.roll(vc, WINDOW - tq, axis=1))

    s = []
    for b in seqs:
        q = q_ref[b * tq:(b + 1) * tq, :]
        parts = []
        for g in groups:
            qs = jnp.concatenate([q[:, h * hd:(h + 1) * hd]
                                  for h in range(g * SWA_GROUP, (g + 1) * SWA_GROUP)], axis=0)
            parts.append(jnp.dot(qs.astype(BF16), k_all[b, g], preferred_element_type=F32))
        s.append(jnp.concatenate(parts, axis=0))

    sink = sink_ref[...]
    s = [jnp.where(allowed, sb * (hd ** -0.5) - penalty, -jnp.inf) for sb in s]
    m = [jnp.maximum(jnp.max(sb, axis=-1, keepdims=True), sink) for sb in s]
    p = [jnp.exp(s[b] - m[b]) for b in seqs]
    inv = [1.0 / (jnp.sum(p[b], axis=-1, keepdims=True) + jnp.exp(sink - m[b])) for b in seqs]

    for b in seqs:
        pb = p[b].astype(BF16)
        o = jnp.concatenate(
            [lax.dot_general(pb[g * grows:(g + 1) * grows], v_all[b, g], _NT_DIMS,
                             preferred_element_type=F32) for g in groups], axis=0) * inv[b]
        o = jnp.concatenate([o[h * tq:(h + 1) * tq] for h in range(SWA_HEADS)], axis=1)
        z = z_ref[b * tq:(b + 1) * tq, :]
        out_ref[b * tq:(b + 1) * tq, :] = (o * _silu(z)).astype(out_ref.dtype)


def _attn_sample(qz, kv, cache_k, cache_v, sinks, *, batch, seq):
    nb = SAMPLE_SEQS_PER_STEP
    assert batch % nb == 0
    rows = nb * seq
    sink_col = jnp.repeat(sinks, seq).reshape(SWA_HEADS * seq, 1)
    slope_col = jnp.repeat(jnp.asarray([_alibi_slope(h) for h in range(SWA_HEADS)], F32),
                           seq).reshape(SWA_HEADS * seq, 1)
    col_spec = pl.BlockSpec((SWA_HEADS * seq, 1), lambda i: (0, 0))
    win_spec = pl.BlockSpec((nb, SWA_KV_HEADS, SWA_HEAD_DIM, WINDOW), lambda i: (i, 0, 0, 0))
    win_shape = jax.ShapeDtypeStruct((batch, SWA_KV_HEADS, SWA_HEAD_DIM, WINDOW), F32)
    to_lanes = lambda c: jnp.transpose(c, (0, 2, 3, 1))
    att, k_win, v_win = pl.pallas_call(
        functools.partial(_attn_sample_kernel, tq=seq, nb=nb),
        grid=(batch // nb,),
        in_specs=[
            col_spec,
            col_spec,
            pl.BlockSpec((rows, SWA_WIDTH), lambda i: (i, 0)),
            pl.BlockSpec((rows, SWA_WIDTH), lambda i: (i, 1)),
            pl.BlockSpec((rows, SWA_KV_WIDTH), lambda i: (i, 0)),
            pl.BlockSpec((rows, SWA_KV_WIDTH), lambda i: (i, 1)),
            win_spec,
            win_spec,
        ],
        out_specs=[pl.BlockSpec((rows, SWA_WIDTH), lambda i: (i, 0)), win_spec, win_spec],
        out_shape=[jax.ShapeDtypeStruct((batch * seq, SWA_WIDTH), F32), win_shape, win_shape],
        compiler_params=_params(("parallel",)),
        name="attn_sample",
    )(sink_col, slope_col, qz, qz, kv, kv, to_lanes(cache_k), to_lanes(cache_v))
    return att, jnp.transpose(k_win, (0, 3, 1, 2)), jnp.transpose(v_win, (0, 3, 1, 2))


def kernel(x_prompt, x_sample, state_gla, cache_k_win, cache_v_win, g_norm_a, w_in_a,
           w_gate_up, b_gate, g_onorm_a, w_out_a, g_norm_kv, w_kv, g_norm_b, w_in_b, sinks,
           w_out_b, g_final):
    assert w_in_a.shape[0] == 1 and w_in_b.shape[0] == 1, "one GLA layer, one SWA layer"
    assert cache_k_win.shape[1] == WINDOW
    pb, ps, _ = x_prompt.shape
    sb, ss, _ = x_sample.shape
    pm, sm = pb * ps, sb * ss
    g_norm_a, w_gate_up, b_gate, g_onorm_a = g_norm_a[0], w_gate_up[0], b_gate[0], g_onorm_a[0]
    g_norm_b, sinks = g_norm_b[0], sinks[0]
    w_in_a_t = jnp.swapaxes(w_in_a, 1, 2)
    w_glow = w_in_a_t[0, GLA_MAIN_COLS:, :]
    h_p = x_prompt.reshape(pm, D_MODEL)
    h_s = x_sample.reshape(sm, D_MODEL)

    proj_s, glow_s, w_in_a_bf16, w_glow_bf16 = _norm_matmul(
        h_s, g_norm_a, w_in_a_t, tm=sm, tn=SAMPLE_COL_TILE, n=GLA_MAIN_COLS, out_dtype=F32,
        w_extra=w_glow, emit_weights=True, w_transposed=True)
    bcum_s = _gla_gates(glow_s, w_gate_up, b_gate, chunk=ss, tm=sm)
    proj_p, glow_p = _norm_matmul(
        h_p, g_norm_a, w_in_a_bf16, tm=PROMPT_ROW_TILE, tn=PROMPT_COL_TILE, n=GLA_MAIN_COLS,
        out_dtype=BF16, w_extra=w_glow_bf16, w_transposed=True)
    bcum_p, o_s, gla_s = _gla_gates(
        glow_p, w_gate_up, b_gate, chunk=GLA_PROMPT_CHUNK, tm=GATE_ROW_TILE,
        rider=dict(proj=proj_s, bcum=bcum_s, gon=g_onorm_a, batch=sb, seq=ss,
                   s0=state_gla.reshape(state_gla.shape[1:])))
    h_s, w_out_a_bf16 = _matmul_residual(o_s, w_out_a, h_s, tm=sm, tk=SAMPLE_K_TILE,
                                         emit_weights=True)
    o_p, gla_p = _gla(proj_p, bcum_p, g_onorm_a, batch=pb, seq=ps, chunk=GLA_PROMPT_CHUNK,
                      out_dtype=BF16)
    h_p = _matmul_residual(o_p, w_out_a_bf16, h_p, tm=RESIDUAL_ROW_TILE)

    qz_s, kv_s, w_in_b_bf16, w_kv_bf16 = _norm_matmul(
        h_s, g_norm_b, w_in_b, tm=sm, tn=SAMPLE_COL_TILE, out_dtype=F32,
        w_extra=w_kv, g_extra=g_norm_kv, emit_weights=True)
    att_s, k_s, v_s = _attn_sample(qz_s, kv_s, cache_k_win, cache_v_win, sinks,
                                   batch=sb, seq=ss)
    y_s, w_out_b_bf16 = _matmul_residual(att_s, w_out_b, h_s, tm=sm, tk=SAMPLE_K_TILE,
                                         g_final=g_final, emit_weights=True)
    qz_p, kv_p = _norm_matmul(
        h_p, g_norm_b, w_in_b_bf16, tm=PROMPT_ROW_TILE, tn=QZ_COL_TILE, out_dtype=BF16,
        w_extra=w_kv_bf16, g_extra=g_norm_kv)
    att_p, k_p, v_p = _attn_prompt(qz_p, kv_p, sinks, batch=pb, seq=ps)
    y_p = _matmul_residual(att_p, w_out_b_bf16, h_p, tm=RESIDUAL_ROW_TILE, g_final=g_final)

    return (y_p.reshape(pb, ps, D_MODEL), y_s.reshape(sb, ss, D_MODEL), gla_p[None],
            gla_s[None], k_p, v_p, k_s, v_s)
```

```python
import functools

import jax
import jax.numpy as jnp
from jax import lax
from jax.experimental import pallas as pl
from jax.experimental.pallas import tpu as pltpu

F32 = jnp.float32
BF16 = jnp.bfloat16

D_MODEL = 2048
GLA_HEADS = 4
GLA_KEY_DIM = D_MODEL // 2
GLA_VALUE_DIM = D_MODEL
GLA_DK = GLA_KEY_DIM // GLA_HEADS
GLA_DV = GLA_VALUE_DIM // GLA_HEADS
GLA_GATE_RANK = 16
GLA_GATE_TEMP = 16.0
GLA_MAIN_COLS = 2 * GLA_KEY_DIM + 2 * GLA_VALUE_DIM
SWA_HEAD_DIM = 64
SWA_HEADS = D_MODEL // SWA_HEAD_DIM
SWA_KV_HEADS = 4
SWA_GROUP = SWA_HEADS // SWA_KV_HEADS
SWA_WIDTH = SWA_HEADS * SWA_HEAD_DIM
SWA_KV_WIDTH = SWA_KV_HEADS * SWA_HEAD_DIM
WINDOW = 128
RMS_EPS = 1e-6
LOG2E = 1.4426950408889634

V7X_VMEM_LIMIT_BYTES = 61 * 1024 * 1024
GLA_PROMPT_CHUNK = 128
GLA_SEQS_PER_STEP = 4
ATTN_BLOCKS_PER_STEP = 4
ATTN_BLOCKS_IN_FLIGHT = 2
SAMPLE_SEQS_PER_STEP = 8
NORM_ROW_CHUNK = 256
GATE_CUMSUM_ROWS = 128
GATE_ROW_TILE = 1024
PROMPT_ROW_TILE = 1024
PROMPT_COL_TILE = 2048
SAMPLE_COL_TILE = 1024
QZ_COL_TILE = 2048
SAMPLE_K_TILE = 1024
RESIDUAL_ROW_TILE = 512

_NT_DIMS = (((1,), (1,)), ((), ()))
_TN_DIMS = (((0,), (0,)), ((), ()))


def _params(semantics):
    return pltpu.CompilerParams(dimension_semantics=semantics,
                                vmem_limit_bytes=V7X_VMEM_LIMIT_BYTES)


def _silu(x):
    return x / (1.0 + jnp.exp(-x))


def _rms_scale(x):
    return lax.rsqrt(jnp.mean(x * x, axis=-1, keepdims=True) + RMS_EPS)


def _norm_matmul_kernel(x_ref, g_ref, w_ref, *rest, has_extra, extra_gain, emit_weights,
                        w_transposed, silu_from):
    dims = _NT_DIMS if w_transposed else (((1,), (0,)), ((), ()))

    def mm(a, b):
        return lax.dot_general(a, b, dims, preferred_element_type=F32)

    rest = list(rest)
    g2_ref = rest.pop(0) if extra_gain else None
    w2_ref = rest.pop(0) if has_extra else None
    o_ref = rest.pop(0)
    o2_ref = rest.pop(0) if has_extra else None
    wb_ref = rest.pop(0) if emit_weights else None
    (xn_ref,) = rest

    first = pl.program_id(1) == 0

    def weights():
        w = w_ref[...].astype(BF16)
        if emit_weights:
            wb_ref[...] = w
        return w

    @pl.when(first)
    def _():
        tm = x_ref.shape[0]
        rc = min(tm, NORM_ROW_CHUNK)
        w = weights()
        for c in range(tm // rc):
            rows = slice(c * rc, (c + 1) * rc)
            x = x_ref[rows, :]
            xs = x * _rms_scale(x)
            xn = (xs * g_ref[...]).astype(BF16)
            xn_ref[rows, :] = xn
            o_ref[rows, :] = mm(xn, w).astype(o_ref.dtype)
            if has_extra:
                xn2 = (xs * g2_ref[...]).astype(BF16) if extra_gain else xn
                o2_ref[rows, :] = mm(xn2, w2_ref[...].astype(BF16)).astype(o2_ref.dtype)

    def later_block(gated):
        res = mm(xn_ref[...], weights())
        o_ref[...] = (_silu(res) if gated else res).astype(o_ref.dtype)

    j = pl.program_id(1)
    if silu_from is None or silu_from == 1:
        pl.when(j >= 1)(functools.partial(later_block, silu_from == 1))
    else:
        pl.when((j >= 1) & (j < silu_from))(functools.partial(later_block, False))
        pl.when(j >= silu_from)(functools.partial(later_block, True))


def _norm_matmul(x, g, w, *, tm, tn, out_dtype, n=None, w_extra=None, g_extra=None,
                 emit_weights=False, w_transposed=False, gate_cols=0):
    m, k = x.shape
    n_axis = -2 if w_transposed else -1
    n = w.shape[n_axis] if n is None else n
    assert n % tn == 0 and m % tm == 0
    assert not emit_weights or m == tm, "each weight block must be visited exactly once"
    assert gate_cols % tn == 0 and gate_cols <= n - tn, "gate columns: whole blocks after block 0"
    silu_from = (n - gate_cols) // tn if gate_cols else None
    grid = (m // tm, n // tn)
    w_block = (tn, k) if w_transposed else (k, tn)
    w_index = (lambda i, j: (j, 0)) if w_transposed else (lambda i, j: (0, j))
    if w.ndim == 3:
        w_spec = pl.BlockSpec((None,) + w_block, lambda i, j: (0,) + w_index(i, j))
    else:
        w_spec = pl.BlockSpec(w_block, w_index)
    in_specs = [
        pl.BlockSpec((tm, k), lambda i, j: (i, 0)),
        pl.BlockSpec((1, k), lambda i, j: (0, 0)),
        w_spec,
    ]
    out_shape = [jax.ShapeDtypeStruct((m, n), out_dtype)]
    out_specs = [pl.BlockSpec((tm, tn), lambda i, j: (i, j))]
    args = [x, g.reshape(1, k), w]
    if g_extra is not None:
        in_specs.append(pl.BlockSpec((1, k), lambda i, j: (0, 0)))
        args.append(g_extra.reshape(1, k))
    if w_extra is not None:
        n2 = w_extra.shape[n_axis]
        in_specs.append(pl.BlockSpec(w_extra.shape, lambda i, j: (0, 0)))
        out_shape.append(jax.ShapeDtypeStruct((m, n2), F32))
        out_specs.append(pl.BlockSpec((tm, n2), lambda i, j: (i, 0)))
        args.append(w_extra)
    if emit_weights:
        out_shape.append(jax.ShapeDtypeStruct((n, k) if w_transposed else (k, n), BF16))
        out_specs.append(pl.BlockSpec(w_block, w_index))
    res = pl.pallas_call(
        functools.partial(_norm_matmul_kernel, has_extra=w_extra is not None,
                          extra_gain=g_extra is not None, emit_weights=emit_weights,
                          w_transposed=w_transposed, silu_from=silu_from),
        grid=grid,
        in_specs=in_specs,
        out_specs=out_specs,
        out_shape=out_shape,
        scratch_shapes=[pltpu.VMEM((tm, k), BF16)],
        compiler_params=_params(("parallel", "arbitrary")),
        name=f"norm_matmul_{m}x{n}",
    )(*args)
    return res if len(res) > 1 else res[0]


def _matmul_residual_kernel(a_ref, w_ref, r_ref, *rest, final_norm, emit_weights, nk):
    rest = list(rest)
    g_ref = rest.pop(0) if final_norm else None
    o_ref = rest.pop(0)
    w = w_ref[...].astype(BF16)
    if emit_weights:
        rest[0][...] = w
    part = jnp.dot(a_ref[...].astype(BF16), w, preferred_element_type=F32)

    if nk == 1:
        h = r_ref[...] + part
        if final_norm:
            h = h * _rms_scale(h) * g_ref[...]
        o_ref[...] = h
        return

    kk = pl.program_id(1)

    @pl.when(kk == 0)
    def _():
        o_ref[...] = r_ref[...] + part

    @pl.when(kk > 0)
    def _():
        o_ref[...] += part

    if final_norm:
        @pl.when(kk == nk - 1)
        def _():
            h = o_ref[...]
            o_ref[...] = h * _rms_scale(h) * g_ref[...]


def _matmul_residual(a, w, res, *, tm, tk=None, g_final=None, emit_weights=False):
    m, k = a.shape
    n = w.shape[-1]
    tk = k if tk is None else tk
    assert m % tm == 0 and k % tk == 0
    assert not emit_weights or m == tm, "each weight block must be visited exactly once"
    if w.ndim == 3:
        w_spec = pl.BlockSpec((None, tk, n), lambda i, j: (0, j, 0))
    else:
        w_spec = pl.BlockSpec((tk, n), lambda i, j: (j, 0))
    in_specs = [
        pl.BlockSpec((tm, tk), lambda i, j: (i, j)),
        w_spec,
        pl.BlockSpec((tm, n), lambda i, j: (i, 0)),
    ]
    args = [a, w, res]
    if g_final is not None:
        in_specs.append(pl.BlockSpec((1, n), lambda i, j: (0, 0)))
        args.append(g_final.reshape(1, n))
    out_shape = [jax.ShapeDtypeStruct((m, n), F32)]
    out_specs = [pl.BlockSpec((tm, n), lambda i, j: (i, 0))]
    if emit_weights:
        out_shape.append(jax.ShapeDtypeStruct((k, n), BF16))
        out_specs.append(pl.BlockSpec((tk, n), lambda i, j: (j, 0)))
    res = pl.pallas_call(
        functools.partial(_matmul_residual_kernel, final_norm=g_final is not None,
                          emit_weights=emit_weights, nk=k // tk),
        grid=(m // tm, k // tk),
        in_specs=in_specs,
        out_specs=out_specs,
        out_shape=out_shape,
        compiler_params=_params(("parallel", "arbitrary")),
        name=f"matmul_residual_{m}" + ("_final" if g_final is not None else ""),
    )(*args)
    return res if emit_weights else res[0]


def _split_bf16(x):
    hi = x.astype(BF16)
    return hi, (x - hi.astype(F32)).astype(BF16)


def _gla_gates_kernel(glow_ref, wg_ref, bg_ref, bcum_ref, *, chunk):
    rows = glow_ref.shape[0]
    g_hi, g_lo = _split_bf16(glow_ref[...])
    w_hi, w_lo = _split_bf16(wg_ref[...])
    x = jnp.dot(jnp.concatenate([g_hi, g_lo, g_hi], axis=1),
                jnp.concatenate([w_hi, w_hi, w_lo], axis=0),
                preferred_element_type=F32) + bg_ref[...]
    softplus2 = jnp.log2(1.0 + jnp.exp2(jnp.abs(x) * (-LOG2E)))
    logg = jnp.minimum(x, 0.0) * (LOG2E / GLA_GATE_TEMP) - softplus2 * (1.0 / GLA_GATE_TEMP)

    span = GATE_CUMSUM_ROWS
    row = lax.broadcasted_iota(jnp.int32, (span, span), 0)
    col = lax.broadcasted_iota(jnp.int32, (span, span), 1)
    same_chunk = (row // chunk) == (col // chunk) if chunk < span else True
    tril = jnp.where((col <= row) & same_chunk, 1.0, 0.0).astype(BF16)
    tril2 = jnp.concatenate([tril, tril], axis=1)
    for i in range(rows // span):
        hi, lo = _split_bf16(logg[i * span:(i + 1) * span])
        bcum_ref[i * span:(i + 1) * span, :] = jnp.dot(
            tril2, jnp.concatenate([hi, lo], axis=0), preferred_element_type=F32)


def _gates_with_gla_kernel(glow_ref, wg_ref, bg_ref, *rest, chunk, n_gla_in):
    gla_in, (bcum_ref, o_ref, sfin_ref) = rest[:n_gla_in], rest[n_gla_in:]
    _gla_kernel(*gla_in, o_ref, sfin_ref, has_init=True, n_chunks=1)
    _gla_gates_kernel(glow_ref, wg_ref, bg_ref, bcum_ref, chunk=chunk)


def _gla_gates(glow, wg, bg, *, chunk, tm, rider=None):
    m = glow.shape[0]
    assert m % tm == 0 and tm % GATE_CUMSUM_ROWS == 0
    assert GATE_CUMSUM_ROWS % chunk == 0 or chunk % GATE_CUMSUM_ROWS == 0
    assert chunk <= GATE_CUMSUM_ROWS, "cumulative sums do not cross row spans"
    steps = m // tm
    in_specs = [
        pl.BlockSpec((tm, GLA_GATE_RANK), lambda i: (i, 0)),
        pl.BlockSpec((GLA_GATE_RANK, GLA_KEY_DIM), lambda i: (0, 0)),
        pl.BlockSpec((1, GLA_KEY_DIM), lambda i: (0, 0)),
    ]
    args = [glow, wg, bg.reshape(1, -1)]
    out_specs = [pl.BlockSpec((tm, GLA_KEY_DIM), lambda i: (i, 0))]
    out_shape = [jax.ShapeDtypeStruct((m, GLA_KEY_DIM), F32)]
    if rider is None:
        body = functools.partial(_gla_gates_kernel, chunk=chunk)
    else:
        batch, seq = rider["batch"], rider["seq"]
        assert batch % steps == 0
        g_in, g_args, g_out, g_shape = _gla_operands(
            rider["proj"], rider["bcum"], rider["gon"], rider["s0"], batch=batch, seq=seq,
            chunk=seq, ns=batch // steps, out_dtype=F32, index=lambda i: (i, 0))
        body = functools.partial(_gates_with_gla_kernel, chunk=chunk, n_gla_in=len(g_in))
        in_specs += g_in
        args += g_args
        out_specs += g_out
        out_shape += g_shape
    res = pl.pallas_call(
        body,
        grid=(steps,),
        in_specs=in_specs,
        out_specs=out_specs,
        out_shape=out_shape,
        compiler_params=_params(("parallel",)),
        name=f"gla_gates_{m}",
    )(*args)
    if rider is None:
        return res[0]
    bcum, o, s_fin = res
    return bcum, o.reshape(rider["batch"] * rider["seq"], GLA_VALUE_DIM), s_fin


def _gla_kernel(q_ref, k_ref, v_ref, r_ref, bcum_ref, gon_ref, *rest, has_init, n_chunks):
    rest = list(rest)
    s0_ref = rest.pop(0) if has_init else None
    o_ref, sfin_ref = rest[:2]
    s_ref = rest[2] if n_chunks > 1 else None
    c = pl.program_id(1) if n_chunks > 1 else None
    nseq, chunk = q_ref.shape[:2]
    chains = [(s, h) for s in range(nseq) for h in range(GLA_HEADS)]

    if n_chunks > 1:
        @pl.when(c == 0)
        def _():
            if has_init:
                s_ref[...] = s0_ref[...]
            else:
                s_ref[...] = jnp.zeros_like(s_ref)

    row = lax.broadcasted_iota(jnp.int32, (chunk, chunk), 0)
    col = lax.broadcasted_iota(jnp.int32, (chunk, chunk), 1)
    causal = col <= row
    heads = range(GLA_HEADS)
    ks = [slice(h * GLA_DK, (h + 1) * GLA_DK) for h in heads]
    vs = [slice(h * GLA_DV, (h + 1) * GLA_DV) for h in heads]

    def state(s, h):
        if n_chunks > 1:
            return s_ref[s, h]
        return s0_ref[s, h] if has_init else jnp.zeros((GLA_DK, GLA_DV), F32)

    q_inter, k_state, scores, decay = {}, {}, {}, {}
    for s, h in chains:
        b = bcum_ref[s, :, ks[h]]
        b_last = b[chunk - 1:chunk, :]
        b_mid = b[chunk // 2 - 1:chunk // 2, :]
        q = q_ref[s, :, ks[h]].astype(F32)
        k = k_ref[s, :, ks[h]].astype(F32)
        q_inter[s, h] = (q * jnp.exp2(b)).astype(BF16)
        q_intra = (q * jnp.exp2(b - b_mid)).astype(BF16)
        k_intra = (k * jnp.exp2(b_mid - b)).astype(BF16)
        k_state[s, h] = (k * jnp.exp2(b_last - b)).astype(BF16)
        scores[s, h] = lax.dot_general(q_intra, k_intra, _NT_DIMS,
                                       preferred_element_type=F32)
        decay[s, h] = jnp.exp2(jnp.broadcast_to(b_last, (128, GLA_DK))).T

    o = {}
    for s, h in chains:
        sc = jnp.where(causal, scores[s, h], 0.0).astype(BF16)
        v = v_ref[s, :, vs[h]].astype(BF16)
        o[s, h] = (jnp.dot(q_inter[s, h], state(s, h).astype(BF16),
                           preferred_element_type=F32)
                   + jnp.dot(sc, v, preferred_element_type=F32))
        upd = lax.dot_general(k_state[s, h], v, _TN_DIMS, preferred_element_type=F32)
        s_new = (state(s, h) * jnp.concatenate([decay[s, h]] * (GLA_DV // 128), axis=1)
                 + upd)
        if n_chunks > 1:
            s_ref[s, h] = s_new
        else:
            sfin_ref[s, h] = s_new

    qs = GLA_DK ** -0.5
    scale = {sh: qs * lax.rsqrt(jnp.mean(o[sh] * o[sh], axis=-1, keepdims=True) * (qs * qs)
                                + RMS_EPS) for sh in chains}
    for s, h in chains:
        r = r_ref[s, :, vs[h]].astype(F32)
        o_ref[s, :, vs[h]] = (o[s, h] * scale[s, h] * gon_ref[:, vs[h]]
                              * r).astype(o_ref.dtype)

    if n_chunks > 1:
        @pl.when(c == n_chunks - 1)
        def _():
            sfin_ref[...] = s_ref[...]


def _gla_operands(proj, bcum, gon, s0, *, batch, seq, chunk, ns, out_dtype, index):
    assert batch % ns == 0 and seq % chunk == 0
    proj = proj.reshape(batch, seq, proj.shape[-1])
    bcum = bcum.reshape(batch, seq, bcum.shape[-1])
    kb, vb, rb = 1, 2 * GLA_KEY_DIM // GLA_VALUE_DIM, 2 * GLA_KEY_DIM // GLA_VALUE_DIM + 1

    def rows(width, col):
        return pl.BlockSpec((ns, chunk, width), lambda *g: (*index(*g), col))

    state_spec = pl.BlockSpec((ns, GLA_HEADS, GLA_DK, GLA_DV),
                              lambda *g: (index(*g)[0], 0, 0, 0))
    in_specs = [rows(GLA_KEY_DIM, 0), rows(GLA_KEY_DIM, kb), rows(GLA_VALUE_DIM, vb),
                rows(GLA_VALUE_DIM, rb), rows(GLA_KEY_DIM, 0),
                pl.BlockSpec((1, GLA_VALUE_DIM), lambda *g: (0, 0))]
    args = [proj, proj, proj, proj, bcum, gon.reshape(1, -1)]
    if s0 is not None:
        in_specs.append(state_spec)
        args.append(s0)
    out_specs = [rows(GLA_VALUE_DIM, 0), state_spec]
    out_shape = [jax.ShapeDtypeStruct((batch, seq, GLA_VALUE_DIM), out_dtype),
                 jax.ShapeDtypeStruct((batch, GLA_HEADS, GLA_DK, GLA_DV), F32)]
    return in_specs, args, out_specs, out_shape


def _gla(proj, bcum, gon, *, batch, seq, chunk, out_dtype, s0=None):
    n = seq // chunk
    ns = GLA_SEQS_PER_STEP
    in_specs, args, out_specs, out_shape = _gla_operands(
        proj, bcum, gon, s0, batch=batch, seq=seq, chunk=chunk, ns=ns, out_dtype=out_dtype,
        index=lambda b, c: (b, c))
    scratch = [pltpu.VMEM((ns, GLA_HEADS, GLA_DK, GLA_DV), F32)] if n > 1 else []
    o, s_fin = pl.pallas_call(
        functools.partial(_gla_kernel, has_init=s0 is not None, n_chunks=n),
        grid=(batch // ns, n),
        in_specs=in_specs,
        out_specs=out_specs,
        out_shape=out_shape,
        scratch_shapes=scratch,
        compiler_params=_params(("parallel", "arbitrary")),
        name=f"gla_chunk{chunk}",
    )(*args)
    return o.reshape(batch * seq, GLA_VALUE_DIM), s_fin


def _alibi_slope(head):
    return 2.0 ** (-8.0 * (head + 1) / SWA_HEADS)


def _attn_prompt_kernel(sinks_ref, q_ref, z_ref, kp_ref, ko_ref, vp_ref, vo_ref, out_ref,
                        kwin_ref, vwin_ref, bias_ref):
    hd, nkeys = SWA_HEAD_DIM, 2 * WINDOW
    pair_w = 2 * hd
    pairs_per_group = SWA_GROUP // 2
    blk = pl.program_id(1)

    @pl.when((pl.program_id(0) == 0) & (blk == 0))
    def _():
        kj = lax.broadcasted_iota(jnp.int32, (nkeys, WINDOW), 0)
        qi = lax.broadcasted_iota(jnp.int32, (nkeys, WINDOW), 1)
        dist = WINDOW + qi - kj
        ok = (dist >= 0) & (dist <= WINDOW)
        ok_first = ok & (kj >= WINDOW)
        distf = dist.astype(F32)
        for h in range(SWA_HEADS):
            pen = (-_alibi_slope(h) * LOG2E) * distf
            sl = slice((h % 2) * WINDOW, (h % 2 + 1) * WINDOW)
            bias_ref[0, h // 2, :, sl] = jnp.where(ok_first, pen, -jnp.inf)
            bias_ref[1, h // 2, :, sl] = jnp.where(ok, pen, -jnp.inf)

    n_sub = q_ref.shape[0] // WINDOW
    k_rows = jnp.concatenate([kp_ref[...], ko_ref[...]], axis=0)
    v_rows = jnp.concatenate([vp_ref[...], vo_ref[...]], axis=0)
    ones = jnp.ones((16, nkeys), F32)
    lane = lax.broadcasted_iota(jnp.int32, (nkeys, pair_w), 1)
    qlane = lax.broadcasted_iota(jnp.int32, (WINDOW, pair_w), 1)
    qk_scale = (hd ** -0.5) * LOG2E
    quad_pairs = 2
    n_quads = SWA_HEADS // (2 * quad_pairs)
    quads_per_group = pairs_per_group // quad_pairs
    quarter = lax.broadcasted_iota(jnp.int32, (1, 2 * quad_pairs * WINDOW), 1) // WINDOW

    def prepare(sub):
        k = k_rows[sub * WINDOW:sub * WINDOW + nkeys]
        vt = v_rows[sub * WINDOW:sub * WINDOW + nkeys].T
        k2, vt1 = [], []
        for g in range(SWA_KV_HEADS):
            kblk = k[:, (g // 2) * pair_w:(g // 2 + 1) * pair_w]
            k_here = jnp.where((lane < hd) if g % 2 == 0 else (lane >= hd), kblk, 0.0)
            k2.append((k_here + pltpu.roll(k_here, hd, axis=1)).astype(BF16))
            vt1.append(jnp.concatenate([vt[g * hd:(g + 1) * hd], ones], axis=0).astype(BF16))
        return dict(qrows=slice(sub * WINDOW, (sub + 1) * WINDOW), k2=k2, vt1=vt1,
                    tbl=jnp.minimum(blk, 1) if sub == 0 else 1)

    def scores(blkst, quad):
        parts = []
        for j in range(quad_pairs):
            col = (quad * quad_pairs + j) * pair_w
            q_pair = q_ref[blkst["qrows"], col:col + pair_w]
            zero = jnp.zeros_like(q_pair)
            parts += [jnp.where(qlane < hd, q_pair, zero), jnp.where(qlane >= hd, q_pair, zero)]
        return lax.dot_general(blkst["k2"][quad // quads_per_group],
                               jnp.concatenate(parts, axis=0),
                               _NT_DIMS, preferred_element_type=F32)

    def finish(blkst, quad, st):
        pair0 = quad * quad_pairs
        bias = jnp.concatenate([bias_ref[blkst["tbl"], pair0 + j] for j in range(quad_pairs)],
                               axis=1)
        s2 = st * qk_scale + bias
        sink2 = sinks_ref[2 * pair0]
        for t in range(1, 2 * quad_pairs):
            sink2 = jnp.where(quarter == t, sinks_ref[2 * pair0 + t], sink2)
        sink2 = sink2 * LOG2E
        m = jnp.maximum(jnp.max(s2, axis=0, keepdims=True), sink2)
        p = jnp.exp2(s2 - m).astype(BF16)
        oa = jnp.dot(blkst["vt1"][quad // quads_per_group], p, preferred_element_type=F32)
        denom = oa[hd:hd + 1] + jnp.exp2(sink2 - m)
        on = oa[0:hd] * (1.0 / denom)
        for j in range(quad_pairs):
            lo = 2 * j * WINDOW
            o_pair = jnp.concatenate([on[:, lo:lo + WINDOW],
                                      on[:, lo + WINDOW:lo + 2 * WINDOW]], axis=0).T
            col = (pair0 + j) * pair_w
            z_pair = z_ref[blkst["qrows"], col:col + pair_w].astype(F32)
            out_ref[blkst["qrows"], col:col + pair_w] = (
                o_pair * z_pair).astype(out_ref.dtype)

    for first in range(0, n_sub, ATTN_BLOCKS_IN_FLIGHT):
        blocks = [prepare(sub) for sub in range(first, min(first + ATTN_BLOCKS_IN_FLIGHT, n_sub))]
        st_next = [scores(b, 0) for b in blocks]
        for quad in range(n_quads):
            for i, b in enumerate(blocks):
                st = st_next[i]
                if quad + 1 < n_quads:
                    st_next[i] = scores(b, quad + 1)
                finish(b, quad, st)

    @pl.when(blk == pl.num_programs(1) - 1)
    def _():
        last = slice((n_sub - 1) * WINDOW, n_sub * WINDOW)
        kwin_ref[...] = ko_ref[last, :].T
        vwin_ref[...] = vo_ref[last, :].T


def _attn_prompt(qz, kv, sinks, *, batch, seq):
    sub = ATTN_BLOCKS_PER_STEP
    tq = sub * WINDOW
    nb = seq // tq
    assert seq % tq == 0
    row = lambda b, i: b * nb + i
    prev = lambda b, i: (b * nb + i) * sub - jnp.minimum(i, 1)
    win_spec = pl.BlockSpec((None, SWA_KV_WIDTH, WINDOW), lambda b, i: (b, 0, 0))
    win_shape = jax.ShapeDtypeStruct((batch, SWA_KV_WIDTH, WINDOW), F32)
    att, k_win, v_win = pl.pallas_call(
        _attn_prompt_kernel,
        grid=(batch, nb),
        in_specs=[
            pl.BlockSpec(memory_space=pltpu.SMEM),
            pl.BlockSpec((tq, SWA_WIDTH), lambda b, i: (row(b, i), 0)),
            pl.BlockSpec((tq, SWA_WIDTH), lambda b, i: (row(b, i), 1)),
            pl.BlockSpec((WINDOW, SWA_KV_WIDTH), lambda b, i: (prev(b, i), 0)),
            pl.BlockSpec((tq, SWA_KV_WIDTH), lambda b, i: (row(b, i), 0)),
            pl.BlockSpec((WINDOW, SWA_KV_WIDTH), lambda b, i: (prev(b, i), 1)),
            pl.BlockSpec((tq, SWA_KV_WIDTH), lambda b, i: (row(b, i), 1)),
        ],
        out_specs=[pl.BlockSpec((tq, SWA_WIDTH), lambda b, i: (row(b, i), 0)),
                   win_spec, win_spec],
        out_shape=[jax.ShapeDtypeStruct((batch * seq, SWA_WIDTH), BF16), win_shape, win_shape],
        scratch_shapes=[pltpu.VMEM((2, SWA_HEADS // 2, 2 * WINDOW, 2 * WINDOW), F32)],
        compiler_params=_params(("arbitrary", "arbitrary")),
        name="attn_prompt",
    )(sinks, qz, qz, kv, kv, kv, kv)
    to_rows = lambda w: jnp.transpose(
        w.reshape(batch, SWA_KV_HEADS, SWA_HEAD_DIM, WINDOW), (0, 3, 1, 2))
    return att, to_rows(k_win), to_rows(v_win)


def _attn_sample_kernel(sink_ref, slope_ref, q_ref, z_ref, kn_ref, vn_ref, kc_ref, vc_ref,
                        out_ref, kwin_ref, vwin_ref, *, tq, nb):
    hd, nk = SWA_HEAD_DIM, 2 * WINDOW
    rows = SWA_HEADS * tq
    grows = SWA_GROUP * tq
    seqs = range(nb)
    groups = range(SWA_KV_HEADS)

    lane = lax.broadcasted_iota(jnp.int32, (rows, nk), 1)
    tok = lax.broadcasted_iota(jnp.int32, (rows, nk), 0) % tq
    in_buffer = lane < WINDOW
    dist = jnp.where(in_buffer, WINDOW + tok - lane, (nk - tq) + tok - lane)
    allowed = (dist >= 0) & (dist <= WINDOW) & (in_buffer | (lane >= nk - tq))
    penalty = slope_ref[...] * dist.astype(F32)
    new_lanes = lax.broadcasted_iota(jnp.int32, (hd, WINDOW), 1) >= WINDOW - tq

    def new_rows_t(ref, b):
        x = jnp.concatenate([jnp.zeros((WINDOW - tq, SWA_KV_WIDTH), F32),
                             ref[b * tq:(b + 1) * tq, :]], axis=0)
        xt = [x[:, c * WINDOW:(c + 1) * WINDOW].T for c in range(SWA_KV_WIDTH // WINDOW)]
        per_block = WINDOW // hd
        return [xt[g // per_block][(g % per_block) * hd:(g % per_block + 1) * hd]
                for g in groups]

    k_all, v_all = {}, {}
    for b in seqs:
        kn_t, vn_t = new_rows_t(kn_ref, b), new_rows_t(vn_ref, b)
        for g in groups:
            kc, vc = kc_ref[b, g], vc_ref[b, g]
            k_all[b, g] = jnp.concatenate([kc, kn_t[g]], axis=1).astype(BF16)
            v_all[b, g] = jnp.concatenate([vc, vn_t[g]], axis=1).astype(BF16)
            kwin_ref[b, g] = jnp.where(new_lanes, kn_t[g], pltpu.roll(kc, WINDOW - tq, axis=1))
            vwin_ref[b, g] = jnp.where(new_lanes, vn_t[g], pltpu.roll(vc, WINDOW - tq, axis=1))

    s = []
    for b in seqs:
        q = q_ref[b * tq:(b + 1) * tq, :]
        parts = []
        for g in groups:
            qs = jnp.concatenate([q[:, h * hd:(h + 1) * hd]
                                  for h in range(g * SWA_GROUP, (g + 1) * SWA_GROUP)], axis=0)
            parts.append(jnp.dot(qs.astype(BF16), k_all[b, g], preferred_element_type=F32))
        s.append(jnp.concatenate(parts, axis=0))

    sink = sink_ref[...]
    s = [jnp.where(allowed, sb * (hd ** -0.5) - penalty, -jnp.inf) for sb in s]
    m = [jnp.maximum(jnp.max(sb, axis=-1, keepdims=True), sink) for sb in s]
    p = [jnp.exp(s[b] - m[b]) for b in seqs]
    inv = [1.0 / (jnp.sum(p[b], axis=-1, keepdims=True) + jnp.exp(sink - m[b])) for b in seqs]

    for b in seqs:
        pb = p[b].astype(BF16)
        o = jnp.concatenate(
            [lax.dot_general(pb[g * grows:(g + 1) * grows], v_all[b, g], _NT_DIMS,
                             preferred_element_type=F32) for g in groups], axis=0) * inv[b]
        o = jnp.concatenate([o[h * tq:(h + 1) * tq] for h in range(SWA_HEADS)], axis=1)
        z = z_ref[b * tq:(b + 1) * tq, :]
        out_ref[b * tq:(b + 1) * tq, :] = (o * z).astype(out_ref.dtype)


def _attn_sample(qz, kv, cache_k, cache_v, sinks, *, batch, seq):
    nb = SAMPLE_SEQS_PER_STEP
    assert batch % nb == 0
    rows = nb * seq
    sink_col = jnp.repeat(sinks, seq).reshape(SWA_HEADS * seq, 1)
    slope_col = jnp.repeat(jnp.asarray([_alibi_slope(h) for h in range(SWA_HEADS)], F32),
                           seq).reshape(SWA_HEADS * seq, 1)
    col_spec = pl.BlockSpec((SWA_HEADS * seq, 1), lambda i: (0, 0))
    win_spec = pl.BlockSpec((nb, SWA_KV_HEADS, SWA_HEAD_DIM, WINDOW), lambda i: (i, 0, 0, 0))
    win_shape = jax.ShapeDtypeStruct((batch, SWA_KV_HEADS, SWA_HEAD_DIM, WINDOW), F32)
    to_lanes = lambda c: jnp.transpose(c, (0, 2, 3, 1))
    att, k_win, v_win = pl.pallas_call(
        functools.partial(_attn_sample_kernel, tq=seq, nb=nb),
        grid=(batch // nb,),
        in_specs=[
            col_spec,
            col_spec,
            pl.BlockSpec((rows, SWA_WIDTH), lambda i: (i, 0)),
            pl.BlockSpec((rows, SWA_WIDTH), lambda i: (i, 1)),
            pl.BlockSpec((rows, SWA_KV_WIDTH), lambda i: (i, 0)),
            pl.BlockSpec((rows, SWA_KV_WIDTH), lambda i: (i, 1)),
            win_spec,
            win_spec,
        ],
        out_specs=[pl.BlockSpec((rows, SWA_WIDTH), lambda i: (i, 0)), win_spec, win_spec],
        out_shape=[jax.ShapeDtypeStruct((batch * seq, SWA_WIDTH), F32), win_shape, win_shape],
        compiler_params=_params(("parallel",)),
        name="attn_sample",
    )(sink_col, slope_col, qz, qz, kv, kv, to_lanes(cache_k), to_lanes(cache_v))
    return att, jnp.transpose(k_win, (0, 3, 1, 2)), jnp.transpose(v_win, (0, 3, 1, 2))


def kernel(x_prompt, x_sample, state_gla, cache_k_win, cache_v_win, g_norm_a, w_in_a,
           w_gate_up, b_gate, g_onorm_a, w_out_a, g_norm_kv, w_kv, g_norm_b, w_in_b, sinks,
           w_out_b, g_final):
    assert w_in_a.shape[0] == 1 and w_in_b.shape[0] == 1, "one GLA layer, one SWA layer"
    assert cache_k_win.shape[1] == WINDOW
    pb, ps, _ = x_prompt.shape
    sb, ss, _ = x_sample.shape
    pm, sm = pb * ps, sb * ss
    g_norm_a, w_gate_up, b_gate, g_onorm_a = g_norm_a[0], w_gate_up[0], b_gate[0], g_onorm_a[0]
    g_norm_b, sinks = g_norm_b[0], sinks[0]
    w_kv = w_kv.astype(BF16)
    w_in_a_t = jnp.swapaxes(w_in_a, 1, 2)
    w_glow = w_in_a_t[0, GLA_MAIN_COLS:, :]
    h_p = x_prompt.reshape(pm, D_MODEL)
    h_s = x_sample.reshape(sm, D_MODEL)

    proj_s, glow_s, w_in_a_bf16 = _norm_matmul(
        h_s, g_norm_a, w_in_a_t, tm=sm, tn=SAMPLE_COL_TILE, n=GLA_MAIN_COLS, out_dtype=F32,
        w_extra=w_glow, emit_weights=True, w_transposed=True, gate_cols=GLA_VALUE_DIM)
    bcum_s = _gla_gates(glow_s, w_gate_up, b_gate, chunk=ss, tm=sm)
    proj_p, glow_p = _norm_matmul(
        h_p, g_norm_a, w_in_a_bf16, tm=PROMPT_ROW_TILE, tn=PROMPT_COL_TILE, n=GLA_MAIN_COLS,
        out_dtype=BF16, w_extra=w_glow, w_transposed=True, gate_cols=GLA_VALUE_DIM)
    bcum_p, o_s, gla_s = _gla_gates(
        glow_p, w_gate_up, b_gate, chunk=GLA_PROMPT_CHUNK, tm=GATE_ROW_TILE,
        rider=dict(proj=proj_s, bcum=bcum_s, gon=g_onorm_a, batch=sb, seq=ss,
                   s0=state_gla.reshape(state_gla.shape[1:])))
    h_s, w_out_a_bf16 = _matmul_residual(o_s, w_out_a, h_s, tm=sm, tk=SAMPLE_K_TILE,
                                         emit_weights=True)
    o_p, gla_p = _gla(proj_p, bcum_p, g_onorm_a, batch=pb, seq=ps, chunk=GLA_PROMPT_CHUNK,
                      out_dtype=BF16)
    h_p = _matmul_residual(o_p, w_out_a_bf16, h_p, tm=RESIDUAL_ROW_TILE)

    qz_s, kv_s, w_in_b_bf16 = _norm_matmul(
        h_s, g_norm_b, w_in_b, tm=sm, tn=QZ_COL_TILE, out_dtype=F32,
        w_extra=w_kv, g_extra=g_norm_kv, emit_weights=True, gate_cols=SWA_WIDTH)
    att_s, k_s, v_s = _attn_sample(qz_s, kv_s, cache_k_win, cache_v_win, sinks,
                                   batch=sb, seq=ss)
    y_s, w_out_b_bf16 = _matmul_residual(att_s, w_out_b, h_s, tm=sm, tk=SAMPLE_K_TILE,
                                         g_final=g_final, emit_weights=True)
    qz_p, kv_p = _norm_matmul(
        h_p, g_norm_b, w_in_b_bf16, tm=PROMPT_ROW_TILE, tn=QZ_COL_TILE, out_dtype=BF16,
        w_extra=w_kv, g_extra=g_norm_kv, gate_cols=SWA_WIDTH)
    att_p, k_p, v_p = _attn_prompt(qz_p, kv_p, sinks, batch=pb, seq=ps)
    y_p = _matmul_residual(att_p, w_out_b_bf16, h_p, tm=RESIDUAL_ROW_TILE, g_final=g_final)

    return (y_p.reshape(pb, ps, D_MODEL), y_s.reshape(sb, ss, D_MODEL), gla_p[None],
            gla_s[None], k_p, v_p, k_s, v_s)
```

```python
import functools

import jax
import jax.numpy as jnp
from jax import lax
from jax.experimental import pallas as pl
from jax.experimental.pallas import tpu as pltpu

F32 = jnp.float32
BF16 = jnp.bfloat16

D_MODEL = 2048
GLA_HEADS = 4
GLA_KEY_DIM = D_MODEL // 2
GLA_VALUE_DIM = D_MODEL
GLA_DK = GLA_KEY_DIM // GLA_HEADS
GLA_DV = GLA_VALUE_DIM // GLA_HEADS
GLA_GATE_RANK = 16
GLA_GATE_TEMP = 16.0
GLA_MAIN_COLS = 2 * GLA_KEY_DIM + 2 * GLA_VALUE_DIM
SWA_HEAD_DIM = 64
SWA_HEADS = D_MODEL // SWA_HEAD_DIM
SWA_KV_HEADS = 4
SWA_GROUP = SWA_HEADS // SWA_KV_HEADS
SWA_WIDTH = SWA_HEADS * SWA_HEAD_DIM
SWA_KV_WIDTH = SWA_KV_HEADS * SWA_HEAD_DIM
WINDOW = 128
RMS_EPS = 1e-6
LOG2E = 1.4426950408889634

V7X_VMEM_LIMIT_BYTES = 61 * 1024 * 1024
GLA_PROMPT_CHUNK = 128
GLA_SEQS_PER_STEP = 4
ATTN_BLOCKS_PER_STEP = 4
ATTN_BLOCKS_IN_FLIGHT = 2
SAMPLE_SEQS_PER_STEP = 8
NORM_ROW_CHUNK = 256
GATE_CUMSUM_ROWS = 128
GATE_ROW_TILE = 1024
PROMPT_ROW_TILE = 1024
PROMPT_COL_TILE = 2048
SAMPLE_COL_TILE = 1024
QZ_COL_TILE = 2048
SAMPLE_K_TILE = 1024
RESIDUAL_ROW_TILE = 512

_NT_DIMS = (((1,), (1,)), ((), ()))
_TN_DIMS = (((0,), (0,)), ((), ()))


def _params(semantics):
    return pltpu.CompilerParams(dimension_semantics=semantics,
                                vmem_limit_bytes=V7X_VMEM_LIMIT_BYTES)


def _silu(x):
    return x / (1.0 + jnp.exp(-x))


def _rms_scale(x):
    return lax.rsqrt(jnp.mean(x * x, axis=-1, keepdims=True) + RMS_EPS)


def _norm_matmul_kernel(x_ref, g_ref, w_ref, *rest, has_extra, extra_gain, emit_weights,
                        w_transposed, silu_from):
    dims = _NT_DIMS if w_transposed else (((1,), (0,)), ((), ()))

    def mm(a, b):
        return lax.dot_general(a, b, dims, preferred_element_type=F32)

    rest = list(rest)
    g2_ref = rest.pop(0) if extra_gain else None
    w2_ref = rest.pop(0) if has_extra else None
    o_ref = rest.pop(0)
    o2_ref = rest.pop(0) if has_extra else None
    wb_ref = rest.pop(0) if emit_weights else None
    (xn_ref,) = rest

    first = pl.program_id(1) == 0

    def weights():
        w = w_ref[...].astype(BF16)
        if emit_weights:
            wb_ref[...] = w
        return w

    @pl.when(first)
    def _():
        tm = x_ref.shape[0]
        rc = min(tm, NORM_ROW_CHUNK)
        w = weights()
        for c in range(tm // rc):
            rows = slice(c * rc, (c + 1) * rc)
            x = x_ref[rows, :]
            xs = x * _rms_scale(x)
            xn = (xs * g_ref[...]).astype(BF16)
            xn_ref[rows, :] = xn
            o_ref[rows, :] = mm(xn, w).astype(o_ref.dtype)
            if has_extra:
                xn2 = (xs * g2_ref[...]).astype(BF16) if extra_gain else xn
                o2_ref[rows, :] = mm(xn2, w2_ref[...].astype(BF16)).astype(o2_ref.dtype)

    def later_block(gated):
        w = weights()
        if not gated:
            o_ref[...] = mm(xn_ref[...], w).astype(o_ref.dtype)
            return
        tm = x_ref.shape[0]
        rc = min(tm, NORM_ROW_CHUNK)
        for c in range(tm // rc):
            rows = slice(c * rc, (c + 1) * rc)
            o_ref[rows, :] = _silu(mm(xn_ref[rows, :], w)).astype(o_ref.dtype)

    j = pl.program_id(1)
    if silu_from is None or silu_from == 1:
        pl.when(j >= 1)(functools.partial(later_block, silu_from == 1))
    else:
        pl.when((j >= 1) & (j < silu_from))(functools.partial(later_block, False))
        pl.when(j >= silu_from)(functools.partial(later_block, True))


def _norm_matmul(x, g, w, *, tm, tn, out_dtype, n=None, w_extra=None, g_extra=None,
                 emit_weights=False, w_transposed=False, gate_cols=0):
    m, k = x.shape
    n_axis = -2 if w_transposed else -1
    n = w.shape[n_axis] if n is None else n
    assert n % tn == 0 and m % tm == 0
    assert not emit_weights or m == tm, "each weight block must be visited exactly once"
    assert gate_cols % tn == 0 and gate_cols <= n - tn, "gate columns: whole blocks after block 0"
    silu_from = (n - gate_cols) // tn if gate_cols else None
    grid = (m // tm, n // tn)
    w_block = (tn, k) if w_transposed else (k, tn)
    w_index = (lambda i, j: (j, 0)) if w_transposed else (lambda i, j: (0, j))
    if w.ndim == 3:
        w_spec = pl.BlockSpec((None,) + w_block, lambda i, j: (0,) + w_index(i, j))
    else:
        w_spec = pl.BlockSpec(w_block, w_index)
    in_specs = [
        pl.BlockSpec((tm, k), lambda i, j: (i, 0)),
        pl.BlockSpec((1, k), lambda i, j: (0, 0)),
        w_spec,
    ]
    out_shape = [jax.ShapeDtypeStruct((m, n), out_dtype)]
    out_specs = [pl.BlockSpec((tm, tn), lambda i, j: (i, j))]
    args = [x, g.reshape(1, k), w]
    if g_extra is not None:
        in_specs.append(pl.BlockSpec((1, k), lambda i, j: (0, 0)))
        args.append(g_extra.reshape(1, k))
    if w_extra is not None:
        n2 = w_extra.shape[n_axis]
        in_specs.append(pl.BlockSpec(w_extra.shape, lambda i, j: (0, 0)))
        out_shape.append(jax.ShapeDtypeStruct((m, n2), F32))
        out_specs.append(pl.BlockSpec((tm, n2), lambda i, j: (i, 0)))
        args.append(w_extra)
    if emit_weights:
        out_shape.append(jax.ShapeDtypeStruct((n, k) if w_transposed else (k, n), BF16))
        out_specs.append(pl.BlockSpec(w_block, w_index))
    res = pl.pallas_call(
        functools.partial(_norm_matmul_kernel, has_extra=w_extra is not None,
                          extra_gain=g_extra is not None, emit_weights=emit_weights,
                          w_transposed=w_transposed, silu_from=silu_from),
        grid=grid,
        in_specs=in_specs,
        out_specs=out_specs,
        out_shape=out_shape,
        scratch_shapes=[pltpu.VMEM((tm, k), BF16)],
        compiler_params=_params(("parallel", "arbitrary")),
        name=f"norm_matmul_{m}x{n}",
    )(*args)
    return res if len(res) > 1 else res[0]


def _matmul_residual_kernel(a_ref, w_ref, r_ref, *rest, final_norm, emit_weights, nk):
    rest = list(rest)
    g_ref = rest.pop(0) if final_norm else None
    o_ref = rest.pop(0)
    w = w_ref[...].astype(BF16)
    if emit_weights:
        rest[0][...] = w
    part = jnp.dot(a_ref[...].astype(BF16), w, preferred_element_type=F32)

    if nk == 1:
        h = r_ref[...] + part
        if final_norm:
            h = h * _rms_scale(h) * g_ref[...]
        o_ref[...] = h
        return

    kk = pl.program_id(1)

    @pl.when(kk == 0)
    def _():
        o_ref[...] = r_ref[...] + part

    @pl.when(kk > 0)
    def _():
        o_ref[...] += part

    if final_norm:
        @pl.when(kk == nk - 1)
        def _():
            h = o_ref[...]
            o_ref[...] = h * _rms_scale(h) * g_ref[...]


def _matmul_residual(a, w, res, *, tm, tk=None, g_final=None, emit_weights=False):
    m, k = a.shape
    n = w.shape[-1]
    tk = k if tk is None else tk
    assert m % tm == 0 and k % tk == 0
    assert not emit_weights or m == tm, "each weight block must be visited exactly once"
    if w.ndim == 3:
        w_spec = pl.BlockSpec((None, tk, n), lambda i, j: (0, j, 0))
    else:
        w_spec = pl.BlockSpec((tk, n), lambda i, j: (j, 0))
    in_specs = [
        pl.BlockSpec((tm, tk), lambda i, j: (i, j)),
        w_spec,
        pl.BlockSpec((tm, n), lambda i, j: (i, 0)),
    ]
    args = [a, w, res]
    if g_final is not None:
        in_specs.append(pl.BlockSpec((1, n), lambda i, j: (0, 0)))
        args.append(g_final.reshape(1, n))
    out_shape = [jax.ShapeDtypeStruct((m, n), F32)]
    out_specs = [pl.BlockSpec((tm, n), lambda i, j: (i, 0))]
    if emit_weights:
        out_shape.append(jax.ShapeDtypeStruct((k, n), BF16))
        out_specs.append(pl.BlockSpec((tk, n), lambda i, j: (j, 0)))
    res = pl.pallas_call(
        functools.partial(_matmul_residual_kernel, final_norm=g_final is not None,
                          emit_weights=emit_weights, nk=k // tk),
        grid=(m // tm, k // tk),
        in_specs=in_specs,
        out_specs=out_specs,
        out_shape=out_shape,
        compiler_params=_params(("parallel", "arbitrary")),
        name=f"matmul_residual_{m}" + ("_final" if g_final is not None else ""),
    )(*args)
    return res if emit_weights else res[0]


def _split_bf16(x):
    hi = x.astype(BF16)
    return hi, (x - hi.astype(F32)).astype(BF16)


def _gla_gates_kernel(glow_ref, wg_ref, bg_ref, bcum_ref, *, chunk):
    rows = glow_ref.shape[0]
    g_hi, g_lo = _split_bf16(glow_ref[...])
    w_hi, w_lo = _split_bf16(wg_ref[...])
    x = jnp.dot(jnp.concatenate([g_hi, g_lo, g_hi], axis=1),
                jnp.concatenate([w_hi, w_hi, w_lo], axis=0),
                preferred_element_type=F32) + bg_ref[...]
    softplus2 = jnp.log2(1.0 + jnp.exp2(jnp.abs(x) * (-LOG2E)))
    logg = jnp.minimum(x, 0.0) * (LOG2E / GLA_GATE_TEMP) - softplus2 * (1.0 / GLA_GATE_TEMP)

    span = GATE_CUMSUM_ROWS
    row = lax.broadcasted_iota(jnp.int32, (span, span), 0)
    col = lax.broadcasted_iota(jnp.int32, (span, span), 1)
    same_chunk = (row // chunk) == (col // chunk) if chunk < span else True
    tril = jnp.where((col <= row) & same_chunk, 1.0, 0.0).astype(BF16)
    tril2 = jnp.concatenate([tril, tril], axis=1)
    for i in range(rows // span):
        hi, lo = _split_bf16(logg[i * span:(i + 1) * span])
        bcum_ref[i * span:(i + 1) * span, :] = jnp.dot(
            tril2, jnp.concatenate([hi, lo], axis=0), preferred_element_type=F32)


def _gates_with_gla_kernel(glow_ref, wg_ref, bg_ref, *rest, chunk, n_gla_in):
    gla_in, (bcum_ref, o_ref, sfin_ref) = rest[:n_gla_in], rest[n_gla_in:]
    _gla_kernel(*gla_in, o_ref, sfin_ref, has_init=True, n_chunks=1)
    _gla_gates_kernel(glow_ref, wg_ref, bg_ref, bcum_ref, chunk=chunk)


def _gla_gates(glow, wg, bg, *, chunk, tm, rider=None):
    m = glow.shape[0]
    assert m % tm == 0 and tm % GATE_CUMSUM_ROWS == 0
    assert GATE_CUMSUM_ROWS % chunk == 0 or chunk % GATE_CUMSUM_ROWS == 0
    assert chunk <= GATE_CUMSUM_ROWS, "cumulative sums do not cross row spans"
    steps = m // tm
    in_specs = [
        pl.BlockSpec((tm, GLA_GATE_RANK), lambda i: (i, 0)),
        pl.BlockSpec((GLA_GATE_RANK, GLA_KEY_DIM), lambda i: (0, 0)),
        pl.BlockSpec((1, GLA_KEY_DIM), lambda i: (0, 0)),
    ]
    args = [glow, wg, bg.reshape(1, -1)]
    out_specs = [pl.BlockSpec((tm, GLA_KEY_DIM), lambda i: (i, 0))]
    out_shape = [jax.ShapeDtypeStruct((m, GLA_KEY_DIM), F32)]
    if rider is None:
        body = functools.partial(_gla_gates_kernel, chunk=chunk)
    else:
        batch, seq = rider["batch"], rider["seq"]
        assert batch % steps == 0
        g_in, g_args, g_out, g_shape = _gla_operands(
            rider["proj"], rider["bcum"], rider["gon"], rider["s0"], batch=batch, seq=seq,
            chunk=seq, ns=batch // steps, out_dtype=F32, index=lambda i: (i, 0))
        body = functools.partial(_gates_with_gla_kernel, chunk=chunk, n_gla_in=len(g_in))
        in_specs += g_in
        args += g_args
        out_specs += g_out
        out_shape += g_shape
    res = pl.pallas_call(
        body,
        grid=(steps,),
        in_specs=in_specs,
        out_specs=out_specs,
        out_shape=out_shape,
        compiler_params=_params(("parallel",)),
        name=f"gla_gates_{m}",
    )(*args)
    if rider is None:
        return res[0]
    bcum, o, s_fin = res
    return bcum, o.reshape(rider["batch"] * rider["seq"], GLA_VALUE_DIM), s_fin


def _gla_kernel(q_ref, k_ref, v_ref, r_ref, bcum_ref, gon_ref, *rest, has_init, n_chunks):
    rest = list(rest)
    s0_ref = rest.pop(0) if has_init else None
    o_ref, sfin_ref = rest[:2]
    s_ref = rest[2] if n_chunks > 1 else None
    c = pl.program_id(1) if n_chunks > 1 else None
    nseq, chunk = q_ref.shape[:2]
    chains = [(s, h) for s in range(nseq) for h in range(GLA_HEADS)]

    if n_chunks > 1:
        @pl.when(c == 0)
        def _():
            if has_init:
                s_ref[...] = s0_ref[...]
            else:
                s_ref[...] = jnp.zeros_like(s_ref)

    row = lax.broadcasted_iota(jnp.int32, (chunk, chunk), 0)
    col = lax.broadcasted_iota(jnp.int32, (chunk, chunk), 1)
    causal = col <= row
    heads = range(GLA_HEADS)
    ks = [slice(h * GLA_DK, (h + 1) * GLA_DK) for h in heads]
    vs = [slice(h * GLA_DV, (h + 1) * GLA_DV) for h in heads]

    def state(s, h):
        if n_chunks > 1:
            return s_ref[s, h]
        return s0_ref[s, h] if has_init else jnp.zeros((GLA_DK, GLA_DV), F32)

    q_inter, k_state, scores, decay = {}, {}, {}, {}
    for s, h in chains:
        b = bcum_ref[s, :, ks[h]]
        b_last = b[chunk - 1:chunk, :]
        b_mid = b[chunk // 2 - 1:chunk // 2, :]
        q = q_ref[s, :, ks[h]].astype(F32)
        k = k_ref[s, :, ks[h]].astype(F32)
        q_inter[s, h] = (q * jnp.exp2(b)).astype(BF16)
        q_intra = (q * jnp.exp2(b - b_mid)).astype(BF16)
        k_intra = (k * jnp.exp2(b_mid - b)).astype(BF16)
        k_state[s, h] = (k * jnp.exp2(b_last - b)).astype(BF16)
        scores[s, h] = lax.dot_general(q_intra, k_intra, _NT_DIMS,
                                       preferred_element_type=F32)
        decay[s, h] = jnp.exp2(jnp.broadcast_to(b_last, (128, GLA_DK))).T

    o = {}
    for s, h in chains:
        sc = jnp.where(causal, scores[s, h], 0.0).astype(BF16)
        v = v_ref[s, :, vs[h]].astype(BF16)
        o[s, h] = (jnp.dot(q_inter[s, h], state(s, h).astype(BF16),
                           preferred_element_type=F32)
                   + jnp.dot(sc, v, preferred_element_type=F32))
        upd = lax.dot_general(k_state[s, h], v, _TN_DIMS, preferred_element_type=F32)
        s_new = (state(s, h) * jnp.concatenate([decay[s, h]] * (GLA_DV // 128), axis=1)
                 + upd)
        if n_chunks > 1:
            s_ref[s, h] = s_new
        else:
            sfin_ref[s, h] = s_new

    qs = GLA_DK ** -0.5
    scale = {sh: qs * lax.rsqrt(jnp.mean(o[sh] * o[sh], axis=-1, keepdims=True) * (qs * qs)
                                + RMS_EPS) for sh in chains}
    for s, h in chains:
        r = r_ref[s, :, vs[h]].astype(F32)
        o_ref[s, :, vs[h]] = (o[s, h] * scale[s, h] * gon_ref[:, vs[h]]
                              * r).astype(o_ref.dtype)

    if n_chunks > 1:
        @pl.when(c == n_chunks - 1)
        def _():
            sfin_ref[...] = s_ref[...]


def _gla_operands(proj, bcum, gon, s0, *, batch, seq, chunk, ns, out_dtype, index):
    assert batch % ns == 0 and seq % chunk == 0
    proj = proj.reshape(batch, seq, proj.shape[-1])
    bcum = bcum.reshape(batch, seq, bcum.shape[-1])
    kb, vb, rb = 1, 2 * GLA_KEY_DIM // GLA_VALUE_DIM, 2 * GLA_KEY_DIM // GLA_VALUE_DIM + 1

    def rows(width, col):
        return pl.BlockSpec((ns, chunk, width), lambda *g: (*index(*g), col))

    state_spec = pl.BlockSpec((ns, GLA_HEADS, GLA_DK, GLA_DV),
                              lambda *g: (index(*g)[0], 0, 0, 0))
    in_specs = [rows(GLA_KEY_DIM, 0), rows(GLA_KEY_DIM, kb), rows(GLA_VALUE_DIM, vb),
                rows(GLA_VALUE_DIM, rb), rows(GLA_KEY_DIM, 0),
                pl.BlockSpec((1, GLA_VALUE_DIM), lambda *g: (0, 0))]
    args = [proj, proj, proj, proj, bcum, gon.reshape(1, -1)]
    if s0 is not None:
        in_specs.append(state_spec)
        args.append(s0)
    out_specs = [rows(GLA_VALUE_DIM, 0), state_spec]
    out_shape = [jax.ShapeDtypeStruct((batch, seq, GLA_VALUE_DIM), out_dtype),
                 jax.ShapeDtypeStruct((batch, GLA_HEADS, GLA_DK, GLA_DV), F32)]
    return in_specs, args, out_specs, out_shape


def _gla(proj, bcum, gon, *, batch, seq, chunk, out_dtype, s0=None):
    n = seq // chunk
    ns = GLA_SEQS_PER_STEP
    in_specs, args, out_specs, out_shape = _gla_operands(
        proj, bcum, gon, s0, batch=batch, seq=seq, chunk=chunk, ns=ns, out_dtype=out_dtype,
        index=lambda b, c: (b, c))
    scratch = [pltpu.VMEM((ns, GLA_HEADS, GLA_DK, GLA_DV), F32)] if n > 1 else []
    o, s_fin = pl.pallas_call(
        functools.partial(_gla_kernel, has_init=s0 is not None, n_chunks=n),
        grid=(batch // ns, n),
        in_specs=in_specs,
        out_specs=out_specs,
        out_shape=out_shape,
        scratch_shapes=scratch,
        compiler_params=_params(("parallel", "arbitrary")),
        name=f"gla_chunk{chunk}",
    )(*args)
    return o.reshape(batch * seq, GLA_VALUE_DIM), s_fin


def _alibi_slope(head):
    return 2.0 ** (-8.0 * (head + 1) / SWA_HEADS)


def _attn_prompt_kernel(sinks_ref, q_ref, z_ref, kp_ref, ko_ref, vp_ref, vo_ref, out_ref,
                        kwin_ref, vwin_ref, bias_ref):
    hd, nkeys = SWA_HEAD_DIM, 2 * WINDOW
    pair_w = 2 * hd
    pairs_per_group = SWA_GROUP // 2
    blk = pl.program_id(1)

    @pl.when((pl.program_id(0) == 0) & (blk == 0))
    def _():
        kj = lax.broadcasted_iota(jnp.int32, (nkeys, WINDOW), 0)
        qi = lax.broadcasted_iota(jnp.int32, (nkeys, WINDOW), 1)
        dist = WINDOW + qi - kj
        ok = (dist >= 0) & (dist <= WINDOW)
        ok_first = ok & (kj >= WINDOW)
        distf = dist.astype(F32)
        for h in range(SWA_HEADS):
            pen = (-_alibi_slope(h) * LOG2E) * distf
            sl = slice((h % 2) * WINDOW, (h % 2 + 1) * WINDOW)
            bias_ref[0, h // 2, :, sl] = jnp.where(ok_first, pen, -jnp.inf)
            bias_ref[1, h // 2, :, sl] = jnp.where(ok, pen, -jnp.inf)

    n_sub = q_ref.shape[0] // WINDOW
    k_rows = jnp.concatenate([kp_ref[...], ko_ref[...]], axis=0)
    v_rows = jnp.concatenate([vp_ref[...], vo_ref[...]], axis=0)
    ones = jnp.ones((16, nkeys), F32)
    lane = lax.broadcasted_iota(jnp.int32, (nkeys, pair_w), 1)
    qlane = lax.broadcasted_iota(jnp.int32, (WINDOW, pair_w), 1)
    qk_scale = (hd ** -0.5) * LOG2E
    quad_pairs = 2
    n_quads = SWA_HEADS // (2 * quad_pairs)
    quads_per_group = pairs_per_group // quad_pairs
    quarter = lax.broadcasted_iota(jnp.int32, (1, 2 * quad_pairs * WINDOW), 1) // WINDOW

    def prepare(sub):
        k = k_rows[sub * WINDOW:sub * WINDOW + nkeys]
        vt = v_rows[sub * WINDOW:sub * WINDOW + nkeys].T
        k2, vt1 = [], []
        for g in range(SWA_KV_HEADS):
            kblk = k[:, (g // 2) * pair_w:(g // 2 + 1) * pair_w]
            k_here = jnp.where((lane < hd) if g % 2 == 0 else (lane >= hd), kblk, 0.0)
            k2.append((k_here + pltpu.roll(k_here, hd, axis=1)).astype(BF16))
            vt1.append(jnp.concatenate([vt[g * hd:(g + 1) * hd], ones], axis=0).astype(BF16))
        return dict(qrows=slice(sub * WINDOW, (sub + 1) * WINDOW), k2=k2, vt1=vt1,
                    tbl=jnp.minimum(blk, 1) if sub == 0 else 1)

    def scores(blkst, quad):
        parts = []
        for j in range(quad_pairs):
            col = (quad * quad_pairs + j) * pair_w
            q_pair = q_ref[blkst["qrows"], col:col + pair_w]
            zero = jnp.zeros_like(q_pair)
            parts += [jnp.where(qlane < hd, q_pair, zero), jnp.where(qlane >= hd, q_pair, zero)]
        return lax.dot_general(blkst["k2"][quad // quads_per_group],
                               jnp.concatenate(parts, axis=0),
                               _NT_DIMS, preferred_element_type=F32)

    def finish(blkst, quad, st):
        pair0 = quad * quad_pairs
        bias = jnp.concatenate([bias_ref[blkst["tbl"], pair0 + j] for j in range(quad_pairs)],
                               axis=1)
        s2 = st * qk_scale + bias
        sink2 = sinks_ref[2 * pair0]
        for t in range(1, 2 * quad_pairs):
            sink2 = jnp.where(quarter == t, sinks_ref[2 * pair0 + t], sink2)
        sink2 = sink2 * LOG2E
        m = jnp.maximum(jnp.max(s2, axis=0, keepdims=True), sink2)
        p = jnp.exp2(s2 - m).astype(BF16)
        oa = jnp.dot(blkst["vt1"][quad // quads_per_group], p, preferred_element_type=F32)
        denom = oa[hd:hd + 1] + jnp.exp2(sink2 - m)
        on = oa[0:hd] * (1.0 / denom)
        for j in range(quad_pairs):
            lo = 2 * j * WINDOW
            o_pair = jnp.concatenate([on[:, lo:lo + WINDOW],
                                      on[:, lo + WINDOW:lo + 2 * WINDOW]], axis=0).T
            col = (pair0 + j) * pair_w
            z_pair = z_ref[blkst["qrows"], col:col + pair_w].astype(F32)
            out_ref[blkst["qrows"], col:col + pair_w] = (
                o_pair * z_pair).astype(out_ref.dtype)

    for first in range(0, n_sub, ATTN_BLOCKS_IN_FLIGHT):
        blocks = [prepare(sub) for sub in range(first, min(first + ATTN_BLOCKS_IN_FLIGHT, n_sub))]
        st_next = [scores(b, 0) for b in blocks]
        for quad in range(n_quads):
            for i, b in enumerate(blocks):
                st = st_next[i]
                if quad + 1 < n_quads:
                    st_next[i] = scores(b, quad + 1)
                finish(b, quad, st)

    @pl.when(blk == pl.num_programs(1) - 1)
    def _():
        last = slice((n_sub - 1) * WINDOW, n_sub * WINDOW)
        kwin_ref[...] = ko_ref[last, :].T
        vwin_ref[...] = vo_ref[last, :].T


def _attn_prompt(qz, kv, sinks, *, batch, seq):
    sub = ATTN_BLOCKS_PER_STEP
    tq = sub * WINDOW
    nb = seq // tq
    assert seq % tq == 0
    row = lambda b, i: b * nb + i
    prev = lambda b, i: (b * nb + i) * sub - jnp.minimum(i, 1)
    win_spec = pl.BlockSpec((None, SWA_KV_WIDTH, WINDOW), lambda b, i: (b, 0, 0))
    win_shape = jax.ShapeDtypeStruct((batch, SWA_KV_WIDTH, WINDOW), F32)
    att, k_win, v_win = pl.pallas_call(
        _attn_prompt_kernel,
        grid=(batch, nb),
        in_specs=[
            pl.BlockSpec(memory_space=pltpu.SMEM),
            pl.BlockSpec((tq, SWA_WIDTH), lambda b, i: (row(b, i), 0)),
            pl.BlockSpec((tq, SWA_WIDTH), lambda b, i: (row(b, i), 1)),
            pl.BlockSpec((WINDOW, SWA_KV_WIDTH), lambda b, i: (prev(b, i), 0)),
            pl.BlockSpec((tq, SWA_KV_WIDTH), lambda b, i: (row(b, i), 0)),
            pl.BlockSpec((WINDOW, SWA_KV_WIDTH), lambda b, i: (prev(b, i), 1)),
            pl.BlockSpec((tq, SWA_KV_WIDTH), lambda b, i: (row(b, i), 1)),
        ],
        out_specs=[pl.BlockSpec((tq, SWA_WIDTH), lambda b, i: (row(b, i), 0)),
                   win_spec, win_spec],
        out_shape=[jax.ShapeDtypeStruct((batch * seq, SWA_WIDTH), BF16), win_shape, win_shape],
        scratch_shapes=[pltpu.VMEM((2, SWA_HEADS // 2, 2 * WINDOW, 2 * WINDOW), F32)],
        compiler_params=_params(("arbitrary", "arbitrary")),
        name="attn_prompt",
    )(sinks, qz, qz, kv, kv, kv, kv)
    to_rows = lambda w: jnp.transpose(
        w.reshape(batch, SWA_KV_HEADS, SWA_HEAD_DIM, WINDOW), (0, 3, 1, 2))
    return att, to_rows(k_win), to_rows(v_win)


def _attn_sample_kernel(sink_ref, slope_ref, q_ref, z_ref, kn_ref, vn_ref, kc_ref, vc_ref,
                        out_ref, kwin_ref, vwin_ref, *, tq, nb):
    hd, nk = SWA_HEAD_DIM, 2 * WINDOW
    rows = SWA_HEADS * tq
    grows = SWA_GROUP * tq
    seqs = range(nb)
    groups = range(SWA_KV_HEADS)

    lane = lax.broadcasted_iota(jnp.int32, (rows, nk), 1)
    tok = lax.broadcasted_iota(jnp.int32, (rows, nk), 0) % tq
    in_buffer = lane < WINDOW
    dist = jnp.where(in_buffer, WINDOW + tok - lane, (nk - tq) + tok - lane)
    allowed = (dist >= 0) & (dist <= WINDOW) & (in_buffer | (lane >= nk - tq))
    penalty = slope_ref[...] * dist.astype(F32)
    new_lanes = lax.broadcasted_iota(jnp.int32, (hd, WINDOW), 1) >= WINDOW - tq

    def new_rows_t(ref, b):
        x = jnp.concatenate([jnp.zeros((WINDOW - tq, SWA_KV_WIDTH), F32),
                             ref[b * tq:(b + 1) * tq, :]], axis=0)
        xt = [x[:, c * WINDOW:(c + 1) * WINDOW].T for c in range(SWA_KV_WIDTH // WINDOW)]
        per_block = WINDOW // hd
        return [xt[g // per_block][(g % per_block) * hd:(g % per_block + 1) * hd]
                for g in groups]

    k_all, v_all = {}, {}
    for b in seqs:
        kn_t, vn_t = new_rows_t(kn_ref, b), new_rows_t(vn_ref, b)
        for g in groups:
            kc, vc = kc_ref[b, g], vc_ref[b, g]
            k_all[b, g] = jnp.concatenate([kc, kn_t[g]], axis=1).astype(BF16)
            v_all[b, g] = jnp.concatenate([vc, vn_t[g]], axis=1).astype(BF16)
            kwin_ref[b, g] = jnp.where(new_lanes, kn_t[g], pltpu.roll(kc, WINDOW - tq, axis=1))
            vwin_ref[b, g] = jnp.where(new_lanes, vn_t[g], pltpu.roll(vc, WINDOW - tq, axis=1))

    s = []
    for b in seqs:
        q = q_ref[b * tq:(b + 1) * tq, :]
        parts = []
        for g in groups:
            qs = jnp.concatenate([q[:, h * hd:(h + 1) * hd]
                                  for h in range(g * SWA_GROUP, (g + 1) * SWA_GROUP)], axis=0)
            parts.append(jnp.dot(qs.astype(BF16), k_all[b, g], preferred_element_type=F32))
        s.append(jnp.concatenate(parts, axis=0))

    sink = sink_ref[...]
    s = [jnp.where(allowed, sb * (hd ** -0.5) - penalty, -jnp.inf) for sb in s]
    m = [jnp.maximum(jnp.max(sb, axis=-1, keepdims=True), sink) for sb in s]
    p = [jnp.exp(s[b] - m[b]) for b in seqs]
    inv = [1.0 / (jnp.sum(p[b], axis=-1, keepdims=True) + jnp.exp(sink - m[b])) for b in seqs]

    for b in seqs:
        pb = p[b].astype(BF16)
        o = jnp.concatenate(
            [lax.dot_general(pb[g * grows:(g + 1) * grows], v_all[b, g], _NT_DIMS,
                             preferred_element_type=F32) for g in groups], axis=0) * inv[b]
        o = jnp.concatenate([o[h * tq:(h + 1) * tq] for h in range(SWA_HEADS)], axis=1)
        z = z_ref[b * tq:(b + 1) * tq, :]
        out_ref[b * tq:(b + 1) * tq, :] = (o * z).astype(out_ref.dtype)


def _attn_sample(qz, kv, cache_k, cache_v, sinks, *, batch, seq):
    nb = SAMPLE_SEQS_PER_STEP
    assert batch % nb == 0
    rows = nb * seq
    sink_col = jnp.repeat(sinks, seq).reshape(SWA_HEADS * seq, 1)
    slope_col = jnp.repeat(jnp.asarray([_alibi_slope(h) for h in range(SWA_HEADS)], F32),
                           seq).reshape(SWA_HEADS * seq, 1)
    col_spec = pl.BlockSpec((SWA_HEADS * seq, 1), lambda i: (0, 0))
    win_spec = pl.BlockSpec((nb, SWA_KV_HEADS, SWA_HEAD_DIM, WINDOW), lambda i: (i, 0, 0, 0))
    win_shape = jax.ShapeDtypeStruct((batch, SWA_KV_HEADS, SWA_HEAD_DIM, WINDOW), F32)
    to_lanes = lambda c: jnp.transpose(c, (0, 2, 3, 1))
    att, k_win, v_win = pl.pallas_call(
        functools.partial(_attn_sample_kernel, tq=seq, nb=nb),
        grid=(batch // nb,),
        in_specs=[
            col_spec,
            col_spec,
            pl.BlockSpec((rows, SWA_WIDTH), lambda i: (i, 0)),
            pl.BlockSpec((rows, SWA_WIDTH), lambda i: (i, 1)),
            pl.BlockSpec((rows, SWA_KV_WIDTH), lambda i: (i, 0)),
            pl.BlockSpec((rows, SWA_KV_WIDTH), lambda i: (i, 1)),
            win_spec,
            win_spec,
        ],
        out_specs=[pl.BlockSpec((rows, SWA_WIDTH), lambda i: (i, 0)), win_spec, win_spec],
        out_shape=[jax.ShapeDtypeStruct((batch * seq, SWA_WIDTH), F32), win_shape, win_shape],
        compiler_params=_params(("parallel",)),
        name="attn_sample",
    )(sink_col, slope_col, qz, qz, kv, kv, to_lanes(cache_k), to_lanes(cache_v))
    return att, jnp.transpose(k_win, (0, 3, 1, 2)), jnp.transpose(v_win, (0, 3, 1, 2))


def kernel(x_prompt, x_sample, state_gla, cache_k_win, cache_v_win, g_norm_a, w_in_a,
           w_gate_up, b_gate, g_onorm_a, w_out_a, g_norm_kv, w_kv, g_norm_b, w_in_b, sinks,
           w_out_b, g_final):
    assert w_in_a.shape[0] == 1 and w_in_b.shape[0] == 1, "one GLA layer, one SWA layer"
    assert cache_k_win.shape[1] == WINDOW
    pb, ps, _ = x_prompt.shape
    sb, ss, _ = x_sample.shape
    pm, sm = pb * ps, sb * ss
    g_norm_a, w_gate_up, b_gate, g_onorm_a = g_norm_a[0], w_gate_up[0], b_gate[0], g_onorm_a[0]
    g_norm_b, sinks = g_norm_b[0], sinks[0]
    w_kv = w_kv.astype(BF16)
    w_in_a_t = jnp.swapaxes(w_in_a, 1, 2)
    w_glow = w_in_a_t[0, GLA_MAIN_COLS:, :]
    h_p = x_prompt.reshape(pm, D_MODEL)
    h_s = x_sample.reshape(sm, D_MODEL)

    proj_s, glow_s, w_in_a_bf16 = _norm_matmul(
        h_s, g_norm_a, w_in_a_t, tm=sm, tn=SAMPLE_COL_TILE, n=GLA_MAIN_COLS, out_dtype=F32,
        w_extra=w_glow, emit_weights=True, w_transposed=True, gate_cols=GLA_VALUE_DIM)
    bcum_s = _gla_gates(glow_s, w_gate_up, b_gate, chunk=ss, tm=sm)
    proj_p, glow_p = _norm_matmul(
        h_p, g_norm_a, w_in_a_bf16, tm=PROMPT_ROW_TILE, tn=PROMPT_COL_TILE, n=GLA_MAIN_COLS,
        out_dtype=BF16, w_extra=w_glow, w_transposed=True, gate_cols=GLA_VALUE_DIM)
    bcum_p, o_s, gla_s = _gla_gates(
        glow_p, w_gate_up, b_gate, chunk=GLA_PROMPT_CHUNK, tm=GATE_ROW_TILE,
        rider=dict(proj=proj_s, bcum=bcum_s, gon=g_onorm_a, batch=sb, seq=ss,
                   s0=state_gla.reshape(state_gla.shape[1:])))
    h_s, w_out_a_bf16 = _matmul_residual(o_s, w_out_a, h_s, tm=sm, tk=SAMPLE_K_TILE,
                                         emit_weights=True)
    o_p, gla_p = _gla(proj_p, bcum_p, g_onorm_a, batch=pb, seq=ps, chunk=GLA_PROMPT_CHUNK,
                      out_dtype=BF16)
    h_p = _matmul_residual(o_p, w_out_a_bf16, h_p, tm=RESIDUAL_ROW_TILE)

    qz_s, kv_s, w_in_b_bf16 = _norm_matmul(
        h_s, g_norm_b, w_in_b, tm=sm, tn=QZ_COL_TILE, out_dtype=F32,
        w_extra=w_kv, g_extra=g_norm_kv, emit_weights=True, gate_cols=SWA_WIDTH)
    att_s, k_s, v_s = _attn_sample(qz_s, kv_s, cache_k_win, cache_v_win, sinks,
                                   batch=sb, seq=ss)
    y_s, w_out_b_bf16 = _matmul_residual(att_s, w_out_b, h_s, tm=sm, tk=SAMPLE_K_TILE,
                                         g_final=g_final, emit_weights=True)
    qz_p, kv_p = _norm_matmul(
        h_p, g_norm_b, w_in_b_bf16, tm=PROMPT_ROW_TILE, tn=QZ_COL_TILE, out_dtype=BF16,
        w_extra=w_kv, g_extra=g_norm_kv, gate_cols=SWA_WIDTH)
    att_p, k_p, v_p = _attn_prompt(qz_p, kv_p, sinks, batch=pb, seq=ps)
    y_p = _matmul_residual(att_p, w_out_b_bf16, h_p, tm=RESIDUAL_ROW_TILE, g_final=g_final)

    return (y_p.reshape(pb, ps, D_MODEL), y_s.reshape(sb, ss, D_MODEL), gla_p[None],
            gla_s[None], k_p, v_p, k_s, v_s)
```

```python
import functools

import jax
import jax.numpy as jnp
from jax import lax
from jax.experimental import pallas as pl
from jax.experimental.pallas import tpu as pltpu

F32 = jnp.float32
BF16 = jnp.bfloat16

D_MODEL = 2048
GLA_HEADS = 4
GLA_KEY_DIM = D_MODEL // 2
GLA_VALUE_DIM = D_MODEL
GLA_DK = GLA_KEY_DIM // GLA_HEADS
GLA_DV = GLA_VALUE_DIM // GLA_HEADS
GLA_GATE_RANK = 16
GLA_GATE_TEMP = 16.0
GLA_MAIN_COLS = 2 * GLA_KEY_DIM + 2 * GLA_VALUE_DIM
SWA_HEAD_DIM = 64
SWA_HEADS = D_MODEL // SWA_HEAD_DIM
SWA_KV_HEADS = 4
SWA_GROUP = SWA_HEADS // SWA_KV_HEADS
SWA_WIDTH = SWA_HEADS * SWA_HEAD_DIM
SWA_KV_WIDTH = SWA_KV_HEADS * SWA_HEAD_DIM
WINDOW = 128
RMS_EPS = 1e-6
LOG2E = 1.4426950408889634

V7X_VMEM_LIMIT_BYTES = 61 * 1024 * 1024
GLA_PROMPT_CHUNK = 128
GLA_SEQS_PER_STEP = 4
ATTN_BLOCKS_PER_STEP = 4
ATTN_BLOCKS_IN_FLIGHT = 2
SAMPLE_SEQS_PER_STEP = 8
NORM_ROW_CHUNK = 256
GATE_CUMSUM_ROWS = 128
GATE_ROW_TILE = 1024
PROMPT_ROW_TILE = 1024
PROMPT_COL_TILE = 2048
SAMPLE_COL_TILE = 1024
QZ_COL_TILE = 2048
SAMPLE_K_TILE = 1024
RESIDUAL_ROW_TILE = 512

_NT_DIMS = (((1,), (1,)), ((), ()))
_TN_DIMS = (((0,), (0,)), ((), ()))


def _params(semantics):
    return pltpu.CompilerParams(dimension_semantics=semantics,
                                vmem_limit_bytes=V7X_VMEM_LIMIT_BYTES)


def _silu(x):
    hx = 0.5 * x
    return hx + hx * jnp.tanh(hx)


def _rms_scale(x):
    return lax.rsqrt(jnp.mean(x * x, axis=-1, keepdims=True) + RMS_EPS)


def _norm_matmul_kernel(x_ref, g_ref, w_ref, *rest, has_extra, extra_gain, emit_weights,
                        w_transposed, silu_from):
    dims = _NT_DIMS if w_transposed else (((1,), (0,)), ((), ()))

    def mm(a, b):
        return lax.dot_general(a, b, dims, preferred_element_type=F32)

    rest = list(rest)
    g2_ref = rest.pop(0) if extra_gain else None
    w2_ref = rest.pop(0) if has_extra else None
    o_ref = rest.pop(0)
    o2_ref = rest.pop(0) if has_extra else None
    wb_ref = rest.pop(0) if emit_weights else None
    (xn_ref,) = rest

    first = pl.program_id(1) == 0

    def weights():
        w = w_ref[...].astype(BF16)
        if emit_weights:
            wb_ref[...] = w
        return w

    @pl.when(first)
    def _():
        tm = x_ref.shape[0]
        rc = min(tm, NORM_ROW_CHUNK)
        w = weights()
        for c in range(tm // rc):
            rows = slice(c * rc, (c + 1) * rc)
            x = x_ref[rows, :]
            xs = x * _rms_scale(x)
            xn = (xs * g_ref[...]).astype(BF16)
            xn_ref[rows, :] = xn
            o_ref[rows, :] = mm(xn, w).astype(o_ref.dtype)
            if has_extra:
                xn2 = (xs * g2_ref[...]).astype(BF16) if extra_gain else xn
                o2_ref[rows, :] = mm(xn2, w2_ref[...].astype(BF16)).astype(o2_ref.dtype)

    def later_block(gated):
        w = weights()
        if not gated:
            o_ref[...] = mm(xn_ref[...], w).astype(o_ref.dtype)
            return
        tm = x_ref.shape[0]
        rc = min(tm, NORM_ROW_CHUNK)
        for c in range(tm // rc):
            rows = slice(c * rc, (c + 1) * rc)
            o_ref[rows, :] = _silu(mm(xn_ref[rows, :], w)).astype(o_ref.dtype)

    j = pl.program_id(1)
    if silu_from is None or silu_from == 1:
        pl.when(j >= 1)(functools.partial(later_block, silu_from == 1))
    else:
        pl.when((j >= 1) & (j < silu_from))(functools.partial(later_block, False))
        pl.when(j >= silu_from)(functools.partial(later_block, True))


def _norm_matmul(x, g, w, *, tm, tn, out_dtype, n=None, w_extra=None, g_extra=None,
                 emit_weights=False, w_transposed=False, gate_cols=0):
    m, k = x.shape
    n_axis = -2 if w_transposed else -1
    n = w.shape[n_axis] if n is None else n
    assert n % tn == 0 and m % tm == 0
    assert not emit_weights or m == tm, "each weight block must be visited exactly once"
    assert gate_cols % tn == 0 and gate_cols <= n - tn, "gate columns: whole blocks after block 0"
    silu_from = (n - gate_cols) // tn if gate_cols else None
    grid = (m // tm, n // tn)
    w_block = (tn, k) if w_transposed else (k, tn)
    w_index = (lambda i, j: (j, 0)) if w_transposed else (lambda i, j: (0, j))
    if w.ndim == 3:
        w_spec = pl.BlockSpec((None,) + w_block, lambda i, j: (0,) + w_index(i, j))
    else:
        w_spec = pl.BlockSpec(w_block, w_index)
    in_specs = [
        pl.BlockSpec((tm, k), lambda i, j: (i, 0)),
        pl.BlockSpec((1, k), lambda i, j: (0, 0)),
        w_spec,
    ]
    out_shape = [jax.ShapeDtypeStruct((m, n), out_dtype)]
    out_specs = [pl.BlockSpec((tm, tn), lambda i, j: (i, j))]
    args = [x, g.reshape(1, k), w]
    if g_extra is not None:
        in_specs.append(pl.BlockSpec((1, k), lambda i, j: (0, 0)))
        args.append(g_extra.reshape(1, k))
    if w_extra is not None:
        n2 = w_extra.shape[n_axis]
        in_specs.append(pl.BlockSpec(w_extra.shape, lambda i, j: (0, 0)))
        out_shape.append(jax.ShapeDtypeStruct((m, n2), F32))
        out_specs.append(pl.BlockSpec((tm, n2), lambda i, j: (i, 0)))
        args.append(w_extra)
    if emit_weights:
        out_shape.append(jax.ShapeDtypeStruct((n, k) if w_transposed else (k, n), BF16))
        out_specs.append(pl.BlockSpec(w_block, w_index))
    res = pl.pallas_call(
        functools.partial(_norm_matmul_kernel, has_extra=w_extra is not None,
                          extra_gain=g_extra is not None, emit_weights=emit_weights,
                          w_transposed=w_transposed, silu_from=silu_from),
        grid=grid,
        in_specs=in_specs,
        out_specs=out_specs,
        out_shape=out_shape,
        scratch_shapes=[pltpu.VMEM((tm, k), BF16)],
        compiler_params=_params(("parallel", "arbitrary")),
        name=f"norm_matmul_{m}x{n}",
    )(*args)
    return res if len(res) > 1 else res[0]


def _matmul_residual_kernel(a_ref, w_ref, r_ref, *rest, final_norm, emit_weights, nk):
    rest = list(rest)
    g_ref = rest.pop(0) if final_norm else None
    o_ref = rest.pop(0)
    w = w_ref[...].astype(BF16)
    if emit_weights:
        rest[0][...] = w
    part = jnp.dot(a_ref[...].astype(BF16), w, preferred_element_type=F32)

    if nk == 1:
        h = r_ref[...] + part
        if final_norm:
            h = h * _rms_scale(h) * g_ref[...]
        o_ref[...] = h
        return

    kk = pl.program_id(1)

    @pl.when(kk == 0)
    def _():
        o_ref[...] = r_ref[...] + part

    @pl.when(kk > 0)
    def _():
        o_ref[...] += part

    if final_norm:
        @pl.when(kk == nk - 1)
        def _():
            h = o_ref[...]
            o_ref[...] = h * _rms_scale(h) * g_ref[...]


def _matmul_residual(a, w, res, *, tm, tk=None, g_final=None, emit_weights=False):
    m, k = a.shape
    n = w.shape[-1]
    tk = k if tk is None else tk
    assert m % tm == 0 and k % tk == 0
    assert not emit_weights or m == tm, "each weight block must be visited exactly once"
    if w.ndim == 3:
        w_spec = pl.BlockSpec((None, tk, n), lambda i, j: (0, j, 0))
    else:
        w_spec = pl.BlockSpec((tk, n), lambda i, j: (j, 0))
    in_specs = [
        pl.BlockSpec((tm, tk), lambda i, j: (i, j)),
        w_spec,
        pl.BlockSpec((tm, n), lambda i, j: (i, 0)),
    ]
    args = [a, w, res]
    if g_final is not None:
        in_specs.append(pl.BlockSpec((1, n), lambda i, j: (0, 0)))
        args.append(g_final.reshape(1, n))
    out_shape = [jax.ShapeDtypeStruct((m, n), F32)]
    out_specs = [pl.BlockSpec((tm, n), lambda i, j: (i, 0))]
    if emit_weights:
        out_shape.append(jax.ShapeDtypeStruct((k, n), BF16))
        out_specs.append(pl.BlockSpec((tk, n), lambda i, j: (j, 0)))
    res = pl.pallas_call(
        functools.partial(_matmul_residual_kernel, final_norm=g_final is not None,
                          emit_weights=emit_weights, nk=k // tk),
        grid=(m // tm, k // tk),
        in_specs=in_specs,
        out_specs=out_specs,
        out_shape=out_shape,
        compiler_params=_params(("parallel", "arbitrary")),
        name=f"matmul_residual_{m}" + ("_final" if g_final is not None else ""),
    )(*args)
    return res if emit_weights else res[0]


def _split_bf16(x):
    hi = x.astype(BF16)
    return hi, (x - hi.astype(F32)).astype(BF16)


def _gla_gates_kernel(glow_ref, wg_ref, bg_ref, bcum_ref, *, chunk):
    rows = glow_ref.shape[0]
    g_hi, g_lo = _split_bf16(glow_ref[...])
    w_hi, w_lo = _split_bf16(wg_ref[...])
    x = jnp.dot(jnp.concatenate([g_hi, g_lo, g_hi], axis=1),
                jnp.concatenate([w_hi, w_hi, w_lo], axis=0),
                preferred_element_type=F32) + bg_ref[...]
    softplus2 = jnp.log2(1.0 + jnp.exp2(jnp.abs(x) * (-LOG2E)))
    logg = jnp.minimum(x, 0.0) * (LOG2E / GLA_GATE_TEMP) - softplus2 * (1.0 / GLA_GATE_TEMP)

    span = GATE_CUMSUM_ROWS
    row = lax.broadcasted_iota(jnp.int32, (span, span), 0)
    col = lax.broadcasted_iota(jnp.int32, (span, span), 1)
    same_chunk = (row // chunk) == (col // chunk) if chunk < span else True
    tril = jnp.where((col <= row) & same_chunk, 1.0, 0.0).astype(BF16)
    tril2 = jnp.concatenate([tril, tril], axis=1)
    for i in range(rows // span):
        hi, lo = _split_bf16(logg[i * span:(i + 1) * span])
        bcum_ref[i * span:(i + 1) * span, :] = jnp.dot(
            tril2, jnp.concatenate([hi, lo], axis=0), preferred_element_type=F32)


def _gates_with_gla_kernel(glow_ref, wg_ref, bg_ref, *rest, chunk, n_gla_in):
    gla_in, (bcum_ref, o_ref, sfin_ref) = rest[:n_gla_in], rest[n_gla_in:]
    _gla_kernel(*gla_in, o_ref, sfin_ref, has_init=True, n_chunks=1)
    _gla_gates_kernel(glow_ref, wg_ref, bg_ref, bcum_ref, chunk=chunk)


def _gla_gates(glow, wg, bg, *, chunk, tm, rider=None):
    m = glow.shape[0]
    assert m % tm == 0 and tm % GATE_CUMSUM_ROWS == 0
    assert GATE_CUMSUM_ROWS % chunk == 0 or chunk % GATE_CUMSUM_ROWS == 0
    assert chunk <= GATE_CUMSUM_ROWS, "cumulative sums do not cross row spans"
    steps = m // tm
    in_specs = [
        pl.BlockSpec((tm, GLA_GATE_RANK), lambda i: (i, 0)),
        pl.BlockSpec((GLA_GATE_RANK, GLA_KEY_DIM), lambda i: (0, 0)),
        pl.BlockSpec((1, GLA_KEY_DIM), lambda i: (0, 0)),
    ]
    args = [glow, wg, bg.reshape(1, -1)]
    out_specs = [pl.BlockSpec((tm, GLA_KEY_DIM), lambda i: (i, 0))]
    out_shape = [jax.ShapeDtypeStruct((m, GLA_KEY_DIM), F32)]
    if rider is None:
        body = functools.partial(_gla_gates_kernel, chunk=chunk)
    else:
        batch, seq = rider["batch"], rider["seq"]
        assert batch % steps == 0
        g_in, g_args, g_out, g_shape = _gla_operands(
            rider["proj"], rider["bcum"], rider["gon"], rider["s0"], batch=batch, seq=seq,
            chunk=seq, ns=batch // steps, out_dtype=F32, index=lambda i: (i, 0))
        body = functools.partial(_gates_with_gla_kernel, chunk=chunk, n_gla_in=len(g_in))
        in_specs += g_in
        args += g_args
        out_specs += g_out
        out_shape += g_shape
    res = pl.pallas_call(
        body,
        grid=(steps,),
        in_specs=in_specs,
        out_specs=out_specs,
        out_shape=out_shape,
        compiler_params=_params(("parallel",)),
        name=f"gla_gates_{m}",
    )(*args)
    if rider is None:
        return res[0]
    bcum, o, s_fin = res
    return bcum, o.reshape(rider["batch"] * rider["seq"], GLA_VALUE_DIM), s_fin


def _gla_kernel(q_ref, k_ref, v_ref, r_ref, bcum_ref, gon_ref, *rest, has_init, n_chunks):
    rest = list(rest)
    s0_ref = rest.pop(0) if has_init else None
    o_ref, sfin_ref = rest[:2]
    s_ref = rest[2] if n_chunks > 1 else None
    c = pl.program_id(1) if n_chunks > 1 else None
    nseq, chunk = q_ref.shape[:2]
    chains = [(s, h) for s in range(nseq) for h in range(GLA_HEADS)]

    if n_chunks > 1:
        @pl.when(c == 0)
        def _():
            if has_init:
                s_ref[...] = s0_ref[...]
            else:
                s_ref[...] = jnp.zeros_like(s_ref)

    row = lax.broadcasted_iota(jnp.int32, (chunk, chunk), 0)
    col = lax.broadcasted_iota(jnp.int32, (chunk, chunk), 1)
    causal = col <= row
    heads = range(GLA_HEADS)
    ks = [slice(h * GLA_DK, (h + 1) * GLA_DK) for h in heads]
    vs = [slice(h * GLA_DV, (h + 1) * GLA_DV) for h in heads]

    def state(s, h):
        if n_chunks > 1:
            return s_ref[s, h]
        return s0_ref[s, h] if has_init else jnp.zeros((GLA_DK, GLA_DV), F32)

    q_inter, k_state, scores, decay = {}, {}, {}, {}
    for s, h in chains:
        b = bcum_ref[s, :, ks[h]]
        b_last = b[chunk - 1:chunk, :]
        b_mid = b[chunk // 2 - 1:chunk // 2, :]
        q = q_ref[s, :, ks[h]].astype(F32)
        k = k_ref[s, :, ks[h]].astype(F32)
        q_inter[s, h] = (q * jnp.exp2(b)).astype(BF16)
        q_intra = (q * jnp.exp2(b - b_mid)).astype(BF16)
        k_intra = (k * jnp.exp2(b_mid - b)).astype(BF16)
        k_state[s, h] = (k * jnp.exp2(b_last - b)).astype(BF16)
        scores[s, h] = lax.dot_general(q_intra, k_intra, _NT_DIMS,
                                       preferred_element_type=F32)
        decay[s, h] = jnp.exp2(jnp.broadcast_to(b_last, (128, GLA_DK))).T

    o = {}
    for s, h in chains:
        sc = jnp.where(causal, scores[s, h], 0.0).astype(BF16)
        v = v_ref[s, :, vs[h]].astype(BF16)
        o[s, h] = (jnp.dot(q_inter[s, h], state(s, h).astype(BF16),
                           preferred_element_type=F32)
                   + jnp.dot(sc, v, preferred_element_type=F32))
        upd = lax.dot_general(k_state[s, h], v, _TN_DIMS, preferred_element_type=F32)
        s_new = (state(s, h) * jnp.concatenate([decay[s, h]] * (GLA_DV // 128), axis=1)
                 + upd)
        if n_chunks > 1:
            s_ref[s, h] = s_new
        else:
            sfin_ref[s, h] = s_new

    qs = GLA_DK ** -0.5
    scale = {sh: qs * lax.rsqrt(jnp.mean(o[sh] * o[sh], axis=-1, keepdims=True) * (qs * qs)
                                + RMS_EPS) for sh in chains}
    for s, h in chains:
        r = r_ref[s, :, vs[h]].astype(F32)
        o_ref[s, :, vs[h]] = (o[s, h] * scale[s, h] * gon_ref[:, vs[h]]
                              * r).astype(o_ref.dtype)

    if n_chunks > 1:
        @pl.when(c == n_chunks - 1)
        def _():
            sfin_ref[...] = s_ref[...]


def _gla_operands(proj, bcum, gon, s0, *, batch, seq, chunk, ns, out_dtype, index):
    assert batch % ns == 0 and seq % chunk == 0
    proj = proj.reshape(batch, seq, proj.shape[-1])
    bcum = bcum.reshape(batch, seq, bcum.shape[-1])
    kb, vb, rb = 1, 2 * GLA_KEY_DIM // GLA_VALUE_DIM, 2 * GLA_KEY_DIM // GLA_VALUE_DIM + 1

    def rows(width, col):
        return pl.BlockSpec((ns, chunk, width), lambda *g: (*index(*g), col))

    state_spec = pl.BlockSpec((ns, GLA_HEADS, GLA_DK, GLA_DV),
                              lambda *g: (index(*g)[0], 0, 0, 0))
    in_specs = [rows(GLA_KEY_DIM, 0), rows(GLA_KEY_DIM, kb), rows(GLA_VALUE_DIM, vb),
                rows(GLA_VALUE_DIM, rb), rows(GLA_KEY_DIM, 0),
                pl.BlockSpec((1, GLA_VALUE_DIM), lambda *g: (0, 0))]
    args = [proj, proj, proj, proj, bcum, gon.reshape(1, -1)]
    if s0 is not None:
        in_specs.append(state_spec)
        args.append(s0)
    out_specs = [rows(GLA_VALUE_DIM, 0), state_spec]
    out_shape = [jax.ShapeDtypeStruct((batch, seq, GLA_VALUE_DIM), out_dtype),
                 jax.ShapeDtypeStruct((batch, GLA_HEADS, GLA_DK, GLA_DV), F32)]
    return in_specs, args, out_specs, out_shape


def _gla(proj, bcum, gon, *, batch, seq, chunk, out_dtype, s0=None):
    n = seq // chunk
    ns = GLA_SEQS_PER_STEP
    in_specs, args, out_specs, out_shape = _gla_operands(
        proj, bcum, gon, s0, batch=batch, seq=seq, chunk=chunk, ns=ns, out_dtype=out_dtype,
        index=lambda b, c: (b, c))
    scratch = [pltpu.VMEM((ns, GLA_HEADS, GLA_DK, GLA_DV), F32)] if n > 1 else []
    o, s_fin = pl.pallas_call(
        functools.partial(_gla_kernel, has_init=s0 is not None, n_chunks=n),
        grid=(batch // ns, n),
        in_specs=in_specs,
        out_specs=out_specs,
        out_shape=out_shape,
        scratch_shapes=scratch,
        compiler_params=_params(("parallel", "arbitrary")),
        name=f"gla_chunk{chunk}",
    )(*args)
    return o.reshape(batch * seq, GLA_VALUE_DIM), s_fin


def _alibi_slope(head):
    return 2.0 ** (-8.0 * (head + 1) / SWA_HEADS)


def _attn_prompt_kernel(sinks_ref, q_ref, z_ref, kp_ref, ko_ref, vp_ref, vo_ref, out_ref,
                        kwin_ref, vwin_ref, bias_ref):
    hd, nkeys = SWA_HEAD_DIM, 2 * WINDOW
    pair_w = 2 * hd
    pairs_per_group = SWA_GROUP // 2
    blk = pl.program_id(1)

    @pl.when((pl.program_id(0) == 0) & (blk == 0))
    def _():
        kj = lax.broadcasted_iota(jnp.int32, (nkeys, WINDOW), 0)
        qi = lax.broadcasted_iota(jnp.int32, (nkeys, WINDOW), 1)
        dist = WINDOW + qi - kj
        ok = (dist >= 0) & (dist <= WINDOW)
        ok_first = ok & (kj >= WINDOW)
        distf = dist.astype(F32)
        for h in range(SWA_HEADS):
            pen = (-_alibi_slope(h) * LOG2E) * distf
            sl = slice((h % 2) * WINDOW, (h % 2 + 1) * WINDOW)
            bias_ref[0, h // 2, :, sl] = jnp.where(ok_first, pen, -jnp.inf)
            bias_ref[1, h // 2, :, sl] = jnp.where(ok, pen, -jnp.inf)

    n_sub = q_ref.shape[0] // WINDOW
    k_rows = jnp.concatenate([kp_ref[...], ko_ref[...]], axis=0)
    v_rows = jnp.concatenate([vp_ref[...], vo_ref[...]], axis=0)
    ones = jnp.ones((16, nkeys), F32)
    lane = lax.broadcasted_iota(jnp.int32, (nkeys, pair_w), 1)
    qlane = lax.broadcasted_iota(jnp.int32, (WINDOW, pair_w), 1)
    qk_scale = (hd ** -0.5) * LOG2E
    quad_pairs = 2
    n_quads = SWA_HEADS // (2 * quad_pairs)
    quads_per_group = pairs_per_group // quad_pairs
    quarter = lax.broadcasted_iota(jnp.int32, (1, 2 * quad_pairs * WINDOW), 1) // WINDOW

    def prepare(sub):
        k = k_rows[sub * WINDOW:sub * WINDOW + nkeys]
        vt = v_rows[sub * WINDOW:sub * WINDOW + nkeys].T
        k2, vt1 = [], []
        for g in range(SWA_KV_HEADS):
            kblk = k[:, (g // 2) * pair_w:(g // 2 + 1) * pair_w]
            k_here = jnp.where((lane < hd) if g % 2 == 0 else (lane >= hd), kblk, 0.0)
            k2.append((k_here + pltpu.roll(k_here, hd, axis=1)).astype(BF16))
            vt1.append(jnp.concatenate([vt[g * hd:(g + 1) * hd], ones], axis=0).astype(BF16))
        return dict(qrows=slice(sub * WINDOW, (sub + 1) * WINDOW), k2=k2, vt1=vt1,
                    tbl=jnp.minimum(blk, 1) if sub == 0 else 1)

    def scores(blkst, quad):
        parts = []
        for j in range(quad_pairs):
            col = (quad * quad_pairs + j) * pair_w
            q_pair = q_ref[blkst["qrows"], col:col + pair_w]
            zero = jnp.zeros_like(q_pair)
            parts += [jnp.where(qlane < hd, q_pair, zero), jnp.where(qlane >= hd, q_pair, zero)]
        return lax.dot_general(blkst["k2"][quad // quads_per_group],
                               jnp.concatenate(parts, axis=0),
                               _NT_DIMS, preferred_element_type=F32)

    def finish(blkst, quad, st):
        pair0 = quad * quad_pairs
        bias = jnp.concatenate([bias_ref[blkst["tbl"], pair0 + j] for j in range(quad_pairs)],
                               axis=1)
        s2 = st * qk_scale + bias
        sink2 = sinks_ref[2 * pair0]
        for t in range(1, 2 * quad_pairs):
            sink2 = jnp.where(quarter == t, sinks_ref[2 * pair0 + t], sink2)
        sink2 = sink2 * LOG2E
        m = jnp.maximum(jnp.max(s2, axis=0, keepdims=True), sink2)
        p = jnp.exp2(s2 - m).astype(BF16)
        oa = jnp.dot(blkst["vt1"][quad // quads_per_group], p, preferred_element_type=F32)
        denom = oa[hd:hd + 1] + jnp.exp2(sink2 - m)
        on = oa[0:hd] * (1.0 / denom)
        for j in range(quad_pairs):
            lo = 2 * j * WINDOW
            o_pair = jnp.concatenate([on[:, lo:lo + WINDOW],
                                      on[:, lo + WINDOW:lo + 2 * WINDOW]], axis=0).T
            col = (pair0 + j) * pair_w
            z_pair = z_ref[blkst["qrows"], col:col + pair_w].astype(F32)
            out_ref[blkst["qrows"], col:col + pair_w] = (
                o_pair * z_pair).astype(out_ref.dtype)

    for first in range(0, n_sub, ATTN_BLOCKS_IN_FLIGHT):
        blocks = [prepare(sub) for sub in range(first, min(first + ATTN_BLOCKS_IN_FLIGHT, n_sub))]
        st_next = [scores(b, 0) for b in blocks]
        for quad in range(n_quads):
            for i, b in enumerate(blocks):
                st = st_next[i]
                if quad + 1 < n_quads:
                    st_next[i] = scores(b, quad + 1)
                finish(b, quad, st)

    @pl.when(blk == pl.num_programs(1) - 1)
    def _():
        last = slice((n_sub - 1) * WINDOW, n_sub * WINDOW)
        kwin_ref[...] = ko_ref[last, :].T
        vwin_ref[...] = vo_ref[last, :].T


def _attn_prompt(qz, kv, sinks, *, batch, seq):
    sub = ATTN_BLOCKS_PER_STEP
    tq = sub * WINDOW
    nb = seq // tq
    assert seq % tq == 0
    row = lambda b, i: b * nb + i
    prev = lambda b, i: (b * nb + i) * sub - jnp.minimum(i, 1)
    win_spec = pl.BlockSpec((None, SWA_KV_WIDTH, WINDOW), lambda b, i: (b, 0, 0))
    win_shape = jax.ShapeDtypeStruct((batch, SWA_KV_WIDTH, WINDOW), F32)
    att, k_win, v_win = pl.pallas_call(
        _attn_prompt_kernel,
        grid=(batch, nb),
        in_specs=[
            pl.BlockSpec(memory_space=pltpu.SMEM),
            pl.BlockSpec((tq, SWA_WIDTH), lambda b, i: (row(b, i), 0)),
            pl.BlockSpec((tq, SWA_WIDTH), lambda b, i: (row(b, i), 1)),
            pl.BlockSpec((WINDOW, SWA_KV_WIDTH), lambda b, i: (prev(b, i), 0)),
            pl.BlockSpec((tq, SWA_KV_WIDTH), lambda b, i: (row(b, i), 0)),
            pl.BlockSpec((WINDOW, SWA_KV_WIDTH), lambda b, i: (prev(b, i), 1)),
            pl.BlockSpec((tq, SWA_KV_WIDTH), lambda b, i: (row(b, i), 1)),
        ],
        out_specs=[pl.BlockSpec((tq, SWA_WIDTH), lambda b, i: (row(b, i), 0)),
                   win_spec, win_spec],
        out_shape=[jax.ShapeDtypeStruct((batch * seq, SWA_WIDTH), BF16), win_shape, win_shape],
        scratch_shapes=[pltpu.VMEM((2, SWA_HEADS // 2, 2 * WINDOW, 2 * WINDOW), F32)],
        compiler_params=_params(("arbitrary", "arbitrary")),
        name="attn_prompt",
    )(sinks, qz, qz, kv, kv, kv, kv)
    to_rows = lambda w: jnp.transpose(
        w.reshape(batch, SWA_KV_HEADS, SWA_HEAD_DIM, WINDOW), (0, 3, 1, 2))
    return att, to_rows(k_win), to_rows(v_win)


def _attn_sample_kernel(sink_ref, slope_ref, q_ref, z_ref, kn_ref, vn_ref, kc_ref, vc_ref,
                        out_ref, kwin_ref, vwin_ref, *, tq, nb):
    hd, nk = SWA_HEAD_DIM, 2 * WINDOW
    rows = SWA_HEADS * tq
    grows = SWA_GROUP * tq
    seqs = range(nb)
    groups = range(SWA_KV_HEADS)

    lane = lax.broadcasted_iota(jnp.int32, (rows, nk), 1)
    tok = lax.broadcasted_iota(jnp.int32, (rows, nk), 0) % tq
    in_buffer = lane < WINDOW
    dist = jnp.where(in_buffer, WINDOW + tok - lane, (nk - tq) + tok - lane)
    allowed = (dist >= 0) & (dist <= WINDOW) & (in_buffer | (lane >= nk - tq))
    penalty = slope_ref[...] * dist.astype(F32)
    new_lanes = lax.broadcasted_iota(jnp.int32, (hd, WINDOW), 1) >= WINDOW - tq

    def new_rows_t(ref, b):
        x = jnp.concatenate([jnp.zeros((WINDOW - tq, SWA_KV_WIDTH), F32),
                             ref[b * tq:(b + 1) * tq, :]], axis=0)
        xt = [x[:, c * WINDOW:(c + 1) * WINDOW].T for c in range(SWA_KV_WIDTH // WINDOW)]
        per_block = WINDOW // hd
        return [xt[g // per_block][(g % per_block) * hd:(g % per_block + 1) * hd]
                for g in groups]

    k_all, v_all = {}, {}
    for b in seqs:
        kn_t, vn_t = new_rows_t(kn_ref, b), new_rows_t(vn_ref, b)
        for g in groups:
            kc, vc = kc_ref[b, g], vc_ref[b, g]
            k_all[b, g] = jnp.concatenate([kc, kn_t[g]], axis=1).astype(BF16)
            v_all[b, g] = jnp.concatenate([vc, vn_t[g]], axis=1).astype(BF16)
            kwin_ref[b, g] = jnp.where(new_lanes, kn_t[g], pltpu.roll(kc, WINDOW - tq, axis=1))
            vwin_ref[b, g] = jnp.where(new_lanes, vn_t[g], pltpu.roll(vc, WINDOW - tq, axis=1))

    s = []
    for b in seqs:
        q = q_ref[b * tq:(b + 1) * tq, :]
        parts = []
        for g in groups:
            qs = jnp.concatenate([q[:, h * hd:(h + 1) * hd]
                                  for h in range(g * SWA_GROUP, (g + 1) * SWA_GROUP)], axis=0)
            parts.append(jnp.dot(qs.astype(BF16), k_all[b, g], preferred_element_type=F32))
        s.append(jnp.concatenate(parts, axis=0))

    sink = sink_ref[...]
    s = [jnp.where(allowed, sb * (hd ** -0.5) - penalty, -jnp.inf) for sb in s]
    m = [jnp.maximum(jnp.max(sb, axis=-1, keepdims=True), sink) for sb in s]
    p = [jnp.exp(s[b] - m[b]) for b in seqs]
    inv = [1.0 / (jnp.sum(p[b], axis=-1, keepdims=True) + jnp.exp(sink - m[b])) for b in seqs]

    for b in seqs:
        pb = p[b].astype(BF16)
        o = jnp.concatenate(
            [lax.dot_general(pb[g * grows:(g + 1) * grows], v_all[b, g], _NT_DIMS,
                             preferred_element_type=F32) for g in groups], axis=0) * inv[b]
        o = jnp.concatenate([o[h * tq:(h + 1) * tq] for h in range(SWA_HEADS)], axis=1)
        z = z_ref[b * tq:(b + 1) * tq, :]
        out_ref[b * tq:(b + 1) * tq, :] = (o * z).astype(out_ref.dtype)


def _attn_sample(qz, kv, cache_k, cache_v, sinks, *, batch, seq):
    nb = SAMPLE_SEQS_PER_STEP
    assert batch % nb == 0
    rows = nb * seq
    sink_col = jnp.repeat(sinks, seq).reshape(SWA_HEADS * seq, 1)
    slope_col = jnp.repeat(jnp.asarray([_alibi_slope(h) for h in range(SWA_HEADS)], F32),
                           seq).reshape(SWA_HEADS * seq, 1)
    col_spec = pl.BlockSpec((SWA_HEADS * seq, 1), lambda i: (0, 0))
    win_spec = pl.BlockSpec((nb, SWA_KV_HEADS, SWA_HEAD_DIM, WINDOW), lambda i: (i, 0, 0, 0))
    win_shape = jax.ShapeDtypeStruct((batch, SWA_KV_HEADS, SWA_HEAD_DIM, WINDOW), F32)
    to_lanes = lambda c: jnp.transpose(c, (0, 2, 3, 1))
    att, k_win, v_win = pl.pallas_call(
        functools.partial(_attn_sample_kernel, tq=seq, nb=nb),
        grid=(batch // nb,),
        in_specs=[
            col_spec,
            col_spec,
            pl.BlockSpec((rows, SWA_WIDTH), lambda i: (i, 0)),
            pl.BlockSpec((rows, SWA_WIDTH), lambda i: (i, 1)),
            pl.BlockSpec((rows, SWA_KV_WIDTH), lambda i: (i, 0)),
            pl.BlockSpec((rows, SWA_KV_WIDTH), lambda i: (i, 1)),
            win_spec,
            win_spec,
        ],
        out_specs=[pl.BlockSpec((rows, SWA_WIDTH), lambda i: (i, 0)), win_spec, win_spec],
        out_shape=[jax.ShapeDtypeStruct((batch * seq, SWA_WIDTH), F32), win_shape, win_shape],
        compiler_params=_params(("parallel",)),
        name="attn_sample",
    )(sink_col, slope_col, qz, qz, kv, kv, to_lanes(cache_k), to_lanes(cache_v))
    return att, jnp.transpose(k_win, (0, 3, 1, 2)), jnp.transpose(v_win, (0, 3, 1, 2))


def kernel(x_prompt, x_sample, state_gla, cache_k_win, cache_v_win, g_norm_a, w_in_a,
           w_gate_up, b_gate, g_onorm_a, w_out_a, g_norm_kv, w_kv, g_norm_b, w_in_b, sinks,
           w_out_b, g_final):
    assert w_in_a.shape[0] == 1 and w_in_b.shape[0] == 1, "one GLA layer, one SWA layer"
    assert cache_k_win.shape[1] == WINDOW
    pb, ps, _ = x_prompt.shape
    sb, ss, _ = x_sample.shape
    pm, sm = pb * ps, sb * ss
    g_norm_a, w_gate_up, b_gate, g_onorm_a = g_norm_a[0], w_gate_up[0], b_gate[0], g_onorm_a[0]
    g_norm_b, sinks = g_norm_b[0], sinks[0]
    w_kv = w_kv.astype(BF16)
    w_in_a_t = jnp.swapaxes(w_in_a, 1, 2)
    w_glow = w_in_a_t[0, GLA_MAIN_COLS:, :]
    h_p = x_prompt.reshape(pm, D_MODEL)
    h_s = x_sample.reshape(sm, D_MODEL)

    proj_s, glow_s, w_in_a_bf16 = _norm_matmul(
        h_s, g_norm_a, w_in_a_t, tm=sm, tn=SAMPLE_COL_TILE, n=GLA_MAIN_COLS, out_dtype=F32,
        w_extra=w_glow, emit_weights=True, w_transposed=True, gate_cols=GLA_VALUE_DIM)
    bcum_s = _gla_gates(glow_s, w_gate_up, b_gate, chunk=ss, tm=sm)
    proj_p, glow_p = _norm_matmul(
        h_p, g_norm_a, w_in_a_bf16, tm=PROMPT_ROW_TILE, tn=PROMPT_COL_TILE, n=GLA_MAIN_COLS,
        out_dtype=BF16, w_extra=w_glow, w_transposed=True, gate_cols=GLA_VALUE_DIM)
    bcum_p, o_s, gla_s = _gla_gates(
        glow_p, w_gate_up, b_gate, chunk=GLA_PROMPT_CHUNK, tm=GATE_ROW_TILE,
        rider=dict(proj=proj_s, bcum=bcum_s, gon=g_onorm_a, batch=sb, seq=ss,
                   s0=state_gla.reshape(state_gla.shape[1:])))
    h_s, w_out_a_bf16 = _matmul_residual(o_s, w_out_a, h_s, tm=sm, tk=SAMPLE_K_TILE,
                                         emit_weights=True)
    o_p, gla_p = _gla(proj_p, bcum_p, g_onorm_a, batch=pb, seq=ps, chunk=GLA_PROMPT_CHUNK,
                      out_dtype=BF16)
    h_p = _matmul_residual(o_p, w_out_a_bf16, h_p, tm=RESIDUAL_ROW_TILE)

    qz_s, kv_s, w_in_b_bf16 = _norm_matmul(
        h_s, g_norm_b, w_in_b, tm=sm, tn=QZ_COL_TILE, out_dtype=F32,
        w_extra=w_kv, g_extra=g_norm_kv, emit_weights=True, gate_cols=SWA_WIDTH)
    att_s, k_s, v_s = _attn_sample(qz_s, kv_s, cache_k_win, cache_v_win, sinks,
                                   batch=sb, seq=ss)
    y_s, w_out_b_bf16 = _matmul_residual(att_s, w_out_b, h_s, tm=sm, tk=SAMPLE_K_TILE,
                                         g_final=g_final, emit_weights=True)
    qz_p, kv_p = _norm_matmul(
        h_p, g_norm_b, w_in_b_bf16, tm=PROMPT_ROW_TILE, tn=QZ_COL_TILE, out_dtype=BF16,
        w_extra=w_kv, g_extra=g_norm_kv, gate_cols=SWA_WIDTH)
    att_p, k_p, v_p = _attn_prompt(qz_p, kv_p, sinks, batch=pb, seq=ps)
    y_p = _matmul_residual(att_p, w_out_b_bf16, h_p, tm=RESIDUAL_ROW_TILE, g_final=g_final)

    return (y_p.reshape(pb, ps, D_MODEL), y_s.reshape(sb, ss, D_MODEL), gla_p[None],
            gla_s[None], k_p, v_p, k_s, v_s)
```

```python
import functools

import jax
import jax.numpy as jnp
from jax import lax
from jax.experimental import pallas as pl
from jax.experimental.pallas import tpu as pltpu

F32 = jnp.float32
BF16 = jnp.bfloat16

D_MODEL = 2048
GLA_HEADS = 4
GLA_KEY_DIM = D_MODEL // 2
GLA_VALUE_DIM = D_MODEL
GLA_DK = GLA_KEY_DIM // GLA_HEADS
GLA_DV = GLA_VALUE_DIM // GLA_HEADS
GLA_GATE_RANK = 16
GLA_GATE_TEMP = 16.0
GLA_MAIN_COLS = 2 * GLA_KEY_DIM + 2 * GLA_VALUE_DIM
SWA_HEAD_DIM = 64
SWA_HEADS = D_MODEL // SWA_HEAD_DIM
SWA_KV_HEADS = 4
SWA_GROUP = SWA_HEADS // SWA_KV_HEADS
SWA_WIDTH = SWA_HEADS * SWA_HEAD_DIM
SWA_KV_WIDTH = SWA_KV_HEADS * SWA_HEAD_DIM
WINDOW = 128
RMS_EPS = 1e-6
LOG2E = 1.4426950408889634

V7X_VMEM_LIMIT_BYTES = 61 * 1024 * 1024
GLA_PROMPT_CHUNK = 128
GLA_SEQS_PER_STEP = 4
ATTN_BLOCKS_PER_STEP = 8
ATTN_BLOCKS_IN_FLIGHT = 2
SAMPLE_SEQS_PER_STEP = 8
NORM_ROW_CHUNK = 256
GATE_CUMSUM_ROWS = 128
GATE_ROW_TILE = 1024
PROMPT_ROW_TILE = 1024
PROMPT_COL_TILE = 2048
SAMPLE_COL_TILE = 1024
QZ_COL_TILE = 2048
SAMPLE_K_TILE = 1024
RESIDUAL_ROW_TILE = 512

_NT_DIMS = (((1,), (1,)), ((), ()))
_TN_DIMS = (((0,), (0,)), ((), ()))


def _params(semantics):
    return pltpu.CompilerParams(dimension_semantics=semantics,
                                vmem_limit_bytes=V7X_VMEM_LIMIT_BYTES)


def _silu(x):
    return x / (1.0 + jnp.exp(-x))


def _rms_scale(x):
    return lax.rsqrt(jnp.mean(x * x, axis=-1, keepdims=True) + RMS_EPS)


def _norm_matmul_kernel(x_ref, g_ref, w_ref, *rest, has_extra, extra_gain, emit_weights,
                        w_transposed, silu_from):
    dims = _NT_DIMS if w_transposed else (((1,), (0,)), ((), ()))

    def mm(a, b):
        return lax.dot_general(a, b, dims, preferred_element_type=F32)

    rest = list(rest)
    g2_ref = rest.pop(0) if extra_gain else None
    w2_ref = rest.pop(0) if has_extra else None
    o_ref = rest.pop(0)
    o2_ref = rest.pop(0) if has_extra else None
    wb_ref = rest.pop(0) if emit_weights else None
    (xn_ref,) = rest

    first = pl.program_id(1) == 0

    def weights():
        w = w_ref[...].astype(BF16)
        if emit_weights:
            wb_ref[...] = w
        return w

    @pl.when(first)
    def _():
        tm = x_ref.shape[0]
        rc = min(tm, NORM_ROW_CHUNK)
        w = weights()
        for c in range(tm // rc):
            rows = slice(c * rc, (c + 1) * rc)
            x = x_ref[rows, :]
            xs = x * _rms_scale(x)
            xn = (xs * g_ref[...]).astype(BF16)
            xn_ref[rows, :] = xn
            o_ref[rows, :] = mm(xn, w).astype(o_ref.dtype)
            if has_extra:
                xn2 = (xs * g2_ref[...]).astype(BF16) if extra_gain else xn
                o2_ref[rows, :] = mm(xn2, w2_ref[...].astype(BF16)).astype(o2_ref.dtype)

    def later_block(gated):
        w = weights()
        if not gated:
            o_ref[...] = mm(xn_ref[...], w).astype(o_ref.dtype)
            return
        tm = x_ref.shape[0]
        rc = min(tm, NORM_ROW_CHUNK)
        for c in range(tm // rc):
            rows = slice(c * rc, (c + 1) * rc)
            o_ref[rows, :] = _silu(mm(xn_ref[rows, :], w)).astype(o_ref.dtype)

    j = pl.program_id(1)
    if silu_from is None or silu_from == 1:
        pl.when(j >= 1)(functools.partial(later_block, silu_from == 1))
    else:
        pl.when((j >= 1) & (j < silu_from))(functools.partial(later_block, False))
        pl.when(j >= silu_from)(functools.partial(later_block, True))


def _norm_matmul(x, g, w, *, tm, tn, out_dtype, n=None, w_extra=None, g_extra=None,
                 emit_weights=False, w_transposed=False, gate_cols=0):
    m, k = x.shape
    n_axis = -2 if w_transposed else -1
    n = w.shape[n_axis] if n is None else n
    assert n % tn == 0 and m % tm == 0
    assert not emit_weights or m == tm, "each weight block must be visited exactly once"
    assert gate_cols % tn == 0 and gate_cols <= n - tn, "gate columns: whole blocks after block 0"
    silu_from = (n - gate_cols) // tn if gate_cols else None
    grid = (m // tm, n // tn)
    w_block = (tn, k) if w_transposed else (k, tn)
    w_index = (lambda i, j: (j, 0)) if w_transposed else (lambda i, j: (0, j))
    if w.ndim == 3:
        w_spec = pl.BlockSpec((None,) + w_block, lambda i, j: (0,) + w_index(i, j))
    else:
        w_spec = pl.BlockSpec(w_block, w_index)
    in_specs = [
        pl.BlockSpec((tm, k), lambda i, j: (i, 0)),
        pl.BlockSpec((1, k), lambda i, j: (0, 0)),
        w_spec,
    ]
    out_shape = [jax.ShapeDtypeStruct((m, n), out_dtype)]
    out_specs = [pl.BlockSpec((tm, tn), lambda i, j: (i, j))]
    args = [x, g.reshape(1, k), w]
    if g_extra is not None:
        in_specs.append(pl.BlockSpec((1, k), lambda i, j: (0, 0)))
        args.append(g_extra.reshape(1, k))
    if w_extra is not None:
        n2 = w_extra.shape[n_axis]
        in_specs.append(pl.BlockSpec(w_extra.shape, lambda i, j: (0, 0)))
        out_shape.append(jax.ShapeDtypeStruct((m, n2), F32))
        out_specs.append(pl.BlockSpec((tm, n2), lambda i, j: (i, 0)))
        args.append(w_extra)
    if emit_weights:
        out_shape.append(jax.ShapeDtypeStruct((n, k) if w_transposed else (k, n), BF16))
        out_specs.append(pl.BlockSpec(w_block, w_index))
    res = pl.pallas_call(
        functools.partial(_norm_matmul_kernel, has_extra=w_extra is not None,
                          extra_gain=g_extra is not None, emit_weights=emit_weights,
                          w_transposed=w_transposed, silu_from=silu_from),
        grid=grid,
        in_specs=in_specs,
        out_specs=out_specs,
        out_shape=out_shape,
        scratch_shapes=[pltpu.VMEM((tm, k), BF16)],
        compiler_params=_params(("parallel", "arbitrary")),
        name=f"norm_matmul_{m}x{n}",
    )(*args)
    return res if len(res) > 1 else res[0]


def _matmul_residual_kernel(a_ref, w_ref, r_ref, *rest, final_norm, emit_weights, nk):
    rest = list(rest)
    g_ref = rest.pop(0) if final_norm else None
    o_ref = rest.pop(0)
    w = w_ref[...].astype(BF16)
    if emit_weights:
        rest[0][...] = w
    part = jnp.dot(a_ref[...].astype(BF16), w, preferred_element_type=F32)

    if nk == 1:
        h = r_ref[...] + part
        if final_norm:
            h = h * _rms_scale(h) * g_ref[...]
        o_ref[...] = h
        return

    kk = pl.program_id(1)

    @pl.when(kk == 0)
    def _():
        o_ref[...] = r_ref[...] + part

    @pl.when(kk > 0)
    def _():
        o_ref[...] += part

    if final_norm:
        @pl.when(kk == nk - 1)
        def _():
            h = o_ref[...]
            o_ref[...] = h * _rms_scale(h) * g_ref[...]


def _matmul_residual(a, w, res, *, tm, tk=None, g_final=None, emit_weights=False):
    m, k = a.shape
    n = w.shape[-1]
    tk = k if tk is None else tk
    assert m % tm == 0 and k % tk == 0
    assert not emit_weights or m == tm, "each weight block must be visited exactly once"
    if w.ndim == 3:
        w_spec = pl.BlockSpec((None, tk, n), lambda i, j: (0, j, 0))
    else:
        w_spec = pl.BlockSpec((tk, n), lambda i, j: (j, 0))
    in_specs = [
        pl.BlockSpec((tm, tk), lambda i, j: (i, j)),
        w_spec,
        pl.BlockSpec((tm, n), lambda i, j: (i, 0)),
    ]
    args = [a, w, res]
    if g_final is not None:
        in_specs.append(pl.BlockSpec((1, n), lambda i, j: (0, 0)))
        args.append(g_final.reshape(1, n))
    out_shape = [jax.ShapeDtypeStruct((m, n), F32)]
    out_specs = [pl.BlockSpec((tm, n), lambda i, j: (i, 0))]
    if emit_weights:
        out_shape.append(jax.ShapeDtypeStruct((k, n), BF16))
        out_specs.append(pl.BlockSpec((tk, n), lambda i, j: (j, 0)))
    res = pl.pallas_call(
        functools.partial(_matmul_residual_kernel, final_norm=g_final is not None,
                          emit_weights=emit_weights, nk=k // tk),
        grid=(m // tm, k // tk),
        in_specs=in_specs,
        out_specs=out_specs,
        out_shape=out_shape,
        compiler_params=_params(("parallel", "arbitrary")),
        name=f"matmul_residual_{m}" + ("_final" if g_final is not None else ""),
    )(*args)
    return res if emit_weights else res[0]


def _split_bf16(x):
    hi = x.astype(BF16)
    return hi, (x - hi.astype(F32)).astype(BF16)


def _gla_gates_kernel(glow_ref, wg_ref, bg_ref, bcum_ref, *, chunk):
    rows = glow_ref.shape[0]
    g_hi, g_lo = _split_bf16(glow_ref[...])
    w_hi, w_lo = _split_bf16(wg_ref[...])
    x = jnp.dot(jnp.concatenate([g_hi, g_lo, g_hi], axis=1),
                jnp.concatenate([w_hi, w_hi, w_lo], axis=0),
                preferred_element_type=F32) + bg_ref[...]
    softplus2 = jnp.log2(1.0 + jnp.exp2(jnp.abs(x) * (-LOG2E)))
    logg = jnp.minimum(x, 0.0) * (LOG2E / GLA_GATE_TEMP) - softplus2 * (1.0 / GLA_GATE_TEMP)

    span = GATE_CUMSUM_ROWS
    row = lax.broadcasted_iota(jnp.int32, (span, span), 0)
    col = lax.broadcasted_iota(jnp.int32, (span, span), 1)
    same_chunk = (row // chunk) == (col // chunk) if chunk < span else True
    tril = jnp.where((col <= row) & same_chunk, 1.0, 0.0).astype(BF16)
    tril2 = jnp.concatenate([tril, tril], axis=1)
    for i in range(rows // span):
        hi, lo = _split_bf16(logg[i * span:(i + 1) * span])
        bcum_ref[i * span:(i + 1) * span, :] = jnp.dot(
            tril2, jnp.concatenate([hi, lo], axis=0), preferred_element_type=F32)


def _gates_with_gla_kernel(glow_ref, wg_ref, bg_ref, *rest, chunk, n_gla_in):
    gla_in, (bcum_ref, o_ref, sfin_ref) = rest[:n_gla_in], rest[n_gla_in:]
    _gla_kernel(*gla_in, o_ref, sfin_ref, has_init=True, n_chunks=1)
    _gla_gates_kernel(glow_ref, wg_ref, bg_ref, bcum_ref, chunk=chunk)


def _gla_gates(glow, wg, bg, *, chunk, tm, rider=None):
    m = glow.shape[0]
    assert m % tm == 0 and tm % GATE_CUMSUM_ROWS == 0
    assert GATE_CUMSUM_ROWS % chunk == 0 or chunk % GATE_CUMSUM_ROWS == 0
    assert chunk <= GATE_CUMSUM_ROWS, "cumulative sums do not cross row spans"
    steps = m // tm
    in_specs = [
        pl.BlockSpec((tm, GLA_GATE_RANK), lambda i: (i, 0)),
        pl.BlockSpec((GLA_GATE_RANK, GLA_KEY_DIM), lambda i: (0, 0)),
        pl.BlockSpec((1, GLA_KEY_DIM), lambda i: (0, 0)),
    ]
    args = [glow, wg, bg.reshape(1, -1)]
    out_specs = [pl.BlockSpec((tm, GLA_KEY_DIM), lambda i: (i, 0))]
    out_shape = [jax.ShapeDtypeStruct((m, GLA_KEY_DIM), F32)]
    if rider is None:
        body = functools.partial(_gla_gates_kernel, chunk=chunk)
    else:
        batch, seq = rider["batch"], rider["seq"]
        assert batch % steps == 0
        g_in, g_args, g_out, g_shape = _gla_operands(
            rider["proj"], rider["bcum"], rider["gon"], rider["s0"], batch=batch, seq=seq,
            chunk=seq, ns=batch // steps, out_dtype=F32, index=lambda i: (i, 0))
        body = functools.partial(_gates_with_gla_kernel, chunk=chunk, n_gla_in=len(g_in))
        in_specs += g_in
        args += g_args
        out_specs += g_out
        out_shape += g_shape
    res = pl.pallas_call(
        body,
        grid=(steps,),
        in_specs=in_specs,
        out_specs=out_specs,
        out_shape=out_shape,
        compiler_params=_params(("parallel",)),
        name=f"gla_gates_{m}",
    )(*args)
    if rider is None:
        return res[0]
    bcum, o, s_fin = res
    return bcum, o.reshape(rider["batch"] * rider["seq"], GLA_VALUE_DIM), s_fin


def _gla_kernel(q_ref, k_ref, v_ref, r_ref, bcum_ref, gon_ref, *rest, has_init, n_chunks):
    rest = list(rest)
    s0_ref = rest.pop(0) if has_init else None
    o_ref, sfin_ref = rest[:2]
    s_ref = rest[2] if n_chunks > 1 else None
    c = pl.program_id(1) if n_chunks > 1 else None
    nseq, chunk = q_ref.shape[:2]
    chains = [(s, h) for s in range(nseq) for h in range(GLA_HEADS)]

    if n_chunks > 1:
        @pl.when(c == 0)
        def _():
            if has_init:
                s_ref[...] = s0_ref[...]
            else:
                s_ref[...] = jnp.zeros_like(s_ref)

    row = lax.broadcasted_iota(jnp.int32, (chunk, chunk), 0)
    col = lax.broadcasted_iota(jnp.int32, (chunk, chunk), 1)
    causal = col <= row
    heads = range(GLA_HEADS)
    ks = [slice(h * GLA_DK, (h + 1) * GLA_DK) for h in heads]
    vs = [slice(h * GLA_DV, (h + 1) * GLA_DV) for h in heads]

    def state(s, h):
        if n_chunks > 1:
            return s_ref[s, h]
        return s0_ref[s, h] if has_init else jnp.zeros((GLA_DK, GLA_DV), F32)

    q_inter, k_state, scores, decay = {}, {}, {}, {}
    for s, h in chains:
        b = bcum_ref[s, :, ks[h]]
        b_last = b[chunk - 1:chunk, :]
        b_mid = b[chunk // 2 - 1:chunk // 2, :]
        q = q_ref[s, :, ks[h]].astype(F32)
        k = k_ref[s, :, ks[h]].astype(F32)
        q_inter[s, h] = (q * jnp.exp2(b)).astype(BF16)
        q_intra = (q * jnp.exp2(b - b_mid)).astype(BF16)
        k_intra = (k * jnp.exp2(b_mid - b)).astype(BF16)
        k_state[s, h] = (k * jnp.exp2(b_last - b)).astype(BF16)
        scores[s, h] = lax.dot_general(q_intra, k_intra, _NT_DIMS,
                                       preferred_element_type=F32)
        decay[s, h] = jnp.exp2(jnp.broadcast_to(b_last, (128, GLA_DK))).T

    o = {}
    for s, h in chains:
        sc = jnp.where(causal, scores[s, h], 0.0).astype(BF16)
        v = v_ref[s, :, vs[h]].astype(BF16)
        o[s, h] = (jnp.dot(q_inter[s, h], state(s, h).astype(BF16),
                           preferred_element_type=F32)
                   + jnp.dot(sc, v, preferred_element_type=F32))
        upd = lax.dot_general(k_state[s, h], v, _TN_DIMS, preferred_element_type=F32)
        s_new = (state(s, h) * jnp.concatenate([decay[s, h]] * (GLA_DV // 128), axis=1)
                 + upd)
        if n_chunks > 1:
            s_ref[s, h] = s_new
        else:
            sfin_ref[s, h] = s_new

    qs = GLA_DK ** -0.5
    scale = {sh: qs * lax.rsqrt(jnp.mean(o[sh] * o[sh], axis=-1, keepdims=True) * (qs * qs)
                                + RMS_EPS) for sh in chains}
    for s, h in chains:
        r = r_ref[s, :, vs[h]].astype(F32)
        o_ref[s, :, vs[h]] = (o[s, h] * scale[s, h] * gon_ref[:, vs[h]]
                              * r).astype(o_ref.dtype)

    if n_chunks > 1:
        @pl.when(c == n_chunks - 1)
        def _():
            sfin_ref[...] = s_ref[...]


def _gla_operands(proj, bcum, gon, s0, *, batch, seq, chunk, ns, out_dtype, index):
    assert batch % ns == 0 and seq % chunk == 0
    proj = proj.reshape(batch, seq, proj.shape[-1])
    bcum = bcum.reshape(batch, seq, bcum.shape[-1])
    kb, vb, rb = 1, 2 * GLA_KEY_DIM // GLA_VALUE_DIM, 2 * GLA_KEY_DIM // GLA_VALUE_DIM + 1

    def rows(width, col):
        return pl.BlockSpec((ns, chunk, width), lambda *g: (*index(*g), col))

    state_spec = pl.BlockSpec((ns, GLA_HEADS, GLA_DK, GLA_DV),
                              lambda *g: (index(*g)[0], 0, 0, 0))
    in_specs = [rows(GLA_KEY_DIM, 0), rows(GLA_KEY_DIM, kb), rows(GLA_VALUE_DIM, vb),
                rows(GLA_VALUE_DIM, rb), rows(GLA_KEY_DIM, 0),
                pl.BlockSpec((1, GLA_VALUE_DIM), lambda *g: (0, 0))]
    args = [proj, proj, proj, proj, bcum, gon.reshape(1, -1)]
    if s0 is not None:
        in_specs.append(state_spec)
        args.append(s0)
    out_specs = [rows(GLA_VALUE_DIM, 0), state_spec]
    out_shape = [jax.ShapeDtypeStruct((batch, seq, GLA_VALUE_DIM), out_dtype),
                 jax.ShapeDtypeStruct((batch, GLA_HEADS, GLA_DK, GLA_DV), F32)]
    return in_specs, args, out_specs, out_shape


def _gla(proj, bcum, gon, *, batch, seq, chunk, out_dtype, s0=None):
    n = seq // chunk
    ns = GLA_SEQS_PER_STEP
    in_specs, args, out_specs, out_shape = _gla_operands(
        proj, bcum, gon, s0, batch=batch, seq=seq, chunk=chunk, ns=ns, out_dtype=out_dtype,
        index=lambda b, c: (b, c))
    scratch = [pltpu.VMEM((ns, GLA_HEADS, GLA_DK, GLA_DV), F32)] if n > 1 else []
    o, s_fin = pl.pallas_call(
        functools.partial(_gla_kernel, has_init=s0 is not None, n_chunks=n),
        grid=(batch // ns, n),
        in_specs=in_specs,
        out_specs=out_specs,
        out_shape=out_shape,
        scratch_shapes=scratch,
        compiler_params=_params(("parallel", "arbitrary")),
        name=f"gla_chunk{chunk}",
    )(*args)
    return o.reshape(batch * seq, GLA_VALUE_DIM), s_fin


def _alibi_slope(head):
    return 2.0 ** (-8.0 * (head + 1) / SWA_HEADS)


def _attn_prompt_kernel(sinks_ref, q_ref, z_ref, kp_ref, ko_ref, vp_ref, vo_ref, out_ref,
                        kwin_ref, vwin_ref, bias_ref):
    hd, nkeys = SWA_HEAD_DIM, 2 * WINDOW
    pair_w = 2 * hd
    pairs_per_group = SWA_GROUP // 2
    blk = pl.program_id(1)

    @pl.when((pl.program_id(0) == 0) & (blk == 0))
    def _():
        kj = lax.broadcasted_iota(jnp.int32, (nkeys, WINDOW), 0)
        qi = lax.broadcasted_iota(jnp.int32, (nkeys, WINDOW), 1)
        dist = WINDOW + qi - kj
        ok = (dist >= 0) & (dist <= WINDOW)
        ok_first = ok & (kj >= WINDOW)
        distf = dist.astype(F32)
        for h in range(SWA_HEADS):
            pen = (-_alibi_slope(h) * LOG2E) * distf
            sl = slice((h % 2) * WINDOW, (h % 2 + 1) * WINDOW)
            bias_ref[0, h // 2, :, sl] = jnp.where(ok_first, pen, -jnp.inf)
            bias_ref[1, h // 2, :, sl] = jnp.where(ok, pen, -jnp.inf)

    n_sub = q_ref.shape[0] // WINDOW
    k_rows = jnp.concatenate([kp_ref[...], ko_ref[...]], axis=0)
    v_rows = jnp.concatenate([vp_ref[...], vo_ref[...]], axis=0)
    ones = jnp.ones((16, nkeys), F32)
    lane = lax.broadcasted_iota(jnp.int32, (nkeys, pair_w), 1)
    qlane = lax.broadcasted_iota(jnp.int32, (WINDOW, pair_w), 1)
    qk_scale = (hd ** -0.5) * LOG2E
    quad_pairs = 2
    n_quads = SWA_HEADS // (2 * quad_pairs)
    quads_per_group = pairs_per_group // quad_pairs
    quarter = lax.broadcasted_iota(jnp.int32, (1, 2 * quad_pairs * WINDOW), 1) // WINDOW

    def prepare(sub):
        k = k_rows[sub * WINDOW:sub * WINDOW + nkeys]
        vt = v_rows[sub * WINDOW:sub * WINDOW + nkeys].T
        k2, vt1 = [], []
        for g in range(SWA_KV_HEADS):
            kblk = k[:, (g // 2) * pair_w:(g // 2 + 1) * pair_w]
            k_here = jnp.where((lane < hd) if g % 2 == 0 else (lane >= hd), kblk, 0.0)
            k2.append((k_here + pltpu.roll(k_here, hd, axis=1)).astype(BF16))
            vt1.append(jnp.concatenate([vt[g * hd:(g + 1) * hd], ones], axis=0).astype(BF16))
        return dict(qrows=slice(sub * WINDOW, (sub + 1) * WINDOW), k2=k2, vt1=vt1,
                    tbl=jnp.minimum(blk, 1) if sub == 0 else 1)

    def scores(blkst, quad):
        parts = []
        for j in range(quad_pairs):
            col = (quad * quad_pairs + j) * pair_w
            q_pair = q_ref[blkst["qrows"], col:col + pair_w]
            zero = jnp.zeros_like(q_pair)
            parts += [jnp.where(qlane < hd, q_pair, zero), jnp.where(qlane >= hd, q_pair, zero)]
        return lax.dot_general(blkst["k2"][quad // quads_per_group],
                               jnp.concatenate(parts, axis=0),
                               _NT_DIMS, preferred_element_type=F32)

    def finish(blkst, quad, st):
        pair0 = quad * quad_pairs
        bias = jnp.concatenate([bias_ref[blkst["tbl"], pair0 + j] for j in range(quad_pairs)],
                               axis=1)
        s2 = st * qk_scale + bias
        sink2 = sinks_ref[2 * pair0]
        for t in range(1, 2 * quad_pairs):
            sink2 = jnp.where(quarter == t, sinks_ref[2 * pair0 + t], sink2)
        sink2 = sink2 * LOG2E
        m = jnp.maximum(jnp.max(s2, axis=0, keepdims=True), sink2)
        p = jnp.exp2(s2 - m).astype(BF16)
        oa = jnp.dot(blkst["vt1"][quad // quads_per_group], p, preferred_element_type=F32)
        denom = oa[hd:hd + 1] + jnp.exp2(sink2 - m)
        on = oa[0:hd] * (1.0 / denom)
        for j in range(quad_pairs):
            lo = 2 * j * WINDOW
            o_pair = jnp.concatenate([on[:, lo:lo + WINDOW],
                                      on[:, lo + WINDOW:lo + 2 * WINDOW]], axis=0).T
            col = (pair0 + j) * pair_w
            z_pair = z_ref[blkst["qrows"], col:col + pair_w].astype(F32)
            out_ref[blkst["qrows"], col:col + pair_w] = (
                o_pair * z_pair).astype(out_ref.dtype)

    for first in range(0, n_sub, ATTN_BLOCKS_IN_FLIGHT):
        blocks = [prepare(sub) for sub in range(first, min(first + ATTN_BLOCKS_IN_FLIGHT, n_sub))]
        st_next = [scores(b, 0) for b in blocks]
        for quad in range(n_quads):
            for i, b in enumerate(blocks):
                st = st_next[i]
                if quad + 1 < n_quads:
                    st_next[i] = scores(b, quad + 1)
                finish(b, quad, st)

    @pl.when(blk == pl.num_programs(1) - 1)
    def _():
        last = slice((n_sub - 1) * WINDOW, n_sub * WINDOW)
        kwin_ref[...] = ko_ref[last, :].T
        vwin_ref[...] = vo_ref[last, :].T


def _attn_prompt(qz, kv, sinks, *, batch, seq):
    sub = ATTN_BLOCKS_PER_STEP
    tq = sub * WINDOW
    nb = seq // tq
    assert seq % tq == 0
    row = lambda b, i: b * nb + i
    prev = lambda b, i: (b * nb + i) * sub - jnp.minimum(i, 1)
    win_spec = pl.BlockSpec((None, SWA_KV_WIDTH, WINDOW), lambda b, i: (b, 0, 0))
    win_shape = jax.ShapeDtypeStruct((batch, SWA_KV_WIDTH, WINDOW), F32)
    att, k_win, v_win = pl.pallas_call(
        _attn_prompt_kernel,
        grid=(batch, nb),
        in_specs=[
            pl.BlockSpec(memory_space=pltpu.SMEM),
            pl.BlockSpec((tq, SWA_WIDTH), lambda b, i: (row(b, i), 0)),
            pl.BlockSpec((tq, SWA_WIDTH), lambda b, i: (row(b, i), 1)),
            pl.BlockSpec((WINDOW, SWA_KV_WIDTH), lambda b, i: (prev(b, i), 0)),
            pl.BlockSpec((tq, SWA_KV_WIDTH), lambda b, i: (row(b, i), 0)),
            pl.BlockSpec((WINDOW, SWA_KV_WIDTH), lambda b, i: (prev(b, i), 1)),
            pl.BlockSpec((tq, SWA_KV_WIDTH), lambda b, i: (row(b, i), 1)),
        ],
        out_specs=[pl.BlockSpec((tq, SWA_WIDTH), lambda b, i: (row(b, i), 0)),
                   win_spec, win_spec],
        out_shape=[jax.ShapeDtypeStruct((batch * seq, SWA_WIDTH), BF16), win_shape, win_shape],
        scratch_shapes=[pltpu.VMEM((2, SWA_HEADS // 2, 2 * WINDOW, 2 * WINDOW), F32)],
        compiler_params=_params(("arbitrary", "arbitrary")),
        name="attn_prompt",
    )(sinks, qz, qz, kv, kv, kv, kv)
    to_rows = lambda w: jnp.transpose(
        w.reshape(batch, SWA_KV_HEADS, SWA_HEAD_DIM, WINDOW), (0, 3, 1, 2))
    return att, to_rows(k_win), to_rows(v_win)


def _attn_sample_kernel(sink_ref, slope_ref, q_ref, z_ref, kn_ref, vn_ref, kc_ref, vc_ref,
                        out_ref, kwin_ref, vwin_ref, *, tq, nb):
    hd, nk = SWA_HEAD_DIM, 2 * WINDOW
    rows = SWA_HEADS * tq
    grows = SWA_GROUP * tq
    seqs = range(nb)
    groups = range(SWA_KV_HEADS)

    lane = lax.broadcasted_iota(jnp.int32, (rows, nk), 1)
    tok = lax.broadcasted_iota(jnp.int32, (rows, nk), 0) % tq
    in_buffer = lane < WINDOW
    dist = jnp.where(in_buffer, WINDOW + tok - lane, (nk - tq) + tok - lane)
    allowed = (dist >= 0) & (dist <= WINDOW) & (in_buffer | (lane >= nk - tq))
    penalty = slope_ref[...] * dist.astype(F32)
    new_lanes = lax.broadcasted_iota(jnp.int32, (hd, WINDOW), 1) >= WINDOW - tq

    def new_rows_t(ref, b):
        x = jnp.concatenate([jnp.zeros((WINDOW - tq, SWA_KV_WIDTH), F32),
                             ref[b * tq:(b + 1) * tq, :]], axis=0)
        xt = [x[:, c * WINDOW:(c + 1) * WINDOW].T for c in range(SWA_KV_WIDTH // WINDOW)]
        per_block = WINDOW // hd
        return [xt[g // per_block][(g % per_block) * hd:(g % per_block + 1) * hd]
                for g in groups]

    k_all, v_all = {}, {}
    for b in seqs:
        kn_t, vn_t = new_rows_t(kn_ref, b), new_rows_t(vn_ref, b)
        for g in groups:
            kc, vc = kc_ref[b, g], vc_ref[b, g]
            k_all[b, g] = jnp.concatenate([kc, kn_t[g]], axis=1).astype(BF16)
            v_all[b, g] = jnp.concatenate([vc, vn_t[g]], axis=1).astype(BF16)
            kwin_ref[b, g] = jnp.where(new_lanes, kn_t[g], pltpu.roll(kc, WINDOW - tq, axis=1))
            vwin_ref[b, g] = jnp.where(new_lanes, vn_t[g], pltpu.roll(vc, WINDOW - tq, axis=1))

    s = []
    for b in seqs:
        q = q_ref[b * tq:(b + 1) * tq, :]
        parts = []
        for g in groups:
            qs = jnp.concatenate([q[:, h * hd:(h + 1) * hd]
                                  for h in range(g * SWA_GROUP, (g + 1) * SWA_GROUP)], axis=0)
            parts.append(jnp.dot(qs.astype(BF16), k_all[b, g], preferred_element_type=F32))
        s.append(jnp.concatenate(parts, axis=0))

    sink = sink_ref[...]
    s = [jnp.where(allowed, sb * (hd ** -0.5) - penalty, -jnp.inf) for sb in s]
    m = [jnp.maximum(jnp.max(sb, axis=-1, keepdims=True), sink) for sb in s]
    p = [jnp.exp(s[b] - m[b]) for b in seqs]
    inv = [1.0 / (jnp.sum(p[b], axis=-1, keepdims=True) + jnp.exp(sink - m[b])) for b in seqs]

    for b in seqs:
        pb = p[b].astype(BF16)
        o = jnp.concatenate(
            [lax.dot_general(pb[g * grows:(g + 1) * grows], v_all[b, g], _NT_DIMS,
                             preferred_element_type=F32) for g in groups], axis=0) * inv[b]
        o = jnp.concatenate([o[h * tq:(h + 1) * tq] for h in range(SWA_HEADS)], axis=1)
        z = z_ref[b * tq:(b + 1) * tq, :]
        out_ref[b * tq:(b + 1) * tq, :] = (o * z).astype(out_ref.dtype)


def _attn_sample(qz, kv, cache_k, cache_v, sinks, *, batch, seq):
    nb = SAMPLE_SEQS_PER_STEP
    assert batch % nb == 0
    rows = nb * seq
    sink_col = jnp.repeat(sinks, seq).reshape(SWA_HEADS * seq, 1)
    slope_col = jnp.repeat(jnp.asarray([_alibi_slope(h) for h in range(SWA_HEADS)], F32),
                           seq).reshape(SWA_HEADS * seq, 1)
    col_spec = pl.BlockSpec((SWA_HEADS * seq, 1), lambda i: (0, 0))
    win_spec = pl.BlockSpec((nb, SWA_KV_HEADS, SWA_HEAD_DIM, WINDOW), lambda i: (i, 0, 0, 0))
    win_shape = jax.ShapeDtypeStruct((batch, SWA_KV_HEADS, SWA_HEAD_DIM, WINDOW), F32)
    to_lanes = lambda c: jnp.transpose(c, (0, 2, 3, 1))
    att, k_win, v_win = pl.pallas_call(
        functools.partial(_attn_sample_kernel, tq=seq, nb=nb),
        grid=(batch // nb,),
        in_specs=[
            col_spec,
            col_spec,
            pl.BlockSpec((rows, SWA_WIDTH), lambda i: (i, 0)),
            pl.BlockSpec((rows, SWA_WIDTH), lambda i: (i, 1)),
            pl.BlockSpec((rows, SWA_KV_WIDTH), lambda i: (i, 0)),
            pl.BlockSpec((rows, SWA_KV_WIDTH), lambda i: (i, 1)),
            win_spec,
            win_spec,
        ],
        out_specs=[pl.BlockSpec((rows, SWA_WIDTH), lambda i: (i, 0)), win_spec, win_spec],
        out_shape=[jax.ShapeDtypeStruct((batch * seq, SWA_WIDTH), F32), win_shape, win_shape],
        compiler_params=_params(("parallel",)),
        name="attn_sample",
    )(sink_col, slope_col, qz, qz, kv, kv, to_lanes(cache_k), to_lanes(cache_v))
    return att, jnp.transpose(k_win, (0, 3, 1, 2)), jnp.transpose(v_win, (0, 3, 1, 2))


def kernel(x_prompt, x_sample, state_gla, cache_k_win, cache_v_win, g_norm_a, w_in_a,
           w_gate_up, b_gate, g_onorm_a, w_out_a, g_norm_kv, w_kv, g_norm_b, w_in_b, sinks,
           w_out_b, g_final):
    assert w_in_a.shape[0] == 1 and w_in_b.shape[0] == 1, "one GLA layer, one SWA layer"
    assert cache_k_win.shape[1] == WINDOW
    pb, ps, _ = x_prompt.shape
    sb, ss, _ = x_sample.shape
    pm, sm = pb * ps, sb * ss
    g_norm_a, w_gate_up, b_gate, g_onorm_a = g_norm_a[0], w_gate_up[0], b_gate[0], g_onorm_a[0]
    g_norm_b, sinks = g_norm_b[0], sinks[0]
    w_kv = w_kv.astype(BF16)
    w_in_a_t = jnp.swapaxes(w_in_a, 1, 2)
    w_glow = w_in_a_t[0, GLA_MAIN_COLS:, :]
    h_p = x_prompt.reshape(pm, D_MODEL)
    h_s = x_sample.reshape(sm, D_MODEL)

    proj_s, glow_s, w_in_a_bf16 = _norm_matmul(
        h_s, g_norm_a, w_in_a_t, tm=sm, tn=SAMPLE_COL_TILE, n=GLA_MAIN_COLS, out_dtype=F32,
        w_extra=w_glow, emit_weights=True, w_transposed=True, gate_cols=GLA_VALUE_DIM)
    bcum_s = _gla_gates(glow_s, w_gate_up, b_gate, chunk=ss, tm=sm)
    proj_p, glow_p = _norm_matmul(
        h_p, g_norm_a, w_in_a_bf16, tm=PROMPT_ROW_TILE, tn=PROMPT_COL_TILE, n=GLA_MAIN_COLS,
        out_dtype=BF16, w_extra=w_glow, w_transposed=True, gate_cols=GLA_VALUE_DIM)
    bcum_p, o_s, gla_s = _gla_gates(
        glow_p, w_gate_up, b_gate, chunk=GLA_PROMPT_CHUNK, tm=GATE_ROW_TILE,
        rider=dict(proj=proj_s, bcum=bcum_s, gon=g_onorm_a, batch=sb, seq=ss,
                   s0=state_gla.reshape(state_gla.shape[1:])))
    h_s, w_out_a_bf16 = _matmul_residual(o_s, w_out_a, h_s, tm=sm, tk=SAMPLE_K_TILE,
                                         emit_weights=True)
    o_p, gla_p = _gla(proj_p, bcum_p, g_onorm_a, batch=pb, seq=ps, chunk=GLA_PROMPT_CHUNK,
                      out_dtype=BF16)
    h_p = _matmul_residual(o_p, w_out_a_bf16, h_p, tm=RESIDUAL_ROW_TILE)

    qz_s, kv_s, w_in_b_bf16 = _norm_matmul(
        h_s, g_norm_b, w_in_b, tm=sm, tn=QZ_COL_TILE, out_dtype=F32,
        w_extra=w_kv, g_extra=g_norm_kv, emit_weights=True, gate_cols=SWA_WIDTH)
    att_s, k_s, v_s = _attn_sample(qz_s, kv_s, cache_k_win, cache_v_win, sinks,
                                   batch=sb, seq=ss)
    y_s, w_out_b_bf16 = _matmul_residual(att_s, w_out_b, h_s, tm=sm, tk=SAMPLE_K_TILE,
                                         g_final=g_final, emit_weights=True)
    qz_p, kv_p = _norm_matmul(
        h_p, g_norm_b, w_in_b_bf16, tm=PROMPT_ROW_TILE, tn=QZ_COL_TILE, out_dtype=BF16,
        w_extra=w_kv, g_extra=g_norm_kv, gate_cols=SWA_WIDTH)
    att_p, k_p, v_p = _attn_prompt(qz_p, kv_p, sinks, batch=pb, seq=ps)
    y_p = _matmul_residual(att_p, w_out_b_bf16, h_p, tm=RESIDUAL_ROW_TILE, g_final=g_final)

    return (y_p.reshape(pb, ps, D_MODEL), y_s.reshape(sb, ss, D_MODEL), gla_p[None],
            gla_s[None], k_p, v_p, k_s, v_s)
```

```python
import functools

import jax
import jax.numpy as jnp
from jax import lax
from jax.experimental import pallas as pl
from jax.experimental.pallas import tpu as pltpu

F32 = jnp.float32
BF16 = jnp.bfloat16

D_MODEL = 2048
GLA_HEADS = 4
GLA_KEY_DIM = D_MODEL // 2
GLA_VALUE_DIM = D_MODEL
GLA_DK = GLA_KEY_DIM // GLA_HEADS
GLA_DV = GLA_VALUE_DIM // GLA_HEADS
GLA_GATE_RANK = 16
GLA_GATE_TEMP = 16.0
GLA_MAIN_COLS = 2 * GLA_KEY_DIM + 2 * GLA_VALUE_DIM
SWA_HEAD_DIM = 64
SWA_HEADS = D_MODEL // SWA_HEAD_DIM
SWA_KV_HEADS = 4
SWA_GROUP = SWA_HEADS // SWA_KV_HEADS
SWA_WIDTH = SWA_HEADS * SWA_HEAD_DIM
SWA_KV_WIDTH = SWA_KV_HEADS * SWA_HEAD_DIM
WINDOW = 128
RMS_EPS = 1e-6
LOG2E = 1.4426950408889634

V7X_VMEM_LIMIT_BYTES = 61 * 1024 * 1024
GLA_PROMPT_CHUNK = 128
GLA_SEQS_PER_STEP = 4
ATTN_BLOCKS_PER_STEP = 8
ATTN_BLOCKS_IN_FLIGHT = 2
SAMPLE_SEQS_PER_STEP = 8
NORM_ROW_CHUNK = 256
GATE_CUMSUM_ROWS = 128
GATE_ROW_TILE = 1024
PROMPT_ROW_TILE = 1024
PROMPT_COL_TILE = 2048
SAMPLE_COL_TILE = 1024
QZ_COL_TILE = 2048
SAMPLE_K_TILE = 1024
RESIDUAL_ROW_TILE = 512

_NT_DIMS = (((1,), (1,)), ((), ()))
_TN_DIMS = (((0,), (0,)), ((), ()))


def _params(semantics):
    return pltpu.CompilerParams(dimension_semantics=semantics,
                                vmem_limit_bytes=V7X_VMEM_LIMIT_BYTES)


def _silu(x):
    return x / (1.0 + jnp.exp(-x))


def _rms_scale(x):
    return lax.rsqrt(jnp.mean(x * x, axis=-1, keepdims=True) + RMS_EPS)


def _norm_matmul_kernel(x_ref, g_ref, w_ref, *rest, has_extra, extra_gain, emit_weights,
                        w_transposed, silu_from):
    dims = _NT_DIMS if w_transposed else (((1,), (0,)), ((), ()))

    def mm(a, b):
        return lax.dot_general(a, b, dims, preferred_element_type=F32)

    rest = list(rest)
    g2_ref = rest.pop(0) if extra_gain else None
    w2_ref = rest.pop(0) if has_extra else None
    o_ref = rest.pop(0)
    o2_ref = rest.pop(0) if has_extra else None
    wb_ref = rest.pop(0) if emit_weights else None
    (xn_ref,) = rest

    first = pl.program_id(1) == 0

    def weights():
        w = w_ref[...].astype(BF16)
        if emit_weights:
            wb_ref[...] = w
        return w

    @pl.when(first)
    def _():
        tm = x_ref.shape[0]
        rc = min(tm, NORM_ROW_CHUNK)
        w = weights()
        for c in range(tm // rc):
            rows = slice(c * rc, (c + 1) * rc)
            x = x_ref[rows, :]
            xs = x * _rms_scale(x)
            xn = (xs * g_ref[...]).astype(BF16)
            xn_ref[rows, :] = xn
            o_ref[rows, :] = mm(xn, w).astype(o_ref.dtype)
            if has_extra:
                xn2 = (xs * g2_ref[...]).astype(BF16) if extra_gain else xn
                o2_ref[rows, :] = mm(xn2, w2_ref[...].astype(BF16)).astype(o2_ref.dtype)

    def later_block(gated):
        w = weights()
        if not gated:
            o_ref[...] = mm(xn_ref[...], w).astype(o_ref.dtype)
            return
        tm = x_ref.shape[0]
        rc = min(tm, NORM_ROW_CHUNK)
        for c in range(tm // rc):
            rows = slice(c * rc, (c + 1) * rc)
            o_ref[rows, :] = _silu(mm(xn_ref[rows, :], w)).astype(o_ref.dtype)

    j = pl.program_id(1)
    if silu_from is None or silu_from == 1:
        pl.when(j >= 1)(functools.partial(later_block, silu_from == 1))
    else:
        pl.when((j >= 1) & (j < silu_from))(functools.partial(later_block, False))
        pl.when(j >= silu_from)(functools.partial(later_block, True))


def _norm_matmul(x, g, w, *, tm, tn, out_dtype, n=None, w_extra=None, g_extra=None,
                 emit_weights=False, w_transposed=False, gate_cols=0):
    m, k = x.shape
    n_axis = -2 if w_transposed else -1
    n = w.shape[n_axis] if n is None else n
    assert n % tn == 0 and m % tm == 0
    assert not emit_weights or m == tm, "each weight block must be visited exactly once"
    assert gate_cols % tn == 0 and gate_cols <= n - tn, "gate columns: whole blocks after block 0"
    silu_from = (n - gate_cols) // tn if gate_cols else None
    grid = (m // tm, n // tn)
    w_block = (tn, k) if w_transposed else (k, tn)
    w_index = (lambda i, j: (j, 0)) if w_transposed else (lambda i, j: (0, j))
    if w.ndim == 3:
        w_spec = pl.BlockSpec((None,) + w_block, lambda i, j: (0,) + w_index(i, j))
    else:
        w_spec = pl.BlockSpec(w_block, w_index)
    in_specs = [
        pl.BlockSpec((tm, k), lambda i, j: (i, 0)),
        pl.BlockSpec((1, k), lambda i, j: (0, 0)),
        w_spec,
    ]
    out_shape = [jax.ShapeDtypeStruct((m, n), out_dtype)]
    out_specs = [pl.BlockSpec((tm, tn), lambda i, j: (i, j))]
    args = [x, g.reshape(1, k), w]
    if g_extra is not None:
        in_specs.append(pl.BlockSpec((1, k), lambda i, j: (0, 0)))
        args.append(g_extra.reshape(1, k))
    if w_extra is not None:
        n2 = w_extra.shape[n_axis]
        in_specs.append(pl.BlockSpec(w_extra.shape, lambda i, j: (0, 0)))
        out_shape.append(jax.ShapeDtypeStruct((m, n2), F32))
        out_specs.append(pl.BlockSpec((tm, n2), lambda i, j: (i, 0)))
        args.append(w_extra)
    if emit_weights:
        out_shape.append(jax.ShapeDtypeStruct((n, k) if w_transposed else (k, n), BF16))
        out_specs.append(pl.BlockSpec(w_block, w_index))
    res = pl.pallas_call(
        functools.partial(_norm_matmul_kernel, has_extra=w_extra is not None,
                          extra_gain=g_extra is not None, emit_weights=emit_weights,
                          w_transposed=w_transposed, silu_from=silu_from),
        grid=grid,
        in_specs=in_specs,
        out_specs=out_specs,
        out_shape=out_shape,
        scratch_shapes=[pltpu.VMEM((tm, k), BF16)],
        compiler_params=_params(("parallel", "arbitrary")),
        name=f"norm_matmul_{m}x{n}",
    )(*args)
    return res if len(res) > 1 else res[0]


def _matmul_residual_kernel(a_ref, w_ref, r_ref, *rest, final_norm, emit_weights, nk):
    rest = list(rest)
    g_ref = rest.pop(0) if final_norm else None
    o_ref = rest.pop(0)
    w = w_ref[...].astype(BF16)
    if emit_weights:
        rest[0][...] = w
    part = jnp.dot(a_ref[...].astype(BF16), w, preferred_element_type=F32)

    if nk == 1:
        h = r_ref[...] + part
        if final_norm:
            h = h * _rms_scale(h) * g_ref[...]
        o_ref[...] = h
        return

    kk = pl.program_id(1)

    @pl.when(kk == 0)
    def _():
        o_ref[...] = r_ref[...] + part

    @pl.when(kk > 0)
    def _():
        o_ref[...] += part

    if final_norm:
        @pl.when(kk == nk - 1)
        def _():
            h = o_ref[...]
            o_ref[...] = h * _rms_scale(h) * g_ref[...]


def _matmul_residual(a, w, res, *, tm, tk=None, g_final=None, emit_weights=False):
    m, k = a.shape
    n = w.shape[-1]
    tk = k if tk is None else tk
    assert m % tm == 0 and k % tk == 0
    assert not emit_weights or m == tm, "each weight block must be visited exactly once"
    if w.ndim == 3:
        w_spec = pl.BlockSpec((None, tk, n), lambda i, j: (0, j, 0))
    else:
        w_spec = pl.BlockSpec((tk, n), lambda i, j: (j, 0))
    in_specs = [
        pl.BlockSpec((tm, tk), lambda i, j: (i, j)),
        w_spec,
        pl.BlockSpec((tm, n), lambda i, j: (i, 0)),
    ]
    args = [a, w, res]
    if g_final is not None:
        in_specs.append(pl.BlockSpec((1, n), lambda i, j: (0, 0)))
        args.append(g_final.reshape(1, n))
    out_shape = [jax.ShapeDtypeStruct((m, n), F32)]
    out_specs = [pl.BlockSpec((tm, n), lambda i, j: (i, 0))]
    if emit_weights:
        out_shape.append(jax.ShapeDtypeStruct((k, n), BF16))
        out_specs.append(pl.BlockSpec((tk, n), lambda i, j: (j, 0)))
    res = pl.pallas_call(
        functools.partial(_matmul_residual_kernel, final_norm=g_final is not None,
                          emit_weights=emit_weights, nk=k // tk),
        grid=(m // tm, k // tk),
        in_specs=in_specs,
        out_specs=out_specs,
        out_shape=out_shape,
        compiler_params=_params(("parallel", "arbitrary")),
        name=f"matmul_residual_{m}" + ("_final" if g_final is not None else ""),
    )(*args)
    return res if emit_weights else res[0]


def _matmul_residual_resident_kernel(a_ref, w_hbm, r_ref, *rest, final_norm, n_parts):
    rest = list(rest)
    g_ref = rest.pop(0) if final_norm else None
    o_ref, w_vmem, sem = rest
    kp = w_vmem.shape[0] // n_parts

    def slab_copy(p):
        return pltpu.make_async_copy(w_hbm.at[pl.ds(p * kp, kp)], w_vmem.at[pl.ds(p * kp, kp)],
                                     sem.at[p])

    def finish(h):
        if final_norm:
            h = h * _rms_scale(h) * g_ref[...]
        o_ref[...] = h

    first = pl.program_id(0) == 0

    @pl.when(first)
    def _():
        for p in range(n_parts):
            slab_copy(p).start()
        h = r_ref[...]
        for p in range(n_parts):
            slab_copy(p).wait()
            h = h + jnp.dot(a_ref[:, p * kp:(p + 1) * kp].astype(BF16),
                            w_vmem[p * kp:(p + 1) * kp, :], preferred_element_type=F32)
        finish(h)

    @pl.when(jnp.logical_not(first))
    def _():
        finish(r_ref[...] + jnp.dot(a_ref[...].astype(BF16), w_vmem[...],
                                    preferred_element_type=F32))


def _matmul_residual_resident(a, w, res, *, tm, g_final=None, n_parts=4):
    m, k = a.shape
    n = w.shape[-1]
    assert m % tm == 0 and k % n_parts == 0
    in_specs = [
        pl.BlockSpec((tm, k), lambda i: (i, 0)),
        pl.BlockSpec(memory_space=pl.ANY),
        pl.BlockSpec((tm, n), lambda i: (i, 0)),
    ]
    args = [a, w, res]
    if g_final is not None:
        in_specs.append(pl.BlockSpec((1, n), lambda i: (0, 0)))
        args.append(g_final.reshape(1, n))
    return pl.pallas_call(
        functools.partial(_matmul_residual_resident_kernel, final_norm=g_final is not None,
                          n_parts=n_parts),
        grid=(m // tm,),
        in_specs=in_specs,
        out_specs=pl.BlockSpec((tm, n), lambda i: (i, 0)),
        out_shape=jax.ShapeDtypeStruct((m, n), F32),
        scratch_shapes=[pltpu.VMEM((k, n), BF16), pltpu.SemaphoreType.DMA((n_parts,))],
        compiler_params=_params(("arbitrary",)),
        name=f"matmul_residual_resident_{m}" + ("_final" if g_final is not None else ""),
    )(*args)


def _split_bf16(x):
    hi = x.astype(BF16)
    return hi, (x - hi.astype(F32)).astype(BF16)


def _gla_gates_kernel(glow_ref, wg_ref, bg_ref, bcum_ref, *, chunk):
    rows = glow_ref.shape[0]
    g_hi, g_lo = _split_bf16(glow_ref[...])
    w_hi, w_lo = _split_bf16(wg_ref[...])
    x = jnp.dot(jnp.concatenate([g_hi, g_lo, g_hi], axis=1),
                jnp.concatenate([w_hi, w_hi, w_lo], axis=0),
                preferred_element_type=F32) + bg_ref[...]
    softplus2 = jnp.log2(1.0 + jnp.exp2(jnp.abs(x) * (-LOG2E)))
    logg = jnp.minimum(x, 0.0) * (LOG2E / GLA_GATE_TEMP) - softplus2 * (1.0 / GLA_GATE_TEMP)

    span = GATE_CUMSUM_ROWS
    row = lax.broadcasted_iota(jnp.int32, (span, span), 0)
    col = lax.broadcasted_iota(jnp.int32, (span, span), 1)
    same_chunk = (row // chunk) == (col // chunk) if chunk < span else True
    tril = jnp.where((col <= row) & same_chunk, 1.0, 0.0).astype(BF16)
    tril2 = jnp.concatenate([tril, tril], axis=1)
    for i in range(rows // span):
        hi, lo = _split_bf16(logg[i * span:(i + 1) * span])
        bcum_ref[i * span:(i + 1) * span, :] = jnp.dot(
            tril2, jnp.concatenate([hi, lo], axis=0), preferred_element_type=F32)


def _gates_with_gla_kernel(glow_ref, wg_ref, bg_ref, *rest, chunk, n_gla_in):
    gla_in, (bcum_ref, o_ref, sfin_ref) = rest[:n_gla_in], rest[n_gla_in:]
    _gla_kernel(*gla_in, o_ref, sfin_ref, has_init=True, n_chunks=1)
    _gla_gates_kernel(glow_ref, wg_ref, bg_ref, bcum_ref, chunk=chunk)


def _gla_gates(glow, wg, bg, *, chunk, tm, rider=None):
    m = glow.shape[0]
    assert m % tm == 0 and tm % GATE_CUMSUM_ROWS == 0
    assert GATE_CUMSUM_ROWS % chunk == 0 or chunk % GATE_CUMSUM_ROWS == 0
    assert chunk <= GATE_CUMSUM_ROWS, "cumulative sums do not cross row spans"
    steps = m // tm
    in_specs = [
        pl.BlockSpec((tm, GLA_GATE_RANK), lambda i: (i, 0)),
        pl.BlockSpec((GLA_GATE_RANK, GLA_KEY_DIM), lambda i: (0, 0)),
        pl.BlockSpec((1, GLA_KEY_DIM), lambda i: (0, 0)),
    ]
    args = [glow, wg, bg.reshape(1, -1)]
    out_specs = [pl.BlockSpec((tm, GLA_KEY_DIM), lambda i: (i, 0))]
    out_shape = [jax.ShapeDtypeStruct((m, GLA_KEY_DIM), F32)]
    if rider is None:
        body = functools.partial(_gla_gates_kernel, chunk=chunk)
    else:
        batch, seq = rider["batch"], rider["seq"]
        assert batch % steps == 0
        g_in, g_args, g_out, g_shape = _gla_operands(
            rider["proj"], rider["bcum"], rider["gon"], rider["s0"], batch=batch, seq=seq,
            chunk=seq, ns=batch // steps, out_dtype=F32, index=lambda i: (i, 0))
        body = functools.partial(_gates_with_gla_kernel, chunk=chunk, n_gla_in=len(g_in))
        in_specs += g_in
        args += g_args
        out_specs += g_out
        out_shape += g_shape
    res = pl.pallas_call(
        body,
        grid=(steps,),
        in_specs=in_specs,
        out_specs=out_specs,
        out_shape=out_shape,
        compiler_params=_params(("parallel",)),
        name=f"gla_gates_{m}",
    )(*args)
    if rider is None:
        return res[0]
    bcum, o, s_fin = res
    return bcum, o.reshape(rider["batch"] * rider["seq"], GLA_VALUE_DIM), s_fin


def _gla_kernel(q_ref, k_ref, v_ref, r_ref, bcum_ref, gon_ref, *rest, has_init, n_chunks):
    rest = list(rest)
    s0_ref = rest.pop(0) if has_init else None
    o_ref, sfin_ref = rest[:2]
    s_ref = rest[2] if n_chunks > 1 else None
    c = pl.program_id(1) if n_chunks > 1 else None
    nseq, chunk = q_ref.shape[:2]
    chains = [(s, h) for s in range(nseq) for h in range(GLA_HEADS)]

    if n_chunks > 1:
        @pl.when(c == 0)
        def _():
            if has_init:
                s_ref[...] = s0_ref[...]
            else:
                s_ref[...] = jnp.zeros_like(s_ref)

    row = lax.broadcasted_iota(jnp.int32, (chunk, chunk), 0)
    col = lax.broadcasted_iota(jnp.int32, (chunk, chunk), 1)
    causal = col <= row
    heads = range(GLA_HEADS)
    ks = [slice(h * GLA_DK, (h + 1) * GLA_DK) for h in heads]
    vs = [slice(h * GLA_DV, (h + 1) * GLA_DV) for h in heads]

    def state(s, h):
        if n_chunks > 1:
            return s_ref[s, h]
        return s0_ref[s, h] if has_init else jnp.zeros((GLA_DK, GLA_DV), F32)

    q_inter, k_state, scores, decay = {}, {}, {}, {}
    for s, h in chains:
        b = bcum_ref[s, :, ks[h]]
        b_last = b[chunk - 1:chunk, :]
        b_mid = b[chunk // 2 - 1:chunk // 2, :]
        q = q_ref[s, :, ks[h]].astype(F32)
        k = k_ref[s, :, ks[h]].astype(F32)
        q_inter[s, h] = (q * jnp.exp2(b)).astype(BF16)
        q_intra = (q * jnp.exp2(b - b_mid)).astype(BF16)
        k_intra = (k * jnp.exp2(b_mid - b)).astype(BF16)
        k_state[s, h] = (k * jnp.exp2(b_last - b)).astype(BF16)
        scores[s, h] = lax.dot_general(q_intra, k_intra, _NT_DIMS,
                                       preferred_element_type=F32)
        decay[s, h] = jnp.exp2(jnp.broadcast_to(b_last, (128, GLA_DK))).T

    o = {}
    for s, h in chains:
        sc = jnp.where(causal, scores[s, h], 0.0).astype(BF16)
        v = v_ref[s, :, vs[h]].astype(BF16)
        o[s, h] = (jnp.dot(q_inter[s, h], state(s, h).astype(BF16),
                           preferred_element_type=F32)
                   + jnp.dot(sc, v, preferred_element_type=F32))
        upd = lax.dot_general(k_state[s, h], v, _TN_DIMS, preferred_element_type=F32)
        s_new = (state(s, h) * jnp.concatenate([decay[s, h]] * (GLA_DV // 128), axis=1)
                 + upd)
        if n_chunks > 1:
            s_ref[s, h] = s_new
        else:
            sfin_ref[s, h] = s_new

    qs = GLA_DK ** -0.5
    scale = {sh: qs * lax.rsqrt(jnp.mean(o[sh] * o[sh], axis=-1, keepdims=True) * (qs * qs)
                                + RMS_EPS) for sh in chains}
    for s, h in chains:
        r = r_ref[s, :, vs[h]].astype(F32)
        o_ref[s, :, vs[h]] = (o[s, h] * scale[s, h] * gon_ref[:, vs[h]]
                              * r).astype(o_ref.dtype)

    if n_chunks > 1:
        @pl.when(c == n_chunks - 1)
        def _():
            sfin_ref[...] = s_ref[...]


def _gla_operands(proj, bcum, gon, s0, *, batch, seq, chunk, ns, out_dtype, index):
    assert batch % ns == 0 and seq % chunk == 0
    proj = proj.reshape(batch, seq, proj.shape[-1])
    bcum = bcum.reshape(batch, seq, bcum.shape[-1])
    kb, vb, rb = 1, 2 * GLA_KEY_DIM // GLA_VALUE_DIM, 2 * GLA_KEY_DIM // GLA_VALUE_DIM + 1

    def rows(width, col):
        return pl.BlockSpec((ns, chunk, width), lambda *g: (*index(*g), col))

    state_spec = pl.BlockSpec((ns, GLA_HEADS, GLA_DK, GLA_DV),
                              lambda *g: (index(*g)[0], 0, 0, 0))
    in_specs = [rows(GLA_KEY_DIM, 0), rows(GLA_KEY_DIM, kb), rows(GLA_VALUE_DIM, vb),
                rows(GLA_VALUE_DIM, rb), rows(GLA_KEY_DIM, 0),
                pl.BlockSpec((1, GLA_VALUE_DIM), lambda *g: (0, 0))]
    args = [proj, proj, proj, proj, bcum, gon.reshape(1, -1)]
    if s0 is not None:
        in_specs.append(state_spec)
        args.append(s0)
    out_specs = [rows(GLA_VALUE_DIM, 0), state_spec]
    out_shape = [jax.ShapeDtypeStruct((batch, seq, GLA_VALUE_DIM), out_dtype),
                 jax.ShapeDtypeStruct((batch, GLA_HEADS, GLA_DK, GLA_DV), F32)]
    return in_specs, args, out_specs, out_shape


def _gla(proj, bcum, gon, *, batch, seq, chunk, out_dtype, s0=None):
    n = seq // chunk
    ns = GLA_SEQS_PER_STEP
    in_specs, args, out_specs, out_shape = _gla_operands(
        proj, bcum, gon, s0, batch=batch, seq=seq, chunk=chunk, ns=ns, out_dtype=out_dtype,
        index=lambda b, c: (b, c))
    scratch = [pltpu.VMEM((ns, GLA_HEADS, GLA_DK, GLA_DV), F32)] if n > 1 else []
    o, s_fin = pl.pallas_call(
        functools.partial(_gla_kernel, has_init=s0 is not None, n_chunks=n),
        grid=(batch // ns, n),
        in_specs=in_specs,
        out_specs=out_specs,
        out_shape=out_shape,
        scratch_shapes=scratch,
        compiler_params=_params(("parallel", "arbitrary")),
        name=f"gla_chunk{chunk}",
    )(*args)
    return o.reshape(batch * seq, GLA_VALUE_DIM), s_fin


def _alibi_slope(head):
    return 2.0 ** (-8.0 * (head + 1) / SWA_HEADS)


def _attn_prompt_kernel(sinks_ref, q_ref, z_ref, kp_ref, ko_ref, vp_ref, vo_ref, out_ref,
                        kwin_ref, vwin_ref, bias_ref):
    hd, nkeys = SWA_HEAD_DIM, 2 * WINDOW
    pair_w = 2 * hd
    pairs_per_group = SWA_GROUP // 2
    blk = pl.program_id(1)

    @pl.when((pl.program_id(0) == 0) & (blk == 0))
    def _():
        kj = lax.broadcasted_iota(jnp.int32, (nkeys, WINDOW), 0)
        qi = lax.broadcasted_iota(jnp.int32, (nkeys, WINDOW), 1)
        dist = WINDOW + qi - kj
        ok = (dist >= 0) & (dist <= WINDOW)
        ok_first = ok & (kj >= WINDOW)
        distf = dist.astype(F32)
        for h in range(SWA_HEADS):
            pen = (-_alibi_slope(h) * LOG2E) * distf
            sl = slice((h % 2) * WINDOW, (h % 2 + 1) * WINDOW)
            bias_ref[0, h // 2, :, sl] = jnp.where(ok_first, pen, -jnp.inf)
            bias_ref[1, h // 2, :, sl] = jnp.where(ok, pen, -jnp.inf)

    n_sub = q_ref.shape[0] // WINDOW
    k_rows = jnp.concatenate([kp_ref[...], ko_ref[...]], axis=0)
    v_rows = jnp.concatenate([vp_ref[...], vo_ref[...]], axis=0)
    ones = jnp.ones((16, nkeys), F32)
    lane = lax.broadcasted_iota(jnp.int32, (nkeys, pair_w), 1)
    qlane = lax.broadcasted_iota(jnp.int32, (WINDOW, pair_w), 1)
    qk_scale = (hd ** -0.5) * LOG2E
    quad_pairs = 2
    n_quads = SWA_HEADS // (2 * quad_pairs)
    quads_per_group = pairs_per_group // quad_pairs
    quarter = lax.broadcasted_iota(jnp.int32, (1, 2 * quad_pairs * WINDOW), 1) // WINDOW

    def prepare(sub):
        k = k_rows[sub * WINDOW:sub * WINDOW + nkeys]
        vt = v_rows[sub * WINDOW:sub * WINDOW + nkeys].T
        k2, vt1 = [], []
        for g in range(SWA_KV_HEADS):
            kblk = k[:, (g // 2) * pair_w:(g // 2 + 1) * pair_w]
            k_here = jnp.where((lane < hd) if g % 2 == 0 else (lane >= hd), kblk, 0.0)
            k2.append((k_here + pltpu.roll(k_here, hd, axis=1)).astype(BF16))
            vt1.append(jnp.concatenate([vt[g * hd:(g + 1) * hd], ones], axis=0).astype(BF16))
        return dict(qrows=slice(sub * WINDOW, (sub + 1) * WINDOW), k2=k2, vt1=vt1,
                    tbl=jnp.minimum(blk, 1) if sub == 0 else 1)

    def scores(blkst, quad):
        parts = []
        for j in range(quad_pairs):
            col = (quad * quad_pairs + j) * pair_w
            q_pair = q_ref[blkst["qrows"], col:col + pair_w]
            zero = jnp.zeros_like(q_pair)
            parts += [jnp.where(qlane < hd, q_pair, zero), jnp.where(qlane >= hd, q_pair, zero)]
        return lax.dot_general(blkst["k2"][quad // quads_per_group],
                               jnp.concatenate(parts, axis=0),
                               _NT_DIMS, preferred_element_type=F32)

    def finish(blkst, quad, st):
        pair0 = quad * quad_pairs
        bias = jnp.concatenate([bias_ref[blkst["tbl"], pair0 + j] for j in range(quad_pairs)],
                               axis=1)
        s2 = st * qk_scale + bias
        sink2 = sinks_ref[2 * pair0]
        for t in range(1, 2 * quad_pairs):
            sink2 = jnp.where(quarter == t, sinks_ref[2 * pair0 + t], sink2)
        sink2 = sink2 * LOG2E
        m = jnp.maximum(jnp.max(s2, axis=0, keepdims=True), sink2)
        p = jnp.exp2(s2 - m).astype(BF16)
        oa = jnp.dot(blkst["vt1"][quad // quads_per_group], p, preferred_element_type=F32)
        denom = oa[hd:hd + 1] + jnp.exp2(sink2 - m)
        on = oa[0:hd] * (1.0 / denom)
        for j in range(quad_pairs):
            lo = 2 * j * WINDOW
            o_pair = jnp.concatenate([on[:, lo:lo + WINDOW],
                                      on[:, lo + WINDOW:lo + 2 * WINDOW]], axis=0).T
            col = (pair0 + j) * pair_w
            z_pair = z_ref[blkst["qrows"], col:col + pair_w].astype(F32)
            out_ref[blkst["qrows"], col:col + pair_w] = (
                o_pair * z_pair).astype(out_ref.dtype)

    for first in range(0, n_sub, ATTN_BLOCKS_IN_FLIGHT):
        blocks = [prepare(sub) for sub in range(first, min(first + ATTN_BLOCKS_IN_FLIGHT, n_sub))]
        st_next = [scores(b, 0) for b in blocks]
        for quad in range(n_quads):
            for i, b in enumerate(blocks):
                st = st_next[i]
                if quad + 1 < n_quads:
                    st_next[i] = scores(b, quad + 1)
                finish(b, quad, st)

    @pl.when(blk == pl.num_programs(1) - 1)
    def _():
        last = slice((n_sub - 1) * WINDOW, n_sub * WINDOW)
        kwin_ref[...] = ko_ref[last, :].T
        vwin_ref[...] = vo_ref[last, :].T


def _attn_prompt(qz, kv, sinks, *, batch, seq):
    sub = ATTN_BLOCKS_PER_STEP
    tq = sub * WINDOW
    nb = seq // tq
    assert seq % tq == 0
    row = lambda b, i: b * nb + i
    prev = lambda b, i: (b * nb + i) * sub - jnp.minimum(i, 1)
    win_spec = pl.BlockSpec((None, SWA_KV_WIDTH, WINDOW), lambda b, i: (b, 0, 0))
    win_shape = jax.ShapeDtypeStruct((batch, SWA_KV_WIDTH, WINDOW), F32)
    att, k_win, v_win = pl.pallas_call(
        _attn_prompt_kernel,
        grid=(batch, nb),
        in_specs=[
            pl.BlockSpec(memory_space=pltpu.SMEM),
            pl.BlockSpec((tq, SWA_WIDTH), lambda b, i: (row(b, i), 0)),
            pl.BlockSpec((tq, SWA_WIDTH), lambda b, i: (row(b, i), 1)),
            pl.BlockSpec((WINDOW, SWA_KV_WIDTH), lambda b, i: (prev(b, i), 0)),
            pl.BlockSpec((tq, SWA_KV_WIDTH), lambda b, i: (row(b, i), 0)),
            pl.BlockSpec((WINDOW, SWA_KV_WIDTH), lambda b, i: (prev(b, i), 1)),
            pl.BlockSpec((tq, SWA_KV_WIDTH), lambda b, i: (row(b, i), 1)),
        ],
        out_specs=[pl.BlockSpec((tq, SWA_WIDTH), lambda b, i: (row(b, i), 0)),
                   win_spec, win_spec],
        out_shape=[jax.ShapeDtypeStruct((batch * seq, SWA_WIDTH), BF16), win_shape, win_shape],
        scratch_shapes=[pltpu.VMEM((2, SWA_HEADS // 2, 2 * WINDOW, 2 * WINDOW), F32)],
        compiler_params=_params(("arbitrary", "arbitrary")),
        name="attn_prompt",
    )(sinks, qz, qz, kv, kv, kv, kv)
    to_rows = lambda w: jnp.transpose(
        w.reshape(batch, SWA_KV_HEADS, SWA_HEAD_DIM, WINDOW), (0, 3, 1, 2))
    return att, to_rows(k_win), to_rows(v_win)


def _attn_sample_kernel(sink_ref, slope_ref, q_ref, z_ref, kn_ref, vn_ref, kc_ref, vc_ref,
                        out_ref, kwin_ref, vwin_ref, *, tq, nb):
    hd, nk = SWA_HEAD_DIM, 2 * WINDOW
    rows = SWA_HEADS * tq
    grows = SWA_GROUP * tq
    seqs = range(nb)
    groups = range(SWA_KV_HEADS)

    lane = lax.broadcasted_iota(jnp.int32, (rows, nk), 1)
    tok = lax.broadcasted_iota(jnp.int32, (rows, nk), 0) % tq
    in_buffer = lane < WINDOW
    dist = jnp.where(in_buffer, WINDOW + tok - lane, (nk - tq) + tok - lane)
    allowed = (dist >= 0) & (dist <= WINDOW) & (in_buffer | (lane >= nk - tq))
    penalty = slope_ref[...] * dist.astype(F32)
    new_lanes = lax.broadcasted_iota(jnp.int32, (hd, WINDOW), 1) >= WINDOW - tq

    def new_rows_t(ref, b):
        x = jnp.concatenate([jnp.zeros((WINDOW - tq, SWA_KV_WIDTH), F32),
                             ref[b * tq:(b + 1) * tq, :]], axis=0)
        xt = [x[:, c * WINDOW:(c + 1) * WINDOW].T for c in range(SWA_KV_WIDTH // WINDOW)]
        per_block = WINDOW // hd
        return [xt[g // per_block][(g % per_block) * hd:(g % per_block + 1) * hd]
                for g in groups]

    k_all, v_all = {}, {}
    for b in seqs:
        kn_t, vn_t = new_rows_t(kn_ref, b), new_rows_t(vn_ref, b)
        for g in groups:
            kc, vc = kc_ref[b, g], vc_ref[b, g]
            k_all[b, g] = jnp.concatenate([kc, kn_t[g]], axis=1).astype(BF16)
            v_all[b, g] = jnp.concatenate([vc, vn_t[g]], axis=1).astype(BF16)
            kwin_ref[b, g] = jnp.where(new_lanes, kn_t[g], pltpu.roll(kc, WINDOW - tq, axis=1))
            vwin_ref[b, g] = jnp.where(new_lanes, vn_t[g], pltpu.roll(vc, WINDOW - tq, axis=1))

    s = []
    for b in seqs:
        q = q_ref[b * tq:(b + 1) * tq, :]
        parts = []
        for g in groups:
            qs = jnp.concatenate([q[:, h * hd:(h + 1) * hd]
                                  for h in range(g * SWA_GROUP, (g + 1) * SWA_GROUP)], axis=0)
            parts.append(jnp.dot(qs.astype(BF16), k_all[b, g], preferred_element_type=F32))
        s.append(jnp.concatenate(parts, axis=0))

    sink = sink_ref[...]
    s = [jnp.where(allowed, sb * (hd ** -0.5) - penalty, -jnp.inf) for sb in s]
    m = [jnp.maximum(jnp.max(sb, axis=-1, keepdims=True), sink) for sb in s]
    p = [jnp.exp(s[b] - m[b]) for b in seqs]
    inv = [1.0 / (jnp.sum(p[b], axis=-1, keepdims=True) + jnp.exp(sink - m[b])) for b in seqs]

    for b in seqs:
        pb = p[b].astype(BF16)
        o = jnp.concatenate(
            [lax.dot_general(pb[g * grows:(g + 1) * grows], v_all[b, g], _NT_DIMS,
                             preferred_element_type=F32) for g in groups], axis=0) * inv[b]
        o = jnp.concatenate([o[h * tq:(h + 1) * tq] for h in range(SWA_HEADS)], axis=1)
        z = z_ref[b * tq:(b + 1) * tq, :]
        out_ref[b * tq:(b + 1) * tq, :] = (o * z).astype(out_ref.dtype)


def _attn_sample(qz, kv, cache_k, cache_v, sinks, *, batch, seq):
    nb = SAMPLE_SEQS_PER_STEP
    assert batch % nb == 0
    rows = nb * seq
    sink_col = jnp.repeat(sinks, seq).reshape(SWA_HEADS * seq, 1)
    slope_col = jnp.repeat(jnp.asarray([_alibi_slope(h) for h in range(SWA_HEADS)], F32),
                           seq).reshape(SWA_HEADS * seq, 1)
    col_spec = pl.BlockSpec((SWA_HEADS * seq, 1), lambda i: (0, 0))
    win_spec = pl.BlockSpec((nb, SWA_KV_HEADS, SWA_HEAD_DIM, WINDOW), lambda i: (i, 0, 0, 0))
    win_shape = jax.ShapeDtypeStruct((batch, SWA_KV_HEADS, SWA_HEAD_DIM, WINDOW), F32)
    to_lanes = lambda c: jnp.transpose(c, (0, 2, 3, 1))
    att, k_win, v_win = pl.pallas_call(
        functools.partial(_attn_sample_kernel, tq=seq, nb=nb),
        grid=(batch // nb,),
        in_specs=[
            col_spec,
            col_spec,
            pl.BlockSpec((rows, SWA_WIDTH), lambda i: (i, 0)),
            pl.BlockSpec((rows, SWA_WIDTH), lambda i: (i, 1)),
            pl.BlockSpec((rows, SWA_KV_WIDTH), lambda i: (i, 0)),
            pl.BlockSpec((rows, SWA_KV_WIDTH), lambda i: (i, 1)),
            win_spec,
            win_spec,
        ],
        out_specs=[pl.BlockSpec((rows, SWA_WIDTH), lambda i: (i, 0)), win_spec, win_spec],
        out_shape=[jax.ShapeDtypeStruct((batch * seq, SWA_WIDTH), F32), win_shape, win_shape],
        compiler_params=_params(("parallel",)),
        name="attn_sample",
    )(sink_col, slope_col, qz, qz, kv, kv, to_lanes(cache_k), to_lanes(cache_v))
    return att, jnp.transpose(k_win, (0, 3, 1, 2)), jnp.transpose(v_win, (0, 3, 1, 2))


def kernel(x_prompt, x_sample, state_gla, cache_k_win, cache_v_win, g_norm_a, w_in_a,
           w_gate_up, b_gate, g_onorm_a, w_out_a, g_norm_kv, w_kv, g_norm_b, w_in_b, sinks,
           w_out_b, g_final):
    assert w_in_a.shape[0] == 1 and w_in_b.shape[0] == 1, "one GLA layer, one SWA layer"
    assert cache_k_win.shape[1] == WINDOW
    pb, ps, _ = x_prompt.shape
    sb, ss, _ = x_sample.shape
    pm, sm = pb * ps, sb * ss
    g_norm_a, w_gate_up, b_gate, g_onorm_a = g_norm_a[0], w_gate_up[0], b_gate[0], g_onorm_a[0]
    g_norm_b, sinks = g_norm_b[0], sinks[0]
    w_kv = w_kv.astype(BF16)
    w_in_a_t = jnp.swapaxes(w_in_a, 1, 2)
    w_glow = w_in_a_t[0, GLA_MAIN_COLS:, :]
    h_p = x_prompt.reshape(pm, D_MODEL)
    h_s = x_sample.reshape(sm, D_MODEL)

    proj_s, glow_s, w_in_a_bf16 = _norm_matmul(
        h_s, g_norm_a, w_in_a_t, tm=sm, tn=SAMPLE_COL_TILE, n=GLA_MAIN_COLS, out_dtype=F32,
        w_extra=w_glow, emit_weights=True, w_transposed=True, gate_cols=GLA_VALUE_DIM)
    bcum_s = _gla_gates(glow_s, w_gate_up, b_gate, chunk=ss, tm=sm)
    proj_p, glow_p = _norm_matmul(
        h_p, g_norm_a, w_in_a_bf16, tm=PROMPT_ROW_TILE, tn=PROMPT_COL_TILE, n=GLA_MAIN_COLS,
        out_dtype=BF16, w_extra=w_glow, w_transposed=True, gate_cols=GLA_VALUE_DIM)
    bcum_p, o_s, gla_s = _gla_gates(
        glow_p, w_gate_up, b_gate, chunk=GLA_PROMPT_CHUNK, tm=GATE_ROW_TILE,
        rider=dict(proj=proj_s, bcum=bcum_s, gon=g_onorm_a, batch=sb, seq=ss,
                   s0=state_gla.reshape(state_gla.shape[1:])))
    h_s, w_out_a_bf16 = _matmul_residual(o_s, w_out_a, h_s, tm=sm, tk=SAMPLE_K_TILE,
                                         emit_weights=True)
    o_p, gla_p = _gla(proj_p, bcum_p, g_onorm_a, batch=pb, seq=ps, chunk=GLA_PROMPT_CHUNK,
                      out_dtype=BF16)
    h_p = _matmul_residual_resident(o_p, w_out_a_bf16, h_p, tm=RESIDUAL_ROW_TILE)

    qz_s, kv_s, w_in_b_bf16 = _norm_matmul(
        h_s, g_norm_b, w_in_b, tm=sm, tn=QZ_COL_TILE, out_dtype=F32,
        w_extra=w_kv, g_extra=g_norm_kv, emit_weights=True, gate_cols=SWA_WIDTH)
    att_s, k_s, v_s = _attn_sample(qz_s, kv_s, cache_k_win, cache_v_win, sinks,
                                   batch=sb, seq=ss)
    y_s, w_out_b_bf16 = _matmul_residual(att_s, w_out_b, h_s, tm=sm, tk=SAMPLE_K_TILE,
                                         g_final=g_final, emit_weights=True)
    qz_p, kv_p = _norm_matmul(
        h_p, g_norm_b, w_in_b_bf16, tm=PROMPT_ROW_TILE, tn=QZ_COL_TILE, out_dtype=BF16,
        w_extra=w_kv, g_extra=g_norm_kv, gate_cols=SWA_WIDTH)
    att_p, k_p, v_p = _attn_prompt(qz_p, kv_p, sinks, batch=pb, seq=ps)
    y_p = _matmul_residual_resident(att_p, w_out_b_bf16, h_p, tm=RESIDUAL_ROW_TILE,
                                    g_final=g_final)

    return (y_p.reshape(pb, ps, D_MODEL), y_s.reshape(sb, ss, D_MODEL), gla_p[None],
            gla_s[None], k_p, v_p, k_s, v_s)
```
